```python
import jax
import jax.numpy as jnp
from jax import lax
import numpy as np

D_MODEL = 1024
BATCH = 8
SEQ = 2048
DEPTH = 2

CTX_LEN = 256
GRID_W = 64
EPS = 1e-6
ROPE_THETA = 10000.0
Q_BLOCK = 128

D_MIX = D_MODEL
N_MIXERS = 4
GROUP_W = D_MIX // N_MIXERS

A_HEADS = 4
A_HEAD_DIM = GROUP_W // A_HEADS
CHUNK = 128

CONV_W = 3

C_HEADS = 4
C_KV_HEADS = 2
C_GROUP = C_HEADS // C_KV_HEADS
C_HEAD_DIM = GROUP_W // C_HEADS
C_SCALE = C_HEAD_DIM ** -0.5

D_HEADS = 4
D_V = GROUP_W // D_HEADS
D_NOPE = D_V
D_ROPE = D_NOPE // 2
D_Q_LORA = D_MIX // 4
D_KV_LORA = D_MIX // 8
D_SCALE = (D_NOPE + D_ROPE) ** -0.5

IN_SPLITS = (GROUP_W, GROUP_W,
             GROUP_W, GROUP_W, GROUP_W,
             C_HEADS * C_HEAD_DIM, C_KV_HEADS * C_HEAD_DIM, C_KV_HEADS * C_HEAD_DIM,
             D_Q_LORA, D_KV_LORA, D_ROPE)
IN_COLS = sum(IN_SPLITS)
IN_SPLIT_IDX = tuple(int(i) for i in np.cumsum(IN_SPLITS)[:-1])

N_GROUPS = 4
EXPERTS_PER_GROUP = 4
N_EXPERTS = N_GROUPS * EXPERTS_PER_GROUP
TOP_K = 2
D_EXPERT = D_MODEL // 4

kernel_name = 'hybrid_head_group_moe_dit'


def _rms_norm(x, gain):
    xf = x.astype(jnp.float32)
    y = xf * lax.rsqrt(jnp.mean(xf * xf, axis=-1, keepdims=True) + EPS)
    return (y * gain.astype(jnp.float32)).astype(x.dtype)


def _modulate(h, shift, scale):
    return h * (1 + scale) + shift


def _axial_rope(rows, rot_dim):
    row = jnp.repeat(jnp.arange(rows, dtype=jnp.float32), GRID_W)
    col = jnp.tile(jnp.arange(GRID_W, dtype=jnp.float32), rows)
    n_freq = rot_dim // 4
    inv_freq = ROPE_THETA ** (-jnp.arange(n_freq, dtype=jnp.float32) / n_freq)
    ang = jnp.concatenate([row[:, None] * inv_freq, col[:, None] * inv_freq], axis=-1)
    return jnp.cos(ang), jnp.sin(ang)


def _apply_rope(x, cos, sin):
    xp = x.astype(jnp.float32).reshape(*x.shape[:-1], x.shape[-1] // 2, 2)
    x0, x1 = xp[..., 0], xp[..., 1]
    cs, sn = cos[None, :, None, :], sin[None, :, None, :]
    out = jnp.stack([x0 * cs - x1 * sn, x0 * sn + x1 * cs], axis=-1)
    return out.reshape(x.shape).astype(x.dtype)


def _attention(q, k, v, scale):
    s = jnp.einsum('bqhgd,bkhd->bhgqk', q, k).astype(jnp.float32) * scale
    p = jax.nn.softmax(s, axis=-1).astype(v.dtype)
    return jnp.einsum('bhgqk,bkhd->bqhgd', p, v)


def _blocked_attention(q, k, v, scale):
    b, n = q.shape[:2]
    qb = jnp.moveaxis(q.reshape(b, n // Q_BLOCK, Q_BLOCK, *q.shape[2:]), 1, 0)
    ob = lax.map(lambda qblk: _attention(qblk, k, v, scale), qb)
    return jnp.moveaxis(ob, 0, 1).reshape(b, n, *ob.shape[3:])


def _chunk_sgu(u, v, v_gain, w_s, b_s):
    b, n, _ = u.shape
    u = jax.nn.gelu(u)
    v = _rms_norm(jax.nn.gelu(v).reshape(b, n // CHUNK, CHUNK, A_HEADS, A_HEAD_DIM),
                  v_gain.reshape(A_HEADS, A_HEAD_DIM))
    mixed = jnp.einsum('hij,bcjhd->bcihd', w_s, v) + b_s.T[:, :, None]
    return u * mixed.reshape(b, n, GROUP_W)


def _short_conv(bg, cg, xin, w_conv):
    z = cg * xin
    z = lax.conv_general_dilated(z, w_conv[:, None, :], window_strides=(1,),
                                 padding=((CONV_W // 2, CONV_W // 2),),
                                 dimension_numbers=('NWC', 'WIO', 'NWC'),
                                 feature_group_count=GROUP_W)
    return bg * z


def _gqa_qkv(q, k, v, q_gain, k_gain, rope):
    b, n = q.shape[:2]
    q = _rms_norm(q.reshape(b, n, C_HEADS, C_HEAD_DIM), q_gain)
    k = _rms_norm(k.reshape(b, n, C_KV_HEADS, C_HEAD_DIM), k_gain)
    v = v.reshape(b, n, C_KV_HEADS, C_HEAD_DIM)
    if rope is not None:
        q = _apply_rope(q, *rope)
        k = _apply_rope(k, *rope)
    return q.reshape(b, n, C_KV_HEADS, C_GROUP, C_HEAD_DIM), k, v


def _mla_qkv(cq, ckv, kr, qa_gain, w_qb, kva_gain, w_kvb, q_gain, k_gain, rope):
    b, n = cq.shape[:2]
    q = (_rms_norm(cq, qa_gain) @ w_qb).reshape(b, n, D_HEADS, D_NOPE + D_ROPE)
    kv = (_rms_norm(ckv, kva_gain) @ w_kvb).reshape(b, n, D_HEADS, D_NOPE + D_V)
    q_nope = _rms_norm(q[..., :D_NOPE], q_gain[:D_NOPE])
    q_rope = _rms_norm(q[..., D_NOPE:], q_gain[D_NOPE:])
    k_nope = _rms_norm(kv[..., :D_NOPE], k_gain[:D_NOPE])
    k_rope = _rms_norm(kr[:, :, None, :], k_gain[D_NOPE:])
    if rope is not None:
        q_rope = _apply_rope(q_rope, *rope)
        k_rope = _apply_rope(k_rope, *rope)
    q = jnp.concatenate([q_nope, q_rope], axis=-1)[:, :, :, None, :]
    k = jnp.concatenate([k_nope, jnp.broadcast_to(k_rope, (b, n, D_HEADS, D_ROPE))], axis=-1)
    return q, k, kv[..., D_NOPE:]


def _merge_groups(ys, g_group_out, w_out):
    y = jnp.concatenate([yy.reshape(*yy.shape[:2], GROUP_W) for yy in ys], axis=-1)
    b, n = y.shape[:2]
    y = _rms_norm(y.reshape(b, n, N_MIXERS, GROUP_W), g_group_out.reshape(N_MIXERS, GROUP_W))
    return y.reshape(b, n, D_MIX) @ w_out


def _hier_moe(h, w_rg, b_rg, w_re, b_re, w_up, w_down):
    group_p = jax.nn.softmax((h @ w_rg + b_rg).astype(jnp.float32), axis=-1)
    g_w, g_idx = lax.top_k(group_p, 1)
    e_logits = (h @ w_re + b_re).astype(jnp.float32).reshape(*h.shape[:-1], N_GROUPS, EXPERTS_PER_GROUP)
    e_logits = jnp.take_along_axis(e_logits, g_idx[..., None], axis=-2)[..., 0, :]
    e_w, e_idx = lax.top_k(jax.nn.softmax(e_logits, axis=-1), TOP_K)
    e_w = e_w / jnp.sum(e_w, axis=-1, keepdims=True)
    expert_id = g_idx * EXPERTS_PER_GROUP + e_idx
    combine = jnp.sum(jax.nn.one_hot(expert_id, N_EXPERTS, dtype=jnp.float32) * (g_w * e_w)[..., None],
                      axis=-2).astype(h.dtype)
    gate, up = jnp.split(jnp.einsum('bld,edf->blef', h, w_up), 2, axis=-1)
    return jnp.einsum('blef,efd->bld', jax.nn.silu(gate) * up * combine[..., None], w_down)


def _layer(x, xc, sc, scc, rope_c, rope_d, p, update_ctx):
    (w_mod, b_mod, g_attn_norm, w_in, sgu_v_gain, sgu_w, sgu_b, conv_w,
     gqa_q_gain, gqa_k_gain, mla_qa_gain, mla_w_qb, mla_kva_gain, mla_w_kvb,
     mla_q_gain, mla_k_gain, g_group_out, w_out, g_ffn_norm,
     w_rg, b_rg, w_re, b_re, w_up, w_down) = p
    shift1, scale1, gate1, shift2, scale2, gate2 = jnp.split((sc @ w_mod + b_mod)[:, None, :], 6, axis=-1)
    shift1c, scale1c, gate1c, shift2c, scale2c, gate2c = jnp.split(scc @ w_mod + b_mod, 6, axis=-1)

    h = _modulate(_rms_norm(x, g_attn_norm), shift1, scale1)
    hc = _modulate(_rms_norm(xc, g_attn_norm), shift1c, scale1c)
    a_u, a_v, b_b, b_c, b_x, c_q, c_k, c_v, d_cq, d_ckv, d_kr = jnp.split(h @ w_in, IN_SPLIT_IDX, axis=-1)
    (a_uc, a_vc, b_bc, b_cc, b_xc, c_qc, c_kc, c_vc,
     d_cqc, d_ckvc, d_krc) = jnp.split(hc @ w_in, IN_SPLIT_IDX, axis=-1)
    mla_p = (mla_qa_gain, mla_w_qb, mla_kva_gain, mla_w_kvb, mla_q_gain, mla_k_gain)

    qC_c, kC_c, vC_c = _gqa_qkv(c_qc, c_kc, c_vc, gqa_q_gain, gqa_k_gain, None)
    qD_c, kD_c, vD_c = _mla_qkv(d_cqc, d_ckvc, d_krc, *mla_p, None)
    qC, kC, vC = _gqa_qkv(c_q, c_k, c_v, gqa_q_gain, gqa_k_gain, rope_c)
    qD, kD, vD = _mla_qkv(d_cq, d_ckv, d_kr, *mla_p, rope_d)

    yC = _blocked_attention(qC, jnp.concatenate([kC_c, kC], axis=1), jnp.concatenate([vC_c, vC], axis=1), C_SCALE)
    yD = _blocked_attention(qD, jnp.concatenate([kD_c, kD], axis=1), jnp.concatenate([vD_c, vD], axis=1), D_SCALE)
    yA = _chunk_sgu(a_u, a_v, sgu_v_gain, sgu_w, sgu_b)
    yB = _short_conv(b_b, b_c, b_x, conv_w)
    x = x + gate1 * _merge_groups((yA, yB, yC, yD), g_group_out, w_out)
    x = x + gate2 * _hier_moe(_modulate(_rms_norm(x, g_ffn_norm), shift2, scale2),
                              w_rg, b_rg, w_re, b_re, w_up, w_down)

    if update_ctx:
        yc = _merge_groups((_chunk_sgu(a_uc, a_vc, sgu_v_gain, sgu_w, sgu_b),
                            _short_conv(b_bc, b_cc, b_xc, conv_w),
                            _attention(qC_c, kC_c, vC_c, C_SCALE),
                            _attention(qD_c, kD_c, vD_c, D_SCALE)), g_group_out, w_out)
        xc = xc + gate1c * yc
        xc = xc + gate2c * _hier_moe(_modulate(_rms_norm(xc, g_ffn_norm), shift2c, scale2c),
                                     w_rg, b_rg, w_re, b_re, w_up, w_down)
    return x, xc


def setup_inputs(seed: int = 0) -> dict:
    key = jax.random.key(seed)
    ks = jax.random.split(key, 32)
    L, D = DEPTH, D_MODEL
    f32 = jnp.float32

    def nrm(k, shape, scale):
        return jax.random.normal(k, shape, f32) * scale

    def gain(k, shape):
        return 1.0 + 0.05 * jax.random.normal(k, shape, f32)

    return {
        'x': nrm(ks[0], (BATCH, SEQ, D), 1.0),
        'c': nrm(ks[1], (BATCH, D), 1.0),
        'ctx': nrm(ks[2], (BATCH, CTX_LEN, D), 1.0),
        'c_ctx': nrm(ks[3], (D,), 1.0),
        'w_mod': nrm(ks[4], (L, D, 6 * D), 0.5 * D ** -0.5),
        'b_mod': nrm(ks[5], (L, 6 * D), 0.02),
        'g_attn_norm': gain(ks[6], (L, D)),
        'w_in': nrm(ks[7], (L, D, IN_COLS), D ** -0.5),
        'sgu_v_gain': gain(ks[8], (L, GROUP_W)),
        'sgu_w': nrm(ks[9], (L, A_HEADS, CHUNK, CHUNK), CHUNK ** -0.5),
        'sgu_b': gain(ks[10], (L, A_HEADS, CHUNK)),
        'conv_w': nrm(ks[11], (L, CONV_W, GROUP_W), CONV_W ** -0.5),
        'gqa_q_gain': gain(ks[12], (L, C_HEAD_DIM)),
        'gqa_k_gain': gain(ks[13], (L, C_HEAD_DIM)),
        'mla_qa_gain': gain(ks[14], (L, D_Q_LORA)),
        'mla_w_qb': nrm(ks[15], (L, D_Q_LORA, D_HEADS * (D_NOPE + D_ROPE)), D_Q_LORA ** -0.5),
        'mla_kva_gain': gain(ks[16], (L, D_KV_LORA)),
        'mla_w_kvb': nrm(ks[17], (L, D_KV_LORA, D_HEADS * (D_NOPE + D_V)), D_KV_LORA ** -0.5),
        'mla_q_gain': gain(ks[18], (L, D_NOPE + D_ROPE)),
        'mla_k_gain': gain(ks[19], (L, D_NOPE + D_ROPE)),
        'g_group_out': gain(ks[20], (L, D_MIX)),
        'w_out': nrm(ks[21], (L, D_MIX, D), D_MIX ** -0.5),
        'g_ffn_norm': gain(ks[22], (L, D)),
        'w_router_group': nrm(ks[23], (L, D, N_GROUPS), D ** -0.5),
        'b_router_group': nrm(ks[24], (L, N_GROUPS), 0.01),
        'w_router_expert': nrm(ks[25], (L, D, N_EXPERTS), D ** -0.5),
        'b_router_expert': nrm(ks[26], (L, N_EXPERTS), 0.01),
        'w_expert_up': nrm(ks[27], (L, N_EXPERTS, D, 2 * D_EXPERT), D ** -0.5),
        'w_expert_down': nrm(ks[28], (L, N_EXPERTS, D_EXPERT, D), D_EXPERT ** -0.5),
    }


def reference(x, c, ctx, c_ctx, w_mod, b_mod, g_attn_norm, w_in, sgu_v_gain, sgu_w, sgu_b, conv_w,
              gqa_q_gain, gqa_k_gain, mla_qa_gain, mla_w_qb, mla_kva_gain, mla_w_kvb, mla_q_gain,
              mla_k_gain, g_group_out, w_out, g_ffn_norm, w_router_group, b_router_group,
              w_router_expert, b_router_expert, w_expert_up, w_expert_down):
    rows = x.shape[1] // GRID_W
    rope_c = _axial_rope(rows, C_HEAD_DIM)
    rope_d = _axial_rope(rows, D_ROPE)
    sc = jax.nn.silu(c)
    scc = jax.nn.silu(c_ctx)
    xc = ctx
    for l in range(DEPTH):
        p = (w_mod[l], b_mod[l], g_attn_norm[l], w_in[l], sgu_v_gain[l], sgu_w[l], sgu_b[l], conv_w[l],
             gqa_q_gain[l], gqa_k_gain[l], mla_qa_gain[l], mla_w_qb[l], mla_kva_gain[l], mla_w_kvb[l],
             mla_q_gain[l], mla_k_gain[l], g_group_out[l], w_out[l], g_ffn_norm[l],
             w_router_group[l], b_router_group[l], w_router_expert[l], b_router_expert[l],
             w_expert_up[l], w_expert_down[l])
        x, xc = _layer(x, xc, sc, scc, rope_c, rope_d, p, l < DEPTH - 1)
    return x
```

```python
import functools

import numpy as np
import jax
import jax.numpy as jnp
from jax import lax
from jax.experimental import pallas as pl
from jax.experimental.pallas import tpu as pltpu

F32 = jnp.float32
BF16 = jnp.bfloat16

D_MODEL = 1024
GRID_W = 64
EPS = 1e-6
ROPE_THETA = 10000.0
GROUP_W = 256
CHUNK = 128
A_HEADS = 4
C_HEAD_DIM = 64
C_SCALE = C_HEAD_DIM ** -0.5
D_HEADS = 4
D_NOPE = 64
D_ROPE = 32
D_V = 64
D_Q_LORA = 256
D_KV_LORA = 128
D_SCALE = (D_NOPE + D_ROPE) ** -0.5
N_GROUPS = 4
EXPERTS_PER_GROUP = 4
N_EXPERTS = 16
D_EXPERT = 256
LANES = 128
HALO = 8

IN_W = 2304
VMEM_LIMIT = 56 * 1024 * 1024


def _nt_dot(a, b):
    return lax.dot_general(a, b, (((1,), (1,)), ((), ())), preferred_element_type=F32)


def _dot(a, b):
    return jnp.dot(a, b, preferred_element_type=F32)


def _rms_full(x, gain):
    return x * lax.rsqrt(jnp.mean(x * x, axis=-1, keepdims=True) + EPS) * gain


def _group_ms(x, ones_blocks, inv_count):
    return _dot((x * x).astype(BF16), ones_blocks) * inv_count


def _pair_swap(x, even):
    nxt = pltpu.roll(x, x.shape[1] - 1, axis=1)
    prv = pltpu.roll(x, 1, axis=1)
    return jnp.where(even, nxt, prv)


def _rope(x, cos, sin, even):
    return x * cos + _pair_swap(x, even) * sin


def _mod_kernel(c_ref, w_ref, b_ref, o_ref):
    cv = c_ref[...]
    sc = cv * jax.nn.sigmoid(cv)
    o_ref[0] = jnp.dot(sc, w_ref[0], preferred_element_type=F32,
                       precision=lax.Precision.HIGHEST) + b_ref[0]


def _modulation(c_all, w_mod, b_mod):
    depth = w_mod.shape[0]
    nblk = 4
    wblk = 6 * D_MODEL // nblk
    return pl.pallas_call(
        _mod_kernel,
        grid=(depth, nblk),
        in_specs=[pl.BlockSpec((16, D_MODEL), lambda l, j: (0, 0)),
                  pl.BlockSpec((1, D_MODEL, wblk), lambda l, j: (l, 0, j)),
                  pl.BlockSpec((1, 1, wblk), lambda l, j: (l, 0, j))],
        out_specs=pl.BlockSpec((1, 16, wblk), lambda l, j: (l, 0, j)),
        out_shape=jax.ShapeDtypeStruct((depth, 16, 6 * D_MODEL), F32),
        compiler_params=pltpu.CompilerParams(dimension_semantics=("arbitrary", "arbitrary"),
                                             vmem_limit_bytes=VMEM_LIMIT),
        name="mod",
    )(c_all, w_mod, b_mod.reshape(depth, 1, 6 * D_MODEL))


def _in_kernel(*refs, tm, seq_len, use_rope):
    (x_ref, xp_ref, xn_ref, mod_ref, gattn_ref, win_ref, sguw_ref, sgub_ref, vgain_ref, convw_ref,
     ggo_ref, cqg_ref, ckg_ref, qag_ref, wqb_ref, kvag_ref, wkvb_ref, dqg_ref, dkg_ref, dkrg_ref,
     b64_ref, bq_ref, invq_ref) = refs[:23]
    pos = 23
    if use_rope:
        cosc_ref, sinc_ref, cosq_ref, sinq_ref, cosk_ref, sink_ref = refs[pos:pos + 6]
        pos += 6
    ya_ref, yb_ref, qc_ref, kc_ref, vc_ref, qd_ref, kd_ref, vd_ref = refs[pos:pos + 8]

    i = pl.program_id(0)
    shift = mod_ref[0, 0:1, :]
    scale = mod_ref[0, 1:2, :]
    gattn = gattn_ref[...]

    def prenorm(xv):
        return (_rms_full(xv, gattn) * (1.0 + scale) + shift).astype(BF16)

    hb = prenorm(x_ref[...])
    b64 = b64_ref[...]
    inv64 = 1.0 / 64.0
    lane128 = lax.broadcasted_iota(jnp.int32, (tm, LANES), 1)
    even = (lane128 & 1) == 0

    u = jax.nn.gelu(_dot(hb, win_ref[:, 0:256]))
    v = jax.nn.gelu(_dot(hb, win_ref[:, 256:512]))
    v = (v * lax.rsqrt(_group_ms(v, b64, inv64) + EPS) * vgain_ref[...]).astype(BF16)
    lane_head = lax.broadcasted_iota(jnp.int32, (CHUNK, GROUP_W), 1) // 64
    sguw = sguw_ref[...]
    mixed_chunks = []
    for ci in range(tm // CHUNK):
        r = _dot(sguw, v[ci * CHUNK:(ci + 1) * CHUNK, :])
        m = r[0:CHUNK]
        for h in range(1, A_HEADS):
            m = jnp.where(lane_head == h, r[h * CHUNK:(h + 1) * CHUNK], m)
        mixed_chunks.append(m + sgub_ref[...])
    ya = u * jnp.concatenate(mixed_chunks, axis=0)
    ya_ref[...] = _rms_full(ya, ggo_ref[:, 0:256]).astype(BF16)

    bg = _dot(hb, win_ref[:, 512:768])
    z = _dot(hb, win_ref[:, 768:1024]) * _dot(hb, win_ref[:, 1024:1280])
    hh = prenorm(jnp.concatenate([xp_ref[...], xn_ref[...]], axis=0))
    zh = _dot(hh, win_ref[:, 768:1024]) * _dot(hh, win_ref[:, 1024:1280])
    row = lax.broadcasted_iota(jnp.int32, (tm, GROUP_W), 0)
    posn = (i * tm + row) % seq_len
    z_prev = jnp.where(row == 0, zh[HALO - 1:HALO, :], pltpu.roll(z, 1, axis=0))
    z_prev = jnp.where(posn == 0, 0.0, z_prev)
    z_next = jnp.where(row == tm - 1, zh[HALO:HALO + 1, :], pltpu.roll(z, tm - 1, axis=0))
    z_next = jnp.where(posn == seq_len - 1, 0.0, z_next)
    yb = bg * (convw_ref[0:1, :] * z_prev + convw_ref[1:2, :] * z + convw_ref[2:3, :] * z_next)
    yb_ref[...] = _rms_full(yb, ggo_ref[:, 256:512]).astype(BF16)

    q = _dot(hb, win_ref[:, 1280:1536])
    q = q * lax.rsqrt(_group_ms(q, b64, inv64) + EPS) * cqg_ref[...]
    k = _dot(hb, win_ref[:, 1536:1664])
    k = k * lax.rsqrt(_group_ms(k, b64[0:128, 0:128], inv64) + EPS) * ckg_ref[...]
    if use_rope:
        cosc, sinc = cosc_ref[...], sinc_ref[...]
        q = jnp.concatenate([_rope(q[:, 0:128], cosc, sinc, even),
                             _rope(q[:, 128:256], cosc, sinc, even)], axis=1)
        k = _rope(k, cosc, sinc, even)
    qc_ref[...] = (q * C_SCALE).astype(BF16)
    kc_ref[...] = k.astype(BF16)
    vc_ref[...] = _dot(hb, win_ref[:, 1664:1792]).astype(BF16)

    cq = _rms_full(_dot(hb, win_ref[:, 1792:2048]), qag_ref[...]).astype(BF16)
    qd = _dot(cq, wqb_ref[...])
    bq = bq_ref[...]
    invq = invq_ref[...]
    halves = []
    for hf in range(2):
        qh = qd[:, hf * 256:(hf + 1) * 256]
        qh = qh * lax.rsqrt(_group_ms(qh, bq, invq) + EPS) * dqg_ref[:, hf * 256:(hf + 1) * 256]
        if use_rope:
            cosq, sinq = cosq_ref[...], sinq_ref[...]
            qh = jnp.concatenate([_rope(qh[:, 0:128], cosq, sinq, even),
                                  _rope(qh[:, 128:256], cosq, sinq, even)], axis=1)
        halves.append(qh)
    qd_ref[...] = (jnp.concatenate(halves, axis=1) * D_SCALE).astype(BF16)

    ckv = _rms_full(_dot(hb, win_ref[:, 2048:2176]), kvag_ref[...]).astype(BF16)
    kv = _dot(ckv, wkvb_ref[...])
    vd_ref[...] = kv[:, 512:768].astype(BF16)
    kr = _dot(hb, win_ref[:, 2176:2304])
    kr = kr * lax.rsqrt(jnp.sum(kr * kr, axis=-1, keepdims=True) * (1.0 / D_ROPE) + EPS) * dkrg_ref[...]
    if use_rope:
        kr = _rope(kr, cosk_ref[...], sink_ref[...], even)
    kr = pltpu.roll(kr, D_NOPE, axis=1)
    for hf in range(2):
        kh = kv[:, hf * 256:(hf + 1) * 256]
        kh = kh * lax.rsqrt(_group_ms(kh, b64, inv64) + EPS) * dkg_ref[:, hf * 256:(hf + 1) * 256]
        kd_ref[:, hf * 256:hf * 256 + 128] = (kh[:, 0:128] + kr).astype(BF16)
        kd_ref[:, hf * 256 + 128:(hf + 1) * 256] = (kh[:, 128:256] + kr).astype(BF16)


def _in_call(x2d, mod, lw, rope, *, tm, seq_len, mod_index):
    n = x2d.shape[0]
    nt = n // tm
    nb8 = n // HALO
    tb = tm // HALO
    use_rope = rope is not None
    const2 = lambda i: (0, 0)
    row = lambda i: (i, 0)
    in_specs = [
        pl.BlockSpec((tm, D_MODEL), row),
        pl.BlockSpec((HALO, D_MODEL), lambda i: (jnp.maximum(i * tb - 1, 0), 0)),
        pl.BlockSpec((HALO, D_MODEL), lambda i: (jnp.minimum((i + 1) * tb, nb8 - 1), 0)),
        pl.BlockSpec((1, 6, D_MODEL), lambda i: (mod_index(i), 0, 0)),
    ]
    args = [x2d, x2d, x2d, mod]
    for name in ("g_attn", "w_in", "sgu_w", "sgu_b", "v_gain", "conv_w", "g_go", "cq_gain", "ck_gain",
                 "qa_gain", "w_qb", "kva_gain", "w_kvb", "dq_gain", "dk_gain", "dkr_gain",
                 "b64", "bq", "invq"):
        a = lw[name]
        in_specs.append(pl.BlockSpec(a.shape, const2))
        args.append(a)
    if use_rope:
        ntab = seq_len // tm
        for a in rope:
            in_specs.append(pl.BlockSpec((tm, LANES), lambda i: (i % ntab, 0)))
            args.append(a)
    widths = (256, 256, 256, 128, 128, 512, 512, 256)
    out_specs = [pl.BlockSpec((tm, w), row) for w in widths]
    out_shape = [jax.ShapeDtypeStruct((n, w), BF16) for w in widths]
    return pl.pallas_call(
        functools.partial(_in_kernel, tm=tm, seq_len=seq_len, use_rope=use_rope),
        grid=(nt,),
        in_specs=in_specs,
        out_specs=out_specs,
        out_shape=out_shape,
        compiler_params=pltpu.CompilerParams(dimension_semantics=("parallel",),
                                             vmem_limit_bytes=VMEM_LIMIT),
        name="in_proj",
    )(*args)


def _attn_kernel(*refs, mode, nseg, tq):
    q_ref = refs[0]
    k_refs = refs[1:1 + nseg]
    v_refs = refs[1 + nseg:1 + 2 * nseg]
    gain_ref = refs[1 + 2 * nseg]
    o_ref = refs[2 + 2 * nseg]

    def head(qm, ks, vs):
        ss = [_nt_dot(qm, kk) for kk in ks]
        m = ss[0].max(axis=-1, keepdims=True)
        for s in ss[1:]:
            m = jnp.maximum(m, s.max(axis=-1, keepdims=True))
        ps = [jnp.exp(s - m) for s in ss]
        l = ps[0].sum(axis=-1, keepdims=True)
        for p in ps[1:]:
            l = l + p.sum(axis=-1, keepdims=True)
        o = _dot(ps[0].astype(BF16), vs[0])
        for p, vv in zip(ps[1:], vs[1:]):
            o = o + _dot(p.astype(BF16), vv)
        return o / l

    if mode == "C":
        ks = [r[0] for r in k_refs]
        vs = [r[0] for r in v_refs]
        first_half = lax.broadcasted_iota(jnp.int32, (tq, LANES), 1) < 64
        slabs = []
        for sl in range(2):
            qs = q_ref[0, :, sl * 128:(sl + 1) * 128]
            o0 = head(jnp.where(first_half, qs, jnp.zeros_like(qs)), ks, vs)
            o1 = head(jnp.where(first_half, jnp.zeros_like(qs), qs), ks, vs)
            slabs.append(jnp.where(first_half, o0, o1))
        out = jnp.concatenate(slabs, axis=1)
    else:
        vs = [r[0] for r in v_refs]
        lane_head = lax.broadcasted_iota(jnp.int32, (tq, GROUP_W), 1) // D_V
        out = None
        for h in range(D_HEADS):
            ks = [r[0, :, h * 128:(h + 1) * 128] for r in k_refs]
            o = head(q_ref[0, :, h * 128:(h + 1) * 128], ks, vs)
            out = o if out is None else jnp.where(lane_head == h, o, out)
    o_ref[0] = _rms_full(out, gain_ref[...]).astype(BF16)


def _attn_call(q, ks, vs, gain, *, mode, tq):
    b, s, wq = q.shape
    nseg = len(ks)
    in_specs = [pl.BlockSpec((1, tq, wq), lambda bi, qi: (bi, qi, 0))]
    for a in list(ks) + list(vs):
        in_specs.append(pl.BlockSpec((1,) + a.shape[1:], lambda bi, qi: (bi, 0, 0)))
    in_specs.append(pl.BlockSpec((1, GROUP_W), lambda bi, qi: (0, 0)))
    return pl.pallas_call(
        functools.partial(_attn_kernel, mode=mode, nseg=nseg, tq=tq),
        grid=(b, s // tq),
        in_specs=in_specs,
        out_specs=pl.BlockSpec((1, tq, GROUP_W), lambda bi, qi: (bi, qi, 0)),
        out_shape=jax.ShapeDtypeStruct((b, s, GROUP_W), BF16),
        compiler_params=pltpu.CompilerParams(dimension_semantics=("parallel", "parallel"),
                                             vmem_limit_bytes=VMEM_LIMIT),
        name="attn_" + mode,
    )(q, *ks, *vs, gain)


def _merge_kernel(x_ref, ya_ref, yb_ref, yc_ref, yd_ref, mod_ref, wout_ref, gffn_ref, wr_ref, br_ref,
                  xo_ref, h2_ref, cmb_ref, *, tm):
    y = _dot(ya_ref[...], wout_ref[0:256, :])
    y = y + _dot(yb_ref[...], wout_ref[256:512, :])
    y = y + _dot(yc_ref[...], wout_ref[512:768, :])
    y = y + _dot(yd_ref[...], wout_ref[768:1024, :])
    xn = x_ref[...] + mod_ref[0, 2:3, :] * y
    xo_ref[...] = xn
    h2 = _rms_full(xn, gffn_ref[...]) * (1.0 + mod_ref[0, 4:5, :]) + mod_ref[0, 3:4, :]
    h2_ref[...] = h2.astype(BF16)

    logits = jnp.dot(h2, wr_ref[...], preferred_element_type=F32,
                     precision=lax.Precision.HIGHEST) + br_ref[...]
    lane = lax.broadcasted_iota(jnp.int32, (tm, LANES), 1)
    lanef = lane.astype(F32)
    neg = jnp.float32(-1e30)
    big = jnp.float32(1e9)
    is_g = (lane >= N_EXPERTS) & (lane < N_EXPERTS + N_GROUPS)
    lg = jnp.where(is_g, logits, neg)
    gmax = lg.max(axis=-1, keepdims=True)
    g_idx = jnp.where(is_g & (lg == gmax), lanef, big).min(axis=-1, keepdims=True) - N_EXPERTS
    g_w = 1.0 / jnp.where(is_g, jnp.exp(lg - gmax), 0.0).sum(axis=-1, keepdims=True)
    in_grp = (lane < N_EXPERTS) & ((lane // EXPERTS_PER_GROUP).astype(F32) == g_idx)
    le = jnp.where(in_grp, logits, neg)
    m1 = le.max(axis=-1, keepdims=True)
    i1 = jnp.where(in_grp & (le == m1), lanef, big).min(axis=-1, keepdims=True)
    rest = in_grp & (lanef != i1)
    le2 = jnp.where(rest, logits, neg)
    m2 = le2.max(axis=-1, keepdims=True)
    i2 = jnp.where(rest & (le2 == m2), lanef, big).min(axis=-1, keepdims=True)
    r = jnp.exp(m2 - m1)
    w1 = 1.0 / (1.0 + r)
    w2 = r / (1.0 + r)
    cmb_ref[...] = g_w * jnp.where(lanef == i1, w1, jnp.where(lanef == i2, w2, 0.0))


def _merge_call(x2d, ya, yb, yc, yd, mod, lw, *, tm, mod_index):
    n = x2d.shape[0]
    row = lambda i: (i, 0)
    const2 = lambda i: (0, 0)
    in_specs = [pl.BlockSpec((tm, D_MODEL), row)]
    in_specs += [pl.BlockSpec((tm, GROUP_W), row)] * 4
    in_specs += [pl.BlockSpec((1, 6, D_MODEL), lambda i: (mod_index(i), 0, 0)),
                 pl.BlockSpec((D_MODEL, D_MODEL), const2),
                 pl.BlockSpec((1, D_MODEL), const2),
                 pl.BlockSpec((D_MODEL, LANES), const2),
                 pl.BlockSpec((1, LANES), const2)]
    return pl.pallas_call(
        functools.partial(_merge_kernel, tm=tm),
        grid=(n // tm,),
        in_specs=in_specs,
        out_specs=[pl.BlockSpec((tm, D_MODEL), row), pl.BlockSpec((tm, D_MODEL), row),
                   pl.BlockSpec((tm, LANES), row)],
        out_shape=[jax.ShapeDtypeStruct((n, D_MODEL), F32), jax.ShapeDtypeStruct((n, D_MODEL), BF16),
                   jax.ShapeDtypeStruct((n, LANES), F32)],
        compiler_params=pltpu.CompilerParams(dimension_semantics=("parallel",),
                                             vmem_limit_bytes=VMEM_LIMIT),
        name="merge",
    )(x2d, ya, yb, yc, yd, mod, lw["w_out"], lw["g_ffn"], lw["w_router"], lw["b_router"])


def _moe_kernel(h2_ref, cmb_ref, x_ref, mod_ref, wup_ref, wdn_ref, o_ref, acc_ref, *, tm):
    e = pl.program_id(1)

    @pl.when(e == 0)
    def _():
        acc_ref[...] = jnp.zeros_like(acc_ref)

    gu = _dot(h2_ref[...], wup_ref[0])
    lane = lax.broadcasted_iota(jnp.int32, (tm, LANES), 1)
    w = jnp.where(lane == e, cmb_ref[...], 0.0).sum(axis=-1, keepdims=True)
    g = gu[:, 0:D_EXPERT]
    a = (g * jax.nn.sigmoid(g)) * gu[:, D_EXPERT:2 * D_EXPERT] * w
    acc_ref[...] += _dot(a.astype(BF16), wdn_ref[0])

    @pl.when(e == N_EXPERTS - 1)
    def _():
        o_ref[...] = x_ref[...] + mod_ref[0, 5:6, :] * acc_ref[...]


def _moe_call(h2, cmb, x2d, mod, lw, *, tm, mod_index):
    n = x2d.shape[0]
    row = lambda i, e: (i, 0)
    return pl.pallas_call(
        functools.partial(_moe_kernel, tm=tm),
        grid=(n // tm, N_EXPERTS),
        in_specs=[pl.BlockSpec((tm, D_MODEL), row),
                  pl.BlockSpec((tm, LANES), row),
                  pl.BlockSpec((tm, D_MODEL), row),
                  pl.BlockSpec((1, 6, D_MODEL), lambda i, e: (mod_index(i), 0, 0)),
                  pl.BlockSpec((1, D_MODEL, 2 * D_EXPERT), lambda i, e: (e, 0, 0)),
                  pl.BlockSpec((1, D_EXPERT, D_MODEL), lambda i, e: (e, 0, 0))],
        out_specs=pl.BlockSpec((tm, D_MODEL), row),
        out_shape=jax.ShapeDtypeStruct((n, D_MODEL), F32),
        scratch_shapes=[pltpu.VMEM((tm, D_MODEL), F32)],
        compiler_params=pltpu.CompilerParams(dimension_semantics=("parallel", "arbitrary"),
                                             vmem_limit_bytes=VMEM_LIMIT),
        name="moe",
    )(h2, cmb, x2d, mod, lw["w_up"], lw["w_down"])


def _ones_blocks(sizes):
    n = sum(sizes)
    m = np.zeros((n, n), np.float32)
    o = 0
    for s in sizes:
        m[o:o + s, o:o + s] = 1.0
        o += s
    return jnp.asarray(m, BF16)


def _rope_tables(rows):
    r = jnp.repeat(jnp.arange(rows, dtype=F32), GRID_W)
    c = jnp.tile(jnp.arange(GRID_W, dtype=F32), rows)

    def tab(rot_dim):
        nf = rot_dim // 4
        inv = ROPE_THETA ** (-jnp.arange(nf, dtype=F32) / nf)
        ang = jnp.concatenate([r[:, None] * inv, c[:, None] * inv], axis=-1)
        cos = jnp.repeat(jnp.cos(ang), 2, axis=-1)
        sin = jnp.repeat(jnp.sin(ang), 2, axis=-1) * jnp.tile(jnp.asarray([-1.0, 1.0], F32), rot_dim // 2)
        return cos, sin

    s = rows * GRID_W
    cc, sc = tab(C_HEAD_DIM)
    cd, sd = tab(D_ROPE)
    one = lambda w: jnp.ones((s, w), F32)
    zero = lambda w: jnp.zeros((s, w), F32)
    return (jnp.tile(cc, (1, 2)), jnp.tile(sc, (1, 2)),
            jnp.concatenate([one(64), cd, one(32)], axis=1), jnp.concatenate([zero(64), sd, zero(32)], axis=1),
            jnp.concatenate([cd, one(96)], axis=1), jnp.concatenate([sd, zero(96)], axis=1))


def _layer_weights(l, p):
    (g_attn_norm, w_in, sgu_v_gain, sgu_w, sgu_b, conv_w, gqa_q_gain, gqa_k_gain, mla_qa_gain, mla_w_qb,
     mla_kva_gain, mla_w_kvb, mla_q_gain, mla_k_gain, g_group_out, w_out, g_ffn_norm,
     w_rg, b_rg, w_re, b_re, w_up, w_down) = [a[l] for a in p]
    perm_c = np.concatenate([np.arange(0, 64), np.arange(128, 192), np.arange(64, 128), np.arange(192, 256)])
    wi = w_in
    w_in_p = jnp.concatenate([wi[:, :1280], wi[:, 1280:1536][:, perm_c], wi[:, 1536:2208],
                              jnp.zeros((D_MODEL, IN_W - 2208), F32)], axis=1).astype(BF16)
    z32 = jnp.zeros((D_Q_LORA, 32), F32)
    wqb = jnp.concatenate(
        [jnp.concatenate([mla_w_qb[:, h * 96:h * 96 + 96], z32], axis=1) for h in range(D_HEADS)], axis=1)
    z64 = jnp.zeros((D_KV_LORA, 64), F32)
    wk = jnp.concatenate(
        [jnp.concatenate([mla_w_kvb[:, h * 128:h * 128 + 64], z64], axis=1) for h in range(D_HEADS)], axis=1)
    wv = jnp.concatenate([mla_w_kvb[:, h * 128 + 64:h * 128 + 128] for h in range(D_HEADS)], axis=1)
    qslab = jnp.concatenate([mla_q_gain, jnp.zeros((32,), F32)])
    kslab = jnp.concatenate([mla_k_gain[:64], jnp.zeros((64,), F32)])
    g_go = jnp.concatenate([g_group_out[:512], g_group_out[512:768][perm_c], g_group_out[768:]])
    w_out_p = jnp.concatenate([w_out[:512], w_out[512:768][perm_c], w_out[768:]], axis=0).astype(BF16)
    w_router = jnp.concatenate([w_re, w_rg, jnp.zeros((D_MODEL, LANES - 20), F32)], axis=1)
    b_router = jnp.concatenate([b_re, b_rg, jnp.zeros((LANES - 20,), F32)])[None, :]
    sgub_full = jnp.repeat(sgu_b.T, 64, axis=1)
    return {
        "g_attn": g_attn_norm[None, :], "w_in": w_in_p,
        "sgu_w": sgu_w.reshape(A_HEADS * CHUNK, CHUNK).astype(BF16), "sgu_b": sgub_full,
        "v_gain": sgu_v_gain[None, :], "conv_w": conv_w, "g_go": g_go[None, :],
        "cq_gain": jnp.tile(gqa_q_gain, 4)[None, :], "ck_gain": jnp.tile(gqa_k_gain, 2)[None, :],
        "qa_gain": mla_qa_gain[None, :], "w_qb": wqb.astype(BF16), "kva_gain": mla_kva_gain[None, :],
        "w_kvb": jnp.concatenate([wk, wv], axis=1).astype(BF16),
        "dq_gain": jnp.tile(qslab, 4)[None, :], "dk_gain": jnp.tile(kslab, 4)[None, :],
        "dkr_gain": jnp.concatenate([mla_k_gain[64:], jnp.zeros((96,), F32)])[None, :],
        "b64": _ones_blocks([64] * 4), "bq": _ones_blocks([64, 32, 32] * 2),
        "invq": jnp.tile(jnp.concatenate([jnp.full((64,), 1 / 64.0, F32), jnp.full((64,), 1 / 32.0, F32)]),
                         2)[None, :],
        "w_out": w_out_p, "g_ffn": g_ffn_norm[None, :], "w_router": w_router, "b_router": b_router,
        "w_up": w_up.astype(BF16), "w_down": w_down.astype(BF16),
    }


def kernel(x, c, ctx, c_ctx, w_mod, b_mod, g_attn_norm, w_in, sgu_v_gain, sgu_w, sgu_b, conv_w, gqa_q_gain,
           gqa_k_gain, mla_qa_gain, mla_w_qb, mla_kva_gain, mla_w_kvb, mla_q_gain, mla_k_gain, g_group_out,
           w_out, g_ffn_norm, w_router_group, b_router_group, w_router_expert, b_router_expert,
           w_expert_up, w_expert_down):
    b, s, d = x.shape
    lc = ctx.shape[1]
    depth = w_mod.shape[0]
    params = (g_attn_norm, w_in, sgu_v_gain, sgu_w, sgu_b, conv_w, gqa_q_gain, gqa_k_gain, mla_qa_gain,
              mla_w_qb, mla_kva_gain, mla_w_kvb, mla_q_gain, mla_k_gain, g_group_out, w_out, g_ffn_norm,
              w_router_group, b_router_group, w_router_expert, b_router_expert, w_expert_up, w_expert_down)

    c_all = jnp.concatenate([c, c_ctx[None, :], jnp.zeros((16 - b - 1, d), F32)], axis=0)
    mod_all = _modulation(c_all, w_mod, b_mod).reshape(depth, 16, 6, d)
    rope = _rope_tables(s // GRID_W)

    tm = 512
    tq = 512
    tpb = s // tm
    lat_idx = lambda i: i // tpb
    ctx_idx = lambda i: b

    xl = x.reshape(b * s, d)
    xc = ctx.reshape(b * lc, d)
    for l in range(depth):
        lw = _layer_weights(l, params)
        mod = mod_all[l]
        update_ctx = l < depth - 1
        ya_c, yb_c, qc_c, kc_c, vc_c, qd_c, kd_c, vd_c = _in_call(
            xc, mod, lw, None, tm=tm, seq_len=lc, mod_index=ctx_idx)
        ya, yb, qc, kc, vc, qd, kd, vd = _in_call(xl, mod, lw, rope, tm=tm, seq_len=s, mod_index=lat_idx)
        r3 = lambda a, n: a.reshape(b, n, a.shape[-1])
        g_c = lw["g_go"][:, 512:768]
        g_d = lw["g_go"][:, 768:1024]
        yc = _attn_call(r3(qc, s), [r3(kc_c, lc), r3(kc, s)], [r3(vc_c, lc), r3(vc, s)], g_c, mode="C", tq=tq)
        yd = _attn_call(r3(qd, s), [r3(kd_c, lc), r3(kd, s)], [r3(vd_c, lc), r3(vd, s)], g_d, mode="D", tq=tq)
        xl, h2, cmb = _merge_call(xl, ya, yb, yc.reshape(b * s, GROUP_W), yd.reshape(b * s, GROUP_W),
                                  mod, lw, tm=tm, mod_index=lat_idx)
        xl = _moe_call(h2, cmb, xl, mod, lw, tm=1024, mod_index=lambda i: i // (s // 1024))
        if update_ctx:
            yc_c = _attn_call(r3(qc_c, lc), [r3(kc_c, lc)], [r3(vc_c, lc)], g_c, mode="C", tq=lc)
            yd_c = _attn_call(r3(qd_c, lc), [r3(kd_c, lc)], [r3(vd_c, lc)], g_d, mode="D", tq=lc)
            xc, h2c, cmbc = _merge_call(xc, ya_c, yb_c, yc_c.reshape(b * lc, GROUP_W),
                                        yd_c.reshape(b * lc, GROUP_W), mod, lw, tm=tm, mod_index=ctx_idx)
            xc = _moe_call(h2c, cmbc, xc, mod, lw, tm=1024, mod_index=ctx_idx)
    return xl.reshape(b, s, d)
```

```python
import functools

import numpy as np
import jax
import jax.numpy as jnp
from jax import lax
from jax.experimental import pallas as pl
from jax.experimental.pallas import tpu as pltpu

F32 = jnp.float32
BF16 = jnp.bfloat16

D_MODEL = 1024
GRID_W = 64
EPS = 1e-6
ROPE_THETA = 10000.0
GROUP_W = 256
CHUNK = 128
A_HEADS = 4
C_HEAD_DIM = 64
C_SCALE = C_HEAD_DIM ** -0.5
D_HEADS = 4
D_NOPE = 64
D_ROPE = 32
D_V = 64
D_Q_LORA = 256
D_KV_LORA = 128
D_SCALE = (D_NOPE + D_ROPE) ** -0.5
N_GROUPS = 4
EXPERTS_PER_GROUP = 4
N_EXPERTS = 16
D_EXPERT = 256
LANES = 128
HALO = 8

IN_W = 2304

ROUTE_SEL_LANE = 32
MOE_T = 512
MOE_CHUNK = 16
MOE_RLOC = 1280
MOE_NCH = MOE_RLOC // MOE_CHUNK
MOE_XW = D_MODEL + LANES
MOE_WORK = 256
MOE_CPW = MOE_WORK // MOE_CHUNK
VMEM_LIMIT = 56 * 1024 * 1024


def _nt_dot(a, b):
    return lax.dot_general(a, b, (((1,), (1,)), ((), ())), preferred_element_type=F32)


def _dot(a, b):
    return jnp.dot(a, b, preferred_element_type=F32)


def _rms_full(x, gain):
    return x * lax.rsqrt(jnp.mean(x * x, axis=-1, keepdims=True) + EPS) * gain


def _group_ms(x, ones_blocks, inv_count):
    return _dot((x * x).astype(BF16), ones_blocks) * inv_count


def _pair_swap(x, even):
    nxt = pltpu.roll(x, x.shape[1] - 1, axis=1)
    prv = pltpu.roll(x, 1, axis=1)
    return jnp.where(even, nxt, prv)


def _rope(x, cos, sin, even):
    return x * cos + _pair_swap(x, even) * sin


def _mod_kernel(c_ref, w_ref, b_ref, o_ref):
    cv = c_ref[...]
    sc = cv * jax.nn.sigmoid(cv)
    o_ref[0] = jnp.dot(sc, w_ref[0], preferred_element_type=F32,
                       precision=lax.Precision.HIGHEST) + b_ref[0]


def _modulation(c_all, w_mod, b_mod):
    depth = w_mod.shape[0]
    nblk = 4
    wblk = 6 * D_MODEL // nblk
    return pl.pallas_call(
        _mod_kernel,
        grid=(depth, nblk),
        in_specs=[pl.BlockSpec((16, D_MODEL), lambda l, j: (0, 0)),
                  pl.BlockSpec((1, D_MODEL, wblk), lambda l, j: (l, 0, j)),
                  pl.BlockSpec((1, 1, wblk), lambda l, j: (l, 0, j))],
        out_specs=pl.BlockSpec((1, 16, wblk), lambda l, j: (l, 0, j)),
        out_shape=jax.ShapeDtypeStruct((depth, 16, 6 * D_MODEL), F32),
        compiler_params=pltpu.CompilerParams(dimension_semantics=("arbitrary", "arbitrary"),
                                             vmem_limit_bytes=VMEM_LIMIT),
        name="mod",
    )(c_all, w_mod, b_mod.reshape(depth, 1, 6 * D_MODEL))


def _in_kernel(*refs, tm, seq_len, use_rope):
    (x_ref, xp_ref, xn_ref, mod_ref, gattn_ref, win_ref, sguw_ref, sgub_ref, vgain_ref, convw_ref,
     ggo_ref, cqg_ref, ckg_ref, qag_ref, wqb_ref, kvag_ref, wkvb_ref, dqg_ref, dkg_ref, dkrg_ref,
     b64_ref, bq_ref, invq_ref) = refs[:23]
    pos = 23
    if use_rope:
        cosc_ref, sinc_ref, cosq_ref, sinq_ref, cosk_ref, sink_ref = refs[pos:pos + 6]
        pos += 6
    ya_ref, yb_ref, qc_ref, kc_ref, vc_ref, qd_ref, kd_ref, vd_ref = refs[pos:pos + 8]

    i = pl.program_id(0)
    shift = mod_ref[0, 0:1, :]
    scale = mod_ref[0, 1:2, :]
    gattn = gattn_ref[...]

    def prenorm(xv):
        return (_rms_full(xv, gattn) * (1.0 + scale) + shift).astype(BF16)

    hb = prenorm(x_ref[...])
    b64 = b64_ref[...]
    inv64 = 1.0 / 64.0
    lane128 = lax.broadcasted_iota(jnp.int32, (tm, LANES), 1)
    even = (lane128 & 1) == 0

    u = jax.nn.gelu(_dot(hb, win_ref[:, 0:256]))
    v = jax.nn.gelu(_dot(hb, win_ref[:, 256:512]))
    v = (v * lax.rsqrt(_group_ms(v, b64, inv64) + EPS) * vgain_ref[...]).astype(BF16)
    lane_head = lax.broadcasted_iota(jnp.int32, (CHUNK, GROUP_W), 1) // 64
    sguw = sguw_ref[...]
    mixed_chunks = []
    for ci in range(tm // CHUNK):
        r = _dot(sguw, v[ci * CHUNK:(ci + 1) * CHUNK, :])
        m = r[0:CHUNK]
        for h in range(1, A_HEADS):
            m = jnp.where(lane_head == h, r[h * CHUNK:(h + 1) * CHUNK], m)
        mixed_chunks.append(m + sgub_ref[...])
    ya = u * jnp.concatenate(mixed_chunks, axis=0)
    ya_ref[...] = _rms_full(ya, ggo_ref[:, 0:256]).astype(BF16)

    bg = _dot(hb, win_ref[:, 512:768])
    z = _dot(hb, win_ref[:, 768:1024]) * _dot(hb, win_ref[:, 1024:1280])
    hh = prenorm(jnp.concatenate([xp_ref[...], xn_ref[...]], axis=0))
    zh = _dot(hh, win_ref[:, 768:1024]) * _dot(hh, win_ref[:, 1024:1280])
    row = lax.broadcasted_iota(jnp.int32, (tm, GROUP_W), 0)
    posn = (i * tm + row) % seq_len
    z_prev = jnp.where(row == 0, zh[HALO - 1:HALO, :], pltpu.roll(z, 1, axis=0))
    z_prev = jnp.where(posn == 0, 0.0, z_prev)
    z_next = jnp.where(row == tm - 1, zh[HALO:HALO + 1, :], pltpu.roll(z, tm - 1, axis=0))
    z_next = jnp.where(posn == seq_len - 1, 0.0, z_next)
    yb = bg * (convw_ref[0:1, :] * z_prev + convw_ref[1:2, :] * z + convw_ref[2:3, :] * z_next)
    yb_ref[...] = _rms_full(yb, ggo_ref[:, 256:512]).astype(BF16)

    q = _dot(hb, win_ref[:, 1280:1536])
    q = q * lax.rsqrt(_group_ms(q, b64, inv64) + EPS) * cqg_ref[...]
    k = _dot(hb, win_ref[:, 1536:1664])
    k = k * lax.rsqrt(_group_ms(k, b64[0:128, 0:128], inv64) + EPS) * ckg_ref[...]
    if use_rope:
        cosc, sinc = cosc_ref[...], sinc_ref[...]
        q = jnp.concatenate([_rope(q[:, 0:128], cosc, sinc, even),
                             _rope(q[:, 128:256], cosc, sinc, even)], axis=1)
        k = _rope(k, cosc, sinc, even)
    qc_ref[...] = (q * C_SCALE).astype(BF16)
    kc_ref[...] = k.astype(BF16)
    vc_ref[...] = _dot(hb, win_ref[:, 1664:1792]).astype(BF16)

    cq = _rms_full(_dot(hb, win_ref[:, 1792:2048]), qag_ref[...]).astype(BF16)
    qd = _dot(cq, wqb_ref[...])
    bq = bq_ref[...]
    invq = invq_ref[...]
    halves = []
    for hf in range(2):
        qh = qd[:, hf * 256:(hf + 1) * 256]
        qh = qh * lax.rsqrt(_group_ms(qh, bq, invq) + EPS) * dqg_ref[:, hf * 256:(hf + 1) * 256]
        if use_rope:
            cosq, sinq = cosq_ref[...], sinq_ref[...]
            qh = jnp.concatenate([_rope(qh[:, 0:128], cosq, sinq, even),
                                  _rope(qh[:, 128:256], cosq, sinq, even)], axis=1)
        halves.append(qh)
    qd_ref[...] = (jnp.concatenate(halves, axis=1) * D_SCALE).astype(BF16)

    ckv = _rms_full(_dot(hb, win_ref[:, 2048:2176]), kvag_ref[...]).astype(BF16)
    kv = _dot(ckv, wkvb_ref[...])
    vd_ref[...] = kv[:, 512:768].astype(BF16)
    kr = _dot(hb, win_ref[:, 2176:2304])
    kr = kr * lax.rsqrt(jnp.sum(kr * kr, axis=-1, keepdims=True) * (1.0 / D_ROPE) + EPS) * dkrg_ref[...]
    if use_rope:
        kr = _rope(kr, cosk_ref[...], sink_ref[...], even)
    kr = pltpu.roll(kr, D_NOPE, axis=1)
    for hf in range(2):
        kh = kv[:, hf * 256:(hf + 1) * 256]
        kh = kh * lax.rsqrt(_group_ms(kh, b64, inv64) + EPS) * dkg_ref[:, hf * 256:(hf + 1) * 256]
        kd_ref[:, hf * 256:hf * 256 + 128] = (kh[:, 0:128] + kr).astype(BF16)
        kd_ref[:, hf * 256 + 128:(hf + 1) * 256] = (kh[:, 128:256] + kr).astype(BF16)


def _in_call(x2d, mod, lw, rope, *, tm, seq_len, mod_index):
    n = x2d.shape[0]
    nt = n // tm
    nb8 = n // HALO
    tb = tm // HALO
    use_rope = rope is not None
    const2 = lambda i: (0, 0)
    row = lambda i: (i, 0)
    in_specs = [
        pl.BlockSpec((tm, D_MODEL), row),
        pl.BlockSpec((HALO, D_MODEL), lambda i: (jnp.maximum(i * tb - 1, 0), 0)),
        pl.BlockSpec((HALO, D_MODEL), lambda i: (jnp.minimum((i + 1) * tb, nb8 - 1), 0)),
        pl.BlockSpec((1, 6, D_MODEL), lambda i: (mod_index(i), 0, 0)),
    ]
    args = [x2d, x2d, x2d, mod]
    for name in ("g_attn", "w_in", "sgu_w", "sgu_b", "v_gain", "conv_w", "g_go", "cq_gain", "ck_gain",
                 "qa_gain", "w_qb", "kva_gain", "w_kvb", "dq_gain", "dk_gain", "dkr_gain",
                 "b64", "bq", "invq"):
        a = lw[name]
        in_specs.append(pl.BlockSpec(a.shape, const2))
        args.append(a)
    if use_rope:
        ntab = seq_len // tm
        for a in rope:
            in_specs.append(pl.BlockSpec((tm, LANES), lambda i: (i % ntab, 0)))
            args.append(a)
    widths = (256, 256, 256, 128, 128, 512, 512, 256)
    out_specs = [pl.BlockSpec((tm, w), row) for w in widths]
    out_shape = [jax.ShapeDtypeStruct((n, w), BF16) for w in widths]
    return pl.pallas_call(
        functools.partial(_in_kernel, tm=tm, seq_len=seq_len, use_rope=use_rope),
        grid=(nt,),
        in_specs=in_specs,
        out_specs=out_specs,
        out_shape=out_shape,
        compiler_params=pltpu.CompilerParams(dimension_semantics=("parallel",),
                                             vmem_limit_bytes=VMEM_LIMIT),
        name="in_proj",
    )(*args)


def _attn_kernel(*refs, mode, nseg, tq):
    q_ref = refs[0]
    k_refs = refs[1:1 + nseg]
    v_refs = refs[1 + nseg:1 + 2 * nseg]
    gain_ref = refs[1 + 2 * nseg]
    o_ref = refs[2 + 2 * nseg]

    def head(qm, ks, vs):
        ss = [_nt_dot(qm, kk) for kk in ks]
        m = ss[0].max(axis=-1, keepdims=True)
        for s in ss[1:]:
            m = jnp.maximum(m, s.max(axis=-1, keepdims=True))
        ps = [jnp.exp(s - m) for s in ss]
        l = ps[0].sum(axis=-1, keepdims=True)
        for p in ps[1:]:
            l = l + p.sum(axis=-1, keepdims=True)
        o = _dot(ps[0].astype(BF16), vs[0])
        for p, vv in zip(ps[1:], vs[1:]):
            o = o + _dot(p.astype(BF16), vv)
        return o / l

    if mode == "C":
        ks = [r[0] for r in k_refs]
        vs = [r[0] for r in v_refs]
        first_half = lax.broadcasted_iota(jnp.int32, (tq, LANES), 1) < 64
        slabs = []
        for sl in range(2):
            qs = q_ref[0, :, sl * 128:(sl + 1) * 128]
            o0 = head(jnp.where(first_half, qs, jnp.zeros_like(qs)), ks, vs)
            o1 = head(jnp.where(first_half, jnp.zeros_like(qs), qs), ks, vs)
            slabs.append(jnp.where(first_half, o0, o1))
        out = jnp.concatenate(slabs, axis=1)
    else:
        vs = [r[0] for r in v_refs]
        lane_head = lax.broadcasted_iota(jnp.int32, (tq, GROUP_W), 1) // D_V
        out = None
        for h in range(D_HEADS):
            ks = [r[0, :, h * 128:(h + 1) * 128] for r in k_refs]
            o = head(q_ref[0, :, h * 128:(h + 1) * 128], ks, vs)
            out = o if out is None else jnp.where(lane_head == h, o, out)
    o_ref[0] = _rms_full(out, gain_ref[...]).astype(BF16)


def _attn_call(q, ks, vs, gain, *, mode, tq):
    b, s, wq = q.shape
    nseg = len(ks)
    in_specs = [pl.BlockSpec((1, tq, wq), lambda bi, qi: (bi, qi, 0))]
    for a in list(ks) + list(vs):
        in_specs.append(pl.BlockSpec((1,) + a.shape[1:], lambda bi, qi: (bi, 0, 0)))
    in_specs.append(pl.BlockSpec((1, GROUP_W), lambda bi, qi: (0, 0)))
    return pl.pallas_call(
        functools.partial(_attn_kernel, mode=mode, nseg=nseg, tq=tq),
        grid=(b, s // tq),
        in_specs=in_specs,
        out_specs=pl.BlockSpec((1, tq, GROUP_W), lambda bi, qi: (bi, qi, 0)),
        out_shape=jax.ShapeDtypeStruct((b, s, GROUP_W), BF16),
        compiler_params=pltpu.CompilerParams(dimension_semantics=("parallel", "parallel"),
                                             vmem_limit_bytes=VMEM_LIMIT),
        name="attn_" + mode,
    )(q, *ks, *vs, gain)


def _merge_kernel(x_ref, ya_ref, yb_ref, yc_ref, yd_ref, mod_ref, wout_ref, gffn_ref, wr_ref, br_ref,
                  xo_ref, h2_ref, route_ref, cnt_ref, *, tm):
    y = _dot(ya_ref[...], wout_ref[0:256, :])
    y = y + _dot(yb_ref[...], wout_ref[256:512, :])
    y = y + _dot(yc_ref[...], wout_ref[512:768, :])
    y = y + _dot(yd_ref[...], wout_ref[768:1024, :])
    xn = x_ref[...] + mod_ref[0, 2:3, :] * y
    xo_ref[...] = xn
    h2 = _rms_full(xn, gffn_ref[...]) * (1.0 + mod_ref[0, 4:5, :]) + mod_ref[0, 3:4, :]
    h2b = h2.astype(BF16)
    h2_ref[...] = h2b

    h2lo = (h2 - h2b.astype(F32)).astype(BF16)
    part = _dot(h2b, wr_ref[...]) + _dot(h2lo, wr_ref[...])
    logits = part + pltpu.roll(part, LANES - ROUTE_SEL_LANE, axis=1) + br_ref[...]
    lane = lax.broadcasted_iota(jnp.int32, (tm, LANES), 1)
    lanef = lane.astype(F32)
    neg = jnp.float32(-1e30)
    big = jnp.float32(1e9)
    is_g = (lane >= N_EXPERTS) & (lane < N_EXPERTS + N_GROUPS)
    lg = jnp.where(is_g, logits, neg)
    gmax = lg.max(axis=-1, keepdims=True)
    g_idx = jnp.where(is_g & (lg == gmax), lanef, big).min(axis=-1, keepdims=True) - N_EXPERTS
    g_w = 1.0 / jnp.where(is_g, jnp.exp(lg - gmax), 0.0).sum(axis=-1, keepdims=True)
    in_grp = (lane < N_EXPERTS) & ((lane // EXPERTS_PER_GROUP).astype(F32) == g_idx)
    le = jnp.where(in_grp, logits, neg)
    m1 = le.max(axis=-1, keepdims=True)
    i1 = jnp.where(in_grp & (le == m1), lanef, big).min(axis=-1, keepdims=True)
    rest = in_grp & (lanef != i1)
    le2 = jnp.where(rest, logits, neg)
    m2 = le2.max(axis=-1, keepdims=True)
    i2 = jnp.where(rest & (le2 == m2), lanef, big).min(axis=-1, keepdims=True)
    r = jnp.exp(m2 - m1)
    w1 = 1.0 / (1.0 + r)
    w2 = r / (1.0 + r)
    cmb = g_w * jnp.where(lanef == i1, w1, jnp.where(lanef == i2, w2, 0.0))
    sel = pltpu.roll(jnp.where((lanef == i1) | (lanef == i2), 1.0, 0.0), ROUTE_SEL_LANE, axis=1)
    route_ref[...] = cmb + sel
    cnt_ref[0] = jnp.broadcast_to(sel.sum(axis=0, keepdims=True), (HALO, LANES))


def _merge_call(x2d, ya, yb, yc, yd, mod, lw, *, tm, mod_index):
    n = x2d.shape[0]
    row = lambda i: (i, 0)
    const2 = lambda i: (0, 0)
    in_specs = [pl.BlockSpec((tm, D_MODEL), row)]
    in_specs += [pl.BlockSpec((tm, GROUP_W), row)] * 4
    in_specs += [pl.BlockSpec((1, 6, D_MODEL), lambda i: (mod_index(i), 0, 0)),
                 pl.BlockSpec((D_MODEL, D_MODEL), const2),
                 pl.BlockSpec((1, D_MODEL), const2),
                 pl.BlockSpec((D_MODEL, LANES), const2),
                 pl.BlockSpec((1, LANES), const2)]
    return pl.pallas_call(
        functools.partial(_merge_kernel, tm=tm),
        grid=(n // tm,),
        in_specs=in_specs,
        out_specs=[pl.BlockSpec((tm, D_MODEL), row), pl.BlockSpec((tm, D_MODEL), row),
                   pl.BlockSpec((tm, LANES), row), pl.BlockSpec((1, HALO, LANES), lambda i: (i, 0, 0))],
        out_shape=[jax.ShapeDtypeStruct((n, D_MODEL), F32), jax.ShapeDtypeStruct((n, D_MODEL), BF16),
                   jax.ShapeDtypeStruct((n, LANES), F32),
                   jax.ShapeDtypeStruct((n // tm, HALO, LANES), F32)],
        compiler_params=pltpu.CompilerParams(dimension_semantics=("parallel",),
                                             vmem_limit_bytes=VMEM_LIMIT),
        name="merge",
    )(x2d, ya, yb, yc, yd, mod, lw["w_out"], lw["g_ffn"], lw["w_router"], lw["b_router"])


def _segment_positions(route, loc, tm):
    lane = lax.broadcasted_iota(jnp.int32, (tm, LANES), 1)
    is_sel = (lane >= ROUTE_SEL_LANE) & (lane < ROUTE_SEL_LANE + N_EXPERTS)
    selm = is_sel & (route > 0.5)
    selb = jnp.where(selm, 1.0, 0.0).astype(BF16)
    r_i = lax.broadcasted_iota(jnp.int32, (tm, tm), 0)
    c_i = lax.broadcasted_iota(jnp.int32, (tm, tm), 1)
    earlier = jnp.where(c_i < r_i, 1.0, 0.0).astype(BF16)
    dest = loc + _dot(earlier, selb)
    big = jnp.float32(1e9)
    d_a = jnp.where(selm, dest, big).min(axis=-1, keepdims=True)
    d_b = jnp.where(selm, dest, -big).max(axis=-1, keepdims=True)
    return selm, dest, d_a, d_b


def _to_row(col, tm):
    r_i = lax.broadcasted_iota(jnp.int32, (tm, tm), 0)
    c_i = lax.broadcasted_iota(jnp.int32, (tm, tm), 1)
    return jnp.where(r_i == c_i, col, 0.0).sum(axis=0, keepdims=True)


def _chunk_copy_out(xs_ref, xbuf_ref, sem_ref, slot, src_chunk, dst_chunk):
    return pltpu.make_async_copy(xs_ref.at[slot, pl.ds(src_chunk * MOE_CHUNK, MOE_CHUNK), :],
                                 xbuf_ref.at[pl.ds(dst_chunk * MOE_CHUNK, MOE_CHUNK), :],
                                 sem_ref.at[slot])


def _dispatch_kernel(nch_ref, dst_ref, npad_ref, pad_ref, nused_ref, h2_ref, route_ref, loc_ref, xbuf_ref,
                     xs_ref, sem_ref, *, tm, n_tiles, n_work):
    i = pl.program_id(0)
    slot = i % 2
    zero_chunk = MOE_RLOC // MOE_CHUNK

    def start_tile(t_idx, slot_):
        def body(j, carry):
            _chunk_copy_out(xs_ref, xbuf_ref, sem_ref, slot_, j, dst_ref[t_idx * MOE_NCH + j]).start()
            return carry
        lax.fori_loop(0, nch_ref[t_idx], body, 0)

    def wait_tile(t_idx, slot_):
        def body(j, carry):
            _chunk_copy_out(xs_ref, xbuf_ref, sem_ref, slot_, 0, 0).wait()
            return carry
        lax.fori_loop(0, nch_ref[t_idx], body, 0)

    @pl.when(i == 0)
    def _():
        xs_ref[2, :, :] = jnp.zeros((MOE_RLOC + MOE_CHUNK, MOE_XW), BF16)

        def pad_start(j, carry):
            _chunk_copy_out(xs_ref, xbuf_ref, sem_ref, 2, zero_chunk, pad_ref[j]).start()
            return carry

        def pad_wait(j, carry):
            _chunk_copy_out(xs_ref, xbuf_ref, sem_ref, 2, zero_chunk, 0).wait()
            return carry
        lax.fori_loop(0, npad_ref[0], pad_start, 0)
        lax.fori_loop(0, npad_ref[0], pad_wait, 0)

        def tail_copy(w):
            return pltpu.make_async_copy(xs_ref.at[2, pl.ds(0, MOE_WORK), :],
                                         xbuf_ref.at[pl.ds(w * MOE_WORK, MOE_WORK), :], sem_ref.at[2])

        def tail_start(w, carry):
            tail_copy(w).start()
            return carry

        def tail_wait(w, carry):
            tail_copy(w).wait()
            return carry
        lax.fori_loop(nused_ref[0], n_work, tail_start, 0)
        lax.fori_loop(nused_ref[0], n_work, tail_wait, 0)

    @pl.when(i >= 2)
    def _():
        wait_tile(i - 2, slot)

    route = route_ref[...]
    selm, dest, d_a, d_b = _segment_positions(route, loc_ref[0, 0:1, :], tm)
    cmb_al = pltpu.roll(route, ROUTE_SEL_LANE, axis=1)
    w_a = jnp.where(selm & (dest == d_a), cmb_al, 0.0).sum(axis=-1, keepdims=True)
    w_b = jnp.where(selm & (dest == d_b), cmb_al, 0.0).sum(axis=-1, keepdims=True)
    da_r, db_r, wa_r, wb_r = (_to_row(v, tm) for v in (d_a, d_b, w_a, w_b))
    rio = lax.broadcasted_iota(jnp.int32, (MOE_RLOC, tm), 0).astype(F32)
    hit_a = rio == da_r
    hit_b = rio == db_r
    perm = jnp.where(hit_a | hit_b, 1.0, 0.0).astype(BF16)
    xs_ref[slot, 0:MOE_RLOC, 0:D_MODEL] = _dot(perm, h2_ref[...]).astype(BF16)
    w_sorted = (jnp.where(hit_a, wa_r, 0.0) + jnp.where(hit_b, wb_r, 0.0)).sum(axis=-1, keepdims=True)
    w_hi = w_sorted.astype(BF16).astype(F32)
    r1 = w_sorted - w_hi
    w_mid = r1.astype(BF16).astype(F32)
    w_lo = r1 - w_mid
    lane = lax.broadcasted_iota(jnp.int32, (MOE_RLOC, LANES), 1)
    slab = jnp.where(lane == 0, w_hi, jnp.where(lane == 1, w_mid, jnp.where(lane == 2, w_lo, 0.0)))
    xs_ref[slot, 0:MOE_RLOC, D_MODEL:MOE_XW] = slab.astype(BF16)
    start_tile(i, slot)

    @pl.when(i == n_tiles - 1)
    def _():
        if n_tiles > 1:
            wait_tile(i - 1, 1 - slot)
        wait_tile(i, slot)


def _dispatch_call(h2, route, tabs, *, tm, n_rows):
    n = h2.shape[0]
    n_tiles = n // tm
    grid_spec = pltpu.PrefetchScalarGridSpec(
        num_scalar_prefetch=5,
        grid=(n_tiles,),
        in_specs=[pl.BlockSpec((tm, D_MODEL), lambda i, *_: (i, 0)),
                  pl.BlockSpec((tm, LANES), lambda i, *_: (i, 0)),
                  pl.BlockSpec((1, HALO, LANES), lambda i, *_: (i, 0, 0))],
        out_specs=pl.BlockSpec(memory_space=pl.ANY),
        scratch_shapes=[pltpu.VMEM((3, MOE_RLOC + MOE_CHUNK, MOE_XW), BF16),
                        pltpu.SemaphoreType.DMA((3,))],
    )
    return pl.pallas_call(
        functools.partial(_dispatch_kernel, tm=tm, n_tiles=n_tiles, n_work=n_rows // MOE_WORK),
        grid_spec=grid_spec,
        out_shape=jax.ShapeDtypeStruct((n_rows, MOE_XW), BF16),
        compiler_params=pltpu.CompilerParams(dimension_semantics=("arbitrary",),
                                             vmem_limit_bytes=VMEM_LIMIT),
        name="moe_dispatch",
    )(tabs["nch"], tabs["dst"], tabs["npad"], tabs["pad"], tabs["n_used"], h2, route, tabs["loc"])


def _expert_kernel(te_ref, nused_ref, x_ref, wup_ref, wdn_ref, y_ref):
    used = pl.program_id(0) < nused_ref[0]

    @pl.when(used)
    def _():
        wt = x_ref[:, D_MODEL:MOE_XW].astype(F32).sum(axis=-1, keepdims=True)
        gu = _dot(x_ref[:, 0:D_MODEL], wup_ref[0])
        g = gu[:, 0:D_EXPERT]
        a = (g * jax.nn.sigmoid(g)) * gu[:, D_EXPERT:2 * D_EXPERT] * wt
        y_ref[...] = _dot(a.astype(BF16), wdn_ref[0]).astype(BF16)

    @pl.when(jnp.logical_not(used))
    def _():
        y_ref[...] = jnp.zeros(y_ref.shape, BF16)


def _expert_call(xbuf, tabs, lw):
    n_rows = xbuf.shape[0]
    n_work = n_rows // MOE_WORK
    grid_spec = pltpu.PrefetchScalarGridSpec(
        num_scalar_prefetch=2,
        grid=(n_work,),
        in_specs=[pl.BlockSpec((MOE_WORK, MOE_XW), lambda w, te, nu: (jnp.minimum(w, nu[0] - 1), 0)),
                  pl.BlockSpec((1, D_MODEL, 2 * D_EXPERT), lambda w, te, nu: (te[w], 0, 0)),
                  pl.BlockSpec((1, D_EXPERT, D_MODEL), lambda w, te, nu: (te[w], 0, 0))],
        out_specs=pl.BlockSpec((MOE_WORK, D_MODEL), lambda w, te, nu: (w, 0)),
    )
    return pl.pallas_call(
        _expert_kernel,
        grid_spec=grid_spec,
        out_shape=jax.ShapeDtypeStruct((n_rows, D_MODEL), BF16),
        compiler_params=pltpu.CompilerParams(dimension_semantics=("arbitrary",),
                                             vmem_limit_bytes=VMEM_LIMIT),
        name="moe_experts",
    )(tabs["tile_expert"], tabs["n_used"], xbuf, lw["w_up"], lw["w_down"])


def _chunk_copy_in(ybuf_ref, ys_ref, sem_ref, slot, src_chunk, dst_chunk):
    return pltpu.make_async_copy(ybuf_ref.at[pl.ds(src_chunk * MOE_CHUNK, MOE_CHUNK), :],
                                 ys_ref.at[slot, pl.ds(dst_chunk * MOE_CHUNK, MOE_CHUNK), :],
                                 sem_ref.at[slot])


def _combine_kernel(nch_ref, dst_ref, route_ref, loc_ref, x_ref, mod_ref, ybuf_ref, o_ref,
                    ys_ref, sem_ref, *, tm, n_tiles):
    i = pl.program_id(0)
    slot = i % 2

    def start_tile(t_idx, slot_):
        def body(j, carry):
            _chunk_copy_in(ybuf_ref, ys_ref, sem_ref, slot_, dst_ref[t_idx * MOE_NCH + j], j).start()
            return carry
        lax.fori_loop(0, nch_ref[t_idx], body, 0)

    @pl.when(i == 0)
    def _():
        ys_ref[...] = jnp.zeros(ys_ref.shape, BF16)
        start_tile(0, 0)

    @pl.when(i + 1 < n_tiles)
    def _():
        start_tile(i + 1, 1 - slot)

    def wait_body(j, carry):
        _chunk_copy_in(ybuf_ref, ys_ref, sem_ref, slot, 0, 0).wait()
        return carry
    lax.fori_loop(0, nch_ref[i], wait_body, 0)

    _, _, d_a, d_b = _segment_positions(route_ref[...], loc_ref[0, 0:1, :], tm)
    lio = lax.broadcasted_iota(jnp.int32, (tm, MOE_RLOC), 1).astype(F32)
    perm_t = jnp.where((lio == d_a) | (lio == d_b), 1.0, 0.0).astype(BF16)
    y = _dot(perm_t, ys_ref[slot])
    o_ref[...] = x_ref[...] + mod_ref[0, 5:6, :] * y


def _combine_call(ybuf, route, x2d, mod, tabs, *, tm, mod_index):
    n = x2d.shape[0]
    n_tiles = n // tm
    grid_spec = pltpu.PrefetchScalarGridSpec(
        num_scalar_prefetch=2,
        grid=(n_tiles,),
        in_specs=[pl.BlockSpec((tm, LANES), lambda i, *_: (i, 0)),
                  pl.BlockSpec((1, HALO, LANES), lambda i, *_: (i, 0, 0)),
                  pl.BlockSpec((tm, D_MODEL), lambda i, *_: (i, 0)),
                  pl.BlockSpec((1, 6, D_MODEL), lambda i, *_: (mod_index(i), 0, 0)),
                  pl.BlockSpec(memory_space=pl.ANY)],
        out_specs=pl.BlockSpec((tm, D_MODEL), lambda i, *_: (i, 0)),
        scratch_shapes=[pltpu.VMEM((2, MOE_RLOC, D_MODEL), BF16),
                        pltpu.SemaphoreType.DMA((2,))],
    )
    return pl.pallas_call(
        functools.partial(_combine_kernel, tm=tm, n_tiles=n_tiles),
        grid_spec=grid_spec,
        out_shape=jax.ShapeDtypeStruct((n, D_MODEL), F32),
        compiler_params=pltpu.CompilerParams(dimension_semantics=("arbitrary",),
                                             vmem_limit_bytes=VMEM_LIMIT),
        name="moe_combine",
    )(tabs["nch"], tabs["dst"], route, tabs["loc"], x2d, mod, ybuf)


def _moe_rows(n_tok):
    n_tiles = n_tok // MOE_T
    chunks = 2 * n_tok // MOE_CHUNK + (MOE_CHUNK - 1) * n_tiles + N_EXPERTS * (MOE_CPW - 1)
    return -(-chunks // MOE_CPW) * MOE_WORK


def _route_tables(counts, n_tok):
    n_work = _moe_rows(n_tok) // MOE_WORK
    cnt = counts[:, 0, ROUTE_SEL_LANE:ROUTE_SEL_LANE + N_EXPERTS].astype(jnp.int32)
    nchunk = (cnt + MOE_CHUNK - 1) // MOE_CHUNK
    ends = jnp.cumsum(nchunk, axis=1)
    loc = ends - nchunk
    nch = ends[:, -1]
    per_e = nchunk.sum(axis=0)
    padded_e = (per_e + MOE_CPW - 1) // MOE_CPW * MOE_CPW
    e_end = jnp.cumsum(padded_e)
    e_base = e_end - padded_e
    glob = e_base[None, :] + jnp.cumsum(nchunk, axis=0) - nchunk
    j = jnp.arange(MOE_NCH, dtype=jnp.int32)
    e_of_j = jnp.minimum((ends[:, None, :] <= j[None, :, None]).sum(axis=-1), N_EXPERTS - 1)
    dst = jnp.take_along_axis(glob, e_of_j, axis=1) + j[None, :] - jnp.take_along_axis(loc, e_of_j, axis=1)
    dst = jnp.where(j[None, :] < nch[:, None], dst, 0)
    k = jnp.arange(MOE_CPW - 1, dtype=jnp.int32)
    pad_valid = (k[None, :] < (padded_e - per_e)[:, None]).reshape(-1)
    pad_dst = ((e_base + per_e)[:, None] + k[None, :]).reshape(-1)
    order = jnp.argsort(jnp.logical_not(pad_valid), stable=True)
    n_used = e_end[-1] // MOE_CPW
    w = jnp.minimum(jnp.arange(n_work, dtype=jnp.int32), n_used - 1)
    tile_expert = jnp.minimum(((e_end // MOE_CPW)[None, :] <= w[:, None]).sum(axis=-1), N_EXPERTS - 1)
    loc_rows = jnp.zeros((cnt.shape[0], HALO, LANES), F32).at[:, :, ROUTE_SEL_LANE:ROUTE_SEL_LANE + N_EXPERTS].set(
        (loc * MOE_CHUNK).astype(F32)[:, None, :])
    return {"nch": nch.astype(jnp.int32), "dst": dst.reshape(-1).astype(jnp.int32),
            "npad": pad_valid.sum().astype(jnp.int32).reshape(1), "pad": pad_dst[order].astype(jnp.int32),
            "tile_expert": tile_expert.astype(jnp.int32), "n_used": n_used.astype(jnp.int32).reshape(1),
            "loc": loc_rows}


def _moe_call(h2, route, counts, x2d, mod, lw, *, mod_index):
    n = x2d.shape[0]
    tabs = _route_tables(counts, n)
    xbuf = _dispatch_call(h2, route, tabs, tm=MOE_T, n_rows=_moe_rows(n))
    ybuf = _expert_call(xbuf, tabs, lw)
    return _combine_call(ybuf, route, x2d, mod, tabs, tm=MOE_T, mod_index=mod_index)


def _ones_blocks(sizes):
    n = sum(sizes)
    m = np.zeros((n, n), np.float32)
    o = 0
    for s in sizes:
        m[o:o + s, o:o + s] = 1.0
        o += s
    return jnp.asarray(m, BF16)


def _rope_tables(rows):
    r = jnp.repeat(jnp.arange(rows, dtype=F32), GRID_W)
    c = jnp.tile(jnp.arange(GRID_W, dtype=F32), rows)

    def tab(rot_dim):
        nf = rot_dim // 4
        inv = ROPE_THETA ** (-jnp.arange(nf, dtype=F32) / nf)
        ang = jnp.concatenate([r[:, None] * inv, c[:, None] * inv], axis=-1)
        cos = jnp.repeat(jnp.cos(ang), 2, axis=-1)
        sin = jnp.repeat(jnp.sin(ang), 2, axis=-1) * jnp.tile(jnp.asarray([-1.0, 1.0], F32), rot_dim // 2)
        return cos, sin

    s = rows * GRID_W
    cc, sc = tab(C_HEAD_DIM)
    cd, sd = tab(D_ROPE)
    one = lambda w: jnp.ones((s, w), F32)
    zero = lambda w: jnp.zeros((s, w), F32)
    return (jnp.tile(cc, (1, 2)), jnp.tile(sc, (1, 2)),
            jnp.concatenate([one(64), cd, one(32)], axis=1), jnp.concatenate([zero(64), sd, zero(32)], axis=1),
            jnp.concatenate([cd, one(96)], axis=1), jnp.concatenate([sd, zero(96)], axis=1))


def _layer_weights(l, p):
    (g_attn_norm, w_in, sgu_v_gain, sgu_w, sgu_b, conv_w, gqa_q_gain, gqa_k_gain, mla_qa_gain, mla_w_qb,
     mla_kva_gain, mla_w_kvb, mla_q_gain, mla_k_gain, g_group_out, w_out, g_ffn_norm,
     w_rg, b_rg, w_re, b_re, w_up, w_down) = [a[l] for a in p]
    perm_c = np.concatenate([np.arange(0, 64), np.arange(128, 192), np.arange(64, 128), np.arange(192, 256)])
    wi = w_in
    w_in_p = jnp.concatenate([wi[:, :1280], wi[:, 1280:1536][:, perm_c], wi[:, 1536:2208],
                              jnp.zeros((D_MODEL, IN_W - 2208), F32)], axis=1).astype(BF16)
    z32 = jnp.zeros((D_Q_LORA, 32), F32)
    wqb = jnp.concatenate(
        [jnp.concatenate([mla_w_qb[:, h * 96:h * 96 + 96], z32], axis=1) for h in range(D_HEADS)], axis=1)
    z64 = jnp.zeros((D_KV_LORA, 64), F32)
    wk = jnp.concatenate(
        [jnp.concatenate([mla_w_kvb[:, h * 128:h * 128 + 64], z64], axis=1) for h in range(D_HEADS)], axis=1)
    wv = jnp.concatenate([mla_w_kvb[:, h * 128 + 64:h * 128 + 128] for h in range(D_HEADS)], axis=1)
    qslab = jnp.concatenate([mla_q_gain, jnp.zeros((32,), F32)])
    kslab = jnp.concatenate([mla_k_gain[:64], jnp.zeros((64,), F32)])
    g_go = jnp.concatenate([g_group_out[:512], g_group_out[512:768][perm_c], g_group_out[768:]])
    w_out_p = jnp.concatenate([w_out[:512], w_out[512:768][perm_c], w_out[768:]], axis=0).astype(BF16)
    w_r = jnp.concatenate([w_re, w_rg], axis=1)
    w_r_hi = w_r.astype(BF16)
    w_r_lo = (w_r - w_r_hi.astype(F32)).astype(BF16)
    n_r = N_EXPERTS + N_GROUPS
    w_router = jnp.concatenate([w_r_hi, jnp.zeros((D_MODEL, ROUTE_SEL_LANE - n_r), BF16), w_r_lo,
                                jnp.zeros((D_MODEL, LANES - ROUTE_SEL_LANE - n_r), BF16)], axis=1)
    b_router = jnp.concatenate([b_re, b_rg, jnp.zeros((LANES - 20,), F32)])[None, :]
    sgub_full = jnp.repeat(sgu_b.T, 64, axis=1)
    return {
        "g_attn": g_attn_norm[None, :], "w_in": w_in_p,
        "sgu_w": sgu_w.reshape(A_HEADS * CHUNK, CHUNK).astype(BF16), "sgu_b": sgub_full,
        "v_gain": sgu_v_gain[None, :], "conv_w": conv_w, "g_go": g_go[None, :],
        "cq_gain": jnp.tile(gqa_q_gain, 4)[None, :], "ck_gain": jnp.tile(gqa_k_gain, 2)[None, :],
        "qa_gain": mla_qa_gain[None, :], "w_qb": wqb.astype(BF16), "kva_gain": mla_kva_gain[None, :],
        "w_kvb": jnp.concatenate([wk, wv], axis=1).astype(BF16),
        "dq_gain": jnp.tile(qslab, 4)[None, :], "dk_gain": jnp.tile(kslab, 4)[None, :],
        "dkr_gain": jnp.concatenate([mla_k_gain[64:], jnp.zeros((96,), F32)])[None, :],
        "b64": _ones_blocks([64] * 4), "bq": _ones_blocks([64, 32, 32] * 2),
        "invq": jnp.tile(jnp.concatenate([jnp.full((64,), 1 / 64.0, F32), jnp.full((64,), 1 / 32.0, F32)]),
                         2)[None, :],
        "w_out": w_out_p, "g_ffn": g_ffn_norm[None, :], "w_router": w_router, "b_router": b_router,
        "w_up": w_up.astype(BF16), "w_down": w_down.astype(BF16),
    }


def kernel(x, c, ctx, c_ctx, w_mod, b_mod, g_attn_norm, w_in, sgu_v_gain, sgu_w, sgu_b, conv_w, gqa_q_gain,
           gqa_k_gain, mla_qa_gain, mla_w_qb, mla_kva_gain, mla_w_kvb, mla_q_gain, mla_k_gain, g_group_out,
           w_out, g_ffn_norm, w_router_group, b_router_group, w_router_expert, b_router_expert,
           w_expert_up, w_expert_down):
    b, s, d = x.shape
    lc = ctx.shape[1]
    depth = w_mod.shape[0]
    params = (g_attn_norm, w_in, sgu_v_gain, sgu_w, sgu_b, conv_w, gqa_q_gain, gqa_k_gain, mla_qa_gain,
              mla_w_qb, mla_kva_gain, mla_w_kvb, mla_q_gain, mla_k_gain, g_group_out, w_out, g_ffn_norm,
              w_router_group, b_router_group, w_router_expert, b_router_expert, w_expert_up, w_expert_down)

    c_all = jnp.concatenate([c, c_ctx[None, :], jnp.zeros((16 - b - 1, d), F32)], axis=0)
    mod_all = _modulation(c_all, w_mod, b_mod).reshape(depth, 16, 6, d)
    rope = _rope_tables(s // GRID_W)

    tm = 512
    tq = 512
    tpb = s // tm
    lat_idx = lambda i: i // tpb
    ctx_idx = lambda i: b

    xl = x.reshape(b * s, d)
    xc = ctx.reshape(b * lc, d)
    for l in range(depth):
        lw = _layer_weights(l, params)
        mod = mod_all[l]
        update_ctx = l < depth - 1
        ya_c, yb_c, qc_c, kc_c, vc_c, qd_c, kd_c, vd_c = _in_call(
            xc, mod, lw, None, tm=tm, seq_len=lc, mod_index=ctx_idx)
        ya, yb, qc, kc, vc, qd, kd, vd = _in_call(xl, mod, lw, rope, tm=tm, seq_len=s, mod_index=lat_idx)
        r3 = lambda a, n: a.reshape(b, n, a.shape[-1])
        g_c = lw["g_go"][:, 512:768]
        g_d = lw["g_go"][:, 768:1024]
        yc = _attn_call(r3(qc, s), [r3(kc_c, lc), r3(kc, s)], [r3(vc_c, lc), r3(vc, s)], g_c, mode="C", tq=tq)
        yd = _attn_call(r3(qd, s), [r3(kd_c, lc), r3(kd, s)], [r3(vd_c, lc), r3(vd, s)], g_d, mode="D", tq=tq)
        xl, h2, route, counts = _merge_call(xl, ya, yb, yc.reshape(b * s, GROUP_W), yd.reshape(b * s, GROUP_W),
                                            mod, lw, tm=tm, mod_index=lat_idx)
        xl = _moe_call(h2, route, counts, xl, mod, lw, mod_index=lat_idx)
        if update_ctx:
            yc_c = _attn_call(r3(qc_c, lc), [r3(kc_c, lc)], [r3(vc_c, lc)], g_c, mode="C", tq=lc)
            yd_c = _attn_call(r3(qd_c, lc), [r3(kd_c, lc)], [r3(vd_c, lc)], g_d, mode="D", tq=lc)
            xc, h2c, route_c, counts_c = _merge_call(xc, ya_c, yb_c, yc_c.reshape(b * lc, GROUP_W),
                                                     yd_c.reshape(b * lc, GROUP_W), mod, lw, tm=tm,
                                                     mod_index=ctx_idx)
            xc = _moe_call(h2c, route_c, counts_c, xc, mod, lw, mod_index=ctx_idx)
    return xl.reshape(b, s, d)
```

```python
import functools

import numpy as np
import jax
import jax.numpy as jnp
from jax import lax
from jax.experimental import pallas as pl
from jax.experimental.pallas import tpu as pltpu

F32 = jnp.float32
BF16 = jnp.bfloat16

D_MODEL = 1024
GRID_W = 64
EPS = 1e-6
ROPE_THETA = 10000.0
GROUP_W = 256
CHUNK = 128
A_HEADS = 4
C_HEAD_DIM = 64
C_SCALE = C_HEAD_DIM ** -0.5
D_HEADS = 4
D_NOPE = 64
D_ROPE = 32
D_V = 64
D_Q_LORA = 256
D_KV_LORA = 128
D_SCALE = (D_NOPE + D_ROPE) ** -0.5
N_GROUPS = 4
EXPERTS_PER_GROUP = 4
N_EXPERTS = 16
D_EXPERT = 256
LANES = 128
HALO = 8

IN_W = 2304

ROUTE_SEL_LANE = 32
MOE_T = 512
MOE_CHUNK = 16
MOE_RLOC = 1280
MOE_NCH = MOE_RLOC // MOE_CHUNK
MOE_XW = D_MODEL + LANES
MOE_WORK = 512
MOE_CPW = MOE_WORK // MOE_CHUNK
VMEM_LIMIT = 56 * 1024 * 1024


def _nt_dot(a, b):
    return lax.dot_general(a, b, (((1,), (1,)), ((), ())), preferred_element_type=F32)


def _dot(a, b):
    return jnp.dot(a, b, preferred_element_type=F32)


def _rms_full(x, gain):
    return x * lax.rsqrt(jnp.mean(x * x, axis=-1, keepdims=True) + EPS) * gain


def _group_ms(x, ones_blocks, inv_count):
    return _dot((x * x).astype(BF16), ones_blocks) * inv_count


def _pair_swap(x, even):
    nxt = pltpu.roll(x, x.shape[1] - 1, axis=1)
    prv = pltpu.roll(x, 1, axis=1)
    return jnp.where(even, nxt, prv)


def _rope(x, cos, sin, even):
    return x * cos + _pair_swap(x, even) * sin


def _mod_kernel(c_ref, w_ref, b_ref, o_ref):
    cv = c_ref[...]
    sc = cv * jax.nn.sigmoid(cv)
    o_ref[0] = jnp.dot(sc, w_ref[0], preferred_element_type=F32,
                       precision=lax.Precision.HIGHEST) + b_ref[0]


def _modulation(c_all, w_mod, b_mod):
    depth = w_mod.shape[0]
    nblk = 4
    wblk = 6 * D_MODEL // nblk
    return pl.pallas_call(
        _mod_kernel,
        grid=(depth, nblk),
        in_specs=[pl.BlockSpec((16, D_MODEL), lambda l, j: (0, 0)),
                  pl.BlockSpec((1, D_MODEL, wblk), lambda l, j: (l, 0, j)),
                  pl.BlockSpec((1, 1, wblk), lambda l, j: (l, 0, j))],
        out_specs=pl.BlockSpec((1, 16, wblk), lambda l, j: (l, 0, j)),
        out_shape=jax.ShapeDtypeStruct((depth, 16, 6 * D_MODEL), F32),
        compiler_params=pltpu.CompilerParams(dimension_semantics=("arbitrary", "arbitrary"),
                                             vmem_limit_bytes=VMEM_LIMIT),
        name="mod",
    )(c_all, w_mod, b_mod.reshape(depth, 1, 6 * D_MODEL))


def _in_kernel(*refs, tm, seq_len, use_rope):
    (x_ref, xp_ref, xn_ref, mod_ref, gattn_ref, win_ref, sguw_ref, sgub_ref, vgain_ref, convw_ref,
     ggo_ref, cqg_ref, ckg_ref, qag_ref, wqb_ref, kvag_ref, wkvb_ref, dqg_ref, dkg_ref, dkrg_ref,
     b64_ref, bq_ref, invq_ref) = refs[:23]
    pos = 23
    if use_rope:
        cosc_ref, sinc_ref, cosq_ref, sinq_ref, cosk_ref, sink_ref = refs[pos:pos + 6]
        pos += 6
    ya_ref, yb_ref, qc_ref, kc_ref, vc_ref, qd_ref, kd_ref, vd_ref = refs[pos:pos + 8]

    i = pl.program_id(0)
    shift = mod_ref[0, 0:1, :]
    scale = mod_ref[0, 1:2, :]
    gattn = gattn_ref[...]

    def prenorm(xv):
        return (_rms_full(xv, gattn) * (1.0 + scale) + shift).astype(BF16)

    hb = prenorm(x_ref[...])
    b64 = b64_ref[...]
    inv64 = 1.0 / 64.0
    lane128 = lax.broadcasted_iota(jnp.int32, (tm, LANES), 1)
    even = (lane128 & 1) == 0

    u = jax.nn.gelu(_dot(hb, win_ref[:, 0:256]))
    v = jax.nn.gelu(_dot(hb, win_ref[:, 256:512]))
    v = (v * lax.rsqrt(_group_ms(v, b64, inv64) + EPS) * vgain_ref[...]).astype(BF16)
    lane_head = lax.broadcasted_iota(jnp.int32, (CHUNK, GROUP_W), 1) // 64
    sguw = sguw_ref[...]
    mixed_chunks = []
    for ci in range(tm // CHUNK):
        r = _dot(sguw, v[ci * CHUNK:(ci + 1) * CHUNK, :])
        m = r[0:CHUNK]
        for h in range(1, A_HEADS):
            m = jnp.where(lane_head == h, r[h * CHUNK:(h + 1) * CHUNK], m)
        mixed_chunks.append(m + sgub_ref[...])
    ya = u * jnp.concatenate(mixed_chunks, axis=0)
    ya_ref[...] = _rms_full(ya, ggo_ref[:, 0:256]).astype(BF16)

    bg = _dot(hb, win_ref[:, 512:768])
    z = _dot(hb, win_ref[:, 768:1024]) * _dot(hb, win_ref[:, 1024:1280])
    hh = prenorm(jnp.concatenate([xp_ref[...], xn_ref[...]], axis=0))
    zh = _dot(hh, win_ref[:, 768:1024]) * _dot(hh, win_ref[:, 1024:1280])
    row = lax.broadcasted_iota(jnp.int32, (tm, GROUP_W), 0)
    posn = (i * tm + row) % seq_len
    z_prev = jnp.where(row == 0, zh[HALO - 1:HALO, :], pltpu.roll(z, 1, axis=0))
    z_prev = jnp.where(posn == 0, 0.0, z_prev)
    z_next = jnp.where(row == tm - 1, zh[HALO:HALO + 1, :], pltpu.roll(z, tm - 1, axis=0))
    z_next = jnp.where(posn == seq_len - 1, 0.0, z_next)
    yb = bg * (convw_ref[0:1, :] * z_prev + convw_ref[1:2, :] * z + convw_ref[2:3, :] * z_next)
    yb_ref[...] = _rms_full(yb, ggo_ref[:, 256:512]).astype(BF16)

    q = _dot(hb, win_ref[:, 1280:1536])
    q = q * lax.rsqrt(_group_ms(q, b64, inv64) + EPS) * cqg_ref[...]
    k = _dot(hb, win_ref[:, 1536:1664])
    k = k * lax.rsqrt(_group_ms(k, b64[0:128, 0:128], inv64) + EPS) * ckg_ref[...]
    if use_rope:
        cosc, sinc = cosc_ref[...], sinc_ref[...]
        q = jnp.concatenate([_rope(q[:, 0:128], cosc, sinc, even),
                             _rope(q[:, 128:256], cosc, sinc, even)], axis=1)
        k = _rope(k, cosc, sinc, even)
    qc_ref[...] = (q * C_SCALE).astype(BF16)
    kc_ref[...] = k.astype(BF16)
    vc_ref[...] = _dot(hb, win_ref[:, 1664:1792]).astype(BF16)

    cq = _rms_full(_dot(hb, win_ref[:, 1792:2048]), qag_ref[...]).astype(BF16)
    qd = _dot(cq, wqb_ref[...])
    bq = bq_ref[...]
    invq = invq_ref[...]
    halves = []
    for hf in range(2):
        qh = qd[:, hf * 256:(hf + 1) * 256]
        qh = qh * lax.rsqrt(_group_ms(qh, bq, invq) + EPS) * dqg_ref[:, hf * 256:(hf + 1) * 256]
        if use_rope:
            cosq, sinq = cosq_ref[...], sinq_ref[...]
            qh = jnp.concatenate([_rope(qh[:, 0:128], cosq, sinq, even),
                                  _rope(qh[:, 128:256], cosq, sinq, even)], axis=1)
        halves.append(qh)
    qd_ref[...] = (jnp.concatenate(halves, axis=1) * D_SCALE).astype(BF16)

    ckv = _rms_full(_dot(hb, win_ref[:, 2048:2176]), kvag_ref[...]).astype(BF16)
    kv = _dot(ckv, wkvb_ref[...])
    vd_ref[...] = kv[:, 512:768].astype(BF16)
    kr = _dot(hb, win_ref[:, 2176:2304])
    kr = kr * lax.rsqrt(jnp.sum(kr * kr, axis=-1, keepdims=True) * (1.0 / D_ROPE) + EPS) * dkrg_ref[...]
    if use_rope:
        kr = _rope(kr, cosk_ref[...], sink_ref[...], even)
    kr = pltpu.roll(kr, D_NOPE, axis=1)
    for hf in range(2):
        kh = kv[:, hf * 256:(hf + 1) * 256]
        kh = kh * lax.rsqrt(_group_ms(kh, b64, inv64) + EPS) * dkg_ref[:, hf * 256:(hf + 1) * 256]
        kd_ref[:, hf * 256:hf * 256 + 128] = (kh[:, 0:128] + kr).astype(BF16)
        kd_ref[:, hf * 256 + 128:(hf + 1) * 256] = (kh[:, 128:256] + kr).astype(BF16)


def _in_call(x2d, mod, lw, rope, *, tm, seq_len, mod_index):
    n = x2d.shape[0]
    nt = n // tm
    nb8 = n // HALO
    tb = tm // HALO
    use_rope = rope is not None
    const2 = lambda i: (0, 0)
    row = lambda i: (i, 0)
    in_specs = [
        pl.BlockSpec((tm, D_MODEL), row),
        pl.BlockSpec((HALO, D_MODEL), lambda i: (jnp.maximum(i * tb - 1, 0), 0)),
        pl.BlockSpec((HALO, D_MODEL), lambda i: (jnp.minimum((i + 1) * tb, nb8 - 1), 0)),
        pl.BlockSpec((1, 6, D_MODEL), lambda i: (mod_index(i), 0, 0)),
    ]
    args = [x2d, x2d, x2d, mod]
    for name in ("g_attn", "w_in", "sgu_w", "sgu_b", "v_gain", "conv_w", "g_go", "cq_gain", "ck_gain",
                 "qa_gain", "w_qb", "kva_gain", "w_kvb", "dq_gain", "dk_gain", "dkr_gain",
                 "b64", "bq", "invq"):
        a = lw[name]
        in_specs.append(pl.BlockSpec(a.shape, const2))
        args.append(a)
    if use_rope:
        ntab = seq_len // tm
        for a in rope:
            in_specs.append(pl.BlockSpec((tm, LANES), lambda i: (i % ntab, 0)))
            args.append(a)
    widths = (256, 256, 256, 128, 128, 512, 512, 256)
    out_specs = [pl.BlockSpec((tm, w), row) for w in widths]
    out_shape = [jax.ShapeDtypeStruct((n, w), BF16) for w in widths]
    return pl.pallas_call(
        functools.partial(_in_kernel, tm=tm, seq_len=seq_len, use_rope=use_rope),
        grid=(nt,),
        in_specs=in_specs,
        out_specs=out_specs,
        out_shape=out_shape,
        compiler_params=pltpu.CompilerParams(dimension_semantics=("parallel",),
                                             vmem_limit_bytes=VMEM_LIMIT),
        name="in_proj",
    )(*args)


def _attn_kernel(*refs, mode, nseg, tq):
    q_ref = refs[0]
    k_refs = refs[1:1 + nseg]
    v_refs = refs[1 + nseg:1 + 2 * nseg]
    gain_ref = refs[1 + 2 * nseg]
    o_ref = refs[2 + 2 * nseg]

    def head(qm, ks, vs):
        ss = [_nt_dot(qm, kk) for kk in ks]
        m = ss[0].max(axis=-1, keepdims=True)
        for s in ss[1:]:
            m = jnp.maximum(m, s.max(axis=-1, keepdims=True))
        ps = [jnp.exp(s - m) for s in ss]
        l = ps[0].sum(axis=-1, keepdims=True)
        for p in ps[1:]:
            l = l + p.sum(axis=-1, keepdims=True)
        o = _dot(ps[0].astype(BF16), vs[0])
        for p, vv in zip(ps[1:], vs[1:]):
            o = o + _dot(p.astype(BF16), vv)
        return o / l

    if mode == "C":
        ks = [r[0] for r in k_refs]
        vs = [r[0] for r in v_refs]
        first_half = lax.broadcasted_iota(jnp.int32, (tq, LANES), 1) < 64
        slabs = []
        for sl in range(2):
            qs = q_ref[0, :, sl * 128:(sl + 1) * 128]
            o0 = head(jnp.where(first_half, qs, jnp.zeros_like(qs)), ks, vs)
            o1 = head(jnp.where(first_half, jnp.zeros_like(qs), qs), ks, vs)
            slabs.append(jnp.where(first_half, o0, o1))
        out = jnp.concatenate(slabs, axis=1)
    else:
        vs = [r[0] for r in v_refs]
        lane_head = lax.broadcasted_iota(jnp.int32, (tq, GROUP_W), 1) // D_V
        out = None
        for h in range(D_HEADS):
            ks = [r[0, :, h * 128:(h + 1) * 128] for r in k_refs]
            o = head(q_ref[0, :, h * 128:(h + 1) * 128], ks, vs)
            out = o if out is None else jnp.where(lane_head == h, o, out)
    o_ref[0] = _rms_full(out, gain_ref[...]).astype(BF16)


def _attn_call(q, ks, vs, gain, *, mode, tq):
    b, s, wq = q.shape
    nseg = len(ks)
    in_specs = [pl.BlockSpec((1, tq, wq), lambda bi, qi: (bi, qi, 0))]
    for a in list(ks) + list(vs):
        in_specs.append(pl.BlockSpec((1,) + a.shape[1:], lambda bi, qi: (bi, 0, 0)))
    in_specs.append(pl.BlockSpec((1, GROUP_W), lambda bi, qi: (0, 0)))
    return pl.pallas_call(
        functools.partial(_attn_kernel, mode=mode, nseg=nseg, tq=tq),
        grid=(b, s // tq),
        in_specs=in_specs,
        out_specs=pl.BlockSpec((1, tq, GROUP_W), lambda bi, qi: (bi, qi, 0)),
        out_shape=jax.ShapeDtypeStruct((b, s, GROUP_W), BF16),
        compiler_params=pltpu.CompilerParams(dimension_semantics=("parallel", "parallel"),
                                             vmem_limit_bytes=VMEM_LIMIT),
        name="attn_" + mode,
    )(q, *ks, *vs, gain)


def _merge_kernel(x_ref, ya_ref, yb_ref, yc_ref, yd_ref, mod_ref, wout_ref, gffn_ref, wr_ref, br_ref,
                  xo_ref, h2_ref, route_ref, cnt_ref, *, tm):
    y = _dot(ya_ref[...], wout_ref[0:256, :])
    y = y + _dot(yb_ref[...], wout_ref[256:512, :])
    y = y + _dot(yc_ref[...], wout_ref[512:768, :])
    y = y + _dot(yd_ref[...], wout_ref[768:1024, :])
    xn = x_ref[...] + mod_ref[0, 2:3, :] * y
    xo_ref[...] = xn
    h2 = _rms_full(xn, gffn_ref[...]) * (1.0 + mod_ref[0, 4:5, :]) + mod_ref[0, 3:4, :]
    h2b = h2.astype(BF16)
    h2_ref[...] = h2b

    h2lo = (h2 - h2b.astype(F32)).astype(BF16)
    part = _dot(h2b, wr_ref[...]) + _dot(h2lo, wr_ref[...])
    logits = part + pltpu.roll(part, LANES - ROUTE_SEL_LANE, axis=1) + br_ref[...]
    lane = lax.broadcasted_iota(jnp.int32, (tm, LANES), 1)
    lanef = lane.astype(F32)
    neg = jnp.float32(-1e30)
    big = jnp.float32(1e9)
    is_g = (lane >= N_EXPERTS) & (lane < N_EXPERTS + N_GROUPS)
    lg = jnp.where(is_g, logits, neg)
    gmax = lg.max(axis=-1, keepdims=True)
    g_idx = jnp.where(is_g & (lg == gmax), lanef, big).min(axis=-1, keepdims=True) - N_EXPERTS
    g_w = 1.0 / jnp.where(is_g, jnp.exp(lg - gmax), 0.0).sum(axis=-1, keepdims=True)
    in_grp = (lane < N_EXPERTS) & ((lane // EXPERTS_PER_GROUP).astype(F32) == g_idx)
    le = jnp.where(in_grp, logits, neg)
    m1 = le.max(axis=-1, keepdims=True)
    i1 = jnp.where(in_grp & (le == m1), lanef, big).min(axis=-1, keepdims=True)
    rest = in_grp & (lanef != i1)
    le2 = jnp.where(rest, logits, neg)
    m2 = le2.max(axis=-1, keepdims=True)
    i2 = jnp.where(rest & (le2 == m2), lanef, big).min(axis=-1, keepdims=True)
    r = jnp.exp(m2 - m1)
    w1 = 1.0 / (1.0 + r)
    w2 = r / (1.0 + r)
    cmb = g_w * jnp.where(lanef == i1, w1, jnp.where(lanef == i2, w2, 0.0))
    sel = pltpu.roll(jnp.where((lanef == i1) | (lanef == i2), 1.0, 0.0), ROUTE_SEL_LANE, axis=1)
    route_ref[...] = cmb + sel
    cnt_ref[0] = jnp.broadcast_to(sel.sum(axis=0, keepdims=True), (HALO, LANES))


def _merge_call(x2d, ya, yb, yc, yd, mod, lw, *, tm, mod_index):
    n = x2d.shape[0]
    row = lambda i: (i, 0)
    const2 = lambda i: (0, 0)
    in_specs = [pl.BlockSpec((tm, D_MODEL), row)]
    in_specs += [pl.BlockSpec((tm, GROUP_W), row)] * 4
    in_specs += [pl.BlockSpec((1, 6, D_MODEL), lambda i: (mod_index(i), 0, 0)),
                 pl.BlockSpec((D_MODEL, D_MODEL), const2),
                 pl.BlockSpec((1, D_MODEL), const2),
                 pl.BlockSpec((D_MODEL, LANES), const2),
                 pl.BlockSpec((1, LANES), const2)]
    return pl.pallas_call(
        functools.partial(_merge_kernel, tm=tm),
        grid=(n // tm,),
        in_specs=in_specs,
        out_specs=[pl.BlockSpec((tm, D_MODEL), row), pl.BlockSpec((tm, D_MODEL), row),
                   pl.BlockSpec((tm, LANES), row), pl.BlockSpec((1, HALO, LANES), lambda i: (i, 0, 0))],
        out_shape=[jax.ShapeDtypeStruct((n, D_MODEL), F32), jax.ShapeDtypeStruct((n, D_MODEL), BF16),
                   jax.ShapeDtypeStruct((n, LANES), F32),
                   jax.ShapeDtypeStruct((n // tm, HALO, LANES), F32)],
        compiler_params=pltpu.CompilerParams(dimension_semantics=("parallel",),
                                             vmem_limit_bytes=VMEM_LIMIT),
        name="merge",
    )(x2d, ya, yb, yc, yd, mod, lw["w_out"], lw["g_ffn"], lw["w_router"], lw["b_router"])


def _segment_positions(route, loc, tm):
    lane = lax.broadcasted_iota(jnp.int32, (tm, LANES), 1)
    is_sel = (lane >= ROUTE_SEL_LANE) & (lane < ROUTE_SEL_LANE + N_EXPERTS)
    selm = is_sel & (route > 0.5)
    selb = jnp.where(selm, 1.0, 0.0).astype(BF16)
    r_i = lax.broadcasted_iota(jnp.int32, (tm, tm), 0)
    c_i = lax.broadcasted_iota(jnp.int32, (tm, tm), 1)
    earlier = jnp.where(c_i < r_i, 1.0, 0.0).astype(BF16)
    dest = loc + _dot(earlier, selb)
    big = jnp.float32(1e9)
    d_a = jnp.where(selm, dest, big).min(axis=-1, keepdims=True)
    d_b = jnp.where(selm, dest, -big).max(axis=-1, keepdims=True)
    return selm, dest, d_a, d_b


def _to_row(col, tm):
    r_i = lax.broadcasted_iota(jnp.int32, (tm, tm), 0)
    c_i = lax.broadcasted_iota(jnp.int32, (tm, tm), 1)
    return jnp.where(r_i == c_i, col, 0.0).sum(axis=0, keepdims=True)


def _chunk_copy_out(xs_ref, xbuf_ref, sem_ref, slot, src_chunk, dst_chunk):
    return pltpu.make_async_copy(xs_ref.at[slot, pl.ds(src_chunk * MOE_CHUNK, MOE_CHUNK), :],
                                 xbuf_ref.at[pl.ds(dst_chunk * MOE_CHUNK, MOE_CHUNK), :],
                                 sem_ref.at[slot])


def _dispatch_kernel(nch_ref, dst_ref, plo_ref, phi_ref, nused_ref, h2_ref, route_ref, loc_ref, xbuf_ref,
                     xs_ref, sem_ref, *, tm, n_tiles, n_work):
    i = pl.program_id(0)
    slot = i % 2
    zero_chunk = MOE_RLOC // MOE_CHUNK

    def start_tile(t_idx, slot_):
        def body(j, carry):
            _chunk_copy_out(xs_ref, xbuf_ref, sem_ref, slot_, j, dst_ref[t_idx * MOE_NCH + j]).start()
            return carry
        lax.fori_loop(0, nch_ref[t_idx], body, 0)

    def wait_tile(t_idx, slot_):
        def body(j, carry):
            _chunk_copy_out(xs_ref, xbuf_ref, sem_ref, slot_, 0, 0).wait()
            return carry
        lax.fori_loop(0, nch_ref[t_idx], body, 0)

    @pl.when(i == 0)
    def _():
        xs_ref[2, :, :] = jnp.zeros((MOE_RLOC + MOE_CHUNK, MOE_XW), BF16)

        def pad_start(j, carry):
            _chunk_copy_out(xs_ref, xbuf_ref, sem_ref, 2, zero_chunk, j).start()
            return carry

        def pad_wait(j, carry):
            _chunk_copy_out(xs_ref, xbuf_ref, sem_ref, 2, zero_chunk, 0).wait()
            return carry
        for e in range(N_EXPERTS):
            lax.fori_loop(plo_ref[e], phi_ref[e], pad_start, 0)
        for e in range(N_EXPERTS):
            lax.fori_loop(plo_ref[e], phi_ref[e], pad_wait, 0)

        def tail_copy(w):
            return pltpu.make_async_copy(xs_ref.at[2, pl.ds(0, MOE_WORK), :],
                                         xbuf_ref.at[pl.ds(w * MOE_WORK, MOE_WORK), :], sem_ref.at[2])

        def tail_start(w, carry):
            tail_copy(w).start()
            return carry

        def tail_wait(w, carry):
            tail_copy(w).wait()
            return carry
        lax.fori_loop(nused_ref[0], n_work, tail_start, 0)
        lax.fori_loop(nused_ref[0], n_work, tail_wait, 0)

    @pl.when(i >= 2)
    def _():
        wait_tile(i - 2, slot)

    route = route_ref[...]
    selm, dest, d_a, d_b = _segment_positions(route, loc_ref[0, 0:1, :], tm)
    cmb_al = pltpu.roll(route, ROUTE_SEL_LANE, axis=1)
    w_a = jnp.where(selm & (dest == d_a), cmb_al, 0.0).sum(axis=-1, keepdims=True)
    w_b = jnp.where(selm & (dest == d_b), cmb_al, 0.0).sum(axis=-1, keepdims=True)
    da_r, db_r, wa_r, wb_r = (_to_row(v, tm) for v in (d_a, d_b, w_a, w_b))
    rio = lax.broadcasted_iota(jnp.int32, (MOE_RLOC, tm), 0).astype(F32)
    hit_a = rio == da_r
    hit_b = rio == db_r
    perm = jnp.where(hit_a | hit_b, 1.0, 0.0).astype(BF16)
    xs_ref[slot, 0:MOE_RLOC, 0:D_MODEL] = _dot(perm, h2_ref[...]).astype(BF16)
    w_sorted = (jnp.where(hit_a, wa_r, 0.0) + jnp.where(hit_b, wb_r, 0.0)).sum(axis=-1, keepdims=True)
    w_hi = w_sorted.astype(BF16).astype(F32)
    r1 = w_sorted - w_hi
    w_mid = r1.astype(BF16).astype(F32)
    w_lo = r1 - w_mid
    lane = lax.broadcasted_iota(jnp.int32, (MOE_RLOC, LANES), 1)
    slab = jnp.where(lane == 0, w_hi, jnp.where(lane == 1, w_mid, jnp.where(lane == 2, w_lo, 0.0)))
    xs_ref[slot, 0:MOE_RLOC, D_MODEL:MOE_XW] = slab.astype(BF16)
    start_tile(i, slot)

    @pl.when(i == n_tiles - 1)
    def _():
        if n_tiles > 1:
            wait_tile(i - 1, 1 - slot)
        wait_tile(i, slot)


def _dispatch_call(h2, route, tabs, *, tm, n_rows):
    n = h2.shape[0]
    n_tiles = n // tm
    grid_spec = pltpu.PrefetchScalarGridSpec(
        num_scalar_prefetch=5,
        grid=(n_tiles,),
        in_specs=[pl.BlockSpec((tm, D_MODEL), lambda i, *_: (i, 0)),
                  pl.BlockSpec((tm, LANES), lambda i, *_: (i, 0)),
                  pl.BlockSpec((1, HALO, LANES), lambda i, *_: (i, 0, 0))],
        out_specs=pl.BlockSpec(memory_space=pl.ANY),
        scratch_shapes=[pltpu.VMEM((3, MOE_RLOC + MOE_CHUNK, MOE_XW), BF16),
                        pltpu.SemaphoreType.DMA((3,))],
    )
    return pl.pallas_call(
        functools.partial(_dispatch_kernel, tm=tm, n_tiles=n_tiles, n_work=n_rows // MOE_WORK),
        grid_spec=grid_spec,
        out_shape=jax.ShapeDtypeStruct((n_rows, MOE_XW), BF16),
        compiler_params=pltpu.CompilerParams(dimension_semantics=("arbitrary",),
                                             vmem_limit_bytes=VMEM_LIMIT),
        name="moe_dispatch",
    )(tabs["nch"], tabs["dst"], tabs["pad_lo"], tabs["pad_hi"], tabs["n_used"], h2, route, tabs["loc"])


def _expert_kernel(te_ref, nused_ref, x_ref, wup_ref, wdn_ref, y_ref):
    used = pl.program_id(0) < nused_ref[0]

    @pl.when(used)
    def _():
        wt = x_ref[:, D_MODEL:MOE_XW].astype(F32).sum(axis=-1, keepdims=True)
        gu = _dot(x_ref[:, 0:D_MODEL], wup_ref[0])
        g = gu[:, 0:D_EXPERT]
        a = (g * jax.nn.sigmoid(g)) * gu[:, D_EXPERT:2 * D_EXPERT] * wt
        y_ref[...] = _dot(a.astype(BF16), wdn_ref[0]).astype(BF16)

    @pl.when(jnp.logical_not(used))
    def _():
        y_ref[...] = jnp.zeros(y_ref.shape, BF16)


def _expert_call(xbuf, tabs, lw):
    n_rows = xbuf.shape[0]
    n_work = n_rows // MOE_WORK
    grid_spec = pltpu.PrefetchScalarGridSpec(
        num_scalar_prefetch=2,
        grid=(n_work,),
        in_specs=[pl.BlockSpec((MOE_WORK, MOE_XW), lambda w, te, nu: (jnp.minimum(w, nu[0] - 1), 0)),
                  pl.BlockSpec((1, D_MODEL, 2 * D_EXPERT), lambda w, te, nu: (te[w], 0, 0)),
                  pl.BlockSpec((1, D_EXPERT, D_MODEL), lambda w, te, nu: (te[w], 0, 0))],
        out_specs=pl.BlockSpec((MOE_WORK, D_MODEL), lambda w, te, nu: (w, 0)),
    )
    return pl.pallas_call(
        _expert_kernel,
        grid_spec=grid_spec,
        out_shape=jax.ShapeDtypeStruct((n_rows, D_MODEL), BF16),
        compiler_params=pltpu.CompilerParams(dimension_semantics=("arbitrary",),
                                             vmem_limit_bytes=VMEM_LIMIT),
        name="moe_experts",
    )(tabs["tile_expert"], tabs["n_used"], xbuf, lw["w_up"], lw["w_down"])


def _chunk_copy_in(ybuf_ref, ys_ref, sem_ref, slot, src_chunk, dst_chunk):
    return pltpu.make_async_copy(ybuf_ref.at[pl.ds(src_chunk * MOE_CHUNK, MOE_CHUNK), :],
                                 ys_ref.at[slot, pl.ds(dst_chunk * MOE_CHUNK, MOE_CHUNK), :],
                                 sem_ref.at[slot])


def _combine_kernel(nch_ref, dst_ref, route_ref, loc_ref, x_ref, mod_ref, ybuf_ref, o_ref,
                    ys_ref, sem_ref, *, tm, n_tiles):
    i = pl.program_id(0)
    slot = i % 2

    def start_tile(t_idx, slot_):
        def body(j, carry):
            _chunk_copy_in(ybuf_ref, ys_ref, sem_ref, slot_, dst_ref[t_idx * MOE_NCH + j], j).start()
            return carry
        lax.fori_loop(0, nch_ref[t_idx], body, 0)

    @pl.when(i == 0)
    def _():
        ys_ref[...] = jnp.zeros(ys_ref.shape, BF16)
        start_tile(0, 0)

    @pl.when(i + 1 < n_tiles)
    def _():
        start_tile(i + 1, 1 - slot)

    def wait_body(j, carry):
        _chunk_copy_in(ybuf_ref, ys_ref, sem_ref, slot, 0, 0).wait()
        return carry
    lax.fori_loop(0, nch_ref[i], wait_body, 0)

    _, _, d_a, d_b = _segment_positions(route_ref[...], loc_ref[0, 0:1, :], tm)
    lio = lax.broadcasted_iota(jnp.int32, (tm, MOE_RLOC), 1).astype(F32)
    perm_t = jnp.where((lio == d_a) | (lio == d_b), 1.0, 0.0).astype(BF16)
    y = _dot(perm_t, ys_ref[slot])
    o_ref[...] = x_ref[...] + mod_ref[0, 5:6, :] * y


def _combine_call(ybuf, route, x2d, mod, tabs, *, tm, mod_index):
    n = x2d.shape[0]
    n_tiles = n // tm
    grid_spec = pltpu.PrefetchScalarGridSpec(
        num_scalar_prefetch=2,
        grid=(n_tiles,),
        in_specs=[pl.BlockSpec((tm, LANES), lambda i, *_: (i, 0)),
                  pl.BlockSpec((1, HALO, LANES), lambda i, *_: (i, 0, 0)),
                  pl.BlockSpec((tm, D_MODEL), lambda i, *_: (i, 0)),
                  pl.BlockSpec((1, 6, D_MODEL), lambda i, *_: (mod_index(i), 0, 0)),
                  pl.BlockSpec(memory_space=pl.ANY)],
        out_specs=pl.BlockSpec((tm, D_MODEL), lambda i, *_: (i, 0)),
        scratch_shapes=[pltpu.VMEM((2, MOE_RLOC, D_MODEL), BF16),
                        pltpu.SemaphoreType.DMA((2,))],
    )
    return pl.pallas_call(
        functools.partial(_combine_kernel, tm=tm, n_tiles=n_tiles),
        grid_spec=grid_spec,
        out_shape=jax.ShapeDtypeStruct((n, D_MODEL), F32),
        compiler_params=pltpu.CompilerParams(dimension_semantics=("arbitrary",),
                                             vmem_limit_bytes=VMEM_LIMIT),
        name="moe_combine",
    )(tabs["nch"], tabs["dst"], route, tabs["loc"], x2d, mod, ybuf)


def _moe_rows(n_tok):
    n_tiles = n_tok // MOE_T
    chunks = 2 * n_tok // MOE_CHUNK + (MOE_CHUNK - 1) * n_tiles + N_EXPERTS * (MOE_CPW - 1)
    return -(-chunks // MOE_CPW) * MOE_WORK


def _route_tables(counts, n_tok):
    n_work = _moe_rows(n_tok) // MOE_WORK
    cnt = counts[:, 0, ROUTE_SEL_LANE:ROUTE_SEL_LANE + N_EXPERTS].astype(jnp.int32)
    n_tiles = cnt.shape[0]
    nchunk = (cnt + MOE_CHUNK - 1) // MOE_CHUNK
    ei = jnp.arange(N_EXPERTS, dtype=jnp.int32)
    ti = jnp.arange(n_tiles, dtype=jnp.int32)
    upto_e = (ei[:, None] <= ei[None, :]).astype(jnp.int32)
    upto_t = (ti[None, :] <= ti[:, None]).astype(jnp.int32)
    ends = (nchunk[:, :, None] * upto_e[None, :, :]).sum(axis=1)
    loc = ends - nchunk
    nch = nchunk.sum(axis=1)
    per_e = nchunk.sum(axis=0)
    padded_e = (per_e + MOE_CPW - 1) // MOE_CPW * MOE_CPW
    e_end = (padded_e[:, None] * upto_e).sum(axis=0)
    e_base = e_end - padded_e
    glob = e_base[None, :] + (upto_t[:, :, None] * nchunk[None, :, :]).sum(axis=1) - nchunk
    j = jnp.arange(MOE_NCH, dtype=jnp.int32)
    e_of_j = (ends[:, None, :] <= j[None, :, None]).sum(axis=-1)
    seg = e_of_j[:, :, None] == ei[None, None, :]
    dst = jnp.where(seg, (glob - loc)[:, None, :], 0).sum(axis=-1) + j[None, :]
    dst = jnp.where(j[None, :] < nch[:, None], dst, 0)
    n_used = padded_e.sum() // MOE_CPW
    w = jnp.minimum(jnp.arange(n_work, dtype=jnp.int32), n_used - 1)
    tile_expert = jnp.minimum(((e_end // MOE_CPW)[None, :] <= w[:, None]).sum(axis=-1), N_EXPERTS - 1)
    loc_rows = jnp.pad((loc * MOE_CHUNK).astype(F32),
                       ((0, 0), (ROUTE_SEL_LANE, LANES - ROUTE_SEL_LANE - N_EXPERTS)))
    loc_rows = jnp.broadcast_to(loc_rows[:, None, :], (n_tiles, HALO, LANES))
    return {"nch": nch.astype(jnp.int32), "dst": dst.reshape(-1).astype(jnp.int32),
            "pad_lo": (e_base + per_e).astype(jnp.int32), "pad_hi": e_end.astype(jnp.int32),
            "tile_expert": tile_expert.astype(jnp.int32), "n_used": n_used.astype(jnp.int32).reshape(1),
            "loc": loc_rows}


def _moe_call(h2, route, counts, x2d, mod, lw, *, mod_index):
    n = x2d.shape[0]
    tabs = _route_tables(counts, n)
    xbuf = _dispatch_call(h2, route, tabs, tm=MOE_T, n_rows=_moe_rows(n))
    ybuf = _expert_call(xbuf, tabs, lw)
    return _combine_call(ybuf, route, x2d, mod, tabs, tm=MOE_T, mod_index=mod_index)


def _ones_blocks(sizes):
    n = sum(sizes)
    m = np.zeros((n, n), np.float32)
    o = 0
    for s in sizes:
        m[o:o + s, o:o + s] = 1.0
        o += s
    return jnp.asarray(m, BF16)


def _rope_tables(rows):
    r = jnp.repeat(jnp.arange(rows, dtype=F32), GRID_W)
    c = jnp.tile(jnp.arange(GRID_W, dtype=F32), rows)

    def tab(rot_dim):
        nf = rot_dim // 4
        inv = ROPE_THETA ** (-jnp.arange(nf, dtype=F32) / nf)
        ang = jnp.concatenate([r[:, None] * inv, c[:, None] * inv], axis=-1)
        cos = jnp.repeat(jnp.cos(ang), 2, axis=-1)
        sin = jnp.repeat(jnp.sin(ang), 2, axis=-1) * jnp.tile(jnp.asarray([-1.0, 1.0], F32), rot_dim // 2)
        return cos, sin

    s = rows * GRID_W
    cc, sc = tab(C_HEAD_DIM)
    cd, sd = tab(D_ROPE)
    one = lambda w: jnp.ones((s, w), F32)
    zero = lambda w: jnp.zeros((s, w), F32)
    return (jnp.tile(cc, (1, 2)), jnp.tile(sc, (1, 2)),
            jnp.concatenate([one(64), cd, one(32)], axis=1), jnp.concatenate([zero(64), sd, zero(32)], axis=1),
            jnp.concatenate([cd, one(96)], axis=1), jnp.concatenate([sd, zero(96)], axis=1))


def _layer_weights(l, p):
    (g_attn_norm, w_in, sgu_v_gain, sgu_w, sgu_b, conv_w, gqa_q_gain, gqa_k_gain, mla_qa_gain, mla_w_qb,
     mla_kva_gain, mla_w_kvb, mla_q_gain, mla_k_gain, g_group_out, w_out, g_ffn_norm,
     w_rg, b_rg, w_re, b_re, w_up, w_down) = [a[l] for a in p]
    def heads_0213(a, axis, off):
        sl = lambda lo: lax.slice_in_dim(a, off + lo, off + lo + 64, axis=axis)
        return [sl(0), sl(128), sl(64), sl(192)]

    wi = w_in
    w_in_p = jnp.concatenate([wi[:, :1280]] + heads_0213(wi, 1, 1280) +
                             [wi[:, 1536:2208], jnp.zeros((D_MODEL, IN_W - 2208), F32)], axis=1).astype(BF16)
    z32 = jnp.zeros((D_Q_LORA, 32), F32)
    wqb = jnp.concatenate(
        [jnp.concatenate([mla_w_qb[:, h * 96:h * 96 + 96], z32], axis=1) for h in range(D_HEADS)], axis=1)
    z64 = jnp.zeros((D_KV_LORA, 64), F32)
    wk = jnp.concatenate(
        [jnp.concatenate([mla_w_kvb[:, h * 128:h * 128 + 64], z64], axis=1) for h in range(D_HEADS)], axis=1)
    wv = jnp.concatenate([mla_w_kvb[:, h * 128 + 64:h * 128 + 128] for h in range(D_HEADS)], axis=1)
    qslab = jnp.concatenate([mla_q_gain, jnp.zeros((32,), F32)])
    kslab = jnp.concatenate([mla_k_gain[:64], jnp.zeros((64,), F32)])
    g_go = jnp.concatenate([g_group_out[:512]] + heads_0213(g_group_out, 0, 512) + [g_group_out[768:]])
    w_out_p = jnp.concatenate([w_out[:512]] + heads_0213(w_out, 0, 512) + [w_out[768:]], axis=0).astype(BF16)
    w_r = jnp.concatenate([w_re, w_rg], axis=1)
    w_r_hi = w_r.astype(BF16)
    w_r_lo = (w_r - w_r_hi.astype(F32)).astype(BF16)
    n_r = N_EXPERTS + N_GROUPS
    w_router = jnp.concatenate([w_r_hi, jnp.zeros((D_MODEL, ROUTE_SEL_LANE - n_r), BF16), w_r_lo,
                                jnp.zeros((D_MODEL, LANES - ROUTE_SEL_LANE - n_r), BF16)], axis=1)
    b_router = jnp.concatenate([b_re, b_rg, jnp.zeros((LANES - 20,), F32)])[None, :]
    sgub_full = jnp.repeat(sgu_b.T, 64, axis=1)
    return {
        "g_attn": g_attn_norm[None, :], "w_in": w_in_p,
        "sgu_w": sgu_w.reshape(A_HEADS * CHUNK, CHUNK).astype(BF16), "sgu_b": sgub_full,
        "v_gain": sgu_v_gain[None, :], "conv_w": conv_w, "g_go": g_go[None, :],
        "cq_gain": jnp.tile(gqa_q_gain, 4)[None, :], "ck_gain": jnp.tile(gqa_k_gain, 2)[None, :],
        "qa_gain": mla_qa_gain[None, :], "w_qb": wqb.astype(BF16), "kva_gain": mla_kva_gain[None, :],
        "w_kvb": jnp.concatenate([wk, wv], axis=1).astype(BF16),
        "dq_gain": jnp.tile(qslab, 4)[None, :], "dk_gain": jnp.tile(kslab, 4)[None, :],
        "dkr_gain": jnp.concatenate([mla_k_gain[64:], jnp.zeros((96,), F32)])[None, :],
        "b64": _ones_blocks([64] * 4), "bq": _ones_blocks([64, 32, 32] * 2),
        "invq": jnp.tile(jnp.concatenate([jnp.full((64,), 1 / 64.0, F32), jnp.full((64,), 1 / 32.0, F32)]),
                         2)[None, :],
        "w_out": w_out_p, "g_ffn": g_ffn_norm[None, :], "w_router": w_router, "b_router": b_router,
        "w_up": w_up.astype(BF16), "w_down": w_down.astype(BF16),
    }


def kernel(x, c, ctx, c_ctx, w_mod, b_mod, g_attn_norm, w_in, sgu_v_gain, sgu_w, sgu_b, conv_w, gqa_q_gain,
           gqa_k_gain, mla_qa_gain, mla_w_qb, mla_kva_gain, mla_w_kvb, mla_q_gain, mla_k_gain, g_group_out,
           w_out, g_ffn_norm, w_router_group, b_router_group, w_router_expert, b_router_expert,
           w_expert_up, w_expert_down):
    b, s, d = x.shape
    lc = ctx.shape[1]
    depth = w_mod.shape[0]
    params = (g_attn_norm, w_in, sgu_v_gain, sgu_w, sgu_b, conv_w, gqa_q_gain, gqa_k_gain, mla_qa_gain,
              mla_w_qb, mla_kva_gain, mla_w_kvb, mla_q_gain, mla_k_gain, g_group_out, w_out, g_ffn_norm,
              w_router_group, b_router_group, w_router_expert, b_router_expert, w_expert_up, w_expert_down)

    c_all = jnp.concatenate([c, c_ctx[None, :], jnp.zeros((16 - b - 1, d), F32)], axis=0)
    mod_all = _modulation(c_all, w_mod, b_mod).reshape(depth, 16, 6, d)
    rope = _rope_tables(s // GRID_W)

    tm = 512
    tq = 512
    tpb = s // tm
    lat_idx = lambda i: i // tpb
    ctx_idx = lambda i: b

    xl = x.reshape(b * s, d)
    xc = ctx.reshape(b * lc, d)
    for l in range(depth):
        lw = _layer_weights(l, params)
        mod = mod_all[l]
        update_ctx = l < depth - 1
        ya_c, yb_c, qc_c, kc_c, vc_c, qd_c, kd_c, vd_c = _in_call(
            xc, mod, lw, None, tm=tm, seq_len=lc, mod_index=ctx_idx)
        ya, yb, qc, kc, vc, qd, kd, vd = _in_call(xl, mod, lw, rope, tm=tm, seq_len=s, mod_index=lat_idx)
        r3 = lambda a, n: a.reshape(b, n, a.shape[-1])
        g_c = lw["g_go"][:, 512:768]
        g_d = lw["g_go"][:, 768:1024]
        yc = _attn_call(r3(qc, s), [r3(kc_c, lc), r3(kc, s)], [r3(vc_c, lc), r3(vc, s)], g_c, mode="C", tq=tq)
        yd = _attn_call(r3(qd, s), [r3(kd_c, lc), r3(kd, s)], [r3(vd_c, lc), r3(vd, s)], g_d, mode="D", tq=tq)
        xl, h2, route, counts = _merge_call(xl, ya, yb, yc.reshape(b * s, GROUP_W), yd.reshape(b * s, GROUP_W),
                                            mod, lw, tm=tm, mod_index=lat_idx)
        xl = _moe_call(h2, route, counts, xl, mod, lw, mod_index=lat_idx)
        if update_ctx:
            yc_c = _attn_call(r3(qc_c, lc), [r3(kc_c, lc)], [r3(vc_c, lc)], g_c, mode="C", tq=lc)
            yd_c = _attn_call(r3(qd_c, lc), [r3(kd_c, lc)], [r3(vd_c, lc)], g_d, mode="D", tq=lc)
            xc, h2c, route_c, counts_c = _merge_call(xc, ya_c, yb_c, yc_c.reshape(b * lc, GROUP_W),
                                                     yd_c.reshape(b * lc, GROUP_W), mod, lw, tm=tm,
                                                     mod_index=ctx_idx)
            xc = _moe_call(h2c, route_c, counts_c, xc, mod, lw, mod_index=ctx_idx)
    return xl.reshape(b, s, d)
```

```python
import functools

import numpy as np
import jax
import jax.numpy as jnp
from jax import lax
from jax.experimental import pallas as pl
from jax.experimental.pallas import tpu as pltpu

F32 = jnp.float32
BF16 = jnp.bfloat16

D_MODEL = 1024
GRID_W = 64
EPS = 1e-6
ROPE_THETA = 10000.0
GROUP_W = 256
CHUNK = 128
A_HEADS = 4
C_HEAD_DIM = 64
C_SCALE = C_HEAD_DIM ** -0.5
D_HEADS = 4
D_NOPE = 64
D_ROPE = 32
D_V = 64
D_Q_LORA = 256
D_KV_LORA = 128
D_SCALE = (D_NOPE + D_ROPE) ** -0.5
LOG2E = 1.4426950408889634
ATTN_ROWS = 16
ATTN_KSPLIT = 1024
N_GROUPS = 4
EXPERTS_PER_GROUP = 4
N_EXPERTS = 16
D_EXPERT = 256
LANES = 128
HALO = 8

IN_W = 2304

ROUTE_SEL_LANE = 32
MOE_T = 512
MOE_CHUNK = 16
MOE_RLOC = 1280
MOE_NCH = MOE_RLOC // MOE_CHUNK
MOE_XW = D_MODEL + LANES
MOE_WORK = 512
MOE_CPW = MOE_WORK // MOE_CHUNK
VMEM_LIMIT = 56 * 1024 * 1024


def _nt_dot(a, b):
    return lax.dot_general(a, b, (((1,), (1,)), ((), ())), preferred_element_type=F32)


def _dot(a, b):
    return jnp.dot(a, b, preferred_element_type=F32)


def _rms_full(x, gain):
    return x * lax.rsqrt(jnp.mean(x * x, axis=-1, keepdims=True) + EPS) * gain


def _group_ms(x, ones_blocks, inv_count):
    return _dot((x * x).astype(BF16), ones_blocks) * inv_count


def _pair_swap(x, even):
    nxt = pltpu.roll(x, x.shape[1] - 1, axis=1)
    prv = pltpu.roll(x, 1, axis=1)
    return jnp.where(even, nxt, prv)


def _rope(x, cos, sin, even):
    return x * cos + _pair_swap(x, even) * sin


def _mod_kernel(c_ref, w_ref, b_ref, o_ref):
    cv = c_ref[...]
    sc = cv * jax.nn.sigmoid(cv)
    o_ref[0] = jnp.dot(sc, w_ref[0], preferred_element_type=F32,
                       precision=lax.Precision.HIGHEST) + b_ref[0]


def _modulation(c_all, w_mod, b_mod):
    depth = w_mod.shape[0]
    nblk = 4
    wblk = 6 * D_MODEL // nblk
    return pl.pallas_call(
        _mod_kernel,
        grid=(depth, nblk),
        in_specs=[pl.BlockSpec((16, D_MODEL), lambda l, j: (0, 0)),
                  pl.BlockSpec((1, D_MODEL, wblk), lambda l, j: (l, 0, j)),
                  pl.BlockSpec((1, 1, wblk), lambda l, j: (l, 0, j))],
        out_specs=pl.BlockSpec((1, 16, wblk), lambda l, j: (l, 0, j)),
        out_shape=jax.ShapeDtypeStruct((depth, 16, 6 * D_MODEL), F32),
        compiler_params=pltpu.CompilerParams(dimension_semantics=("arbitrary", "arbitrary"),
                                             vmem_limit_bytes=VMEM_LIMIT),
        name="mod",
    )(c_all, w_mod, b_mod.reshape(depth, 1, 6 * D_MODEL))


def _in_kernel(*refs, tm, seq_len, use_rope):
    (x_ref, xp_ref, xn_ref, mod_ref, gattn_ref, win_ref, sguw_ref, sgub_ref, vgain_ref, convw_ref,
     ggo_ref, cqg_ref, ckg_ref, qag_ref, wqb_ref, kvag_ref, wkvb_ref, dqg_ref, dkg_ref, dkrg_ref,
     b64_ref, bq_ref, invq_ref) = refs[:23]
    pos = 23
    if use_rope:
        cosc_ref, sinc_ref, cosq_ref, sinq_ref, cosk_ref, sink_ref = refs[pos:pos + 6]
        pos += 6
    ya_ref, yb_ref, qc_ref, kc_ref, vc_ref, qd_ref, kd_ref, vd_ref = refs[pos:pos + 8]

    i = pl.program_id(0)
    shift = mod_ref[0, 0:1, :]
    scale = mod_ref[0, 1:2, :]
    gattn = gattn_ref[...]

    def prenorm(xv):
        return (_rms_full(xv, gattn) * (1.0 + scale) + shift).astype(BF16)

    hb = prenorm(x_ref[...])
    b64 = b64_ref[...]
    inv64 = 1.0 / 64.0
    lane128 = lax.broadcasted_iota(jnp.int32, (tm, LANES), 1)
    even = (lane128 & 1) == 0

    u = jax.nn.gelu(_dot(hb, win_ref[:, 0:256]))
    v = jax.nn.gelu(_dot(hb, win_ref[:, 256:512]))
    v = (v * lax.rsqrt(_group_ms(v, b64, inv64) + EPS) * vgain_ref[...]).astype(BF16)
    lane_head = lax.broadcasted_iota(jnp.int32, (CHUNK, GROUP_W), 1) // 64
    sguw = sguw_ref[...]
    mixed_chunks = []
    for ci in range(tm // CHUNK):
        r = _dot(sguw, v[ci * CHUNK:(ci + 1) * CHUNK, :])
        m = r[0:CHUNK]
        for h in range(1, A_HEADS):
            m = jnp.where(lane_head == h, r[h * CHUNK:(h + 1) * CHUNK], m)
        mixed_chunks.append(m + sgub_ref[...])
    ya = u * jnp.concatenate(mixed_chunks, axis=0)
    ya_ref[...] = _rms_full(ya, ggo_ref[:, 0:256]).astype(BF16)

    bg = _dot(hb, win_ref[:, 512:768])
    z = _dot(hb, win_ref[:, 768:1024]) * _dot(hb, win_ref[:, 1024:1280])
    hh = prenorm(jnp.concatenate([xp_ref[...], xn_ref[...]], axis=0))
    zh = _dot(hh, win_ref[:, 768:1024]) * _dot(hh, win_ref[:, 1024:1280])
    row = lax.broadcasted_iota(jnp.int32, (tm, GROUP_W), 0)
    posn = (i * tm + row) % seq_len
    z_prev = jnp.where(row == 0, zh[HALO - 1:HALO, :], pltpu.roll(z, 1, axis=0))
    z_prev = jnp.where(posn == 0, 0.0, z_prev)
    z_next = jnp.where(row == tm - 1, zh[HALO:HALO + 1, :], pltpu.roll(z, tm - 1, axis=0))
    z_next = jnp.where(posn == seq_len - 1, 0.0, z_next)
    yb = bg * (convw_ref[0:1, :] * z_prev + convw_ref[1:2, :] * z + convw_ref[2:3, :] * z_next)
    yb_ref[...] = _rms_full(yb, ggo_ref[:, 256:512]).astype(BF16)

    q = _dot(hb, win_ref[:, 1280:1536])
    q = q * lax.rsqrt(_group_ms(q, b64, inv64) + EPS) * cqg_ref[...]
    k = _dot(hb, win_ref[:, 1536:1664])
    k = k * lax.rsqrt(_group_ms(k, b64[0:128, 0:128], inv64) + EPS) * ckg_ref[...]
    if use_rope:
        cosc, sinc = cosc_ref[...], sinc_ref[...]
        q = jnp.concatenate([_rope(q[:, 0:128], cosc, sinc, even),
                             _rope(q[:, 128:256], cosc, sinc, even)], axis=1)
        k = _rope(k, cosc, sinc, even)
    qc_ref[...] = (q * (C_SCALE * LOG2E)).astype(BF16)
    kc_ref[...] = k.astype(BF16)
    vc_ref[...] = _dot(hb, win_ref[:, 1664:1792]).astype(BF16)

    cq = _rms_full(_dot(hb, win_ref[:, 1792:2048]), qag_ref[...]).astype(BF16)
    qd = _dot(cq, wqb_ref[...])
    bq = bq_ref[...]
    invq = invq_ref[...]
    halves = []
    for hf in range(2):
        qh = qd[:, hf * 256:(hf + 1) * 256]
        qh = qh * lax.rsqrt(_group_ms(qh, bq, invq) + EPS) * dqg_ref[:, hf * 256:(hf + 1) * 256]
        if use_rope:
            cosq, sinq = cosq_ref[...], sinq_ref[...]
            qh = jnp.concatenate([_rope(qh[:, 0:128], cosq, sinq, even),
                                  _rope(qh[:, 128:256], cosq, sinq, even)], axis=1)
        halves.append(qh)
    qd_ref[...] = (jnp.concatenate(halves, axis=1) * (D_SCALE * LOG2E)).astype(BF16)

    ckv = _rms_full(_dot(hb, win_ref[:, 2048:2176]), kvag_ref[...]).astype(BF16)
    kv = _dot(ckv, wkvb_ref[...])
    vd_ref[...] = kv[:, 512:768].astype(BF16)
    kr = _dot(hb, win_ref[:, 2176:2304])
    kr = kr * lax.rsqrt(jnp.sum(kr * kr, axis=-1, keepdims=True) * (1.0 / D_ROPE) + EPS) * dkrg_ref[...]
    if use_rope:
        kr = _rope(kr, cosk_ref[...], sink_ref[...], even)
    kr = pltpu.roll(kr, D_NOPE, axis=1)
    for hf in range(2):
        kh = kv[:, hf * 256:(hf + 1) * 256]
        kh = kh * lax.rsqrt(_group_ms(kh, b64, inv64) + EPS) * dkg_ref[:, hf * 256:(hf + 1) * 256]
        kd_ref[:, hf * 256:hf * 256 + 128] = (kh[:, 0:128] + kr).astype(BF16)
        kd_ref[:, hf * 256 + 128:(hf + 1) * 256] = (kh[:, 128:256] + kr).astype(BF16)


def _in_call(x2d, mod, lw, rope, *, tm, seq_len, mod_index):
    n = x2d.shape[0]
    nt = n // tm
    nb8 = n // HALO
    tb = tm // HALO
    use_rope = rope is not None
    const2 = lambda i: (0, 0)
    row = lambda i: (i, 0)
    in_specs = [
        pl.BlockSpec((tm, D_MODEL), row),
        pl.BlockSpec((HALO, D_MODEL), lambda i: (jnp.maximum(i * tb - 1, 0), 0)),
        pl.BlockSpec((HALO, D_MODEL), lambda i: (jnp.minimum((i + 1) * tb, nb8 - 1), 0)),
        pl.BlockSpec((1, 6, D_MODEL), lambda i: (mod_index(i), 0, 0)),
    ]
    args = [x2d, x2d, x2d, mod]
    for name in ("g_attn", "w_in", "sgu_w", "sgu_b", "v_gain", "conv_w", "g_go", "cq_gain", "ck_gain",
                 "qa_gain", "w_qb", "kva_gain", "w_kvb", "dq_gain", "dk_gain", "dkr_gain",
                 "b64", "bq", "invq"):
        a = lw[name]
        in_specs.append(pl.BlockSpec(a.shape, const2))
        args.append(a)
    if use_rope:
        ntab = seq_len // tm
        for a in rope:
            in_specs.append(pl.BlockSpec((tm, LANES), lambda i: (i % ntab, 0)))
            args.append(a)
    widths = (256, 256, 256, 128, 128, 512, 512, 256)
    out_specs = [pl.BlockSpec((tm, w), row) for w in widths]
    out_shape = [jax.ShapeDtypeStruct((n, w), BF16) for w in widths]
    return pl.pallas_call(
        functools.partial(_in_kernel, tm=tm, seq_len=seq_len, use_rope=use_rope),
        grid=(nt,),
        in_specs=in_specs,
        out_specs=out_specs,
        out_shape=out_shape,
        compiler_params=pltpu.CompilerParams(dimension_semantics=("parallel",),
                                             vmem_limit_bytes=VMEM_LIMIT),
        name="in_proj",
    )(*args)


def _attn_kernel(*refs, mode, nseg, tq):
    q_ref = refs[0]
    k_refs = refs[1:1 + nseg]
    v_refs = refs[1 + nseg:1 + 2 * nseg]
    gain_ref = refs[1 + 2 * nseg]
    o_ref = refs[2 + 2 * nseg]
    p_ref = refs[3 + 2 * nseg]
    nks = [r.shape[1] for r in k_refs]
    offs = [sum(nks[:j]) for j in range(nseg)]
    n_chunks = tq // ATTN_ROWS

    vs = [r[0] for r in v_refs]
    if mode == "C":
        ks_c = [r[0] for r in k_refs]
        first_half = lax.broadcasted_iota(jnp.int32, (tq, LANES), 1) < 64
        heads = []
        for sl in range(2):
            qs = q_ref[0, :, sl * 128:(sl + 1) * 128]
            heads.append((jnp.where(first_half, qs, jnp.zeros_like(qs)), ks_c))
            heads.append((jnp.where(first_half, jnp.zeros_like(qs), qs), ks_c))
    else:
        heads = [(q_ref[0, :, h * 128:(h + 1) * 128], [r[0, :, h * 128:(h + 1) * 128] for r in k_refs])
                 for h in range(D_HEADS)]

    def score(h):
        qm, ks = heads[h]
        return [_nt_dot(qm, kk) for kk in ks]

    def softmax(h, ss):
        sums = []
        for c in range(n_chunks):
            rows = slice(c * ATTN_ROWS, (c + 1) * ATTN_ROWS)
            m = ss[0][rows].max(axis=-1, keepdims=True)
            for s in ss[1:]:
                m = jnp.maximum(m, s[rows].max(axis=-1, keepdims=True))
            tot = None
            for s, off, nk in zip(ss, offs, nks):
                p = jnp.exp2(s[rows] - m)
                ps = p.sum(axis=-1, keepdims=True)
                tot = ps if tot is None else tot + ps
                p_ref[h, rows, off:off + nk] = p.astype(BF16)
            sums.append(tot)
        return jnp.concatenate(sums, axis=0)

    def values(h, row_sum):
        o = None
        for off, nk, vv in zip(offs, nks, vs):
            step = min(nk, ATTN_KSPLIT)
            for k0 in range(0, nk, step):
                part = _dot(p_ref[h, :, off + k0:off + k0 + step], vv[k0:k0 + step])
                o = part if o is None else o + part
        return o / row_sum

    n_heads = len(heads)
    scores = {0: score(0), 1: score(1)}
    row_sums = {}
    outs = []
    for h in range(n_heads):
        row_sums[h] = softmax(h, scores.pop(h))
        if h + 2 < n_heads:
            scores[h + 2] = score(h + 2)
        if h >= 1:
            outs.append(values(h - 1, row_sums.pop(h - 1)))
    outs.append(values(n_heads - 1, row_sums.pop(n_heads - 1)))

    if mode == "C":
        out = jnp.concatenate([jnp.where(first_half, outs[0], outs[1]),
                               jnp.where(first_half, outs[2], outs[3])], axis=1)
    else:
        lane_head = lax.broadcasted_iota(jnp.int32, (tq, GROUP_W), 1) // D_V
        out = outs[0]
        for h in range(1, D_HEADS):
            out = jnp.where(lane_head == h, outs[h], out)
    o_ref[0] = _rms_full(out, gain_ref[...]).astype(BF16)


def _attn_call(q, ks, vs, gain, *, mode, tq):
    b, s, wq = q.shape
    nseg = len(ks)
    in_specs = [pl.BlockSpec((1, tq, wq), lambda bi, qi: (bi, qi, 0))]
    for a in list(ks) + list(vs):
        in_specs.append(pl.BlockSpec((1,) + a.shape[1:], lambda bi, qi: (bi, 0, 0)))
    in_specs.append(pl.BlockSpec((1, GROUP_W), lambda bi, qi: (0, 0)))
    return pl.pallas_call(
        functools.partial(_attn_kernel, mode=mode, nseg=nseg, tq=tq),
        grid=(b, s // tq),
        in_specs=in_specs,
        out_specs=pl.BlockSpec((1, tq, GROUP_W), lambda bi, qi: (bi, qi, 0)),
        out_shape=jax.ShapeDtypeStruct((b, s, GROUP_W), BF16),
        scratch_shapes=[pltpu.VMEM((4, tq, sum(a.shape[1] for a in ks)), BF16)],
        compiler_params=pltpu.CompilerParams(dimension_semantics=("parallel", "parallel"),
                                             vmem_limit_bytes=VMEM_LIMIT),
        name="attn_" + mode,
    )(q, *ks, *vs, gain)


def _merge_kernel(x_ref, ya_ref, yb_ref, yc_ref, yd_ref, mod_ref, wout_ref, gffn_ref, wr_ref, br_ref,
                  xo_ref, h2_ref, route_ref, cnt_ref, *, tm):
    y = _dot(ya_ref[...], wout_ref[0:256, :])
    y = y + _dot(yb_ref[...], wout_ref[256:512, :])
    y = y + _dot(yc_ref[...], wout_ref[512:768, :])
    y = y + _dot(yd_ref[...], wout_ref[768:1024, :])
    xn = x_ref[...] + mod_ref[0, 2:3, :] * y
    xo_ref[...] = xn
    h2 = _rms_full(xn, gffn_ref[...]) * (1.0 + mod_ref[0, 4:5, :]) + mod_ref[0, 3:4, :]
    h2b = h2.astype(BF16)
    h2_ref[...] = h2b

    h2lo = (h2 - h2b.astype(F32)).astype(BF16)
    part = _dot(h2b, wr_ref[...]) + _dot(h2lo, wr_ref[...])
    logits = part + pltpu.roll(part, LANES - ROUTE_SEL_LANE, axis=1) + br_ref[...]
    lane = lax.broadcasted_iota(jnp.int32, (tm, LANES), 1)
    lanef = lane.astype(F32)
    neg = jnp.float32(-1e30)
    big = jnp.float32(1e9)
    is_g = (lane >= N_EXPERTS) & (lane < N_EXPERTS + N_GROUPS)
    lg = jnp.where(is_g, logits, neg)
    gmax = lg.max(axis=-1, keepdims=True)
    g_idx = jnp.where(is_g & (lg == gmax), lanef, big).min(axis=-1, keepdims=True) - N_EXPERTS
    g_w = 1.0 / jnp.where(is_g, jnp.exp(lg - gmax), 0.0).sum(axis=-1, keepdims=True)
    in_grp = (lane < N_EXPERTS) & ((lane // EXPERTS_PER_GROUP).astype(F32) == g_idx)
    le = jnp.where(in_grp, logits, neg)
    m1 = le.max(axis=-1, keepdims=True)
    i1 = jnp.where(in_grp & (le == m1), lanef, big).min(axis=-1, keepdims=True)
    rest = in_grp & (lanef != i1)
    le2 = jnp.where(rest, logits, neg)
    m2 = le2.max(axis=-1, keepdims=True)
    i2 = jnp.where(rest & (le2 == m2), lanef, big).min(axis=-1, keepdims=True)
    r = jnp.exp(m2 - m1)
    w1 = 1.0 / (1.0 + r)
    w2 = r / (1.0 + r)
    cmb = g_w * jnp.where(lanef == i1, w1, jnp.where(lanef == i2, w2, 0.0))
    sel = pltpu.roll(jnp.where((lanef == i1) | (lanef == i2), 1.0, 0.0), ROUTE_SEL_LANE, axis=1)
    route_ref[...] = cmb + sel
    cnt_ref[0] = jnp.broadcast_to(sel.sum(axis=0, keepdims=True), (HALO, LANES))


def _merge_call(x2d, ya, yb, yc, yd, mod, lw, *, tm, mod_index):
    n = x2d.shape[0]
    row = lambda i: (i, 0)
    const2 = lambda i: (0, 0)
    in_specs = [pl.BlockSpec((tm, D_MODEL), row)]
    in_specs += [pl.BlockSpec((tm, GROUP_W), row)] * 4
    in_specs += [pl.BlockSpec((1, 6, D_MODEL), lambda i: (mod_index(i), 0, 0)),
                 pl.BlockSpec((D_MODEL, D_MODEL), const2),
                 pl.BlockSpec((1, D_MODEL), const2),
                 pl.BlockSpec((D_MODEL, LANES), const2),
                 pl.BlockSpec((1, LANES), const2)]
    return pl.pallas_call(
        functools.partial(_merge_kernel, tm=tm),
        grid=(n // tm,),
        in_specs=in_specs,
        out_specs=[pl.BlockSpec((tm, D_MODEL), row), pl.BlockSpec((tm, D_MODEL), row),
                   pl.BlockSpec((tm, LANES), row), pl.BlockSpec((1, HALO, LANES), lambda i: (i, 0, 0))],
        out_shape=[jax.ShapeDtypeStruct((n, D_MODEL), F32), jax.ShapeDtypeStruct((n, D_MODEL), BF16),
                   jax.ShapeDtypeStruct((n, LANES), F32),
                   jax.ShapeDtypeStruct((n // tm, HALO, LANES), F32)],
        compiler_params=pltpu.CompilerParams(dimension_semantics=("parallel",),
                                             vmem_limit_bytes=VMEM_LIMIT),
        name="merge",
    )(x2d, ya, yb, yc, yd, mod, lw["w_out"], lw["g_ffn"], lw["w_router"], lw["b_router"])


def _segment_positions(route, loc, tm):
    lane = lax.broadcasted_iota(jnp.int32, (tm, LANES), 1)
    is_sel = (lane >= ROUTE_SEL_LANE) & (lane < ROUTE_SEL_LANE + N_EXPERTS)
    selm = is_sel & (route > 0.5)
    selb = jnp.where(selm, 1.0, 0.0).astype(BF16)
    r_i = lax.broadcasted_iota(jnp.int32, (tm, tm), 0)
    c_i = lax.broadcasted_iota(jnp.int32, (tm, tm), 1)
    earlier = jnp.where(c_i < r_i, 1.0, 0.0).astype(BF16)
    dest = loc + _dot(earlier, selb)
    big = jnp.float32(1e9)
    d_a = jnp.where(selm, dest, big).min(axis=-1, keepdims=True)
    d_b = jnp.where(selm, dest, -big).max(axis=-1, keepdims=True)
    return selm, dest, d_a, d_b


def _to_row(col, tm):
    r_i = lax.broadcasted_iota(jnp.int32, (tm, tm), 0)
    c_i = lax.broadcasted_iota(jnp.int32, (tm, tm), 1)
    return jnp.where(r_i == c_i, col, 0.0).sum(axis=0, keepdims=True)


def _chunk_copy_out(xs_ref, xbuf_ref, sem_ref, slot, src_chunk, dst_chunk):
    return pltpu.make_async_copy(xs_ref.at[slot, pl.ds(src_chunk * MOE_CHUNK, MOE_CHUNK), :],
                                 xbuf_ref.at[pl.ds(dst_chunk * MOE_CHUNK, MOE_CHUNK), :],
                                 sem_ref.at[slot])


def _dispatch_kernel(nch_ref, dst_ref, plo_ref, phi_ref, nused_ref, h2_ref, route_ref, loc_ref, xbuf_ref,
                     xs_ref, sem_ref, *, tm, n_tiles, n_work):
    i = pl.program_id(0)
    slot = i % 2
    zero_chunk = MOE_RLOC // MOE_CHUNK

    def start_tile(t_idx, slot_):
        def body(j, carry):
            _chunk_copy_out(xs_ref, xbuf_ref, sem_ref, slot_, j, dst_ref[t_idx * MOE_NCH + j]).start()
            return carry
        lax.fori_loop(0, nch_ref[t_idx], body, 0)

    def wait_tile(t_idx, slot_):
        def body(j, carry):
            _chunk_copy_out(xs_ref, xbuf_ref, sem_ref, slot_, 0, 0).wait()
            return carry
        lax.fori_loop(0, nch_ref[t_idx], body, 0)

    @pl.when(i == 0)
    def _():
        xs_ref[2, :, :] = jnp.zeros((MOE_RLOC + MOE_CHUNK, MOE_XW), BF16)

        def pad_start(j, carry):
            _chunk_copy_out(xs_ref, xbuf_ref, sem_ref, 2, zero_chunk, j).start()
            return carry

        def pad_wait(j, carry):
            _chunk_copy_out(xs_ref, xbuf_ref, sem_ref, 2, zero_chunk, 0).wait()
            return carry
        for e in range(N_EXPERTS):
            lax.fori_loop(plo_ref[e], phi_ref[e], pad_start, 0)
        for e in range(N_EXPERTS):
            lax.fori_loop(plo_ref[e], phi_ref[e], pad_wait, 0)

        def tail_copy(w):
            return pltpu.make_async_copy(xs_ref.at[2, pl.ds(0, MOE_WORK), :],
                                         xbuf_ref.at[pl.ds(w * MOE_WORK, MOE_WORK), :], sem_ref.at[2])

        def tail_start(w, carry):
            tail_copy(w).start()
            return carry

        def tail_wait(w, carry):
            tail_copy(w).wait()
            return carry
        lax.fori_loop(nused_ref[0], n_work, tail_start, 0)
        lax.fori_loop(nused_ref[0], n_work, tail_wait, 0)

    @pl.when(i >= 2)
    def _():
        wait_tile(i - 2, slot)

    route = route_ref[...]
    selm, dest, d_a, d_b = _segment_positions(route, loc_ref[0, 0:1, :], tm)
    cmb_al = pltpu.roll(route, ROUTE_SEL_LANE, axis=1)
    w_a = jnp.where(selm & (dest == d_a), cmb_al, 0.0).sum(axis=-1, keepdims=True)
    w_b = jnp.where(selm & (dest == d_b), cmb_al, 0.0).sum(axis=-1, keepdims=True)
    da_r, db_r, wa_r, wb_r = (_to_row(v, tm) for v in (d_a, d_b, w_a, w_b))
    rio = lax.broadcasted_iota(jnp.int32, (MOE_RLOC, tm), 0).astype(F32)
    hit_a = rio == da_r
    hit_b = rio == db_r
    perm = jnp.where(hit_a | hit_b, 1.0, 0.0).astype(BF16)
    xs_ref[slot, 0:MOE_RLOC, 0:D_MODEL] = _dot(perm, h2_ref[...]).astype(BF16)
    w_sorted = (jnp.where(hit_a, wa_r, 0.0) + jnp.where(hit_b, wb_r, 0.0)).sum(axis=-1, keepdims=True)
    w_hi = w_sorted.astype(BF16).astype(F32)
    r1 = w_sorted - w_hi
    w_mid = r1.astype(BF16).astype(F32)
    w_lo = r1 - w_mid
    lane = lax.broadcasted_iota(jnp.int32, (MOE_RLOC, LANES), 1)
    slab = jnp.where(lane == 0, w_hi, jnp.where(lane == 1, w_mid, jnp.where(lane == 2, w_lo, 0.0)))
    xs_ref[slot, 0:MOE_RLOC, D_MODEL:MOE_XW] = slab.astype(BF16)
    start_tile(i, slot)

    @pl.when(i == n_tiles - 1)
    def _():
        if n_tiles > 1:
            wait_tile(i - 1, 1 - slot)
        wait_tile(i, slot)


def _dispatch_call(h2, route, tabs, *, tm, n_rows):
    n = h2.shape[0]
    n_tiles = n // tm
    grid_spec = pltpu.PrefetchScalarGridSpec(
        num_scalar_prefetch=5,
        grid=(n_tiles,),
        in_specs=[pl.BlockSpec((tm, D_MODEL), lambda i, *_: (i, 0)),
                  pl.BlockSpec((tm, LANES), lambda i, *_: (i, 0)),
                  pl.BlockSpec((1, HALO, LANES), lambda i, *_: (i, 0, 0))],
        out_specs=pl.BlockSpec(memory_space=pl.ANY),
        scratch_shapes=[pltpu.VMEM((3, MOE_RLOC + MOE_CHUNK, MOE_XW), BF16),
                        pltpu.SemaphoreType.DMA((3,))],
    )
    return pl.pallas_call(
        functools.partial(_dispatch_kernel, tm=tm, n_tiles=n_tiles, n_work=n_rows // MOE_WORK),
        grid_spec=grid_spec,
        out_shape=jax.ShapeDtypeStruct((n_rows, MOE_XW), BF16),
        compiler_params=pltpu.CompilerParams(dimension_semantics=("arbitrary",),
                                             vmem_limit_bytes=VMEM_LIMIT),
        name="moe_dispatch",
    )(tabs["nch"], tabs["dst"], tabs["pad_lo"], tabs["pad_hi"], tabs["n_used"], h2, route, tabs["loc"])


def _expert_kernel(te_ref, nused_ref, x_ref, wup_ref, wdn_ref, y_ref):
    used = pl.program_id(0) < nused_ref[0]

    @pl.when(used)
    def _():
        wt = x_ref[:, D_MODEL:MOE_XW].astype(F32).sum(axis=-1, keepdims=True)
        gu = _dot(x_ref[:, 0:D_MODEL], wup_ref[0])
        g = gu[:, 0:D_EXPERT]
        a = (g * jax.nn.sigmoid(g)) * gu[:, D_EXPERT:2 * D_EXPERT] * wt
        y_ref[...] = _dot(a.astype(BF16), wdn_ref[0]).astype(BF16)

    @pl.when(jnp.logical_not(used))
    def _():
        y_ref[...] = jnp.zeros(y_ref.shape, BF16)


def _expert_call(xbuf, tabs, lw):
    n_rows = xbuf.shape[0]
    n_work = n_rows // MOE_WORK
    grid_spec = pltpu.PrefetchScalarGridSpec(
        num_scalar_prefetch=2,
        grid=(n_work,),
        in_specs=[pl.BlockSpec((MOE_WORK, MOE_XW), lambda w, te, nu: (jnp.minimum(w, nu[0] - 1), 0)),
                  pl.BlockSpec((1, D_MODEL, 2 * D_EXPERT), lambda w, te, nu: (te[w], 0, 0)),
                  pl.BlockSpec((1, D_EXPERT, D_MODEL), lambda w, te, nu: (te[w], 0, 0))],
        out_specs=pl.BlockSpec((MOE_WORK, D_MODEL), lambda w, te, nu: (w, 0)),
    )
    return pl.pallas_call(
        _expert_kernel,
        grid_spec=grid_spec,
        out_shape=jax.ShapeDtypeStruct((n_rows, D_MODEL), BF16),
        compiler_params=pltpu.CompilerParams(dimension_semantics=("arbitrary",),
                                             vmem_limit_bytes=VMEM_LIMIT),
        name="moe_experts",
    )(tabs["tile_expert"], tabs["n_used"], xbuf, lw["w_up"], lw["w_down"])


def _chunk_copy_in(ybuf_ref, ys_ref, sem_ref, slot, src_chunk, dst_chunk):
    return pltpu.make_async_copy(ybuf_ref.at[pl.ds(src_chunk * MOE_CHUNK, MOE_CHUNK), :],
                                 ys_ref.at[slot, pl.ds(dst_chunk * MOE_CHUNK, MOE_CHUNK), :],
                                 sem_ref.at[slot])


def _combine_kernel(nch_ref, dst_ref, route_ref, loc_ref, x_ref, mod_ref, ybuf_ref, o_ref,
                    ys_ref, sem_ref, *, tm, n_tiles):
    i = pl.program_id(0)
    slot = i % 2

    def start_tile(t_idx, slot_):
        def body(j, carry):
            _chunk_copy_in(ybuf_ref, ys_ref, sem_ref, slot_, dst_ref[t_idx * MOE_NCH + j], j).start()
            return carry
        lax.fori_loop(0, nch_ref[t_idx], body, 0)

    @pl.when(i == 0)
    def _():
        ys_ref[...] = jnp.zeros(ys_ref.shape, BF16)
        start_tile(0, 0)

    @pl.when(i + 1 < n_tiles)
    def _():
        start_tile(i + 1, 1 - slot)

    def wait_body(j, carry):
        _chunk_copy_in(ybuf_ref, ys_ref, sem_ref, slot, 0, 0).wait()
        return carry
    lax.fori_loop(0, nch_ref[i], wait_body, 0)

    _, _, d_a, d_b = _segment_positions(route_ref[...], loc_ref[0, 0:1, :], tm)
    lio = lax.broadcasted_iota(jnp.int32, (tm, MOE_RLOC), 1).astype(F32)
    perm_t = jnp.where((lio == d_a) | (lio == d_b), 1.0, 0.0).astype(BF16)
    y = _dot(perm_t, ys_ref[slot])
    o_ref[...] = x_ref[...] + mod_ref[0, 5:6, :] * y


def _combine_call(ybuf, route, x2d, mod, tabs, *, tm, mod_index):
    n = x2d.shape[0]
    n_tiles = n // tm
    grid_spec = pltpu.PrefetchScalarGridSpec(
        num_scalar_prefetch=2,
        grid=(n_tiles,),
        in_specs=[pl.BlockSpec((tm, LANES), lambda i, *_: (i, 0)),
                  pl.BlockSpec((1, HALO, LANES), lambda i, *_: (i, 0, 0)),
                  pl.BlockSpec((tm, D_MODEL), lambda i, *_: (i, 0)),
                  pl.BlockSpec((1, 6, D_MODEL), lambda i, *_: (mod_index(i), 0, 0)),
                  pl.BlockSpec(memory_space=pl.ANY)],
        out_specs=pl.BlockSpec((tm, D_MODEL), lambda i, *_: (i, 0)),
        scratch_shapes=[pltpu.VMEM((2, MOE_RLOC, D_MODEL), BF16),
                        pltpu.SemaphoreType.DMA((2,))],
    )
    return pl.pallas_call(
        functools.partial(_combine_kernel, tm=tm, n_tiles=n_tiles),
        grid_spec=grid_spec,
        out_shape=jax.ShapeDtypeStruct((n, D_MODEL), F32),
        compiler_params=pltpu.CompilerParams(dimension_semantics=("arbitrary",),
                                             vmem_limit_bytes=VMEM_LIMIT),
        name="moe_combine",
    )(tabs["nch"], tabs["dst"], route, tabs["loc"], x2d, mod, ybuf)


def _moe_rows(n_tok):
    n_tiles = n_tok // MOE_T
    chunks = 2 * n_tok // MOE_CHUNK + (MOE_CHUNK - 1) * n_tiles + N_EXPERTS * (MOE_CPW - 1)
    return -(-chunks // MOE_CPW) * MOE_WORK


def _route_tables(counts, n_tok):
    n_work = _moe_rows(n_tok) // MOE_WORK
    cnt = counts[:, 0, ROUTE_SEL_LANE:ROUTE_SEL_LANE + N_EXPERTS].astype(jnp.int32)
    n_tiles = cnt.shape[0]
    nchunk = (cnt + MOE_CHUNK - 1) // MOE_CHUNK
    ei = jnp.arange(N_EXPERTS, dtype=jnp.int32)
    ti = jnp.arange(n_tiles, dtype=jnp.int32)
    upto_e = (ei[:, None] <= ei[None, :]).astype(jnp.int32)
    upto_t = (ti[None, :] <= ti[:, None]).astype(jnp.int32)
    ends = (nchunk[:, :, None] * upto_e[None, :, :]).sum(axis=1)
    loc = ends - nchunk
    nch = nchunk.sum(axis=1)
    per_e = nchunk.sum(axis=0)
    padded_e = (per_e + MOE_CPW - 1) // MOE_CPW * MOE_CPW
    e_end = (padded_e[:, None] * upto_e).sum(axis=0)
    e_base = e_end - padded_e
    glob = e_base[None, :] + (upto_t[:, :, None] * nchunk[None, :, :]).sum(axis=1) - nchunk
    j = jnp.arange(MOE_NCH, dtype=jnp.int32)
    e_of_j = (ends[:, None, :] <= j[None, :, None]).sum(axis=-1)
    seg = e_of_j[:, :, None] == ei[None, None, :]
    dst = jnp.where(seg, (glob - loc)[:, None, :], 0).sum(axis=-1) + j[None, :]
    dst = jnp.where(j[None, :] < nch[:, None], dst, 0)
    n_used = padded_e.sum() // MOE_CPW
    w = jnp.minimum(jnp.arange(n_work, dtype=jnp.int32), n_used - 1)
    tile_expert = jnp.minimum(((e_end // MOE_CPW)[None, :] <= w[:, None]).sum(axis=-1), N_EXPERTS - 1)
    loc_rows = jnp.pad((loc * MOE_CHUNK).astype(F32),
                       ((0, 0), (ROUTE_SEL_LANE, LANES - ROUTE_SEL_LANE - N_EXPERTS)))
    loc_rows = jnp.broadcast_to(loc_rows[:, None, :], (n_tiles, HALO, LANES))
    return {"nch": nch.astype(jnp.int32), "dst": dst.reshape(-1).astype(jnp.int32),
            "pad_lo": (e_base + per_e).astype(jnp.int32), "pad_hi": e_end.astype(jnp.int32),
            "tile_expert": tile_expert.astype(jnp.int32), "n_used": n_used.astype(jnp.int32).reshape(1),
            "loc": loc_rows}


def _moe_call(h2, route, counts, x2d, mod, lw, *, mod_index):
    n = x2d.shape[0]
    tabs = _route_tables(counts, n)
    xbuf = _dispatch_call(h2, route, tabs, tm=MOE_T, n_rows=_moe_rows(n))
    ybuf = _expert_call(xbuf, tabs, lw)
    return _combine_call(ybuf, route, x2d, mod, tabs, tm=MOE_T, mod_index=mod_index)


def _ones_blocks(sizes):
    n = sum(sizes)
    m = np.zeros((n, n), np.float32)
    o = 0
    for s in sizes:
        m[o:o + s, o:o + s] = 1.0
        o += s
    return jnp.asarray(m, BF16)


def _rope_tables(rows):
    r = jnp.repeat(jnp.arange(rows, dtype=F32), GRID_W)
    c = jnp.tile(jnp.arange(GRID_W, dtype=F32), rows)

    def tab(rot_dim):
        nf = rot_dim // 4
        inv = ROPE_THETA ** (-jnp.arange(nf, dtype=F32) / nf)
        ang = jnp.concatenate([r[:, None] * inv, c[:, None] * inv], axis=-1)
        cos = jnp.repeat(jnp.cos(ang), 2, axis=-1)
        sin = jnp.repeat(jnp.sin(ang), 2, axis=-1) * jnp.tile(jnp.asarray([-1.0, 1.0], F32), rot_dim // 2)
        return cos, sin

    s = rows * GRID_W
    cc, sc = tab(C_HEAD_DIM)
    cd, sd = tab(D_ROPE)
    one = lambda w: jnp.ones((s, w), F32)
    zero = lambda w: jnp.zeros((s, w), F32)
    return (jnp.tile(cc, (1, 2)), jnp.tile(sc, (1, 2)),
            jnp.concatenate([one(64), cd, one(32)], axis=1), jnp.concatenate([zero(64), sd, zero(32)], axis=1),
            jnp.concatenate([cd, one(96)], axis=1), jnp.concatenate([sd, zero(96)], axis=1))


def _layer_weights(l, p):
    (g_attn_norm, w_in, sgu_v_gain, sgu_w, sgu_b, conv_w, gqa_q_gain, gqa_k_gain, mla_qa_gain, mla_w_qb,
     mla_kva_gain, mla_w_kvb, mla_q_gain, mla_k_gain, g_group_out, w_out, g_ffn_norm,
     w_rg, b_rg, w_re, b_re, w_up, w_down) = [a[l] for a in p]
    def heads_0213(a, axis, off):
        sl = lambda lo: lax.slice_in_dim(a, off + lo, off + lo + 64, axis=axis)
        return [sl(0), sl(128), sl(64), sl(192)]

    wi = w_in
    w_in_p = jnp.concatenate([wi[:, :1280]] + heads_0213(wi, 1, 1280) +
                             [wi[:, 1536:2208], jnp.zeros((D_MODEL, IN_W - 2208), F32)], axis=1).astype(BF16)
    z32 = jnp.zeros((D_Q_LORA, 32), F32)
    wqb = jnp.concatenate(
        [jnp.concatenate([mla_w_qb[:, h * 96:h * 96 + 96], z32], axis=1) for h in range(D_HEADS)], axis=1)
    z64 = jnp.zeros((D_KV_LORA, 64), F32)
    wk = jnp.concatenate(
        [jnp.concatenate([mla_w_kvb[:, h * 128:h * 128 + 64], z64], axis=1) for h in range(D_HEADS)], axis=1)
    wv = jnp.concatenate([mla_w_kvb[:, h * 128 + 64:h * 128 + 128] for h in range(D_HEADS)], axis=1)
    qslab = jnp.concatenate([mla_q_gain, jnp.zeros((32,), F32)])
    kslab = jnp.concatenate([mla_k_gain[:64], jnp.zeros((64,), F32)])
    g_go = jnp.concatenate([g_group_out[:512]] + heads_0213(g_group_out, 0, 512) + [g_group_out[768:]])
    w_out_p = jnp.concatenate([w_out[:512]] + heads_0213(w_out, 0, 512) + [w_out[768:]], axis=0).astype(BF16)
    w_r = jnp.concatenate([w_re, w_rg], axis=1)
    w_r_hi = w_r.astype(BF16)
    w_r_lo = (w_r - w_r_hi.astype(F32)).astype(BF16)
    n_r = N_EXPERTS + N_GROUPS
    w_router = jnp.concatenate([w_r_hi, jnp.zeros((D_MODEL, ROUTE_SEL_LANE - n_r), BF16), w_r_lo,
                                jnp.zeros((D_MODEL, LANES - ROUTE_SEL_LANE - n_r), BF16)], axis=1)
    b_router = jnp.concatenate([b_re, b_rg, jnp.zeros((LANES - 20,), F32)])[None, :]
    sgub_full = jnp.repeat(sgu_b.T, 64, axis=1)
    return {
        "g_attn": g_attn_norm[None, :], "w_in": w_in_p,
        "sgu_w": sgu_w.reshape(A_HEADS * CHUNK, CHUNK).astype(BF16), "sgu_b": sgub_full,
        "v_gain": sgu_v_gain[None, :], "conv_w": conv_w, "g_go": g_go[None, :],
        "cq_gain": jnp.tile(gqa_q_gain, 4)[None, :], "ck_gain": jnp.tile(gqa_k_gain, 2)[None, :],
        "qa_gain": mla_qa_gain[None, :], "w_qb": wqb.astype(BF16), "kva_gain": mla_kva_gain[None, :],
        "w_kvb": jnp.concatenate([wk, wv], axis=1).astype(BF16),
        "dq_gain": jnp.tile(qslab, 4)[None, :], "dk_gain": jnp.tile(kslab, 4)[None, :],
        "dkr_gain": jnp.concatenate([mla_k_gain[64:], jnp.zeros((96,), F32)])[None, :],
        "b64": _ones_blocks([64] * 4), "bq": _ones_blocks([64, 32, 32] * 2),
        "invq": jnp.tile(jnp.concatenate([jnp.full((64,), 1 / 64.0, F32), jnp.full((64,), 1 / 32.0, F32)]),
                         2)[None, :],
        "w_out": w_out_p, "g_ffn": g_ffn_norm[None, :], "w_router": w_router, "b_router": b_router,
        "w_up": w_up.astype(BF16), "w_down": w_down.astype(BF16),
    }


def kernel(x, c, ctx, c_ctx, w_mod, b_mod, g_attn_norm, w_in, sgu_v_gain, sgu_w, sgu_b, conv_w, gqa_q_gain,
           gqa_k_gain, mla_qa_gain, mla_w_qb, mla_kva_gain, mla_w_kvb, mla_q_gain, mla_k_gain, g_group_out,
           w_out, g_ffn_norm, w_router_group, b_router_group, w_router_expert, b_router_expert,
           w_expert_up, w_expert_down):
    b, s, d = x.shape
    lc = ctx.shape[1]
    depth = w_mod.shape[0]
    params = (g_attn_norm, w_in, sgu_v_gain, sgu_w, sgu_b, conv_w, gqa_q_gain, gqa_k_gain, mla_qa_gain,
              mla_w_qb, mla_kva_gain, mla_w_kvb, mla_q_gain, mla_k_gain, g_group_out, w_out, g_ffn_norm,
              w_router_group, b_router_group, w_router_expert, b_router_expert, w_expert_up, w_expert_down)

    c_all = jnp.concatenate([c, c_ctx[None, :], jnp.zeros((16 - b - 1, d), F32)], axis=0)
    mod_all = _modulation(c_all, w_mod, b_mod).reshape(depth, 16, 6, d)
    rope = _rope_tables(s // GRID_W)

    tm = 512
    tq = 512
    tpb = s // tm
    lat_idx = lambda i: i // tpb
    ctx_idx = lambda i: b

    xl = x.reshape(b * s, d)
    xc = ctx.reshape(b * lc, d)
    for l in range(depth):
        lw = _layer_weights(l, params)
        mod = mod_all[l]
        update_ctx = l < depth - 1
        ya_c, yb_c, qc_c, kc_c, vc_c, qd_c, kd_c, vd_c = _in_call(
            xc, mod, lw, None, tm=tm, seq_len=lc, mod_index=ctx_idx)
        ya, yb, qc, kc, vc, qd, kd, vd = _in_call(xl, mod, lw, rope, tm=tm, seq_len=s, mod_index=lat_idx)
        r3 = lambda a, n: a.reshape(b, n, a.shape[-1])
        g_c = lw["g_go"][:, 512:768]
        g_d = lw["g_go"][:, 768:1024]
        yc = _attn_call(r3(qc, s), [r3(kc_c, lc), r3(kc, s)], [r3(vc_c, lc), r3(vc, s)], g_c, mode="C", tq=tq)
        yd = _attn_call(r3(qd, s), [r3(kd_c, lc), r3(kd, s)], [r3(vd_c, lc), r3(vd, s)], g_d, mode="D", tq=tq)
        xl, h2, route, counts = _merge_call(xl, ya, yb, yc.reshape(b * s, GROUP_W), yd.reshape(b * s, GROUP_W),
                                            mod, lw, tm=tm, mod_index=lat_idx)
        xl = _moe_call(h2, route, counts, xl, mod, lw, mod_index=lat_idx)
        if update_ctx:
            yc_c = _attn_call(r3(qc_c, lc), [r3(kc_c, lc)], [r3(vc_c, lc)], g_c, mode="C", tq=lc)
            yd_c = _attn_call(r3(qd_c, lc), [r3(kd_c, lc)], [r3(vd_c, lc)], g_d, mode="D", tq=lc)
            xc, h2c, route_c, counts_c = _merge_call(xc, ya_c, yb_c, yc_c.reshape(b * lc, GROUP_W),
                                                     yd_c.reshape(b * lc, GROUP_W), mod, lw, tm=tm,
                                                     mod_index=ctx_idx)
            xc = _moe_call(h2c, route_c, counts_c, xc, mod, lw, mod_index=ctx_idx)
    return xl.reshape(b, s, d)
```

```python
import functools

import numpy as np
import jax
import jax.numpy as jnp
from jax import lax
from jax.experimental import pallas as pl
from jax.experimental.pallas import tpu as pltpu

F32 = jnp.float32
BF16 = jnp.bfloat16

D_MODEL = 1024
GRID_W = 64
EPS = 1e-6
ROPE_THETA = 10000.0
GROUP_W = 256
CHUNK = 128
A_HEADS = 4
C_HEAD_DIM = 64
C_SCALE = C_HEAD_DIM ** -0.5
D_HEADS = 4
D_NOPE = 64
D_ROPE = 32
D_V = 64
D_Q_LORA = 256
D_KV_LORA = 128
D_SCALE = (D_NOPE + D_ROPE) ** -0.5
LOG2E = 1.4426950408889634
ATTN_ROWS = 16
ATTN_KSPLIT = 1024
N_GROUPS = 4
EXPERTS_PER_GROUP = 4
N_EXPERTS = 16
D_EXPERT = 256
LANES = 128
HALO = 8

IN_W = 2304

ROUTE_SEL_LANE = 32
MOE_T = 512
MOE_CHUNK = 16
MOE_RLOC = 1280
MOE_NCH = MOE_RLOC // MOE_CHUNK
MOE_XW = D_MODEL + LANES
MOE_WORK = 512
MOE_CPW = MOE_WORK // MOE_CHUNK
VMEM_LIMIT = 56 * 1024 * 1024


def _nt_dot(a, b):
    return lax.dot_general(a, b, (((1,), (1,)), ((), ())), preferred_element_type=F32)


def _dot(a, b):
    return jnp.dot(a, b, preferred_element_type=F32)


def _rms_full(x, gain):
    return x * lax.rsqrt(jnp.mean(x * x, axis=-1, keepdims=True) + EPS) * gain


def _group_ms(x, ones_blocks, inv_count):
    return _dot((x * x).astype(BF16), ones_blocks) * inv_count


def _pair_swap(x, even):
    nxt = pltpu.roll(x, x.shape[1] - 1, axis=1)
    prv = pltpu.roll(x, 1, axis=1)
    return jnp.where(even, nxt, prv)


def _rope(x, cos, sin, even):
    return x * cos + _pair_swap(x, even) * sin


def _mod_kernel(c_ref, w_ref, b_ref, o_ref):
    cv = c_ref[...]
    sc = cv * jax.nn.sigmoid(cv)
    o_ref[0] = jnp.dot(sc, w_ref[0], preferred_element_type=F32,
                       precision=lax.Precision.HIGHEST) + b_ref[0]


def _modulation(c_all, w_mod, b_mod):
    depth = w_mod.shape[0]
    nblk = 4
    wblk = 6 * D_MODEL // nblk
    return pl.pallas_call(
        _mod_kernel,
        grid=(depth, nblk),
        in_specs=[pl.BlockSpec((16, D_MODEL), lambda l, j: (0, 0)),
                  pl.BlockSpec((1, D_MODEL, wblk), lambda l, j: (l, 0, j)),
                  pl.BlockSpec((1, 1, wblk), lambda l, j: (l, 0, j))],
        out_specs=pl.BlockSpec((1, 16, wblk), lambda l, j: (l, 0, j)),
        out_shape=jax.ShapeDtypeStruct((depth, 16, 6 * D_MODEL), F32),
        compiler_params=pltpu.CompilerParams(dimension_semantics=("arbitrary", "arbitrary"),
                                             vmem_limit_bytes=VMEM_LIMIT),
        name="mod",
    )(c_all, w_mod, b_mod.reshape(depth, 1, 6 * D_MODEL))


def _in_kernel(*refs, tm, seq_len, use_rope):
    (x_ref, xp_ref, xn_ref, mod_ref, gattn_ref, win_ref, sguw_ref, sgub_ref, vgain_ref, convw_ref,
     ggo_ref, cqg_ref, ckg_ref, qag_ref, wqb_ref, kvag_ref, wkvb_ref, dqg_ref, dkg_ref, dkrg_ref,
     b64_ref, bq_ref, invq_ref) = refs[:23]
    pos = 23
    if use_rope:
        cosc_ref, sinc_ref, cosq_ref, sinq_ref, cosk_ref, sink_ref = refs[pos:pos + 6]
        pos += 6
    ya_ref, yb_ref, qc_ref, kc_ref, vc_ref, qd_ref, kd_ref, vd_ref = refs[pos:pos + 8]

    i = pl.program_id(0)
    shift = mod_ref[0, 0:1, :]
    scale = mod_ref[0, 1:2, :]
    gattn = gattn_ref[...]

    def prenorm(xv):
        return (_rms_full(xv, gattn) * (1.0 + scale) + shift).astype(BF16)

    hb = prenorm(x_ref[...])
    b64 = b64_ref[...]
    inv64 = 1.0 / 64.0
    lane128 = lax.broadcasted_iota(jnp.int32, (tm, LANES), 1)
    even = (lane128 & 1) == 0

    u = jax.nn.gelu(_dot(hb, win_ref[:, 0:256]))
    v = jax.nn.gelu(_dot(hb, win_ref[:, 256:512]))
    v = (v * lax.rsqrt(_group_ms(v, b64, inv64) + EPS) * vgain_ref[...]).astype(BF16)
    lane_head = lax.broadcasted_iota(jnp.int32, (CHUNK, GROUP_W), 1) // 64
    sguw = sguw_ref[...]
    mixed_chunks = []
    for ci in range(tm // CHUNK):
        r = _dot(sguw, v[ci * CHUNK:(ci + 1) * CHUNK, :])
        m = r[0:CHUNK]
        for h in range(1, A_HEADS):
            m = jnp.where(lane_head == h, r[h * CHUNK:(h + 1) * CHUNK], m)
        mixed_chunks.append(m + sgub_ref[...])
    ya = u * jnp.concatenate(mixed_chunks, axis=0)
    ya_ref[...] = _rms_full(ya, ggo_ref[:, 0:256]).astype(BF16)

    bg = _dot(hb, win_ref[:, 512:768])
    z = _dot(hb, win_ref[:, 768:1024]) * _dot(hb, win_ref[:, 1024:1280])
    hh = prenorm(jnp.concatenate([xp_ref[...], xn_ref[...]], axis=0))
    zh = _dot(hh, win_ref[:, 768:1024]) * _dot(hh, win_ref[:, 1024:1280])
    row = lax.broadcasted_iota(jnp.int32, (tm, GROUP_W), 0)
    posn = (i * tm + row) % seq_len
    z_prev = jnp.where(row == 0, zh[HALO - 1:HALO, :], pltpu.roll(z, 1, axis=0))
    z_prev = jnp.where(posn == 0, 0.0, z_prev)
    z_next = jnp.where(row == tm - 1, zh[HALO:HALO + 1, :], pltpu.roll(z, tm - 1, axis=0))
    z_next = jnp.where(posn == seq_len - 1, 0.0, z_next)
    yb = bg * (convw_ref[0:1, :] * z_prev + convw_ref[1:2, :] * z + convw_ref[2:3, :] * z_next)
    yb_ref[...] = _rms_full(yb, ggo_ref[:, 256:512]).astype(BF16)

    q = _dot(hb, win_ref[:, 1280:1536])
    q = q * lax.rsqrt(_group_ms(q, b64, inv64) + EPS) * cqg_ref[...]
    k = _dot(hb, win_ref[:, 1536:1664])
    k = k * lax.rsqrt(_group_ms(k, b64[0:128, 0:128], inv64) + EPS) * ckg_ref[...]
    if use_rope:
        cosc, sinc = cosc_ref[...], sinc_ref[...]
        q = jnp.concatenate([_rope(q[:, 0:128], cosc, sinc, even),
                             _rope(q[:, 128:256], cosc, sinc, even)], axis=1)
        k = _rope(k, cosc, sinc, even)
    qc_ref[...] = (q * (C_SCALE * LOG2E)).astype(BF16)
    kc_ref[...] = k.astype(BF16)
    vc_ref[...] = _dot(hb, win_ref[:, 1664:1792]).astype(BF16)

    cq = _rms_full(_dot(hb, win_ref[:, 1792:2048]), qag_ref[...]).astype(BF16)
    qd = _dot(cq, wqb_ref[...])
    bq = bq_ref[...]
    invq = invq_ref[...]
    halves = []
    for hf in range(2):
        qh = qd[:, hf * 256:(hf + 1) * 256]
        qh = qh * lax.rsqrt(_group_ms(qh, bq, invq) + EPS) * dqg_ref[:, hf * 256:(hf + 1) * 256]
        if use_rope:
            cosq, sinq = cosq_ref[...], sinq_ref[...]
            qh = jnp.concatenate([_rope(qh[:, 0:128], cosq, sinq, even),
                                  _rope(qh[:, 128:256], cosq, sinq, even)], axis=1)
        halves.append(qh)
    qd_ref[...] = (jnp.concatenate(halves, axis=1) * (D_SCALE * LOG2E)).astype(BF16)

    ckv = _rms_full(_dot(hb, win_ref[:, 2048:2176]), kvag_ref[...]).astype(BF16)
    kv = _dot(ckv, wkvb_ref[...])
    vd_ref[...] = kv[:, 512:768].astype(BF16)
    kr = _dot(hb, win_ref[:, 2176:2304])
    kr = kr * lax.rsqrt(jnp.sum(kr * kr, axis=-1, keepdims=True) * (1.0 / D_ROPE) + EPS) * dkrg_ref[...]
    if use_rope:
        kr = _rope(kr, cosk_ref[...], sink_ref[...], even)
    kr = pltpu.roll(kr, D_NOPE, axis=1)
    for hf in range(2):
        kh = kv[:, hf * 256:(hf + 1) * 256]
        kh = kh * lax.rsqrt(_group_ms(kh, b64, inv64) + EPS) * dkg_ref[:, hf * 256:(hf + 1) * 256]
        kd_ref[:, hf * 256:hf * 256 + 128] = (kh[:, 0:128] + kr).astype(BF16)
        kd_ref[:, hf * 256 + 128:(hf + 1) * 256] = (kh[:, 128:256] + kr).astype(BF16)


def _layer_spec(a, layer):
    zeros = (0,) * (a.ndim - 1)
    return pl.BlockSpec((None,) + a.shape[1:], lambda *_: (layer,) + zeros)


def _mod_spec(layer, mod_index):
    return pl.BlockSpec((None, 1, 6, D_MODEL), lambda i, *_: (layer, mod_index(i), 0, 0))


def _in_call(x2d, mod, lw, layer, rope, *, tm, seq_len, mod_index):
    n = x2d.shape[0]
    nt = n // tm
    nb8 = n // HALO
    tb = tm // HALO
    use_rope = rope is not None
    row = lambda i: (i, 0)
    in_specs = [
        pl.BlockSpec((tm, D_MODEL), row),
        pl.BlockSpec((HALO, D_MODEL), lambda i: (jnp.maximum(i * tb - 1, 0), 0)),
        pl.BlockSpec((HALO, D_MODEL), lambda i: (jnp.minimum((i + 1) * tb, nb8 - 1), 0)),
        _mod_spec(layer, mod_index),
    ]
    args = [x2d, x2d, x2d, mod]
    for name in ("g_attn", "w_in", "sgu_w", "sgu_b", "v_gain", "conv_w", "g_go", "cq_gain", "ck_gain",
                 "qa_gain", "w_qb", "kva_gain", "w_kvb", "dq_gain", "dk_gain", "dkr_gain"):
        in_specs.append(_layer_spec(lw[name], layer))
        args.append(lw[name])
    for a in (_ones_blocks([64] * 4), _ones_blocks([64, 32, 32] * 2), _inv_counts_q()):
        in_specs.append(pl.BlockSpec(a.shape, lambda i: (0, 0)))
        args.append(a)
    if use_rope:
        ntab = seq_len // tm
        for a in rope:
            in_specs.append(pl.BlockSpec((tm, LANES), lambda i: (i % ntab, 0)))
            args.append(a)
    widths = (256, 256, 256, 128, 128, 512, 512, 256)
    out_specs = [pl.BlockSpec((tm, w), row) for w in widths]
    out_shape = [jax.ShapeDtypeStruct((n, w), BF16) for w in widths]
    return pl.pallas_call(
        functools.partial(_in_kernel, tm=tm, seq_len=seq_len, use_rope=use_rope),
        grid=(nt,),
        in_specs=in_specs,
        out_specs=out_specs,
        out_shape=out_shape,
        compiler_params=pltpu.CompilerParams(dimension_semantics=("parallel",),
                                             vmem_limit_bytes=VMEM_LIMIT),
        name="in_proj",
    )(*args)


def _attn_kernel(*refs, mode, nseg, tq):
    q_ref = refs[0]
    k_refs = refs[1:1 + nseg]
    v_refs = refs[1 + nseg:1 + 2 * nseg]
    gain_ref = refs[1 + 2 * nseg]
    o_ref = refs[2 + 2 * nseg]
    p_ref = refs[3 + 2 * nseg]
    nks = [r.shape[1] for r in k_refs]
    offs = [sum(nks[:j]) for j in range(nseg)]
    n_chunks = tq // ATTN_ROWS

    vs = [r[0] for r in v_refs]
    if mode == "C":
        ks_c = [r[0] for r in k_refs]
        first_half = lax.broadcasted_iota(jnp.int32, (tq, LANES), 1) < 64
        heads = []
        for sl in range(2):
            qs = q_ref[0, :, sl * 128:(sl + 1) * 128]
            heads.append((jnp.where(first_half, qs, jnp.zeros_like(qs)), ks_c))
            heads.append((jnp.where(first_half, jnp.zeros_like(qs), qs), ks_c))
    else:
        heads = [(q_ref[0, :, h * 128:(h + 1) * 128], [r[0, :, h * 128:(h + 1) * 128] for r in k_refs])
                 for h in range(D_HEADS)]

    def score(h):
        qm, ks = heads[h]
        return [_nt_dot(qm, kk) for kk in ks]

    def softmax(h, ss):
        sums = []
        for c in range(n_chunks):
            rows = slice(c * ATTN_ROWS, (c + 1) * ATTN_ROWS)
            m = ss[0][rows].max(axis=-1, keepdims=True)
            for s in ss[1:]:
                m = jnp.maximum(m, s[rows].max(axis=-1, keepdims=True))
            tot = None
            for s, off, nk in zip(ss, offs, nks):
                p = jnp.exp2(s[rows] - m).astype(BF16)
                ps = p.astype(F32).sum(axis=-1, keepdims=True)
                tot = ps if tot is None else tot + ps
                p_ref[h, rows, off:off + nk] = p
            sums.append(tot)
        return jnp.concatenate(sums, axis=0)

    def values(h, row_sum):
        o = None
        for off, nk, vv in zip(offs, nks, vs):
            step = min(nk, ATTN_KSPLIT)
            for k0 in range(0, nk, step):
                part = _dot(p_ref[h, :, off + k0:off + k0 + step], vv[k0:k0 + step])
                o = part if o is None else o + part
        return o / row_sum

    n_heads = len(heads)
    scores = {0: score(0), 1: score(1)}
    row_sums = {}
    outs = []
    for h in range(n_heads):
        row_sums[h] = softmax(h, scores.pop(h))
        if h + 2 < n_heads:
            scores[h + 2] = score(h + 2)
        if h >= 1:
            outs.append(values(h - 1, row_sums.pop(h - 1)))
    outs.append(values(n_heads - 1, row_sums.pop(n_heads - 1)))

    if mode == "C":
        out = jnp.concatenate([jnp.where(first_half, outs[0], outs[1]),
                               jnp.where(first_half, outs[2], outs[3])], axis=1)
    else:
        lane_head = lax.broadcasted_iota(jnp.int32, (tq, GROUP_W), 1) // D_V
        out = outs[0]
        for h in range(1, D_HEADS):
            out = jnp.where(lane_head == h, outs[h], out)
    o_ref[0] = _rms_full(out, gain_ref[...]).astype(BF16)


def _attn_call(q, ks, vs, g_go, layer, *, mode, tq):
    b, s, wq = q.shape
    nseg = len(ks)
    gain_block = 2 if mode == "C" else 3
    in_specs = [pl.BlockSpec((1, tq, wq), lambda bi, qi: (bi, qi, 0))]
    for a in list(ks) + list(vs):
        in_specs.append(pl.BlockSpec((1,) + a.shape[1:], lambda bi, qi: (bi, 0, 0)))
    in_specs.append(pl.BlockSpec((None, 1, GROUP_W), lambda bi, qi: (layer, 0, gain_block)))
    return pl.pallas_call(
        functools.partial(_attn_kernel, mode=mode, nseg=nseg, tq=tq),
        grid=(b, s // tq),
        in_specs=in_specs,
        out_specs=pl.BlockSpec((1, tq, GROUP_W), lambda bi, qi: (bi, qi, 0)),
        out_shape=jax.ShapeDtypeStruct((b, s, GROUP_W), BF16),
        scratch_shapes=[pltpu.VMEM((4, tq, sum(a.shape[1] for a in ks)), BF16)],
        compiler_params=pltpu.CompilerParams(dimension_semantics=("parallel", "parallel"),
                                             vmem_limit_bytes=VMEM_LIMIT),
        name="attn_" + mode,
    )(q, *ks, *vs, g_go)


def _merge_kernel(x_ref, ya_ref, yb_ref, yc_ref, yd_ref, mod_ref, wout_ref, gffn_ref, wr_ref, br_ref,
                  xo_ref, h2_ref, route_ref, cnt_ref, *, tm):
    y = _dot(ya_ref[...], wout_ref[0:256, :])
    y = y + _dot(yb_ref[...], wout_ref[256:512, :])
    y = y + _dot(yc_ref[...], wout_ref[512:768, :])
    y = y + _dot(yd_ref[...], wout_ref[768:1024, :])
    xn = x_ref[...] + mod_ref[0, 2:3, :] * y
    xo_ref[...] = xn
    h2 = _rms_full(xn, gffn_ref[...]) * (1.0 + mod_ref[0, 4:5, :]) + mod_ref[0, 3:4, :]
    h2b = h2.astype(BF16)
    h2_ref[...] = h2b

    h2lo = (h2 - h2b.astype(F32)).astype(BF16)
    part = _dot(h2b, wr_ref[...]) + _dot(h2lo, wr_ref[...])
    logits = part + pltpu.roll(part, LANES - ROUTE_SEL_LANE, axis=1) + br_ref[...]
    lane = lax.broadcasted_iota(jnp.int32, (tm, LANES), 1)
    lanef = lane.astype(F32)
    neg = jnp.float32(-1e30)
    big = jnp.float32(1e9)
    is_g = (lane >= N_EXPERTS) & (lane < N_EXPERTS + N_GROUPS)
    lg = jnp.where(is_g, logits, neg)
    gmax = lg.max(axis=-1, keepdims=True)
    g_idx = jnp.where(is_g & (lg == gmax), lanef, big).min(axis=-1, keepdims=True) - N_EXPERTS
    g_w = 1.0 / jnp.where(is_g, jnp.exp(lg - gmax), 0.0).sum(axis=-1, keepdims=True)
    in_grp = (lane < N_EXPERTS) & ((lane // EXPERTS_PER_GROUP).astype(F32) == g_idx)
    le = jnp.where(in_grp, logits, neg)
    m1 = le.max(axis=-1, keepdims=True)
    i1 = jnp.where(in_grp & (le == m1), lanef, big).min(axis=-1, keepdims=True)
    rest = in_grp & (lanef != i1)
    le2 = jnp.where(rest, logits, neg)
    m2 = le2.max(axis=-1, keepdims=True)
    i2 = jnp.where(rest & (le2 == m2), lanef, big).min(axis=-1, keepdims=True)
    r = jnp.exp(m2 - m1)
    w1 = 1.0 / (1.0 + r)
    w2 = r / (1.0 + r)
    cmb = g_w * jnp.where(lanef == i1, w1, jnp.where(lanef == i2, w2, 0.0))
    sel = pltpu.roll(jnp.where((lanef == i1) | (lanef == i2), 1.0, 0.0), ROUTE_SEL_LANE, axis=1)
    route_ref[...] = cmb + sel
    cnt_ref[0] = jnp.broadcast_to(sel.sum(axis=0, keepdims=True), (HALO, LANES))


def _merge_call(x2d, ya, yb, yc, yd, mod, lw, layer, *, tm, mod_index):
    n = x2d.shape[0]
    row = lambda i: (i, 0)
    in_specs = [pl.BlockSpec((tm, D_MODEL), row)]
    in_specs += [pl.BlockSpec((tm, GROUP_W), row)] * 4
    in_specs += [_mod_spec(layer, mod_index)]
    in_specs += [_layer_spec(lw[k], layer) for k in ("w_out", "g_ffn", "w_router", "b_router")]
    return pl.pallas_call(
        functools.partial(_merge_kernel, tm=tm),
        grid=(n // tm,),
        in_specs=in_specs,
        out_specs=[pl.BlockSpec((tm, D_MODEL), row), pl.BlockSpec((tm, D_MODEL), row),
                   pl.BlockSpec((tm, LANES), row), pl.BlockSpec((1, HALO, LANES), lambda i: (i, 0, 0))],
        out_shape=[jax.ShapeDtypeStruct((n, D_MODEL), F32), jax.ShapeDtypeStruct((n, D_MODEL), BF16),
                   jax.ShapeDtypeStruct((n, LANES), F32),
                   jax.ShapeDtypeStruct((n // tm, HALO, LANES), F32)],
        compiler_params=pltpu.CompilerParams(dimension_semantics=("parallel",),
                                             vmem_limit_bytes=VMEM_LIMIT),
        name="merge",
    )(x2d, ya, yb, yc, yd, mod, lw["w_out"], lw["g_ffn"], lw["w_router"], lw["b_router"])


def _segment_positions(route, loc, tm):
    lane = lax.broadcasted_iota(jnp.int32, (tm, LANES), 1)
    is_sel = (lane >= ROUTE_SEL_LANE) & (lane < ROUTE_SEL_LANE + N_EXPERTS)
    selm = is_sel & (route > 0.5)
    selb = jnp.where(selm, 1.0, 0.0).astype(BF16)
    r_i = lax.broadcasted_iota(jnp.int32, (tm, tm), 0)
    c_i = lax.broadcasted_iota(jnp.int32, (tm, tm), 1)
    earlier = jnp.where(c_i < r_i, 1.0, 0.0).astype(BF16)
    dest = loc + _dot(earlier, selb)
    big = jnp.float32(1e9)
    d_a = jnp.where(selm, dest, big).min(axis=-1, keepdims=True)
    d_b = jnp.where(selm, dest, -big).max(axis=-1, keepdims=True)
    return selm, dest, d_a, d_b


def _to_row(col, tm):
    r_i = lax.broadcasted_iota(jnp.int32, (tm, tm), 0)
    c_i = lax.broadcasted_iota(jnp.int32, (tm, tm), 1)
    return jnp.where(r_i == c_i, col, 0.0).sum(axis=0, keepdims=True)


def _chunk_copy_out(xs_ref, xbuf_ref, sem_ref, slot, src_chunk, dst_chunk):
    return pltpu.make_async_copy(xs_ref.at[slot, pl.ds(src_chunk * MOE_CHUNK, MOE_CHUNK), :],
                                 xbuf_ref.at[pl.ds(dst_chunk * MOE_CHUNK, MOE_CHUNK), :],
                                 sem_ref.at[slot])


def _dispatch_kernel(nch_ref, dst_ref, plo_ref, phi_ref, nused_ref, h2_ref, route_ref, loc_ref, xbuf_ref,
                     xs_ref, sem_ref, *, tm, n_tiles, n_work):
    i = pl.program_id(0)
    slot = i % 2
    zero_chunk = MOE_RLOC // MOE_CHUNK

    def start_tile(t_idx, slot_):
        def body(j, carry):
            _chunk_copy_out(xs_ref, xbuf_ref, sem_ref, slot_, j, dst_ref[t_idx * MOE_NCH + j]).start()
            return carry
        lax.fori_loop(0, nch_ref[t_idx], body, 0)

    def wait_tile(t_idx, slot_):
        def body(j, carry):
            _chunk_copy_out(xs_ref, xbuf_ref, sem_ref, slot_, 0, 0).wait()
            return carry
        lax.fori_loop(0, nch_ref[t_idx], body, 0)

    @pl.when(i == 0)
    def _():
        xs_ref[2, :, :] = jnp.zeros((MOE_RLOC + MOE_CHUNK, MOE_XW), BF16)

        def pad_start(j, carry):
            _chunk_copy_out(xs_ref, xbuf_ref, sem_ref, 2, zero_chunk, j).start()
            return carry

        def pad_wait(j, carry):
            _chunk_copy_out(xs_ref, xbuf_ref, sem_ref, 2, zero_chunk, 0).wait()
            return carry
        for e in range(N_EXPERTS):
            lax.fori_loop(plo_ref[e], phi_ref[e], pad_start, 0)
        for e in range(N_EXPERTS):
            lax.fori_loop(plo_ref[e], phi_ref[e], pad_wait, 0)

        def tail_copy(w):
            return pltpu.make_async_copy(xs_ref.at[2, pl.ds(0, MOE_WORK), :],
                                         xbuf_ref.at[pl.ds(w * MOE_WORK, MOE_WORK), :], sem_ref.at[2])

        def tail_start(w, carry):
            tail_copy(w).start()
            return carry

        def tail_wait(w, carry):
            tail_copy(w).wait()
            return carry
        lax.fori_loop(nused_ref[0], n_work, tail_start, 0)
        lax.fori_loop(nused_ref[0], n_work, tail_wait, 0)

    @pl.when(i >= 2)
    def _():
        wait_tile(i - 2, slot)

    route = route_ref[...]
    selm, dest, d_a, d_b = _segment_positions(route, loc_ref[0, 0:1, :], tm)
    cmb_al = pltpu.roll(route, ROUTE_SEL_LANE, axis=1)
    w_a = jnp.where(selm & (dest == d_a), cmb_al, 0.0).sum(axis=-1, keepdims=True)
    w_b = jnp.where(selm & (dest == d_b), cmb_al, 0.0).sum(axis=-1, keepdims=True)
    da_r, db_r, wa_r, wb_r = (_to_row(v, tm) for v in (d_a, d_b, w_a, w_b))
    rio = lax.broadcasted_iota(jnp.int32, (MOE_RLOC, tm), 0).astype(F32)
    hit_a = rio == da_r
    hit_b = rio == db_r
    perm = jnp.where(hit_a | hit_b, 1.0, 0.0).astype(BF16)
    xs_ref[slot, 0:MOE_RLOC, 0:D_MODEL] = _dot(perm, h2_ref[...]).astype(BF16)
    w_sorted = (jnp.where(hit_a, wa_r, 0.0) + jnp.where(hit_b, wb_r, 0.0)).sum(axis=-1, keepdims=True)
    w_hi = w_sorted.astype(BF16).astype(F32)
    r1 = w_sorted - w_hi
    w_mid = r1.astype(BF16).astype(F32)
    w_lo = r1 - w_mid
    lane = lax.broadcasted_iota(jnp.int32, (MOE_RLOC, LANES), 1)
    slab = jnp.where(lane == 0, w_hi, jnp.where(lane == 1, w_mid, jnp.where(lane == 2, w_lo, 0.0)))
    xs_ref[slot, 0:MOE_RLOC, D_MODEL:MOE_XW] = slab.astype(BF16)
    start_tile(i, slot)

    @pl.when(i == n_tiles - 1)
    def _():
        if n_tiles > 1:
            wait_tile(i - 1, 1 - slot)
        wait_tile(i, slot)


def _dispatch_call(h2, route, tabs, *, tm, n_rows):
    n = h2.shape[0]
    n_tiles = n // tm
    grid_spec = pltpu.PrefetchScalarGridSpec(
        num_scalar_prefetch=5,
        grid=(n_tiles,),
        in_specs=[pl.BlockSpec((tm, D_MODEL), lambda i, *_: (i, 0)),
                  pl.BlockSpec((tm, LANES), lambda i, *_: (i, 0)),
                  pl.BlockSpec((1, HALO, LANES), lambda i, *_: (i, 0, 0))],
        out_specs=pl.BlockSpec(memory_space=pl.ANY),
        scratch_shapes=[pltpu.VMEM((3, MOE_RLOC + MOE_CHUNK, MOE_XW), BF16),
                        pltpu.SemaphoreType.DMA((3,))],
    )
    return pl.pallas_call(
        functools.partial(_dispatch_kernel, tm=tm, n_tiles=n_tiles, n_work=n_rows // MOE_WORK),
        grid_spec=grid_spec,
        out_shape=jax.ShapeDtypeStruct((n_rows, MOE_XW), BF16),
        compiler_params=pltpu.CompilerParams(dimension_semantics=("arbitrary",),
                                             vmem_limit_bytes=VMEM_LIMIT),
        name="moe_dispatch",
    )(tabs["nch"], tabs["dst"], tabs["pad_lo"], tabs["pad_hi"], tabs["n_used"], h2, route, tabs["loc"])


def _expert_kernel(te_ref, nused_ref, x_ref, wup_ref, wdn_ref, y_ref, wup_bf, wdn_bf):
    w = pl.program_id(0)
    used = w < nused_ref[0]

    @pl.when((w == 0) | (te_ref[w] != te_ref[jnp.maximum(w - 1, 0)]))
    def _():
        wup_bf[...] = wup_ref[...].astype(BF16)
        wdn_bf[...] = wdn_ref[...].astype(BF16)

    @pl.when(used)
    def _():
        wt = x_ref[:, D_MODEL:MOE_XW].astype(F32).sum(axis=-1, keepdims=True)
        gu = _dot(x_ref[:, 0:D_MODEL], wup_bf[...])
        g = gu[:, 0:D_EXPERT]
        a = (g * jax.nn.sigmoid(g)) * gu[:, D_EXPERT:2 * D_EXPERT] * wt
        y_ref[...] = _dot(a.astype(BF16), wdn_bf[...]).astype(BF16)

    @pl.when(jnp.logical_not(used))
    def _():
        y_ref[...] = jnp.zeros(y_ref.shape, BF16)


def _expert_call(xbuf, tabs, w_up, w_down, layer):
    n_rows = xbuf.shape[0]
    n_work = n_rows // MOE_WORK
    base = layer * N_EXPERTS
    grid_spec = pltpu.PrefetchScalarGridSpec(
        num_scalar_prefetch=2,
        grid=(n_work,),
        in_specs=[pl.BlockSpec((MOE_WORK, MOE_XW), lambda w, te, nu: (jnp.minimum(w, nu[0] - 1), 0)),
                  pl.BlockSpec((None, D_MODEL, 2 * D_EXPERT), lambda w, te, nu: (base + te[w], 0, 0)),
                  pl.BlockSpec((None, D_EXPERT, D_MODEL), lambda w, te, nu: (base + te[w], 0, 0))],
        out_specs=pl.BlockSpec((MOE_WORK, D_MODEL), lambda w, te, nu: (w, 0)),
        scratch_shapes=[pltpu.VMEM((D_MODEL, 2 * D_EXPERT), BF16), pltpu.VMEM((D_EXPERT, D_MODEL), BF16)],
    )
    return pl.pallas_call(
        _expert_kernel,
        grid_spec=grid_spec,
        out_shape=jax.ShapeDtypeStruct((n_rows, D_MODEL), BF16),
        compiler_params=pltpu.CompilerParams(dimension_semantics=("arbitrary",),
                                             vmem_limit_bytes=VMEM_LIMIT),
        name="moe_experts",
    )(tabs["tile_expert"], tabs["n_used"], xbuf, w_up, w_down)


def _chunk_copy_in(ybuf_ref, ys_ref, sem_ref, slot, src_chunk, dst_chunk):
    return pltpu.make_async_copy(ybuf_ref.at[pl.ds(src_chunk * MOE_CHUNK, MOE_CHUNK), :],
                                 ys_ref.at[slot, pl.ds(dst_chunk * MOE_CHUNK, MOE_CHUNK), :],
                                 sem_ref.at[slot])


def _combine_kernel(nch_ref, dst_ref, route_ref, loc_ref, x_ref, mod_ref, ybuf_ref, o_ref,
                    ys_ref, sem_ref, *, tm, n_tiles):
    i = pl.program_id(0)
    slot = i % 2

    def start_tile(t_idx, slot_):
        def body(j, carry):
            _chunk_copy_in(ybuf_ref, ys_ref, sem_ref, slot_, dst_ref[t_idx * MOE_NCH + j], j).start()
            return carry
        lax.fori_loop(0, nch_ref[t_idx], body, 0)

    @pl.when(i == 0)
    def _():
        ys_ref[...] = jnp.zeros(ys_ref.shape, BF16)
        start_tile(0, 0)

    @pl.when(i + 1 < n_tiles)
    def _():
        start_tile(i + 1, 1 - slot)

    def wait_body(j, carry):
        _chunk_copy_in(ybuf_ref, ys_ref, sem_ref, slot, 0, 0).wait()
        return carry
    lax.fori_loop(0, nch_ref[i], wait_body, 0)

    _, _, d_a, d_b = _segment_positions(route_ref[...], loc_ref[0, 0:1, :], tm)
    lio = lax.broadcasted_iota(jnp.int32, (tm, MOE_RLOC), 1).astype(F32)
    perm_t = jnp.where((lio == d_a) | (lio == d_b), 1.0, 0.0).astype(BF16)
    y = _dot(perm_t, ys_ref[slot])
    o_ref[...] = x_ref[...] + mod_ref[0, 5:6, :] * y


def _combine_call(ybuf, route, x2d, mod, layer, tabs, *, tm, mod_index):
    n = x2d.shape[0]
    n_tiles = n // tm
    grid_spec = pltpu.PrefetchScalarGridSpec(
        num_scalar_prefetch=2,
        grid=(n_tiles,),
        in_specs=[pl.BlockSpec((tm, LANES), lambda i, *_: (i, 0)),
                  pl.BlockSpec((1, HALO, LANES), lambda i, *_: (i, 0, 0)),
                  pl.BlockSpec((tm, D_MODEL), lambda i, *_: (i, 0)),
                  _mod_spec(layer, mod_index),
                  pl.BlockSpec(memory_space=pl.ANY)],
        out_specs=pl.BlockSpec((tm, D_MODEL), lambda i, *_: (i, 0)),
        scratch_shapes=[pltpu.VMEM((2, MOE_RLOC, D_MODEL), BF16),
                        pltpu.SemaphoreType.DMA((2,))],
    )
    return pl.pallas_call(
        functools.partial(_combine_kernel, tm=tm, n_tiles=n_tiles),
        grid_spec=grid_spec,
        out_shape=jax.ShapeDtypeStruct((n, D_MODEL), F32),
        compiler_params=pltpu.CompilerParams(dimension_semantics=("arbitrary",),
                                             vmem_limit_bytes=VMEM_LIMIT),
        name="moe_combine",
    )(tabs["nch"], tabs["dst"], route, tabs["loc"], x2d, mod, ybuf)


def _moe_rows(n_tok):
    n_tiles = n_tok // MOE_T
    chunks = 2 * n_tok // MOE_CHUNK + (MOE_CHUNK - 1) * n_tiles + N_EXPERTS * (MOE_CPW - 1)
    return -(-chunks // MOE_CPW) * MOE_WORK


def _route_tables(counts, n_tok):
    n_work = _moe_rows(n_tok) // MOE_WORK
    cnt = counts[:, 0, ROUTE_SEL_LANE:ROUTE_SEL_LANE + N_EXPERTS].astype(jnp.int32)
    n_tiles = cnt.shape[0]
    nchunk = (cnt + MOE_CHUNK - 1) // MOE_CHUNK
    ei = jnp.arange(N_EXPERTS, dtype=jnp.int32)
    ti = jnp.arange(n_tiles, dtype=jnp.int32)
    upto_e = (ei[:, None] <= ei[None, :]).astype(jnp.int32)
    upto_t = (ti[None, :] <= ti[:, None]).astype(jnp.int32)
    ends = (nchunk[:, :, None] * upto_e[None, :, :]).sum(axis=1)
    loc = ends - nchunk
    nch = nchunk.sum(axis=1)
    per_e = nchunk.sum(axis=0)
    padded_e = (per_e + MOE_CPW - 1) // MOE_CPW * MOE_CPW
    e_end = (padded_e[:, None] * upto_e).sum(axis=0)
    e_base = e_end - padded_e
    glob = e_base[None, :] + (upto_t[:, :, None] * nchunk[None, :, :]).sum(axis=1) - nchunk
    j = jnp.arange(MOE_NCH, dtype=jnp.int32)
    e_of_j = (ends[:, None, :] <= j[None, :, None]).sum(axis=-1)
    seg = e_of_j[:, :, None] == ei[None, None, :]
    dst = jnp.where(seg, (glob - loc)[:, None, :], 0).sum(axis=-1) + j[None, :]
    dst = jnp.where(j[None, :] < nch[:, None], dst, 0)
    n_used = padded_e.sum() // MOE_CPW
    w = jnp.minimum(jnp.arange(n_work, dtype=jnp.int32), n_used - 1)
    tile_expert = jnp.minimum(((e_end // MOE_CPW)[None, :] <= w[:, None]).sum(axis=-1), N_EXPERTS - 1)
    loc_rows = jnp.pad((loc * MOE_CHUNK).astype(F32),
                       ((0, 0), (ROUTE_SEL_LANE, LANES - ROUTE_SEL_LANE - N_EXPERTS)))
    loc_rows = jnp.broadcast_to(loc_rows[:, None, :], (n_tiles, HALO, LANES))
    return {"nch": nch.astype(jnp.int32), "dst": dst.reshape(-1).astype(jnp.int32),
            "pad_lo": (e_base + per_e).astype(jnp.int32), "pad_hi": e_end.astype(jnp.int32),
            "tile_expert": tile_expert.astype(jnp.int32), "n_used": n_used.astype(jnp.int32).reshape(1),
            "loc": loc_rows}


def _moe_call(h2, route, counts, x2d, mod, w_up, w_down, layer, *, mod_index):
    n = x2d.shape[0]
    tabs = _route_tables(counts, n)
    xbuf = _dispatch_call(h2, route, tabs, tm=MOE_T, n_rows=_moe_rows(n))
    ybuf = _expert_call(xbuf, tabs, w_up, w_down, layer)
    return _combine_call(ybuf, route, x2d, mod, layer, tabs, tm=MOE_T, mod_index=mod_index)


def _ones_blocks(sizes):
    n = sum(sizes)
    m = np.zeros((n, n), np.float32)
    o = 0
    for s in sizes:
        m[o:o + s, o:o + s] = 1.0
        o += s
    return jnp.asarray(m, BF16)


def _inv_counts_q():
    return jnp.asarray(np.tile(np.concatenate([np.full(64, 1 / 64.0), np.full(64, 1 / 32.0)]), 2)[None, :], F32)


def _rope_tables(rows):
    r = np.repeat(np.arange(rows, dtype=np.float32), GRID_W)
    c = np.tile(np.arange(GRID_W, dtype=np.float32), rows)

    def tab(rot_dim):
        nf = rot_dim // 4
        inv = (np.float32(ROPE_THETA) ** (-np.arange(nf, dtype=np.float32) / np.float32(nf))).astype(np.float32)
        ang = np.concatenate([r[:, None] * inv, c[:, None] * inv], axis=-1).astype(np.float32)
        cos = np.repeat(np.cos(ang), 2, axis=-1)
        sin = np.repeat(np.sin(ang), 2, axis=-1) * np.tile(np.asarray([-1.0, 1.0], np.float32), rot_dim // 2)
        return cos.astype(np.float32), sin.astype(np.float32)

    s = rows * GRID_W
    cc, sc = tab(C_HEAD_DIM)
    cd, sd = tab(D_ROPE)
    one = lambda w: np.ones((s, w), np.float32)
    zero = lambda w: np.zeros((s, w), np.float32)
    tabs = (np.tile(cc, (1, 2)), np.tile(sc, (1, 2)),
            np.concatenate([one(64), cd, one(32)], axis=1), np.concatenate([zero(64), sd, zero(32)], axis=1),
            np.concatenate([cd, one(96)], axis=1), np.concatenate([sd, zero(96)], axis=1))
    return tuple(jnp.asarray(t) for t in tabs)


def _stacked_weights(p):
    (g_attn_norm, w_in, sgu_v_gain, sgu_w, sgu_b, conv_w, gqa_q_gain, gqa_k_gain, mla_qa_gain, mla_w_qb,
     mla_kva_gain, mla_w_kvb, mla_q_gain, mla_k_gain, g_group_out, w_out, g_ffn_norm,
     w_rg, b_rg, w_re, b_re) = p
    depth = w_in.shape[0]

    def heads_0213(a, axis, off):
        sl = lambda lo: lax.slice_in_dim(a, off + lo, off + lo + 64, axis=axis)
        return [sl(0), sl(128), sl(64), sl(192)]

    zeros = lambda *shape: jnp.zeros((depth,) + shape, F32)
    row = lambda a: a[:, None, :]
    w_in_p = jnp.concatenate([w_in[:, :, :1280]] + heads_0213(w_in, 2, 1280) +
                             [w_in[:, :, 1536:2208], zeros(D_MODEL, IN_W - 2208)], axis=2).astype(BF16)
    wqb = jnp.concatenate([jnp.concatenate([mla_w_qb[:, :, h * 96:h * 96 + 96], zeros(D_Q_LORA, 32)], axis=2)
                           for h in range(D_HEADS)], axis=2)
    wk = jnp.concatenate([jnp.concatenate([mla_w_kvb[:, :, h * 128:h * 128 + 64], zeros(D_KV_LORA, 64)], axis=2)
                          for h in range(D_HEADS)], axis=2)
    wv = jnp.concatenate([mla_w_kvb[:, :, h * 128 + 64:h * 128 + 128] for h in range(D_HEADS)], axis=2)
    qslab = jnp.concatenate([mla_q_gain, zeros(32)], axis=1)
    kslab = jnp.concatenate([mla_k_gain[:, :64], zeros(64)], axis=1)
    g_go = jnp.concatenate([g_group_out[:, :512]] + heads_0213(g_group_out, 1, 512) + [g_group_out[:, 768:]],
                           axis=1)
    w_out_p = jnp.concatenate([w_out[:, :512]] + heads_0213(w_out, 1, 512) + [w_out[:, 768:]],
                              axis=1).astype(BF16)
    w_r = jnp.concatenate([w_re, w_rg], axis=2)
    w_r_hi = w_r.astype(BF16)
    w_r_lo = (w_r - w_r_hi.astype(F32)).astype(BF16)
    n_r = N_EXPERTS + N_GROUPS
    zb = lambda w: jnp.zeros((depth, D_MODEL, w), BF16)
    w_router = jnp.concatenate([w_r_hi, zb(ROUTE_SEL_LANE - n_r), w_r_lo, zb(LANES - ROUTE_SEL_LANE - n_r)], axis=2)
    b_router = jnp.concatenate([b_re, b_rg, zeros(LANES - n_r)], axis=1)
    sgub_full = jnp.repeat(jnp.swapaxes(sgu_b, 1, 2), 64, axis=2)
    return {
        "g_attn": row(g_attn_norm), "w_in": w_in_p,
        "sgu_w": sgu_w.reshape(depth, A_HEADS * CHUNK, CHUNK).astype(BF16), "sgu_b": sgub_full,
        "v_gain": row(sgu_v_gain), "conv_w": conv_w, "g_go": row(g_go),
        "cq_gain": row(jnp.tile(gqa_q_gain, (1, 4))), "ck_gain": row(jnp.tile(gqa_k_gain, (1, 2))),
        "qa_gain": row(mla_qa_gain), "w_qb": wqb.astype(BF16), "kva_gain": row(mla_kva_gain),
        "w_kvb": jnp.concatenate([wk, wv], axis=2).astype(BF16),
        "dq_gain": row(jnp.tile(qslab, (1, 4))), "dk_gain": row(jnp.tile(kslab, (1, 4))),
        "dkr_gain": row(jnp.concatenate([mla_k_gain[:, 64:], zeros(96)], axis=1)),
        "w_out": w_out_p, "g_ffn": row(g_ffn_norm), "w_router": w_router, "b_router": row(b_router),
    }


def kernel(x, c, ctx, c_ctx, w_mod, b_mod, g_attn_norm, w_in, sgu_v_gain, sgu_w, sgu_b, conv_w, gqa_q_gain,
           gqa_k_gain, mla_qa_gain, mla_w_qb, mla_kva_gain, mla_w_kvb, mla_q_gain, mla_k_gain, g_group_out,
           w_out, g_ffn_norm, w_router_group, b_router_group, w_router_expert, b_router_expert,
           w_expert_up, w_expert_down):
    b, s, d = x.shape
    lc = ctx.shape[1]
    depth = w_mod.shape[0]
    lw = _stacked_weights((g_attn_norm, w_in, sgu_v_gain, sgu_w, sgu_b, conv_w, gqa_q_gain, gqa_k_gain,
                           mla_qa_gain, mla_w_qb, mla_kva_gain, mla_w_kvb, mla_q_gain, mla_k_gain, g_group_out,
                           w_out, g_ffn_norm, w_router_group, b_router_group, w_router_expert, b_router_expert))
    w_up = w_expert_up.reshape(depth * N_EXPERTS, d, 2 * D_EXPERT)
    w_down = w_expert_down.reshape(depth * N_EXPERTS, D_EXPERT, d)

    c_all = jnp.concatenate([c, c_ctx[None, :], jnp.zeros((16 - b - 1, d), F32)], axis=0)
    mod = _modulation(c_all, w_mod, b_mod).reshape(depth, 16, 6, d)
    rope = _rope_tables(s // GRID_W)

    tm = MOE_T
    tq = 512
    tpb = s // tm
    lat_idx = lambda i: i // tpb
    ctx_idx = lambda i: b
    r3 = lambda a, n: a.reshape(b, n, a.shape[-1])

    xl = x.reshape(b * s, d)
    xc = ctx.reshape(b * lc, d)
    for l in range(depth):
        update_ctx = l < depth - 1
        ya_c, yb_c, qc_c, kc_c, vc_c, qd_c, kd_c, vd_c = _in_call(
            xc, mod, lw, l, None, tm=tm, seq_len=lc, mod_index=ctx_idx)
        ya, yb, qc, kc, vc, qd, kd, vd = _in_call(xl, mod, lw, l, rope, tm=tm, seq_len=s, mod_index=lat_idx)
        yc = _attn_call(r3(qc, s), [r3(kc_c, lc), r3(kc, s)], [r3(vc_c, lc), r3(vc, s)], lw["g_go"], l,
                        mode="C", tq=tq)
        yd = _attn_call(r3(qd, s), [r3(kd_c, lc), r3(kd, s)], [r3(vd_c, lc), r3(vd, s)], lw["g_go"], l,
                        mode="D", tq=tq)
        xl, h2, route, counts = _merge_call(xl, ya, yb, yc.reshape(b * s, GROUP_W), yd.reshape(b * s, GROUP_W),
                                            mod, lw, l, tm=tm, mod_index=lat_idx)
        xl = _moe_call(h2, route, counts, xl, mod, w_up, w_down, l, mod_index=lat_idx)
        if update_ctx:
            yc_c = _attn_call(r3(qc_c, lc), [r3(kc_c, lc)], [r3(vc_c, lc)], lw["g_go"], l, mode="C", tq=lc)
            yd_c = _attn_call(r3(qd_c, lc), [r3(kd_c, lc)], [r3(vd_c, lc)], lw["g_go"], l, mode="D", tq=lc)
            xc, h2c, route_c, counts_c = _merge_call(xc, ya_c, yb_c, yc_c.reshape(b * lc, GROUP_W),
                                                     yd_c.reshape(b * lc, GROUP_W), mod, lw, l, tm=tm,
                                                     mod_index=ctx_idx)
            xc = _moe_call(h2c, route_c, counts_c, xc, mod, w_up, w_down, l, mod_index=ctx_idx)
    return xl.reshape(b, s, d)
```

```python
import functools

import numpy as np
import jax
import jax.numpy as jnp
from jax import lax
from jax.experimental import pallas as pl
from jax.experimental.pallas import tpu as pltpu

F32 = jnp.float32
BF16 = jnp.bfloat16

D_MODEL = 1024
GRID_W = 64
EPS = 1e-6
ROPE_THETA = 10000.0
GROUP_W = 256
CHUNK = 128
A_HEADS = 4
C_HEAD_DIM = 64
C_SCALE = C_HEAD_DIM ** -0.5
D_HEADS = 4
D_NOPE = 64
D_ROPE = 32
D_V = 64
D_Q_LORA = 256
D_KV_LORA = 128
D_SCALE = (D_NOPE + D_ROPE) ** -0.5
LOG2E = 1.4426950408889634
ATTN_ROWS = 16
ATTN_KSPLIT = 1024
N_GROUPS = 4
EXPERTS_PER_GROUP = 4
N_EXPERTS = 16
D_EXPERT = 256
LANES = 128
HALO = 8

IN_W = 2304
IN_T = 512
IN_SECTIONS = ((0, 256), (256, 512), (512, 768), (768, 1024), (1024, 1280), (1280, 1536), (1536, 1664),
               (1664, 1792), (1792, 2048), (2048, 2176), (2176, 2304))

ROUTE_SEL_LANE = 32
MOE_T = 512
MOE_CHUNK = 16
MOE_RLOC = 1280
MOE_NCH = MOE_RLOC // MOE_CHUNK
MOE_XW = D_MODEL + LANES
MOE_WORK = 512
MOE_CPW = MOE_WORK // MOE_CHUNK
VMEM_LIMIT = 56 * 1024 * 1024


def _nt_dot(a, b):
    return lax.dot_general(a, b, (((1,), (1,)), ((), ())), preferred_element_type=F32)


def _dot(a, b):
    return jnp.dot(a, b, preferred_element_type=F32)


def _rms_full(x, gain):
    return x * lax.rsqrt(jnp.mean(x * x, axis=-1, keepdims=True) + EPS) * gain


def _group_ms(x, ones_blocks, inv_count):
    return _dot((x * x).astype(BF16), ones_blocks) * inv_count


def _pair_swap(x, even):
    nxt = pltpu.roll(x, x.shape[1] - 1, axis=1)
    prv = pltpu.roll(x, 1, axis=1)
    return jnp.where(even, nxt, prv)


def _rope(x, cos, sin, even):
    return x * cos + _pair_swap(x, even) * sin


def _mod_kernel(c_ref, w_ref, b_ref, o_ref):
    cv = c_ref[...]
    sc = cv * jax.nn.sigmoid(cv)
    o_ref[0] = jnp.dot(sc, w_ref[0], preferred_element_type=F32,
                       precision=lax.Precision.HIGHEST) + b_ref[0]


def _modulation(c_all, w_mod, b_mod):
    depth = w_mod.shape[0]
    nblk = 4
    wblk = 6 * D_MODEL // nblk
    return pl.pallas_call(
        _mod_kernel,
        grid=(depth, nblk),
        in_specs=[pl.BlockSpec((16, D_MODEL), lambda l, j: (0, 0)),
                  pl.BlockSpec((1, D_MODEL, wblk), lambda l, j: (l, 0, j)),
                  pl.BlockSpec((1, 1, wblk), lambda l, j: (l, 0, j))],
        out_specs=pl.BlockSpec((1, 16, wblk), lambda l, j: (l, 0, j)),
        out_shape=jax.ShapeDtypeStruct((depth, 16, 6 * D_MODEL), F32),
        compiler_params=pltpu.CompilerParams(dimension_semantics=("arbitrary", "arbitrary"),
                                             vmem_limit_bytes=VMEM_LIMIT),
        name="mod",
    )(c_all, w_mod, b_mod.reshape(depth, 1, 6 * D_MODEL))


def _in_kernel(*refs, tm, seq_len, use_rope):
    n_io = len(refs) - 4
    pj0_ref, pj1_ref, zh0_ref, zh1_ref = refs[n_io:]
    i = pl.program_id(0)

    @pl.when(i == 0)
    def _():
        pj1_ref[...] = jnp.zeros(pj1_ref.shape, F32)
        zh1_ref[...] = jnp.zeros(zh1_ref.shape, F32)

    @pl.when(i % 2 == 0)
    def _():
        _in_stages(refs[:n_io], pj0_ref, zh0_ref, pj1_ref, zh1_ref, tm=tm, seq_len=seq_len, use_rope=use_rope)

    @pl.when(i % 2 == 1)
    def _():
        _in_stages(refs[:n_io], pj1_ref, zh1_ref, pj0_ref, zh0_ref, tm=tm, seq_len=seq_len, use_rope=use_rope)


def _in_stages(refs, pj_w, zh_w, pj_r, zh_r, *, tm, seq_len, use_rope):
    (x_ref, xp_ref, xn_ref, mod_ref, gattn_ref, win_ref, sguw_ref, sgub_ref, vgain_ref, convw_ref,
     ggo_ref, cqg_ref, ckg_ref, qag_ref, wqb_ref, kvag_ref, wkvb_ref, dqg_ref, dkg_ref, dkrg_ref,
     b64_ref, bq_ref, invq_ref) = refs[:23]
    pos = 23
    if use_rope:
        cosc_ref, sinc_ref, cosq_ref, sinq_ref, cosk_ref, sink_ref = refs[pos:pos + 6]
        pos += 6
    ya_ref, yb_ref, qc_ref, kc_ref, vc_ref, qd_ref, kd_ref, vd_ref = refs[pos:pos + 8]

    shift = mod_ref[0, 0:1, :]
    scale = mod_ref[0, 1:2, :]
    gattn = gattn_ref[...]

    def prenorm(xv):
        return (_rms_full(xv, gattn) * (1.0 + scale) + shift).astype(BF16)

    hb = prenorm(x_ref[...])

    def project(first, last):
        for lo, hi in IN_SECTIONS[first:last]:
            pj_w[:, lo:hi] = _dot(hb, win_ref[:, lo:hi])

    project(0, 3)
    hh = prenorm(jnp.concatenate([xp_ref[...], xn_ref[...]], axis=0))
    zh_w[...] = _dot(hh, win_ref[:, 768:1024]) * _dot(hh, win_ref[:, 1024:1280])

    i = jnp.maximum(pl.program_id(0) - 1, 0)
    proj = {ab: pj_r[:, ab[0]:ab[1]] for ab in IN_SECTIONS}
    zh = zh_r[...]
    b64 = b64_ref[...]
    inv64 = 1.0 / 64.0
    lane128 = lax.broadcasted_iota(jnp.int32, (tm, LANES), 1)
    even = (lane128 & 1) == 0

    u = jax.nn.gelu(proj[(0, 256)])
    v = jax.nn.gelu(proj[(256, 512)])
    v = (v * lax.rsqrt(_group_ms(v, b64, inv64) + EPS) * vgain_ref[...]).astype(BF16)
    lane_head = lax.broadcasted_iota(jnp.int32, (CHUNK, GROUP_W), 1) // 64
    sguw = sguw_ref[...]
    mixed_chunks = []
    for ci in range(tm // CHUNK):
        r = _dot(sguw, v[ci * CHUNK:(ci + 1) * CHUNK, :])
        m = r[0:CHUNK]
        for h in range(1, A_HEADS):
            m = jnp.where(lane_head == h, r[h * CHUNK:(h + 1) * CHUNK], m)
        mixed_chunks.append(m + sgub_ref[...])
    ya = u * jnp.concatenate(mixed_chunks, axis=0)
    ya_ref[...] = _rms_full(ya, ggo_ref[:, 0:256]).astype(BF16)

    project(3, 6)
    bg = proj[(512, 768)]
    z = proj[(768, 1024)] * proj[(1024, 1280)]
    row = lax.broadcasted_iota(jnp.int32, (tm, GROUP_W), 0)
    posn = (i * tm + row) % seq_len
    z_prev = jnp.where(row == 0, zh[HALO - 1:HALO, :], pltpu.roll(z, 1, axis=0))
    z_prev = jnp.where(posn == 0, 0.0, z_prev)
    z_next = jnp.where(row == tm - 1, zh[HALO:HALO + 1, :], pltpu.roll(z, tm - 1, axis=0))
    z_next = jnp.where(posn == seq_len - 1, 0.0, z_next)
    yb = bg * (convw_ref[0:1, :] * z_prev + convw_ref[1:2, :] * z + convw_ref[2:3, :] * z_next)
    yb_ref[...] = _rms_full(yb, ggo_ref[:, 256:512]).astype(BF16)

    project(6, 9)
    q = proj[(1280, 1536)]
    q = q * lax.rsqrt(_group_ms(q, b64, inv64) + EPS) * cqg_ref[...]
    k = proj[(1536, 1664)]
    k = k * lax.rsqrt(_group_ms(k, b64[0:128, 0:128], inv64) + EPS) * ckg_ref[...]
    if use_rope:
        cosc, sinc = cosc_ref[...], sinc_ref[...]
        q = jnp.concatenate([_rope(q[:, 0:128], cosc, sinc, even),
                             _rope(q[:, 128:256], cosc, sinc, even)], axis=1)
        k = _rope(k, cosc, sinc, even)
    qc_ref[...] = (q * (C_SCALE * LOG2E)).astype(BF16)
    kc_ref[...] = k.astype(BF16)
    vc_ref[...] = proj[(1664, 1792)].astype(BF16)

    project(9, len(IN_SECTIONS))
    cq = _rms_full(proj[(1792, 2048)], qag_ref[...]).astype(BF16)
    qd = _dot(cq, wqb_ref[...])
    bq = bq_ref[...]
    invq = invq_ref[...]
    halves = []
    for hf in range(2):
        qh = qd[:, hf * 256:(hf + 1) * 256]
        qh = qh * lax.rsqrt(_group_ms(qh, bq, invq) + EPS) * dqg_ref[:, hf * 256:(hf + 1) * 256]
        if use_rope:
            cosq, sinq = cosq_ref[...], sinq_ref[...]
            qh = jnp.concatenate([_rope(qh[:, 0:128], cosq, sinq, even),
                                  _rope(qh[:, 128:256], cosq, sinq, even)], axis=1)
        halves.append(qh)
    qd_ref[...] = (jnp.concatenate(halves, axis=1) * (D_SCALE * LOG2E)).astype(BF16)

    ckv = _rms_full(proj[(2048, 2176)], kvag_ref[...]).astype(BF16)
    kv = _dot(ckv, wkvb_ref[...])
    vd_ref[...] = kv[:, 512:768].astype(BF16)
    kr = proj[(2176, 2304)]
    kr = kr * lax.rsqrt(jnp.sum(kr * kr, axis=-1, keepdims=True) * (1.0 / D_ROPE) + EPS) * dkrg_ref[...]
    if use_rope:
        kr = _rope(kr, cosk_ref[...], sink_ref[...], even)
    kr = pltpu.roll(kr, D_NOPE, axis=1)
    for hf in range(2):
        kh = kv[:, hf * 256:(hf + 1) * 256]
        kh = kh * lax.rsqrt(_group_ms(kh, b64, inv64) + EPS) * dkg_ref[:, hf * 256:(hf + 1) * 256]
        kd_ref[:, hf * 256:hf * 256 + 128] = (kh[:, 0:128] + kr).astype(BF16)
        kd_ref[:, hf * 256 + 128:(hf + 1) * 256] = (kh[:, 128:256] + kr).astype(BF16)


def _layer_spec(a, layer):
    zeros = (0,) * (a.ndim - 1)
    return pl.BlockSpec((None,) + a.shape[1:], lambda *_: (layer,) + zeros)


def _mod_spec(layer, mod_index):
    return pl.BlockSpec((None, 1, 6, D_MODEL), lambda i, *_: (layer, mod_index(i), 0, 0))


def _in_call(x2d, mod, lw, layer, rope, *, tm, seq_len, mod_index):
    n = x2d.shape[0]
    nt = n // tm
    nb8 = n // HALO
    tb = tm // HALO
    use_rope = rope is not None
    cur = lambda i: jnp.minimum(i, nt - 1)
    prev = lambda i: jnp.maximum(i - 1, 0)
    row = lambda i: (prev(i), 0)
    in_specs = [
        pl.BlockSpec((tm, D_MODEL), lambda i: (cur(i), 0)),
        pl.BlockSpec((HALO, D_MODEL), lambda i: (jnp.maximum(cur(i) * tb - 1, 0), 0)),
        pl.BlockSpec((HALO, D_MODEL), lambda i: (jnp.minimum((cur(i) + 1) * tb, nb8 - 1), 0)),
        _mod_spec(layer, lambda i: mod_index(cur(i))),
    ]
    args = [x2d, x2d, x2d, mod]
    for name in ("g_attn", "w_in", "sgu_w", "sgu_b", "v_gain", "conv_w", "g_go", "cq_gain", "ck_gain",
                 "qa_gain", "w_qb", "kva_gain", "w_kvb", "dq_gain", "dk_gain", "dkr_gain"):
        in_specs.append(_layer_spec(lw[name], layer))
        args.append(lw[name])
    for a in (_ones_blocks([64] * 4), _ones_blocks([64, 32, 32] * 2), _inv_counts_q()):
        in_specs.append(pl.BlockSpec(a.shape, lambda i: (0, 0)))
        args.append(a)
    if use_rope:
        ntab = seq_len // tm
        for a in rope:
            in_specs.append(pl.BlockSpec((tm, LANES), lambda i: (prev(i) % ntab, 0)))
            args.append(a)
    widths = (256, 256, 256, 128, 128, 512, 512, 256)
    out_specs = [pl.BlockSpec((tm, w), row) for w in widths]
    out_shape = [jax.ShapeDtypeStruct((n, w), BF16) for w in widths]
    return pl.pallas_call(
        functools.partial(_in_kernel, tm=tm, seq_len=seq_len, use_rope=use_rope),
        grid=(nt + 1,),
        in_specs=in_specs,
        out_specs=out_specs,
        out_shape=out_shape,
        scratch_shapes=[pltpu.VMEM((tm, IN_W), F32), pltpu.VMEM((tm, IN_W), F32),
                        pltpu.VMEM((2 * HALO, GROUP_W), F32), pltpu.VMEM((2 * HALO, GROUP_W), F32)],
        compiler_params=pltpu.CompilerParams(dimension_semantics=("arbitrary",),
                                             vmem_limit_bytes=VMEM_LIMIT),
        name="in_proj",
    )(*args)


def _attn_kernel(*refs, mode, nseg, tq):
    q_ref = refs[0]
    k_refs = refs[1:1 + nseg]
    v_refs = refs[1 + nseg:1 + 2 * nseg]
    gain_ref = refs[1 + 2 * nseg]
    o_ref = refs[2 + 2 * nseg]
    p_ref = refs[3 + 2 * nseg]
    nks = [r.shape[1] for r in k_refs]
    offs = [sum(nks[:j]) for j in range(nseg)]
    n_chunks = tq // ATTN_ROWS

    vs = [r[0] for r in v_refs]
    if mode == "C":
        ks_c = [r[0] for r in k_refs]
        first_half = lax.broadcasted_iota(jnp.int32, (tq, LANES), 1) < 64
        heads = []
        for sl in range(2):
            qs = q_ref[0, :, sl * 128:(sl + 1) * 128]
            heads.append((jnp.where(first_half, qs, jnp.zeros_like(qs)), ks_c))
            heads.append((jnp.where(first_half, jnp.zeros_like(qs), qs), ks_c))
    else:
        heads = [(q_ref[0, :, h * 128:(h + 1) * 128], [r[0, :, h * 128:(h + 1) * 128] for r in k_refs])
                 for h in range(D_HEADS)]

    def score(h):
        qm, ks = heads[h]
        return [_nt_dot(qm, kk) for kk in ks]

    def softmax(h, ss):
        sums = []
        for c in range(n_chunks):
            rows = slice(c * ATTN_ROWS, (c + 1) * ATTN_ROWS)
            m = ss[0][rows].max(axis=-1, keepdims=True)
            for s in ss[1:]:
                m = jnp.maximum(m, s[rows].max(axis=-1, keepdims=True))
            tot = None
            for s, off, nk in zip(ss, offs, nks):
                p = jnp.exp2(s[rows] - m).astype(BF16)
                ps = p.astype(F32).sum(axis=-1, keepdims=True)
                tot = ps if tot is None else tot + ps
                p_ref[h, rows, off:off + nk] = p
            sums.append(tot)
        return jnp.concatenate(sums, axis=0)

    def values(h, row_sum):
        o = None
        for off, nk, vv in zip(offs, nks, vs):
            step = min(nk, ATTN_KSPLIT)
            for k0 in range(0, nk, step):
                part = _dot(p_ref[h, :, off + k0:off + k0 + step], vv[k0:k0 + step])
                o = part if o is None else o + part
        return o / row_sum

    n_heads = len(heads)
    scores = {0: score(0), 1: score(1)}
    row_sums = {}
    outs = []
    for h in range(n_heads):
        row_sums[h] = softmax(h, scores.pop(h))
        if h + 2 < n_heads:
            scores[h + 2] = score(h + 2)
        if h >= 1:
            outs.append(values(h - 1, row_sums.pop(h - 1)))
    outs.append(values(n_heads - 1, row_sums.pop(n_heads - 1)))

    if mode == "C":
        out = jnp.concatenate([jnp.where(first_half, outs[0], outs[1]),
                               jnp.where(first_half, outs[2], outs[3])], axis=1)
    else:
        lane_head = lax.broadcasted_iota(jnp.int32, (tq, GROUP_W), 1) // D_V
        out = outs[0]
        for h in range(1, D_HEADS):
            out = jnp.where(lane_head == h, outs[h], out)
    o_ref[0] = _rms_full(out, gain_ref[...]).astype(BF16)


def _attn_call(q, ks, vs, g_go, layer, *, mode, tq):
    b, s, wq = q.shape
    nseg = len(ks)
    gain_block = 2 if mode == "C" else 3
    in_specs = [pl.BlockSpec((1, tq, wq), lambda bi, qi: (bi, qi, 0))]
    for a in list(ks) + list(vs):
        in_specs.append(pl.BlockSpec((1,) + a.shape[1:], lambda bi, qi: (bi, 0, 0)))
    in_specs.append(pl.BlockSpec((None, 1, GROUP_W), lambda bi, qi: (layer, 0, gain_block)))
    return pl.pallas_call(
        functools.partial(_attn_kernel, mode=mode, nseg=nseg, tq=tq),
        grid=(b, s // tq),
        in_specs=in_specs,
        out_specs=pl.BlockSpec((1, tq, GROUP_W), lambda bi, qi: (bi, qi, 0)),
        out_shape=jax.ShapeDtypeStruct((b, s, GROUP_W), BF16),
        scratch_shapes=[pltpu.VMEM((4, tq, sum(a.shape[1] for a in ks)), BF16)],
        compiler_params=pltpu.CompilerParams(dimension_semantics=("parallel", "parallel"),
                                             vmem_limit_bytes=VMEM_LIMIT),
        name="attn_" + mode,
    )(q, *ks, *vs, g_go)


def _merge_kernel(x_ref, ya_ref, yb_ref, yc_ref, yd_ref, mod_ref, wout_ref, gffn_ref, wr_ref, br_ref,
                  xo_ref, h2_ref, route_ref, cnt_ref, *, tm):
    y = _dot(ya_ref[...], wout_ref[0:256, :])
    y = y + _dot(yb_ref[...], wout_ref[256:512, :])
    y = y + _dot(yc_ref[...], wout_ref[512:768, :])
    y = y + _dot(yd_ref[...], wout_ref[768:1024, :])
    xn = x_ref[...] + mod_ref[0, 2:3, :] * y
    xo_ref[...] = xn
    h2 = _rms_full(xn, gffn_ref[...]) * (1.0 + mod_ref[0, 4:5, :]) + mod_ref[0, 3:4, :]
    h2b = h2.astype(BF16)
    h2_ref[...] = h2b

    h2lo = (h2 - h2b.astype(F32)).astype(BF16)
    part = _dot(h2b, wr_ref[...]) + _dot(h2lo, wr_ref[...])
    logits = part + pltpu.roll(part, LANES - ROUTE_SEL_LANE, axis=1) + br_ref[...]
    lane = lax.broadcasted_iota(jnp.int32, (tm, LANES), 1)
    lanef = lane.astype(F32)
    neg = jnp.float32(-1e30)
    big = jnp.float32(1e9)
    is_g = (lane >= N_EXPERTS) & (lane < N_EXPERTS + N_GROUPS)
    lg = jnp.where(is_g, logits, neg)
    gmax = lg.max(axis=-1, keepdims=True)
    g_idx = jnp.where(is_g & (lg == gmax), lanef, big).min(axis=-1, keepdims=True) - N_EXPERTS
    g_w = 1.0 / jnp.where(is_g, jnp.exp(lg - gmax), 0.0).sum(axis=-1, keepdims=True)
    in_grp = (lane < N_EXPERTS) & ((lane // EXPERTS_PER_GROUP).astype(F32) == g_idx)
    le = jnp.where(in_grp, logits, neg)
    m1 = le.max(axis=-1, keepdims=True)
    i1 = jnp.where(in_grp & (le == m1), lanef, big).min(axis=-1, keepdims=True)
    rest = in_grp & (lanef != i1)
    le2 = jnp.where(rest, logits, neg)
    m2 = le2.max(axis=-1, keepdims=True)
    i2 = jnp.where(rest & (le2 == m2), lanef, big).min(axis=-1, keepdims=True)
    r = jnp.exp(m2 - m1)
    w1 = 1.0 / (1.0 + r)
    w2 = r / (1.0 + r)
    cmb = g_w * jnp.where(lanef == i1, w1, jnp.where(lanef == i2, w2, 0.0))
    sel = pltpu.roll(jnp.where((lanef == i1) | (lanef == i2), 1.0, 0.0), ROUTE_SEL_LANE, axis=1)
    route_ref[...] = cmb + sel
    cnt_ref[0] = jnp.broadcast_to(sel.sum(axis=0, keepdims=True), (HALO, LANES))


def _merge_call(x2d, ya, yb, yc, yd, mod, lw, layer, *, tm, mod_index):
    n = x2d.shape[0]
    row = lambda i: (i, 0)
    in_specs = [pl.BlockSpec((tm, D_MODEL), row)]
    in_specs += [pl.BlockSpec((tm, GROUP_W), row)] * 4
    in_specs += [_mod_spec(layer, mod_index)]
    in_specs += [_layer_spec(lw[k], layer) for k in ("w_out", "g_ffn", "w_router", "b_router")]
    return pl.pallas_call(
        functools.partial(_merge_kernel, tm=tm),
        grid=(n // tm,),
        in_specs=in_specs,
        out_specs=[pl.BlockSpec((tm, D_MODEL), row), pl.BlockSpec((tm, D_MODEL), row),
                   pl.BlockSpec((tm, LANES), row), pl.BlockSpec((1, HALO, LANES), lambda i: (i, 0, 0))],
        out_shape=[jax.ShapeDtypeStruct((n, D_MODEL), F32), jax.ShapeDtypeStruct((n, D_MODEL), BF16),
                   jax.ShapeDtypeStruct((n, LANES), F32),
                   jax.ShapeDtypeStruct((n // tm, HALO, LANES), F32)],
        compiler_params=pltpu.CompilerParams(dimension_semantics=("parallel",),
                                             vmem_limit_bytes=VMEM_LIMIT),
        name="merge",
    )(x2d, ya, yb, yc, yd, mod, lw["w_out"], lw["g_ffn"], lw["w_router"], lw["b_router"])


def _segment_positions(route, loc, tm):
    lane = lax.broadcasted_iota(jnp.int32, (tm, LANES), 1)
    is_sel = (lane >= ROUTE_SEL_LANE) & (lane < ROUTE_SEL_LANE + N_EXPERTS)
    selm = is_sel & (route > 0.5)
    selb = jnp.where(selm, 1.0, 0.0).astype(BF16)
    r_i = lax.broadcasted_iota(jnp.int32, (tm, tm), 0)
    c_i = lax.broadcasted_iota(jnp.int32, (tm, tm), 1)
    earlier = jnp.where(c_i < r_i, 1.0, 0.0).astype(BF16)
    dest = loc + _dot(earlier, selb)
    big = jnp.float32(1e9)
    d_a = jnp.where(selm, dest, big).min(axis=-1, keepdims=True)
    d_b = jnp.where(selm, dest, -big).max(axis=-1, keepdims=True)
    return selm, dest, d_a, d_b


def _to_row(col, tm):
    r_i = lax.broadcasted_iota(jnp.int32, (tm, tm), 0)
    c_i = lax.broadcasted_iota(jnp.int32, (tm, tm), 1)
    return jnp.where(r_i == c_i, col, 0.0).sum(axis=0, keepdims=True)


def _chunk_copy_out(xs_ref, xbuf_ref, sem_ref, slot, src_chunk, dst_chunk):
    return pltpu.make_async_copy(xs_ref.at[slot, pl.ds(src_chunk * MOE_CHUNK, MOE_CHUNK), :],
                                 xbuf_ref.at[pl.ds(dst_chunk * MOE_CHUNK, MOE_CHUNK), :],
                                 sem_ref.at[slot])


def _dispatch_kernel(nch_ref, dst_ref, plo_ref, phi_ref, nused_ref, h2_ref, route_ref, loc_ref, xbuf_ref,
                     xs_ref, sem_ref, *, tm, n_tiles, n_work):
    i = pl.program_id(0)
    slot = i % 2
    zero_chunk = MOE_RLOC // MOE_CHUNK

    def start_tile(t_idx, slot_):
        def body(j, carry):
            _chunk_copy_out(xs_ref, xbuf_ref, sem_ref, slot_, j, dst_ref[t_idx * MOE_NCH + j]).start()
            return carry
        lax.fori_loop(0, nch_ref[t_idx], body, 0)

    def wait_tile(t_idx, slot_):
        def body(j, carry):
            _chunk_copy_out(xs_ref, xbuf_ref, sem_ref, slot_, 0, 0).wait()
            return carry
        lax.fori_loop(0, nch_ref[t_idx], body, 0)

    @pl.when(i == 0)
    def _():
        xs_ref[2, :, :] = jnp.zeros((MOE_RLOC + MOE_CHUNK, MOE_XW), BF16)

        def pad_start(j, carry):
            _chunk_copy_out(xs_ref, xbuf_ref, sem_ref, 2, zero_chunk, j).start()
            return carry

        def pad_wait(j, carry):
            _chunk_copy_out(xs_ref, xbuf_ref, sem_ref, 2, zero_chunk, 0).wait()
            return carry
        for e in range(N_EXPERTS):
            lax.fori_loop(plo_ref[e], phi_ref[e], pad_start, 0)
        for e in range(N_EXPERTS):
            lax.fori_loop(plo_ref[e], phi_ref[e], pad_wait, 0)

        def tail_copy(w):
            return pltpu.make_async_copy(xs_ref.at[2, pl.ds(0, MOE_WORK), :],
                                         xbuf_ref.at[pl.ds(w * MOE_WORK, MOE_WORK), :], sem_ref.at[2])

        def tail_start(w, carry):
            tail_copy(w).start()
            return carry

        def tail_wait(w, carry):
            tail_copy(w).wait()
            return carry
        lax.fori_loop(nused_ref[0], n_work, tail_start, 0)
        lax.fori_loop(nused_ref[0], n_work, tail_wait, 0)

    @pl.when(i >= 2)
    def _():
        wait_tile(i - 2, slot)

    route = route_ref[...]
    selm, dest, d_a, d_b = _segment_positions(route, loc_ref[0, 0:1, :], tm)
    cmb_al = pltpu.roll(route, ROUTE_SEL_LANE, axis=1)
    w_a = jnp.where(selm & (dest == d_a), cmb_al, 0.0).sum(axis=-1, keepdims=True)
    w_b = jnp.where(selm & (dest == d_b), cmb_al, 0.0).sum(axis=-1, keepdims=True)
    da_r, db_r, wa_r, wb_r = (_to_row(v, tm) for v in (d_a, d_b, w_a, w_b))
    rio = lax.broadcasted_iota(jnp.int32, (MOE_RLOC, tm), 0).astype(F32)
    hit_a = rio == da_r
    hit_b = rio == db_r
    perm = jnp.where(hit_a | hit_b, 1.0, 0.0).astype(BF16)
    xs_ref[slot, 0:MOE_RLOC, 0:D_MODEL] = _dot(perm, h2_ref[...]).astype(BF16)
    w_sorted = (jnp.where(hit_a, wa_r, 0.0) + jnp.where(hit_b, wb_r, 0.0)).sum(axis=-1, keepdims=True)
    w_hi = w_sorted.astype(BF16).astype(F32)
    r1 = w_sorted - w_hi
    w_mid = r1.astype(BF16).astype(F32)
    w_lo = r1 - w_mid
    lane = lax.broadcasted_iota(jnp.int32, (MOE_RLOC, LANES), 1)
    slab = jnp.where(lane == 0, w_hi, jnp.where(lane == 1, w_mid, jnp.where(lane == 2, w_lo, 0.0)))
    xs_ref[slot, 0:MOE_RLOC, D_MODEL:MOE_XW] = slab.astype(BF16)
    start_tile(i, slot)

    @pl.when(i == n_tiles - 1)
    def _():
        if n_tiles > 1:
            wait_tile(i - 1, 1 - slot)
        wait_tile(i, slot)


def _dispatch_call(h2, route, tabs, *, tm, n_rows):
    n = h2.shape[0]
    n_tiles = n // tm
    grid_spec = pltpu.PrefetchScalarGridSpec(
        num_scalar_prefetch=5,
        grid=(n_tiles,),
        in_specs=[pl.BlockSpec((tm, D_MODEL), lambda i, *_: (i, 0)),
                  pl.BlockSpec((tm, LANES), lambda i, *_: (i, 0)),
                  pl.BlockSpec((1, HALO, LANES), lambda i, *_: (i, 0, 0))],
        out_specs=pl.BlockSpec(memory_space=pl.ANY),
        scratch_shapes=[pltpu.VMEM((3, MOE_RLOC + MOE_CHUNK, MOE_XW), BF16),
                        pltpu.SemaphoreType.DMA((3,))],
    )
    return pl.pallas_call(
        functools.partial(_dispatch_kernel, tm=tm, n_tiles=n_tiles, n_work=n_rows // MOE_WORK),
        grid_spec=grid_spec,
        out_shape=jax.ShapeDtypeStruct((n_rows, MOE_XW), BF16),
        compiler_params=pltpu.CompilerParams(dimension_semantics=("arbitrary",),
                                             vmem_limit_bytes=VMEM_LIMIT),
        name="moe_dispatch",
    )(tabs["nch"], tabs["dst"], tabs["pad_lo"], tabs["pad_hi"], tabs["n_used"], h2, route, tabs["loc"])


def _expert_kernel(te_ref, nused_ref, x_ref, wup_ref, wdn_ref, y_ref, wup_bf, wdn_bf):
    w = pl.program_id(0)
    used = w < nused_ref[0]

    @pl.when((w == 0) | (te_ref[w] != te_ref[jnp.maximum(w - 1, 0)]))
    def _():
        wup_bf[...] = wup_ref[...].astype(BF16)
        wdn_bf[...] = wdn_ref[...].astype(BF16)

    @pl.when(used)
    def _():
        wt = x_ref[:, D_MODEL:MOE_XW].astype(F32).sum(axis=-1, keepdims=True)
        gu = _dot(x_ref[:, 0:D_MODEL], wup_bf[...])
        g = gu[:, 0:D_EXPERT]
        a = (g * jax.nn.sigmoid(g)) * gu[:, D_EXPERT:2 * D_EXPERT] * wt
        y_ref[...] = _dot(a.astype(BF16), wdn_bf[...]).astype(BF16)

    @pl.when(jnp.logical_not(used))
    def _():
        y_ref[...] = jnp.zeros(y_ref.shape, BF16)


def _expert_call(xbuf, tabs, w_up, w_down, layer):
    n_rows = xbuf.shape[0]
    n_work = n_rows // MOE_WORK
    base = layer * N_EXPERTS
    grid_spec = pltpu.PrefetchScalarGridSpec(
        num_scalar_prefetch=2,
        grid=(n_work,),
        in_specs=[pl.BlockSpec((MOE_WORK, MOE_XW), lambda w, te, nu: (jnp.minimum(w, nu[0] - 1), 0)),
                  pl.BlockSpec((None, D_MODEL, 2 * D_EXPERT), lambda w, te, nu: (base + te[w], 0, 0)),
                  pl.BlockSpec((None, D_EXPERT, D_MODEL), lambda w, te, nu: (base + te[w], 0, 0))],
        out_specs=pl.BlockSpec((MOE_WORK, D_MODEL), lambda w, te, nu: (w, 0)),
        scratch_shapes=[pltpu.VMEM((D_MODEL, 2 * D_EXPERT), BF16), pltpu.VMEM((D_EXPERT, D_MODEL), BF16)],
    )
    return pl.pallas_call(
        _expert_kernel,
        grid_spec=grid_spec,
        out_shape=jax.ShapeDtypeStruct((n_rows, D_MODEL), BF16),
        compiler_params=pltpu.CompilerParams(dimension_semantics=("arbitrary",),
                                             vmem_limit_bytes=VMEM_LIMIT),
        name="moe_experts",
    )(tabs["tile_expert"], tabs["n_used"], xbuf, w_up, w_down)


def _chunk_copy_in(ybuf_ref, ys_ref, sem_ref, slot, src_chunk, dst_chunk):
    return pltpu.make_async_copy(ybuf_ref.at[pl.ds(src_chunk * MOE_CHUNK, MOE_CHUNK), :],
                                 ys_ref.at[slot, pl.ds(dst_chunk * MOE_CHUNK, MOE_CHUNK), :],
                                 sem_ref.at[slot])


def _combine_kernel(nch_ref, dst_ref, route_ref, loc_ref, x_ref, mod_ref, ybuf_ref, o_ref,
                    ys_ref, sem_ref, *, tm, n_tiles):
    i = pl.program_id(0)
    slot = i % 2

    def start_tile(t_idx, slot_):
        def body(j, carry):
            _chunk_copy_in(ybuf_ref, ys_ref, sem_ref, slot_, dst_ref[t_idx * MOE_NCH + j], j).start()
            return carry
        lax.fori_loop(0, nch_ref[t_idx], body, 0)

    @pl.when(i == 0)
    def _():
        ys_ref[...] = jnp.zeros(ys_ref.shape, BF16)
        start_tile(0, 0)

    @pl.when(i + 1 < n_tiles)
    def _():
        start_tile(i + 1, 1 - slot)

    def wait_body(j, carry):
        _chunk_copy_in(ybuf_ref, ys_ref, sem_ref, slot, 0, 0).wait()
        return carry
    lax.fori_loop(0, nch_ref[i], wait_body, 0)

    _, _, d_a, d_b = _segment_positions(route_ref[...], loc_ref[0, 0:1, :], tm)
    lio = lax.broadcasted_iota(jnp.int32, (tm, MOE_RLOC), 1).astype(F32)
    perm_t = jnp.where((lio == d_a) | (lio == d_b), 1.0, 0.0).astype(BF16)
    y = _dot(perm_t, ys_ref[slot])
    o_ref[...] = x_ref[...] + mod_ref[0, 5:6, :] * y


def _combine_call(ybuf, route, x2d, mod, layer, tabs, *, tm, mod_index):
    n = x2d.shape[0]
    n_tiles = n // tm
    grid_spec = pltpu.PrefetchScalarGridSpec(
        num_scalar_prefetch=2,
        grid=(n_tiles,),
        in_specs=[pl.BlockSpec((tm, LANES), lambda i, *_: (i, 0)),
                  pl.BlockSpec((1, HALO, LANES), lambda i, *_: (i, 0, 0)),
                  pl.BlockSpec((tm, D_MODEL), lambda i, *_: (i, 0)),
                  _mod_spec(layer, mod_index),
                  pl.BlockSpec(memory_space=pl.ANY)],
        out_specs=pl.BlockSpec((tm, D_MODEL), lambda i, *_: (i, 0)),
        scratch_shapes=[pltpu.VMEM((2, MOE_RLOC, D_MODEL), BF16),
                        pltpu.SemaphoreType.DMA((2,))],
    )
    return pl.pallas_call(
        functools.partial(_combine_kernel, tm=tm, n_tiles=n_tiles),
        grid_spec=grid_spec,
        out_shape=jax.ShapeDtypeStruct((n, D_MODEL), F32),
        compiler_params=pltpu.CompilerParams(dimension_semantics=("arbitrary",),
                                             vmem_limit_bytes=VMEM_LIMIT),
        name="moe_combine",
    )(tabs["nch"], tabs["dst"], route, tabs["loc"], x2d, mod, ybuf)


def _moe_rows(n_tok):
    n_tiles = n_tok // MOE_T
    chunks = 2 * n_tok // MOE_CHUNK + (MOE_CHUNK - 1) * n_tiles + N_EXPERTS * (MOE_CPW - 1)
    return -(-chunks // MOE_CPW) * MOE_WORK


def _route_tables(counts, n_tok):
    n_work = _moe_rows(n_tok) // MOE_WORK
    cnt = counts[:, 0, ROUTE_SEL_LANE:ROUTE_SEL_LANE + N_EXPERTS].astype(jnp.int32)
    n_tiles = cnt.shape[0]
    nchunk = (cnt + MOE_CHUNK - 1) // MOE_CHUNK
    ei = jnp.arange(N_EXPERTS, dtype=jnp.int32)
    ti = jnp.arange(n_tiles, dtype=jnp.int32)
    upto_e = (ei[:, None] <= ei[None, :]).astype(jnp.int32)
    upto_t = (ti[None, :] <= ti[:, None]).astype(jnp.int32)
    ends = (nchunk[:, :, None] * upto_e[None, :, :]).sum(axis=1)
    loc = ends - nchunk
    nch = nchunk.sum(axis=1)
    per_e = nchunk.sum(axis=0)
    padded_e = (per_e + MOE_CPW - 1) // MOE_CPW * MOE_CPW
    e_end = (padded_e[:, None] * upto_e).sum(axis=0)
    e_base = e_end - padded_e
    glob = e_base[None, :] + (upto_t[:, :, None] * nchunk[None, :, :]).sum(axis=1) - nchunk
    j = jnp.arange(MOE_NCH, dtype=jnp.int32)
    e_of_j = (ends[:, None, :] <= j[None, :, None]).sum(axis=-1)
    seg = e_of_j[:, :, None] == ei[None, None, :]
    dst = jnp.where(seg, (glob - loc)[:, None, :], 0).sum(axis=-1) + j[None, :]
    dst = jnp.where(j[None, :] < nch[:, None], dst, 0)
    n_used = padded_e.sum() // MOE_CPW
    w = jnp.minimum(jnp.arange(n_work, dtype=jnp.int32), n_used - 1)
    tile_expert = jnp.minimum(((e_end // MOE_CPW)[None, :] <= w[:, None]).sum(axis=-1), N_EXPERTS - 1)
    loc_rows = jnp.pad((loc * MOE_CHUNK).astype(F32),
                       ((0, 0), (ROUTE_SEL_LANE, LANES - ROUTE_SEL_LANE - N_EXPERTS)))
    loc_rows = jnp.broadcast_to(loc_rows[:, None, :], (n_tiles, HALO, LANES))
    return {"nch": nch.astype(jnp.int32), "dst": dst.reshape(-1).astype(jnp.int32),
            "pad_lo": (e_base + per_e).astype(jnp.int32), "pad_hi": e_end.astype(jnp.int32),
            "tile_expert": tile_expert.astype(jnp.int32), "n_used": n_used.astype(jnp.int32).reshape(1),
            "loc": loc_rows}


def _moe_call(h2, route, counts, x2d, mod, w_up, w_down, layer, *, mod_index):
    n = x2d.shape[0]
    tabs = _route_tables(counts, n)
    xbuf = _dispatch_call(h2, route, tabs, tm=MOE_T, n_rows=_moe_rows(n))
    ybuf = _expert_call(xbuf, tabs, w_up, w_down, layer)
    return _combine_call(ybuf, route, x2d, mod, layer, tabs, tm=MOE_T, mod_index=mod_index)


def _ones_blocks(sizes):
    n = sum(sizes)
    m = np.zeros((n, n), np.float32)
    o = 0
    for s in sizes:
        m[o:o + s, o:o + s] = 1.0
        o += s
    return jnp.asarray(m, BF16)


def _inv_counts_q():
    return jnp.asarray(np.tile(np.concatenate([np.full(64, 1 / 64.0), np.full(64, 1 / 32.0)]), 2)[None, :], F32)


def _rope_tables(rows):
    r = np.repeat(np.arange(rows, dtype=np.float32), GRID_W)
    c = np.tile(np.arange(GRID_W, dtype=np.float32), rows)

    def tab(rot_dim):
        nf = rot_dim // 4
        inv = (np.float32(ROPE_THETA) ** (-np.arange(nf, dtype=np.float32) / np.float32(nf))).astype(np.float32)
        ang = np.concatenate([r[:, None] * inv, c[:, None] * inv], axis=-1).astype(np.float32)
        cos = np.repeat(np.cos(ang), 2, axis=-1)
        sin = np.repeat(np.sin(ang), 2, axis=-1) * np.tile(np.asarray([-1.0, 1.0], np.float32), rot_dim // 2)
        return cos.astype(np.float32), sin.astype(np.float32)

    s = rows * GRID_W
    cc, sc = tab(C_HEAD_DIM)
    cd, sd = tab(D_ROPE)
    one = lambda w: np.ones((s, w), np.float32)
    zero = lambda w: np.zeros((s, w), np.float32)
    tabs = (np.tile(cc, (1, 2)), np.tile(sc, (1, 2)),
            np.concatenate([one(64), cd, one(32)], axis=1), np.concatenate([zero(64), sd, zero(32)], axis=1),
            np.concatenate([cd, one(96)], axis=1), np.concatenate([sd, zero(96)], axis=1))
    return tuple(jnp.asarray(t) for t in tabs)


def _stacked_weights(p):
    (g_attn_norm, w_in, sgu_v_gain, sgu_w, sgu_b, conv_w, gqa_q_gain, gqa_k_gain, mla_qa_gain, mla_w_qb,
     mla_kva_gain, mla_w_kvb, mla_q_gain, mla_k_gain, g_group_out, w_out, g_ffn_norm,
     w_rg, b_rg, w_re, b_re) = p
    depth = w_in.shape[0]

    def heads_0213(a, axis, off):
        sl = lambda lo: lax.slice_in_dim(a, off + lo, off + lo + 64, axis=axis)
        return [sl(0), sl(128), sl(64), sl(192)]

    zeros = lambda *shape: jnp.zeros((depth,) + shape, F32)
    row = lambda a: a[:, None, :]
    w_in_p = jnp.concatenate([w_in[:, :, :1280]] + heads_0213(w_in, 2, 1280) +
                             [w_in[:, :, 1536:2208], zeros(D_MODEL, IN_W - 2208)], axis=2).astype(BF16)
    wqb = jnp.concatenate([jnp.concatenate([mla_w_qb[:, :, h * 96:h * 96 + 96], zeros(D_Q_LORA, 32)], axis=2)
                           for h in range(D_HEADS)], axis=2)
    wk = jnp.concatenate([jnp.concatenate([mla_w_kvb[:, :, h * 128:h * 128 + 64], zeros(D_KV_LORA, 64)], axis=2)
                          for h in range(D_HEADS)], axis=2)
    wv = jnp.concatenate([mla_w_kvb[:, :, h * 128 + 64:h * 128 + 128] for h in range(D_HEADS)], axis=2)
    qslab = jnp.concatenate([mla_q_gain, zeros(32)], axis=1)
    kslab = jnp.concatenate([mla_k_gain[:, :64], zeros(64)], axis=1)
    g_go = jnp.concatenate([g_group_out[:, :512]] + heads_0213(g_group_out, 1, 512) + [g_group_out[:, 768:]],
                           axis=1)
    w_out_p = jnp.concatenate([w_out[:, :512]] + heads_0213(w_out, 1, 512) + [w_out[:, 768:]],
                              axis=1).astype(BF16)
    w_r = jnp.concatenate([w_re, w_rg], axis=2)
    w_r_hi = w_r.astype(BF16)
    w_r_lo = (w_r - w_r_hi.astype(F32)).astype(BF16)
    n_r = N_EXPERTS + N_GROUPS
    zb = lambda w: jnp.zeros((depth, D_MODEL, w), BF16)
    w_router = jnp.concatenate([w_r_hi, zb(ROUTE_SEL_LANE - n_r), w_r_lo, zb(LANES - ROUTE_SEL_LANE - n_r)], axis=2)
    b_router = jnp.concatenate([b_re, b_rg, zeros(LANES - n_r)], axis=1)
    sgub_full = jnp.repeat(jnp.swapaxes(sgu_b, 1, 2), 64, axis=2)
    return {
        "g_attn": row(g_attn_norm), "w_in": w_in_p,
        "sgu_w": sgu_w.reshape(depth, A_HEADS * CHUNK, CHUNK).astype(BF16), "sgu_b": sgub_full,
        "v_gain": row(sgu_v_gain), "conv_w": conv_w, "g_go": row(g_go),
        "cq_gain": row(jnp.tile(gqa_q_gain, (1, 4))), "ck_gain": row(jnp.tile(gqa_k_gain, (1, 2))),
        "qa_gain": row(mla_qa_gain), "w_qb": wqb.astype(BF16), "kva_gain": row(mla_kva_gain),
        "w_kvb": jnp.concatenate([wk, wv], axis=2).astype(BF16),
        "dq_gain": row(jnp.tile(qslab, (1, 4))), "dk_gain": row(jnp.tile(kslab, (1, 4))),
        "dkr_gain": row(jnp.concatenate([mla_k_gain[:, 64:], zeros(96)], axis=1)),
        "w_out": w_out_p, "g_ffn": row(g_ffn_norm), "w_router": w_router, "b_router": row(b_router),
    }


def kernel(x, c, ctx, c_ctx, w_mod, b_mod, g_attn_norm, w_in, sgu_v_gain, sgu_w, sgu_b, conv_w, gqa_q_gain,
           gqa_k_gain, mla_qa_gain, mla_w_qb, mla_kva_gain, mla_w_kvb, mla_q_gain, mla_k_gain, g_group_out,
           w_out, g_ffn_norm, w_router_group, b_router_group, w_router_expert, b_router_expert,
           w_expert_up, w_expert_down):
    b, s, d = x.shape
    lc = ctx.shape[1]
    depth = w_mod.shape[0]
    lw = _stacked_weights((g_attn_norm, w_in, sgu_v_gain, sgu_w, sgu_b, conv_w, gqa_q_gain, gqa_k_gain,
                           mla_qa_gain, mla_w_qb, mla_kva_gain, mla_w_kvb, mla_q_gain, mla_k_gain, g_group_out,
                           w_out, g_ffn_norm, w_router_group, b_router_group, w_router_expert, b_router_expert))
    w_up = w_expert_up.reshape(depth * N_EXPERTS, d, 2 * D_EXPERT)
    w_down = w_expert_down.reshape(depth * N_EXPERTS, D_EXPERT, d)

    c_all = jnp.concatenate([c, c_ctx[None, :], jnp.zeros((16 - b - 1, d), F32)], axis=0)
    mod = _modulation(c_all, w_mod, b_mod).reshape(depth, 16, 6, d)
    rope = _rope_tables(s // GRID_W)

    tm = MOE_T
    tq = 512
    tpb = s // tm
    lat_idx = lambda i: i // tpb
    ctx_idx = lambda i: b
    r3 = lambda a, n: a.reshape(b, n, a.shape[-1])

    xl = x.reshape(b * s, d)
    xc = ctx.reshape(b * lc, d)
    for l in range(depth):
        update_ctx = l < depth - 1
        ya_c, yb_c, qc_c, kc_c, vc_c, qd_c, kd_c, vd_c = _in_call(
            xc, mod, lw, l, None, tm=tm, seq_len=lc, mod_index=ctx_idx)
        ya, yb, qc, kc, vc, qd, kd, vd = _in_call(xl, mod, lw, l, rope, tm=IN_T, seq_len=s,
                                                  mod_index=lambda i: i // (s // IN_T))
        yc = _attn_call(r3(qc, s), [r3(kc_c, lc), r3(kc, s)], [r3(vc_c, lc), r3(vc, s)], lw["g_go"], l,
                        mode="C", tq=tq)
        yd = _attn_call(r3(qd, s), [r3(kd_c, lc), r3(kd, s)], [r3(vd_c, lc), r3(vd, s)], lw["g_go"], l,
                        mode="D", tq=tq)
        xl, h2, route, counts = _merge_call(xl, ya, yb, yc.reshape(b * s, GROUP_W), yd.reshape(b * s, GROUP_W),
                                            mod, lw, l, tm=tm, mod_index=lat_idx)
        xl = _moe_call(h2, route, counts, xl, mod, w_up, w_down, l, mod_index=lat_idx)
        if update_ctx:
            yc_c = _attn_call(r3(qc_c, lc), [r3(kc_c, lc)], [r3(vc_c, lc)], lw["g_go"], l, mode="C", tq=lc)
            yd_c = _attn_call(r3(qd_c, lc), [r3(kd_c, lc)], [r3(vd_c, lc)], lw["g_go"], l, mode="D", tq=lc)
            xc, h2c, route_c, counts_c = _merge_call(xc, ya_c, yb_c, yc_c.reshape(b * lc, GROUP_W),
                                                     yd_c.reshape(b * lc, GROUP_W), mod, lw, l, tm=tm,
                                                     mod_index=ctx_idx)
            xc = _moe_call(h2c, route_c, counts_c, xc, mod, w_up, w_down, l, mod_index=ctx_idx)
    return xl.reshape(b, s, d)
```

```python
import functools

import numpy as np
import jax
import jax.numpy as jnp
from jax import lax
from jax.experimental import pallas as pl
from jax.experimental.pallas import tpu as pltpu

F32 = jnp.float32
BF16 = jnp.bfloat16

D_MODEL = 1024
GRID_W = 64
EPS = 1e-6
ROPE_THETA = 10000.0
GROUP_W = 256
CHUNK = 128
A_HEADS = 4
C_HEAD_DIM = 64
C_SCALE = C_HEAD_DIM ** -0.5
D_HEADS = 4
D_NOPE = 64
D_ROPE = 32
D_V = 64
D_Q_LORA = 256
D_KV_LORA = 128
D_SCALE = (D_NOPE + D_ROPE) ** -0.5
LOG2E = 1.4426950408889634
ATTN_ROWS = 16
MERGE_ROWS = 256
N_GROUPS = 4
EXPERTS_PER_GROUP = 4
N_EXPERTS = 16
D_EXPERT = 256
LANES = 128
HALO = 8

IN_W = 2304
IN_T = 512
IN_SECTIONS = ((0, 256), (256, 512), (512, 768), (768, 1024), (1024, 1280), (1280, 1536), (1536, 1664),
               (1664, 1792), (1792, 2048), (2048, 2176), (2176, 2304))

ROUTE_SEL_LANE = 32
MOE_T = 512
MOE_CHUNK = 16
MOE_RLOC = 1280
MOE_NCH = MOE_RLOC // MOE_CHUNK
MOE_RBLK = 256
MOE_XW = D_MODEL + LANES
MOE_WORK = 512
MOE_CPW = MOE_WORK // MOE_CHUNK
VMEM_LIMIT = 56 * 1024 * 1024


def _nt_dot(a, b):
    return lax.dot_general(a, b, (((1,), (1,)), ((), ())), preferred_element_type=F32)


def _dot(a, b):
    return jnp.dot(a, b, preferred_element_type=F32)


def _rms_full(x, gain):
    return x * lax.rsqrt(jnp.mean(x * x, axis=-1, keepdims=True) + EPS) * gain


def _group_ms(x, ones_blocks, inv_count):
    return _dot((x * x).astype(BF16), ones_blocks) * inv_count


def _pair_swap(x, even):
    nxt = pltpu.roll(x, x.shape[1] - 1, axis=1)
    prv = pltpu.roll(x, 1, axis=1)
    return jnp.where(even, nxt, prv)


def _rope(x, cos, sin, even):
    return x * cos + _pair_swap(x, even) * sin


def _mod_kernel(c_ref, w_ref, b_ref, o_ref):
    cv = c_ref[...]
    sc = cv * jax.nn.sigmoid(cv)
    o_ref[0] = jnp.dot(sc, w_ref[0], preferred_element_type=F32,
                       precision=lax.Precision.HIGHEST) + b_ref[0]


def _modulation(c_all, w_mod, b_mod):
    depth = w_mod.shape[0]
    nblk = 4
    wblk = 6 * D_MODEL // nblk
    return pl.pallas_call(
        _mod_kernel,
        grid=(depth, nblk),
        in_specs=[pl.BlockSpec((16, D_MODEL), lambda l, j: (0, 0)),
                  pl.BlockSpec((1, D_MODEL, wblk), lambda l, j: (l, 0, j)),
                  pl.BlockSpec((1, 1, wblk), lambda l, j: (l, 0, j))],
        out_specs=pl.BlockSpec((1, 16, wblk), lambda l, j: (l, 0, j)),
        out_shape=jax.ShapeDtypeStruct((depth, 16, 6 * D_MODEL), F32),
        compiler_params=pltpu.CompilerParams(dimension_semantics=("arbitrary", "arbitrary"),
                                             vmem_limit_bytes=VMEM_LIMIT),
        name="mod",
    )(c_all, w_mod, b_mod.reshape(depth, 1, 6 * D_MODEL))


def _in_kernel(*refs, tm, seq_len, use_rope):
    n_io = len(refs) - 4
    pj0_ref, pj1_ref, zh0_ref, zh1_ref = refs[n_io:]
    i = pl.program_id(0)

    @pl.when(i == 0)
    def _():
        pj1_ref[...] = jnp.zeros(pj1_ref.shape, F32)
        zh1_ref[...] = jnp.zeros(zh1_ref.shape, F32)

    @pl.when(i % 2 == 0)
    def _():
        _in_stages(refs[:n_io], pj0_ref, zh0_ref, pj1_ref, zh1_ref, tm=tm, seq_len=seq_len, use_rope=use_rope)

    @pl.when(i % 2 == 1)
    def _():
        _in_stages(refs[:n_io], pj1_ref, zh1_ref, pj0_ref, zh0_ref, tm=tm, seq_len=seq_len, use_rope=use_rope)


def _in_stages(refs, pj_w, zh_w, pj_r, zh_r, *, tm, seq_len, use_rope):
    (x_ref, xp_ref, xn_ref, mod_ref, gattn_ref, win_ref, sguw_ref, sgub_ref, vgain_ref, convw_ref,
     ggo_ref, cqg_ref, ckg_ref, qag_ref, wqb_ref, kvag_ref, wkvb_ref, dqg_ref, dkg_ref, dkrg_ref,
     b64_ref, bq_ref, invq_ref) = refs[:23]
    pos = 23
    if use_rope:
        cosc_ref, sinc_ref, cosq_ref, sinq_ref, cosk_ref, sink_ref = refs[pos:pos + 6]
        pos += 6
    ya_ref, yb_ref, qc_ref, kc_ref, vc_ref, qd_ref, kd_ref, vd_ref = refs[pos:pos + 8]

    shift = mod_ref[0, 0:1, :]
    scale = mod_ref[0, 1:2, :]
    gattn = gattn_ref[...]

    def prenorm(xv):
        return (_rms_full(xv, gattn) * (1.0 + scale) + shift).astype(BF16)

    hb = prenorm(x_ref[...])

    def project(first, last):
        for lo, hi in IN_SECTIONS[first:last]:
            pj_w[:, lo:hi] = _dot(hb, win_ref[:, lo:hi])

    project(0, 3)
    hh = prenorm(jnp.concatenate([xp_ref[...], xn_ref[...]], axis=0))
    zh_w[...] = _dot(hh, win_ref[:, 768:1024]) * _dot(hh, win_ref[:, 1024:1280])

    i = jnp.maximum(pl.program_id(0) - 1, 0)
    proj = {ab: pj_r[:, ab[0]:ab[1]] for ab in IN_SECTIONS}
    zh = zh_r[...]
    b64 = b64_ref[...]
    inv64 = 1.0 / 64.0
    lane128 = lax.broadcasted_iota(jnp.int32, (tm, LANES), 1)
    even = (lane128 & 1) == 0

    u = jax.nn.gelu(proj[(0, 256)])
    v = jax.nn.gelu(proj[(256, 512)])
    v = (v * lax.rsqrt(_group_ms(v, b64, inv64) + EPS) * vgain_ref[...]).astype(BF16)
    lane_head = lax.broadcasted_iota(jnp.int32, (CHUNK, GROUP_W), 1) // 64
    sguw = sguw_ref[...]
    mixed_chunks = []
    for ci in range(tm // CHUNK):
        r = _dot(sguw, v[ci * CHUNK:(ci + 1) * CHUNK, :])
        m = r[0:CHUNK]
        for h in range(1, A_HEADS):
            m = jnp.where(lane_head == h, r[h * CHUNK:(h + 1) * CHUNK], m)
        mixed_chunks.append(m + sgub_ref[...])
    ya = u * jnp.concatenate(mixed_chunks, axis=0)
    ya_ref[...] = _rms_full(ya, ggo_ref[:, 0:256]).astype(BF16)

    project(3, 6)
    bg = proj[(512, 768)]
    z = proj[(768, 1024)] * proj[(1024, 1280)]
    row = lax.broadcasted_iota(jnp.int32, (tm, GROUP_W), 0)
    posn = (i * tm + row) % seq_len
    z_prev = jnp.where(row == 0, zh[HALO - 1:HALO, :], pltpu.roll(z, 1, axis=0))
    z_prev = jnp.where(posn == 0, 0.0, z_prev)
    z_next = jnp.where(row == tm - 1, zh[HALO:HALO + 1, :], pltpu.roll(z, tm - 1, axis=0))
    z_next = jnp.where(posn == seq_len - 1, 0.0, z_next)
    yb = bg * (convw_ref[0:1, :] * z_prev + convw_ref[1:2, :] * z + convw_ref[2:3, :] * z_next)
    yb_ref[...] = _rms_full(yb, ggo_ref[:, 256:512]).astype(BF16)

    project(6, 9)
    q = proj[(1280, 1536)]
    q = q * lax.rsqrt(_group_ms(q, b64, inv64) + EPS) * cqg_ref[...]
    k = proj[(1536, 1664)]
    k = k * lax.rsqrt(_group_ms(k, b64[0:128, 0:128], inv64) + EPS) * ckg_ref[...]
    if use_rope:
        cosc, sinc = cosc_ref[...], sinc_ref[...]
        q = jnp.concatenate([_rope(q[:, 0:128], cosc, sinc, even),
                             _rope(q[:, 128:256], cosc, sinc, even)], axis=1)
        k = _rope(k, cosc, sinc, even)
    qc_ref[...] = (q * (C_SCALE * LOG2E)).astype(BF16)
    kc_ref[...] = k.astype(BF16)
    vc_ref[...] = proj[(1664, 1792)].astype(BF16)

    project(9, len(IN_SECTIONS))
    cq = _rms_full(proj[(1792, 2048)], qag_ref[...]).astype(BF16)
    qd = _dot(cq, wqb_ref[...])
    bq = bq_ref[...]
    invq = invq_ref[...]
    halves = []
    for hf in range(2):
        qh = qd[:, hf * 256:(hf + 1) * 256]
        qh = qh * lax.rsqrt(_group_ms(qh, bq, invq) + EPS) * dqg_ref[:, hf * 256:(hf + 1) * 256]
        if use_rope:
            cosq, sinq = cosq_ref[...], sinq_ref[...]
            qh = jnp.concatenate([_rope(qh[:, 0:128], cosq, sinq, even),
                                  _rope(qh[:, 128:256], cosq, sinq, even)], axis=1)
        halves.append(qh)
    qd_ref[...] = (jnp.concatenate(halves, axis=1) * (D_SCALE * LOG2E)).astype(BF16)

    ckv = _rms_full(proj[(2048, 2176)], kvag_ref[...]).astype(BF16)
    kv = _dot(ckv, wkvb_ref[...])
    vd_ref[...] = kv[:, 512:768].astype(BF16)
    kr = proj[(2176, 2304)]
    kr = kr * lax.rsqrt(jnp.sum(kr * kr, axis=-1, keepdims=True) * (1.0 / D_ROPE) + EPS) * dkrg_ref[...]
    if use_rope:
        kr = _rope(kr, cosk_ref[...], sink_ref[...], even)
    kr = pltpu.roll(kr, D_NOPE, axis=1)
    for hf in range(2):
        kh = kv[:, hf * 256:(hf + 1) * 256]
        kh = kh * lax.rsqrt(_group_ms(kh, b64, inv64) + EPS) * dkg_ref[:, hf * 256:(hf + 1) * 256]
        kd_ref[:, hf * 256:hf * 256 + 128] = (kh[:, 0:128] + kr).astype(BF16)
        kd_ref[:, hf * 256 + 128:(hf + 1) * 256] = (kh[:, 128:256] + kr).astype(BF16)


def _layer_spec(a, layer):
    zeros = (0,) * (a.ndim - 1)
    return pl.BlockSpec((None,) + a.shape[1:], lambda *_: (layer,) + zeros)


def _mod_spec(layer, mod_index):
    return pl.BlockSpec((None, 1, 6, D_MODEL), lambda i, *_: (layer, mod_index(i), 0, 0))


def _in_call(x2d, mod, lw, layer, rope, *, tm, seq_len, mod_index):
    n = x2d.shape[0]
    nt = n // tm
    nb8 = n // HALO
    tb = tm // HALO
    use_rope = rope is not None
    cur = lambda i: jnp.minimum(i, nt - 1)
    prev = lambda i: jnp.maximum(i - 1, 0)
    row = lambda i: (prev(i), 0)
    in_specs = [
        pl.BlockSpec((tm, D_MODEL), lambda i: (cur(i), 0)),
        pl.BlockSpec((HALO, D_MODEL), lambda i: (jnp.maximum(cur(i) * tb - 1, 0), 0)),
        pl.BlockSpec((HALO, D_MODEL), lambda i: (jnp.minimum((cur(i) + 1) * tb, nb8 - 1), 0)),
        _mod_spec(layer, lambda i: mod_index(cur(i))),
    ]
    args = [x2d, x2d, x2d, mod]
    for name in ("g_attn", "w_in", "sgu_w", "sgu_b", "v_gain", "conv_w", "g_go", "cq_gain", "ck_gain",
                 "qa_gain", "w_qb", "kva_gain", "w_kvb", "dq_gain", "dk_gain", "dkr_gain"):
        in_specs.append(_layer_spec(lw[name], layer))
        args.append(lw[name])
    for a in (_ones_blocks([64] * 4), _ones_blocks([64, 32, 32] * 2), _inv_counts_q()):
        in_specs.append(pl.BlockSpec(a.shape, lambda i: (0, 0)))
        args.append(a)
    if use_rope:
        ntab = seq_len // tm
        for a in rope:
            in_specs.append(pl.BlockSpec((tm, LANES), lambda i: (prev(i) % ntab, 0)))
            args.append(a)
    widths = (256, 256, 256, 128, 128, 512, 512, 256)
    out_specs = [pl.BlockSpec((tm, w), row) for w in widths]
    out_shape = [jax.ShapeDtypeStruct((n, w), BF16) for w in widths]
    return pl.pallas_call(
        functools.partial(_in_kernel, tm=tm, seq_len=seq_len, use_rope=use_rope),
        grid=(nt + 1,),
        in_specs=in_specs,
        out_specs=out_specs,
        out_shape=out_shape,
        scratch_shapes=[pltpu.VMEM((tm, IN_W), F32), pltpu.VMEM((tm, IN_W), F32),
                        pltpu.VMEM((2 * HALO, GROUP_W), F32), pltpu.VMEM((2 * HALO, GROUP_W), F32)],
        compiler_params=pltpu.CompilerParams(dimension_semantics=("arbitrary",),
                                             vmem_limit_bytes=VMEM_LIMIT),
        name="in_proj",
    )(*args)


def _attn_kernel(*refs, mode, nseg, tq):
    q_ref = refs[0]
    k_refs = refs[1:1 + nseg]
    v_refs = refs[1 + nseg:1 + 2 * nseg]
    gain_ref = refs[1 + 2 * nseg]
    o_ref = refs[2 + 2 * nseg]
    p_ref = refs[3 + 2 * nseg]
    nks = [r.shape[1] for r in k_refs]
    offs = [sum(nks[:j]) for j in range(nseg)]
    n_chunks = tq // ATTN_ROWS

    vs = [r[0] for r in v_refs]
    if mode == "C":
        ks_c = [r[0] for r in k_refs]
        first_half = lax.broadcasted_iota(jnp.int32, (tq, LANES), 1) < 64
        heads = []
        for sl in range(2):
            qs = q_ref[0, :, sl * 128:(sl + 1) * 128]
            heads.append((jnp.where(first_half, qs, jnp.zeros_like(qs)), ks_c))
            heads.append((jnp.where(first_half, jnp.zeros_like(qs), qs), ks_c))
    else:
        heads = [(q_ref[0, :, h * 128:(h + 1) * 128], [r[0, :, h * 128:(h + 1) * 128] for r in k_refs])
                 for h in range(D_HEADS)]

    def score(h):
        qm, ks = heads[h]
        return [_nt_dot(qm, kk) for kk in ks]

    def softmax(h, ss):
        sums = []
        for c in range(n_chunks):
            rows = slice(c * ATTN_ROWS, (c + 1) * ATTN_ROWS)
            m = ss[0][rows].max(axis=-1, keepdims=True)
            for s in ss[1:]:
                m = jnp.maximum(m, s[rows].max(axis=-1, keepdims=True))
            tot = None
            for s, off, nk in zip(ss, offs, nks):
                p = jnp.exp2(s[rows] - m).astype(BF16)
                ps = p.astype(F32).sum(axis=-1, keepdims=True)
                tot = ps if tot is None else tot + ps
                p_ref[h, rows, off:off + nk] = p
            sums.append(tot)
        return jnp.concatenate(sums, axis=0)

    def values(h, row_sum):
        o = None
        for off, nk, vv in zip(offs, nks, vs):
            part = _dot(p_ref[h, :, off:off + nk], vv)
            o = part if o is None else o + part
        return o / row_sum

    n_heads = len(heads)
    scores = {0: score(0), 1: score(1)}
    row_sums = {}
    outs = []
    for h in range(n_heads):
        row_sums[h] = softmax(h, scores.pop(h))
        if h + 2 < n_heads:
            scores[h + 2] = score(h + 2)
        if h >= 1:
            outs.append(values(h - 1, row_sums.pop(h - 1)))
    outs.append(values(n_heads - 1, row_sums.pop(n_heads - 1)))

    if mode == "C":
        out = jnp.concatenate([jnp.where(first_half, outs[0], outs[1]),
                               jnp.where(first_half, outs[2], outs[3])], axis=1)
    else:
        lane_head = lax.broadcasted_iota(jnp.int32, (tq, GROUP_W), 1) // D_V
        out = outs[0]
        for h in range(1, D_HEADS):
            out = jnp.where(lane_head == h, outs[h], out)
    o_ref[0] = _rms_full(out, gain_ref[...]).astype(BF16)


def _attn_call(q, ks, vs, g_go, layer, *, mode, tq):
    b, s, wq = q.shape
    nseg = len(ks)
    gain_block = 2 if mode == "C" else 3
    in_specs = [pl.BlockSpec((1, tq, wq), lambda bi, qi: (bi, qi, 0))]
    for a in list(ks) + list(vs):
        in_specs.append(pl.BlockSpec((1,) + a.shape[1:], lambda bi, qi: (bi, 0, 0)))
    in_specs.append(pl.BlockSpec((None, 1, GROUP_W), lambda bi, qi: (layer, 0, gain_block)))
    return pl.pallas_call(
        functools.partial(_attn_kernel, mode=mode, nseg=nseg, tq=tq),
        grid=(b, s // tq),
        in_specs=in_specs,
        out_specs=pl.BlockSpec((1, tq, GROUP_W), lambda bi, qi: (bi, qi, 0)),
        out_shape=jax.ShapeDtypeStruct((b, s, GROUP_W), BF16),
        scratch_shapes=[pltpu.VMEM((4, tq, sum(a.shape[1] for a in ks)), BF16)],
        compiler_params=pltpu.CompilerParams(dimension_semantics=("parallel", "parallel"),
                                             vmem_limit_bytes=VMEM_LIMIT),
        name="attn_" + mode,
    )(q, *ks, *vs, g_go)


def _merge_kernel(x_ref, ya_ref, yb_ref, yc_ref, yd_ref, mod_ref, wout_ref, gffn_ref, wr_ref, br_ref,
                  xo_ref, h2_ref, route_ref, cnt_ref, *, tm):
    nb = tm // MERGE_ROWS
    blocks = [slice(j * MERGE_ROWS, (j + 1) * MERGE_ROWS) for j in range(nb)]
    ys = []
    for rs in blocks:
        y = _dot(ya_ref[rs, :], wout_ref[0:256, :])
        y = y + _dot(yb_ref[rs, :], wout_ref[256:512, :])
        y = y + _dot(yc_ref[rs, :], wout_ref[512:768, :])
        y = y + _dot(yd_ref[rs, :], wout_ref[768:1024, :])
        ys.append(y)
    counts = None
    for rs, y in zip(blocks, ys):
        sel = _merge_block(rs, y, x_ref, mod_ref, gffn_ref, wr_ref, br_ref, xo_ref, h2_ref, route_ref)
        part = sel.sum(axis=0, keepdims=True)
        counts = part if counts is None else counts + part
    cnt_ref[0] = jnp.broadcast_to(counts, (HALO, LANES))


def _merge_block(rs, y, x_ref, mod_ref, gffn_ref, wr_ref, br_ref, xo_ref, h2_ref, route_ref):
    rows = rs.stop - rs.start
    xn = x_ref[rs, :] + mod_ref[0, 2:3, :] * y
    xo_ref[rs, :] = xn
    h2 = _rms_full(xn, gffn_ref[...]) * (1.0 + mod_ref[0, 4:5, :]) + mod_ref[0, 3:4, :]
    h2b = h2.astype(BF16)
    h2_ref[rs, :] = h2b

    h2lo = (h2 - h2b.astype(F32)).astype(BF16)
    part = _dot(h2b, wr_ref[...]) + _dot(h2lo, wr_ref[...])
    logits = part + pltpu.roll(part, LANES - ROUTE_SEL_LANE, axis=1) + br_ref[...]
    lane = lax.broadcasted_iota(jnp.int32, (rows, LANES), 1)
    lanef = lane.astype(F32)
    neg = jnp.float32(-1e30)
    big = jnp.float32(1e9)
    is_g = (lane >= N_EXPERTS) & (lane < N_EXPERTS + N_GROUPS)
    lg = jnp.where(is_g, logits, neg)
    gmax = lg.max(axis=-1, keepdims=True)
    g_idx = jnp.where(is_g & (lg == gmax), lanef, big).min(axis=-1, keepdims=True) - N_EXPERTS
    g_w = 1.0 / jnp.where(is_g, jnp.exp(lg - gmax), 0.0).sum(axis=-1, keepdims=True)
    in_grp = (lane < N_EXPERTS) & ((lane // EXPERTS_PER_GROUP).astype(F32) == g_idx)
    le = jnp.where(in_grp, logits, neg)
    m1 = le.max(axis=-1, keepdims=True)
    i1 = jnp.where(in_grp & (le == m1), lanef, big).min(axis=-1, keepdims=True)
    rest = in_grp & (lanef != i1)
    le2 = jnp.where(rest, logits, neg)
    m2 = le2.max(axis=-1, keepdims=True)
    i2 = jnp.where(rest & (le2 == m2), lanef, big).min(axis=-1, keepdims=True)
    r = jnp.exp(m2 - m1)
    w1 = 1.0 / (1.0 + r)
    w2 = r / (1.0 + r)
    cmb = g_w * jnp.where(lanef == i1, w1, jnp.where(lanef == i2, w2, 0.0))
    sel = pltpu.roll(jnp.where((lanef == i1) | (lanef == i2), 1.0, 0.0), ROUTE_SEL_LANE, axis=1)
    route_ref[rs, :] = cmb + sel
    return sel


def _merge_call(x2d, ya, yb, yc, yd, mod, lw, layer, *, tm, mod_index):
    n = x2d.shape[0]
    row = lambda i: (i, 0)
    in_specs = [pl.BlockSpec((tm, D_MODEL), row)]
    in_specs += [pl.BlockSpec((tm, GROUP_W), row)] * 4
    in_specs += [_mod_spec(layer, mod_index)]
    in_specs += [_layer_spec(lw[k], layer) for k in ("w_out", "g_ffn", "w_router", "b_router")]
    return pl.pallas_call(
        functools.partial(_merge_kernel, tm=tm),
        grid=(n // tm,),
        in_specs=in_specs,
        out_specs=[pl.BlockSpec((tm, D_MODEL), row), pl.BlockSpec((tm, D_MODEL), row),
                   pl.BlockSpec((tm, LANES), row), pl.BlockSpec((1, HALO, LANES), lambda i: (i, 0, 0))],
        out_shape=[jax.ShapeDtypeStruct((n, D_MODEL), F32), jax.ShapeDtypeStruct((n, D_MODEL), BF16),
                   jax.ShapeDtypeStruct((n, LANES), F32),
                   jax.ShapeDtypeStruct((n // tm, HALO, LANES), F32)],
        compiler_params=pltpu.CompilerParams(dimension_semantics=("parallel",),
                                             vmem_limit_bytes=VMEM_LIMIT),
        name="merge",
    )(x2d, ya, yb, yc, yd, mod, lw["w_out"], lw["g_ffn"], lw["w_router"], lw["b_router"])


def _segment_positions(route, loc, tm):
    lane = lax.broadcasted_iota(jnp.int32, (tm, LANES), 1)
    is_sel = (lane >= ROUTE_SEL_LANE) & (lane < ROUTE_SEL_LANE + N_EXPERTS)
    selm = is_sel & (route > 0.5)
    selb = jnp.where(selm, 1.0, 0.0).astype(BF16)
    r_i = lax.broadcasted_iota(jnp.int32, (tm, tm), 0)
    c_i = lax.broadcasted_iota(jnp.int32, (tm, tm), 1)
    earlier = jnp.where(c_i < r_i, 1.0, 0.0).astype(BF16)
    dest = loc + _dot(earlier, selb)
    big = jnp.float32(1e9)
    d_a = jnp.where(selm, dest, big).min(axis=-1, keepdims=True)
    d_b = jnp.where(selm, dest, -big).max(axis=-1, keepdims=True)
    return selm, dest, d_a, d_b


def _to_row(col, tm):
    r_i = lax.broadcasted_iota(jnp.int32, (tm, tm), 0)
    c_i = lax.broadcasted_iota(jnp.int32, (tm, tm), 1)
    return jnp.where(r_i == c_i, col, 0.0).sum(axis=0, keepdims=True)


def _chunk_copy_out(xs_ref, xbuf_ref, sem_ref, slot, src_chunk, dst_chunk):
    return pltpu.make_async_copy(xs_ref.at[slot, pl.ds(src_chunk * MOE_CHUNK, MOE_CHUNK), :],
                                 xbuf_ref.at[pl.ds(dst_chunk * MOE_CHUNK, MOE_CHUNK), :],
                                 sem_ref.at[slot])


def _dispatch_kernel(nch_ref, dst_ref, plo_ref, phi_ref, nused_ref, h2_ref, route_ref, loc_ref, xbuf_ref,
                     xs_ref, sem_ref, *, tm, n_tiles, n_work):
    i = pl.program_id(0)
    slot = i % 2
    zero_chunk = MOE_RLOC // MOE_CHUNK

    def start_tile(t_idx, slot_):
        def body(j, carry):
            _chunk_copy_out(xs_ref, xbuf_ref, sem_ref, slot_, j, dst_ref[t_idx * MOE_NCH + j]).start()
            return carry
        lax.fori_loop(0, nch_ref[t_idx], body, 0)

    def wait_tile(t_idx, slot_):
        def body(j, carry):
            _chunk_copy_out(xs_ref, xbuf_ref, sem_ref, slot_, 0, 0).wait()
            return carry
        lax.fori_loop(0, nch_ref[t_idx], body, 0)

    @pl.when(i == 0)
    def _():
        xs_ref[2, :, :] = jnp.zeros((MOE_RLOC + MOE_CHUNK, MOE_XW), BF16)

        def pad_start(j, carry):
            _chunk_copy_out(xs_ref, xbuf_ref, sem_ref, 2, zero_chunk, j).start()
            return carry

        def pad_wait(j, carry):
            _chunk_copy_out(xs_ref, xbuf_ref, sem_ref, 2, zero_chunk, 0).wait()
            return carry
        for e in range(N_EXPERTS):
            lax.fori_loop(plo_ref[e], phi_ref[e], pad_start, 0)
        for e in range(N_EXPERTS):
            lax.fori_loop(plo_ref[e], phi_ref[e], pad_wait, 0)

        def tail_copy(w):
            return pltpu.make_async_copy(xs_ref.at[2, pl.ds(0, MOE_WORK), :],
                                         xbuf_ref.at[pl.ds(w * MOE_WORK, MOE_WORK), :], sem_ref.at[2])

        def tail_start(w, carry):
            tail_copy(w).start()
            return carry

        def tail_wait(w, carry):
            tail_copy(w).wait()
            return carry
        lax.fori_loop(nused_ref[0], n_work, tail_start, 0)
        lax.fori_loop(nused_ref[0], n_work, tail_wait, 0)

    @pl.when(i >= 2)
    def _():
        wait_tile(i - 2, slot)

    route = route_ref[...]
    selm, dest, d_a, d_b = _segment_positions(route, loc_ref[0, 0:1, :], tm)
    cmb_al = pltpu.roll(route, ROUTE_SEL_LANE, axis=1)
    w_a = jnp.where(selm & (dest == d_a), cmb_al, 0.0).sum(axis=-1, keepdims=True)
    w_b = jnp.where(selm & (dest == d_b), cmb_al, 0.0).sum(axis=-1, keepdims=True)
    da_r, db_r, wa_r, wb_r = (_to_row(v, tm) for v in (d_a, d_b, w_a, w_b))
    h2 = h2_ref[...]
    lane = lax.broadcasted_iota(jnp.int32, (MOE_RBLK, LANES), 1)
    for r0 in range(0, MOE_RLOC, MOE_RBLK):
        rio = (lax.broadcasted_iota(jnp.int32, (MOE_RBLK, tm), 0) + r0).astype(F32)
        hit_a = rio == da_r
        hit_b = rio == db_r
        perm = jnp.where(hit_a | hit_b, 1.0, 0.0).astype(BF16)
        xs_ref[slot, r0:r0 + MOE_RBLK, 0:D_MODEL] = _dot(perm, h2).astype(BF16)
        w_sorted = (jnp.where(hit_a, wa_r, 0.0) + jnp.where(hit_b, wb_r, 0.0)).sum(axis=-1, keepdims=True)
        w_hi = w_sorted.astype(BF16).astype(F32)
        r1 = w_sorted - w_hi
        w_mid = r1.astype(BF16).astype(F32)
        w_lo = r1 - w_mid
        slab = jnp.where(lane == 0, w_hi, jnp.where(lane == 1, w_mid, jnp.where(lane == 2, w_lo, 0.0)))
        xs_ref[slot, r0:r0 + MOE_RBLK, D_MODEL:MOE_XW] = slab.astype(BF16)
    start_tile(i, slot)

    @pl.when(i == n_tiles - 1)
    def _():
        if n_tiles > 1:
            wait_tile(i - 1, 1 - slot)
        wait_tile(i, slot)


def _dispatch_call(h2, route, tabs, *, tm, n_rows):
    n = h2.shape[0]
    n_tiles = n // tm
    grid_spec = pltpu.PrefetchScalarGridSpec(
        num_scalar_prefetch=5,
        grid=(n_tiles,),
        in_specs=[pl.BlockSpec((tm, D_MODEL), lambda i, *_: (i, 0)),
                  pl.BlockSpec((tm, LANES), lambda i, *_: (i, 0)),
                  pl.BlockSpec((1, HALO, LANES), lambda i, *_: (i, 0, 0))],
        out_specs=pl.BlockSpec(memory_space=pl.ANY),
        scratch_shapes=[pltpu.VMEM((3, MOE_RLOC + MOE_CHUNK, MOE_XW), BF16),
                        pltpu.SemaphoreType.DMA((3,))],
    )
    return pl.pallas_call(
        functools.partial(_dispatch_kernel, tm=tm, n_tiles=n_tiles, n_work=n_rows // MOE_WORK),
        grid_spec=grid_spec,
        out_shape=jax.ShapeDtypeStruct((n_rows, MOE_XW), BF16),
        compiler_params=pltpu.CompilerParams(dimension_semantics=("arbitrary",),
                                             vmem_limit_bytes=VMEM_LIMIT),
        name="moe_dispatch",
    )(tabs["nch"], tabs["dst"], tabs["pad_lo"], tabs["pad_hi"], tabs["n_used"], h2, route, tabs["loc"])


def _expert_kernel(te_ref, nused_ref, x_ref, wup_ref, wdn_ref, y_ref, wup_bf, wdn_bf):
    w = pl.program_id(0)
    used = w < nused_ref[0]

    @pl.when((w == 0) | (te_ref[w] != te_ref[jnp.maximum(w - 1, 0)]))
    def _():
        wup_bf[...] = wup_ref[...].astype(BF16)
        wdn_bf[...] = wdn_ref[...].astype(BF16)

    @pl.when(used)
    def _():
        wt = x_ref[:, D_MODEL:MOE_XW].astype(F32).sum(axis=-1, keepdims=True)
        gu = _dot(x_ref[:, 0:D_MODEL], wup_bf[...])
        g = gu[:, 0:D_EXPERT]
        a = (g * jax.nn.sigmoid(g)) * gu[:, D_EXPERT:2 * D_EXPERT] * wt
        y_ref[...] = _dot(a.astype(BF16), wdn_bf[...]).astype(BF16)

    @pl.when(jnp.logical_not(used))
    def _():
        y_ref[...] = jnp.zeros(y_ref.shape, BF16)


def _expert_call(xbuf, tabs, w_up, w_down, layer):
    n_rows = xbuf.shape[0]
    n_work = n_rows // MOE_WORK
    base = layer * N_EXPERTS
    grid_spec = pltpu.PrefetchScalarGridSpec(
        num_scalar_prefetch=2,
        grid=(n_work,),
        in_specs=[pl.BlockSpec((MOE_WORK, MOE_XW), lambda w, te, nu: (jnp.minimum(w, nu[0] - 1), 0)),
                  pl.BlockSpec((None, D_MODEL, 2 * D_EXPERT), lambda w, te, nu: (base + te[w], 0, 0)),
                  pl.BlockSpec((None, D_EXPERT, D_MODEL), lambda w, te, nu: (base + te[w], 0, 0))],
        out_specs=pl.BlockSpec((MOE_WORK, D_MODEL), lambda w, te, nu: (w, 0)),
        scratch_shapes=[pltpu.VMEM((D_MODEL, 2 * D_EXPERT), BF16), pltpu.VMEM((D_EXPERT, D_MODEL), BF16)],
    )
    return pl.pallas_call(
        _expert_kernel,
        grid_spec=grid_spec,
        out_shape=jax.ShapeDtypeStruct((n_rows, D_MODEL), BF16),
        compiler_params=pltpu.CompilerParams(dimension_semantics=("arbitrary",),
                                             vmem_limit_bytes=VMEM_LIMIT),
        name="moe_experts",
    )(tabs["tile_expert"], tabs["n_used"], xbuf, w_up, w_down)


def _chunk_copy_in(ybuf_ref, ys_ref, sem_ref, slot, src_chunk, dst_chunk):
    return pltpu.make_async_copy(ybuf_ref.at[pl.ds(src_chunk * MOE_CHUNK, MOE_CHUNK), :],
                                 ys_ref.at[slot, pl.ds(dst_chunk * MOE_CHUNK, MOE_CHUNK), :],
                                 sem_ref.at[slot])


def _combine_kernel(nch_ref, dst_ref, route_ref, loc_ref, x_ref, mod_ref, ybuf_ref, o_ref,
                    ys_ref, sem_ref, *, tm, n_tiles):
    i = pl.program_id(0)
    slot = i % 2

    def start_tile(t_idx, slot_):
        def body(j, carry):
            _chunk_copy_in(ybuf_ref, ys_ref, sem_ref, slot_, dst_ref[t_idx * MOE_NCH + j], j).start()
            return carry
        lax.fori_loop(0, nch_ref[t_idx], body, 0)

    @pl.when(i == 0)
    def _():
        ys_ref[...] = jnp.zeros(ys_ref.shape, BF16)
        start_tile(0, 0)

    @pl.when(i + 1 < n_tiles)
    def _():
        start_tile(i + 1, 1 - slot)

    def wait_body(j, carry):
        _chunk_copy_in(ybuf_ref, ys_ref, sem_ref, slot, 0, 0).wait()
        return carry
    lax.fori_loop(0, nch_ref[i], wait_body, 0)

    _, _, d_a, d_b = _segment_positions(route_ref[...], loc_ref[0, 0:1, :], tm)
    y = None
    for r0 in range(0, MOE_RLOC, MOE_RBLK):
        lio = (lax.broadcasted_iota(jnp.int32, (tm, MOE_RBLK), 1) + r0).astype(F32)
        perm_t = jnp.where((lio == d_a) | (lio == d_b), 1.0, 0.0).astype(BF16)
        part = _dot(perm_t, ys_ref[slot, r0:r0 + MOE_RBLK, :])
        y = part if y is None else y + part
    o_ref[...] = x_ref[...] + mod_ref[0, 5:6, :] * y


def _combine_call(ybuf, route, x2d, mod, layer, tabs, *, tm, mod_index):
    n = x2d.shape[0]
    n_tiles = n // tm
    grid_spec = pltpu.PrefetchScalarGridSpec(
        num_scalar_prefetch=2,
        grid=(n_tiles,),
        in_specs=[pl.BlockSpec((tm, LANES), lambda i, *_: (i, 0)),
                  pl.BlockSpec((1, HALO, LANES), lambda i, *_: (i, 0, 0)),
                  pl.BlockSpec((tm, D_MODEL), lambda i, *_: (i, 0)),
                  _mod_spec(layer, mod_index),
                  pl.BlockSpec(memory_space=pl.ANY)],
        out_specs=pl.BlockSpec((tm, D_MODEL), lambda i, *_: (i, 0)),
        scratch_shapes=[pltpu.VMEM((2, MOE_RLOC, D_MODEL), BF16),
                        pltpu.SemaphoreType.DMA((2,))],
    )
    return pl.pallas_call(
        functools.partial(_combine_kernel, tm=tm, n_tiles=n_tiles),
        grid_spec=grid_spec,
        out_shape=jax.ShapeDtypeStruct((n, D_MODEL), F32),
        compiler_params=pltpu.CompilerParams(dimension_semantics=("arbitrary",),
                                             vmem_limit_bytes=VMEM_LIMIT),
        name="moe_combine",
    )(tabs["nch"], tabs["dst"], route, tabs["loc"], x2d, mod, ybuf)


def _moe_rows(n_tok):
    n_tiles = n_tok // MOE_T
    chunks = 2 * n_tok // MOE_CHUNK + (MOE_CHUNK - 1) * n_tiles + N_EXPERTS * (MOE_CPW - 1)
    return -(-chunks // MOE_CPW) * MOE_WORK


def _route_tables(counts, n_tok):
    n_work = _moe_rows(n_tok) // MOE_WORK
    cnt = counts[:, 0, ROUTE_SEL_LANE:ROUTE_SEL_LANE + N_EXPERTS].astype(jnp.int32)
    n_tiles = cnt.shape[0]
    nchunk = (cnt + MOE_CHUNK - 1) // MOE_CHUNK
    ei = jnp.arange(N_EXPERTS, dtype=jnp.int32)
    ti = jnp.arange(n_tiles, dtype=jnp.int32)
    upto_e = (ei[:, None] <= ei[None, :]).astype(jnp.int32)
    upto_t = (ti[None, :] <= ti[:, None]).astype(jnp.int32)
    ends = (nchunk[:, :, None] * upto_e[None, :, :]).sum(axis=1)
    loc = ends - nchunk
    nch = nchunk.sum(axis=1)
    per_e = nchunk.sum(axis=0)
    padded_e = (per_e + MOE_CPW - 1) // MOE_CPW * MOE_CPW
    e_end = (padded_e[:, None] * upto_e).sum(axis=0)
    e_base = e_end - padded_e
    glob = e_base[None, :] + (upto_t[:, :, None] * nchunk[None, :, :]).sum(axis=1) - nchunk
    j = jnp.arange(MOE_NCH, dtype=jnp.int32)
    e_of_j = (ends[:, None, :] <= j[None, :, None]).sum(axis=-1)
    seg = e_of_j[:, :, None] == ei[None, None, :]
    dst = jnp.where(seg, (glob - loc)[:, None, :], 0).sum(axis=-1) + j[None, :]
    dst = jnp.where(j[None, :] < nch[:, None], dst, 0)
    n_used = padded_e.sum() // MOE_CPW
    w = jnp.minimum(jnp.arange(n_work, dtype=jnp.int32), n_used - 1)
    tile_expert = jnp.minimum(((e_end // MOE_CPW)[None, :] <= w[:, None]).sum(axis=-1), N_EXPERTS - 1)
    loc_rows = jnp.pad((loc * MOE_CHUNK).astype(F32),
                       ((0, 0), (ROUTE_SEL_LANE, LANES - ROUTE_SEL_LANE - N_EXPERTS)))
    loc_rows = jnp.broadcast_to(loc_rows[:, None, :], (n_tiles, HALO, LANES))
    return {"nch": nch.astype(jnp.int32), "dst": dst.reshape(-1).astype(jnp.int32),
            "pad_lo": (e_base + per_e).astype(jnp.int32), "pad_hi": e_end.astype(jnp.int32),
            "tile_expert": tile_expert.astype(jnp.int32), "n_used": n_used.astype(jnp.int32).reshape(1),
            "loc": loc_rows}


def _moe_call(h2, route, counts, x2d, mod, w_up, w_down, layer, *, mod_index):
    n = x2d.shape[0]
    tabs = _route_tables(counts, n)
    xbuf = _dispatch_call(h2, route, tabs, tm=MOE_T, n_rows=_moe_rows(n))
    ybuf = _expert_call(xbuf, tabs, w_up, w_down, layer)
    return _combine_call(ybuf, route, x2d, mod, layer, tabs, tm=MOE_T, mod_index=mod_index)


def _ones_blocks(sizes):
    n = sum(sizes)
    m = np.zeros((n, n), np.float32)
    o = 0
    for s in sizes:
        m[o:o + s, o:o + s] = 1.0
        o += s
    return jnp.asarray(m, BF16)


def _inv_counts_q():
    return jnp.asarray(np.tile(np.concatenate([np.full(64, 1 / 64.0), np.full(64, 1 / 32.0)]), 2)[None, :], F32)


def _rope_tables(rows):
    r = np.repeat(np.arange(rows, dtype=np.float32), GRID_W)
    c = np.tile(np.arange(GRID_W, dtype=np.float32), rows)

    def tab(rot_dim):
        nf = rot_dim // 4
        inv = (np.float32(ROPE_THETA) ** (-np.arange(nf, dtype=np.float32) / np.float32(nf))).astype(np.float32)
        ang = np.concatenate([r[:, None] * inv, c[:, None] * inv], axis=-1).astype(np.float32)
        cos = np.repeat(np.cos(ang), 2, axis=-1)
        sin = np.repeat(np.sin(ang), 2, axis=-1) * np.tile(np.asarray([-1.0, 1.0], np.float32), rot_dim // 2)
        return cos.astype(np.float32), sin.astype(np.float32)

    s = rows * GRID_W
    cc, sc = tab(C_HEAD_DIM)
    cd, sd = tab(D_ROPE)
    one = lambda w: np.ones((s, w), np.float32)
    zero = lambda w: np.zeros((s, w), np.float32)
    tabs = (np.tile(cc, (1, 2)), np.tile(sc, (1, 2)),
            np.concatenate([one(64), cd, one(32)], axis=1), np.concatenate([zero(64), sd, zero(32)], axis=1),
            np.concatenate([cd, one(96)], axis=1), np.concatenate([sd, zero(96)], axis=1))
    return tuple(jnp.asarray(t) for t in tabs)


def _stacked_weights(p):
    (g_attn_norm, w_in, sgu_v_gain, sgu_w, sgu_b, conv_w, gqa_q_gain, gqa_k_gain, mla_qa_gain, mla_w_qb,
     mla_kva_gain, mla_w_kvb, mla_q_gain, mla_k_gain, g_group_out, w_out, g_ffn_norm,
     w_rg, b_rg, w_re, b_re) = p
    depth = w_in.shape[0]

    def heads_0213(a, axis, off):
        sl = lambda lo: lax.slice_in_dim(a, off + lo, off + lo + 64, axis=axis)
        return [sl(0), sl(128), sl(64), sl(192)]

    zeros = lambda *shape: jnp.zeros((depth,) + shape, F32)
    row = lambda a: a[:, None, :]
    w_in_p = jnp.concatenate([w_in[:, :, :1280]] + heads_0213(w_in, 2, 1280) +
                             [w_in[:, :, 1536:2208], zeros(D_MODEL, IN_W - 2208)], axis=2).astype(BF16)
    wqb = jnp.concatenate([jnp.concatenate([mla_w_qb[:, :, h * 96:h * 96 + 96], zeros(D_Q_LORA, 32)], axis=2)
                           for h in range(D_HEADS)], axis=2)
    wk = jnp.concatenate([jnp.concatenate([mla_w_kvb[:, :, h * 128:h * 128 + 64], zeros(D_KV_LORA, 64)], axis=2)
                          for h in range(D_HEADS)], axis=2)
    wv = jnp.concatenate([mla_w_kvb[:, :, h * 128 + 64:h * 128 + 128] for h in range(D_HEADS)], axis=2)
    qslab = jnp.concatenate([mla_q_gain, zeros(32)], axis=1)
    kslab = jnp.concatenate([mla_k_gain[:, :64], zeros(64)], axis=1)
    g_go = jnp.concatenate([g_group_out[:, :512]] + heads_0213(g_group_out, 1, 512) + [g_group_out[:, 768:]],
                           axis=1)
    w_out_p = jnp.concatenate([w_out[:, :512]] + heads_0213(w_out, 1, 512) + [w_out[:, 768:]],
                              axis=1).astype(BF16)
    w_r = jnp.concatenate([w_re, w_rg], axis=2)
    w_r_hi = w_r.astype(BF16)
    w_r_lo = (w_r - w_r_hi.astype(F32)).astype(BF16)
    n_r = N_EXPERTS + N_GROUPS
    zb = lambda w: jnp.zeros((depth, D_MODEL, w), BF16)
    w_router = jnp.concatenate([w_r_hi, zb(ROUTE_SEL_LANE - n_r), w_r_lo, zb(LANES - ROUTE_SEL_LANE - n_r)], axis=2)
    b_router = jnp.concatenate([b_re, b_rg, zeros(LANES - n_r)], axis=1)
    sgub_full = jnp.repeat(jnp.swapaxes(sgu_b, 1, 2), 64, axis=2)
    return {
        "g_attn": row(g_attn_norm), "w_in": w_in_p,
        "sgu_w": sgu_w.reshape(depth, A_HEADS * CHUNK, CHUNK).astype(BF16), "sgu_b": sgub_full,
        "v_gain": row(sgu_v_gain), "conv_w": conv_w, "g_go": row(g_go),
        "cq_gain": row(jnp.tile(gqa_q_gain, (1, 4))), "ck_gain": row(jnp.tile(gqa_k_gain, (1, 2))),
        "qa_gain": row(mla_qa_gain), "w_qb": wqb.astype(BF16), "kva_gain": row(mla_kva_gain),
        "w_kvb": jnp.concatenate([wk, wv], axis=2).astype(BF16),
        "dq_gain": row(jnp.tile(qslab, (1, 4))), "dk_gain": row(jnp.tile(kslab, (1, 4))),
        "dkr_gain": row(jnp.concatenate([mla_k_gain[:, 64:], zeros(96)], axis=1)),
        "w_out": w_out_p, "g_ffn": row(g_ffn_norm), "w_router": w_router, "b_router": row(b_router),
    }


def kernel(x, c, ctx, c_ctx, w_mod, b_mod, g_attn_norm, w_in, sgu_v_gain, sgu_w, sgu_b, conv_w, gqa_q_gain,
           gqa_k_gain, mla_qa_gain, mla_w_qb, mla_kva_gain, mla_w_kvb, mla_q_gain, mla_k_gain, g_group_out,
           w_out, g_ffn_norm, w_router_group, b_router_group, w_router_expert, b_router_expert,
           w_expert_up, w_expert_down):
    b, s, d = x.shape
    lc = ctx.shape[1]
    depth = w_mod.shape[0]
    lw = _stacked_weights((g_attn_norm, w_in, sgu_v_gain, sgu_w, sgu_b, conv_w, gqa_q_gain, gqa_k_gain,
                           mla_qa_gain, mla_w_qb, mla_kva_gain, mla_w_kvb, mla_q_gain, mla_k_gain, g_group_out,
                           w_out, g_ffn_norm, w_router_group, b_router_group, w_router_expert, b_router_expert))
    w_up = w_expert_up.reshape(depth * N_EXPERTS, d, 2 * D_EXPERT)
    w_down = w_expert_down.reshape(depth * N_EXPERTS, D_EXPERT, d)

    c_all = jnp.concatenate([c, c_ctx[None, :], jnp.zeros((16 - b - 1, d), F32)], axis=0)
    mod = _modulation(c_all, w_mod, b_mod).reshape(depth, 16, 6, d)
    rope = _rope_tables(s // GRID_W)

    tm = MOE_T
    tq = 512
    tpb = s // tm
    lat_idx = lambda i: i // tpb
    ctx_idx = lambda i: b
    r3 = lambda a, n: a.reshape(b, n, a.shape[-1])

    xl = x.reshape(b * s, d)
    xc = ctx.reshape(b * lc, d)
    for l in range(depth):
        update_ctx = l < depth - 1
        ya_c, yb_c, qc_c, kc_c, vc_c, qd_c, kd_c, vd_c = _in_call(
            xc, mod, lw, l, None, tm=tm, seq_len=lc, mod_index=ctx_idx)
        ya, yb, qc, kc, vc, qd, kd, vd = _in_call(xl, mod, lw, l, rope, tm=IN_T, seq_len=s,
                                                  mod_index=lambda i: i // (s // IN_T))
        yc = _attn_call(r3(qc, s), [r3(kc_c, lc), r3(kc, s)], [r3(vc_c, lc), r3(vc, s)], lw["g_go"], l,
                        mode="C", tq=tq)
        yd = _attn_call(r3(qd, s), [r3(kd_c, lc), r3(kd, s)], [r3(vd_c, lc), r3(vd, s)], lw["g_go"], l,
                        mode="D", tq=tq)
        xl, h2, route, counts = _merge_call(xl, ya, yb, yc.reshape(b * s, GROUP_W), yd.reshape(b * s, GROUP_W),
                                            mod, lw, l, tm=tm, mod_index=lat_idx)
        xl = _moe_call(h2, route, counts, xl, mod, w_up, w_down, l, mod_index=lat_idx)
        if update_ctx:
            yc_c = _attn_call(r3(qc_c, lc), [r3(kc_c, lc)], [r3(vc_c, lc)], lw["g_go"], l, mode="C", tq=lc)
            yd_c = _attn_call(r3(qd_c, lc), [r3(kd_c, lc)], [r3(vd_c, lc)], lw["g_go"], l, mode="D", tq=lc)
            xc, h2c, route_c, counts_c = _merge_call(xc, ya_c, yb_c, yc_c.reshape(b * lc, GROUP_W),
                                                     yd_c.reshape(b * lc, GROUP_W), mod, lw, l, tm=tm,
                                                     mod_index=ctx_idx)
            xc = _moe_call(h2c, route_c, counts_c, xc, mod, w_up, w_down, l, mod_index=ctx_idx)
    return xl.reshape(b, s, d)
```

```python
import functools

import numpy as np
import jax
import jax.numpy as jnp
from jax import lax
from jax.experimental import pallas as pl
from jax.experimental.pallas import tpu as pltpu

F32 = jnp.float32
BF16 = jnp.bfloat16

D_MODEL = 1024
GRID_W = 64
EPS = 1e-6
ROPE_THETA = 10000.0
GROUP_W = 256
CHUNK = 128
A_HEADS = 4
C_HEAD_DIM = 64
C_SCALE = C_HEAD_DIM ** -0.5
D_HEADS = 4
D_NOPE = 64
D_ROPE = 32
D_V = 64
D_Q_LORA = 256
D_KV_LORA = 128
D_SCALE = (D_NOPE + D_ROPE) ** -0.5
LOG2E = 1.4426950408889634
ATTN_ROWS = 16
MERGE_ROWS = 128
N_GROUPS = 4
EXPERTS_PER_GROUP = 4
N_EXPERTS = 16
D_EXPERT = 256
LANES = 128
HALO = 8

IN_W = 2304
IN_T = 512
IN_SECTIONS = ((0, 256), (256, 512), (512, 768), (768, 1024), (1024, 1280), (1280, 1536), (1536, 1664),
               (1664, 1792), (1792, 2048), (2048, 2176), (2176, 2304))

ROUTE_SEL_LANE = 32
MOE_T = 512
MOE_CHUNK = 16
MOE_RLOC = 1280
MOE_NCH = MOE_RLOC // MOE_CHUNK
MOE_RBLK = 256
MOE_XW = D_MODEL + LANES
MOE_WORK = 1024
MOE_CPW = MOE_WORK // MOE_CHUNK
VMEM_LIMIT = 56 * 1024 * 1024


def _nt_dot(a, b):
    return lax.dot_general(a, b, (((1,), (1,)), ((), ())), preferred_element_type=F32)


def _dot(a, b):
    return jnp.dot(a, b, preferred_element_type=F32)


def _rms_full(x, gain):
    return x * lax.rsqrt(jnp.mean(x * x, axis=-1, keepdims=True) + EPS) * gain


def _group_ms(x, ones_blocks, inv_count):
    return _dot((x * x).astype(BF16), ones_blocks) * inv_count


def _pair_swap(x, even):
    nxt = pltpu.roll(x, x.shape[1] - 1, axis=1)
    prv = pltpu.roll(x, 1, axis=1)
    return jnp.where(even, nxt, prv)


def _rope(x, cos, sin, even):
    return x * cos + _pair_swap(x, even) * sin


def _mod_kernel(c_ref, w_ref, b_ref, o_ref):
    cv = c_ref[...]
    sc = cv * jax.nn.sigmoid(cv)
    o_ref[0] = jnp.dot(sc, w_ref[0], preferred_element_type=F32,
                       precision=lax.Precision.HIGHEST) + b_ref[0]


def _modulation(c_all, w_mod, b_mod):
    depth = w_mod.shape[0]
    nblk = 4
    wblk = 6 * D_MODEL // nblk
    return pl.pallas_call(
        _mod_kernel,
        grid=(depth, nblk),
        in_specs=[pl.BlockSpec((16, D_MODEL), lambda l, j: (0, 0)),
                  pl.BlockSpec((1, D_MODEL, wblk), lambda l, j: (l, 0, j)),
                  pl.BlockSpec((1, 1, wblk), lambda l, j: (l, 0, j))],
        out_specs=pl.BlockSpec((1, 16, wblk), lambda l, j: (l, 0, j)),
        out_shape=jax.ShapeDtypeStruct((depth, 16, 6 * D_MODEL), F32),
        compiler_params=pltpu.CompilerParams(dimension_semantics=("arbitrary", "arbitrary"),
                                             vmem_limit_bytes=VMEM_LIMIT),
        name="mod",
    )(c_all, w_mod, b_mod.reshape(depth, 1, 6 * D_MODEL))


def _in_kernel(*refs, tm, seq_len, use_rope):
    n_io = len(refs) - 4
    pj0_ref, pj1_ref, zh0_ref, zh1_ref = refs[n_io:]
    i = pl.program_id(0)

    @pl.when(i == 0)
    def _():
        pj1_ref[...] = jnp.zeros(pj1_ref.shape, F32)
        zh1_ref[...] = jnp.zeros(zh1_ref.shape, F32)

    @pl.when(i % 2 == 0)
    def _():
        _in_stages(refs[:n_io], pj0_ref, zh0_ref, pj1_ref, zh1_ref, tm=tm, seq_len=seq_len, use_rope=use_rope)

    @pl.when(i % 2 == 1)
    def _():
        _in_stages(refs[:n_io], pj1_ref, zh1_ref, pj0_ref, zh0_ref, tm=tm, seq_len=seq_len, use_rope=use_rope)


def _in_stages(refs, pj_w, zh_w, pj_r, zh_r, *, tm, seq_len, use_rope):
    (x_ref, xp_ref, xn_ref, mod_ref, gattn_ref, win_ref, sguw_ref, sgub_ref, vgain_ref, convw_ref,
     ggo_ref, cqg_ref, ckg_ref, qag_ref, wqb_ref, kvag_ref, wkvb_ref, dqg_ref, dkg_ref, dkrg_ref,
     b64_ref, bq_ref, invq_ref) = refs[:23]
    pos = 23
    if use_rope:
        cosc_ref, sinc_ref, cosq_ref, sinq_ref, cosk_ref, sink_ref = refs[pos:pos + 6]
        pos += 6
    ya_ref, yb_ref, qc_ref, kc_ref, vc_ref, qd_ref, kd_ref, vd_ref = refs[pos:pos + 8]

    shift = mod_ref[0, 0:1, :]
    scale = mod_ref[0, 1:2, :]
    gattn = gattn_ref[...]

    def prenorm(xv):
        return (_rms_full(xv, gattn) * (1.0 + scale) + shift).astype(BF16)

    hb = prenorm(x_ref[...])

    def project(first, last):
        for lo, hi in IN_SECTIONS[first:last]:
            pj_w[:, lo:hi] = _dot(hb, win_ref[:, lo:hi])

    project(0, 3)
    hh = prenorm(jnp.concatenate([xp_ref[...], xn_ref[...]], axis=0))
    zh_w[...] = _dot(hh, win_ref[:, 768:1024]) * _dot(hh, win_ref[:, 1024:1280])

    i = jnp.maximum(pl.program_id(0) - 1, 0)
    proj = {ab: pj_r[:, ab[0]:ab[1]] for ab in IN_SECTIONS}
    zh = zh_r[...]
    b64 = b64_ref[...]
    inv64 = 1.0 / 64.0
    lane128 = lax.broadcasted_iota(jnp.int32, (tm, LANES), 1)
    even = (lane128 & 1) == 0

    u = jax.nn.gelu(proj[(0, 256)])
    v = jax.nn.gelu(proj[(256, 512)])
    v = (v * lax.rsqrt(_group_ms(v, b64, inv64) + EPS) * vgain_ref[...]).astype(BF16)
    lane_head = lax.broadcasted_iota(jnp.int32, (CHUNK, GROUP_W), 1) // 64
    sguw = sguw_ref[...]
    mixed_chunks = []
    for ci in range(tm // CHUNK):
        r = _dot(sguw, v[ci * CHUNK:(ci + 1) * CHUNK, :])
        m = r[0:CHUNK]
        for h in range(1, A_HEADS):
            m = jnp.where(lane_head == h, r[h * CHUNK:(h + 1) * CHUNK], m)
        mixed_chunks.append(m + sgub_ref[...])
    ya = u * jnp.concatenate(mixed_chunks, axis=0)
    ya_ref[...] = _rms_full(ya, ggo_ref[:, 0:256]).astype(BF16)

    project(3, 6)
    bg = proj[(512, 768)]
    z = proj[(768, 1024)] * proj[(1024, 1280)]
    row = lax.broadcasted_iota(jnp.int32, (tm, GROUP_W), 0)
    posn = (i * tm + row) % seq_len
    z_prev = jnp.where(row == 0, zh[HALO - 1:HALO, :], pltpu.roll(z, 1, axis=0))
    z_prev = jnp.where(posn == 0, 0.0, z_prev)
    z_next = jnp.where(row == tm - 1, zh[HALO:HALO + 1, :], pltpu.roll(z, tm - 1, axis=0))
    z_next = jnp.where(posn == seq_len - 1, 0.0, z_next)
    yb = bg * (convw_ref[0:1, :] * z_prev + convw_ref[1:2, :] * z + convw_ref[2:3, :] * z_next)
    yb_ref[...] = _rms_full(yb, ggo_ref[:, 256:512]).astype(BF16)

    project(6, 9)
    q = proj[(1280, 1536)]
    q = q * lax.rsqrt(_group_ms(q, b64, inv64) + EPS) * cqg_ref[...]
    k = proj[(1536, 1664)]
    k = k * lax.rsqrt(_group_ms(k, b64[0:128, 0:128], inv64) + EPS) * ckg_ref[...]
    if use_rope:
        cosc, sinc = cosc_ref[...], sinc_ref[...]
        q = jnp.concatenate([_rope(q[:, 0:128], cosc, sinc, even),
                             _rope(q[:, 128:256], cosc, sinc, even)], axis=1)
        k = _rope(k, cosc, sinc, even)
    qc_ref[...] = (q * (C_SCALE * LOG2E)).astype(BF16)
    kc_ref[...] = k.astype(BF16)
    vc_ref[...] = proj[(1664, 1792)].astype(BF16)

    project(9, len(IN_SECTIONS))
    cq = _rms_full(proj[(1792, 2048)], qag_ref[...]).astype(BF16)
    qd = _dot(cq, wqb_ref[...])
    bq = bq_ref[...]
    invq = invq_ref[...]
    halves = []
    for hf in range(2):
        qh = qd[:, hf * 256:(hf + 1) * 256]
        qh = qh * lax.rsqrt(_group_ms(qh, bq, invq) + EPS) * dqg_ref[:, hf * 256:(hf + 1) * 256]
        if use_rope:
            cosq, sinq = cosq_ref[...], sinq_ref[...]
            qh = jnp.concatenate([_rope(qh[:, 0:128], cosq, sinq, even),
                                  _rope(qh[:, 128:256], cosq, sinq, even)], axis=1)
        halves.append(qh)
    qd_ref[...] = (jnp.concatenate(halves, axis=1) * (D_SCALE * LOG2E)).astype(BF16)

    ckv = _rms_full(proj[(2048, 2176)], kvag_ref[...]).astype(BF16)
    kv = _dot(ckv, wkvb_ref[...])
    vd_ref[...] = kv[:, 512:768].astype(BF16)
    kr = proj[(2176, 2304)]
    kr = kr * lax.rsqrt(jnp.sum(kr * kr, axis=-1, keepdims=True) * (1.0 / D_ROPE) + EPS) * dkrg_ref[...]
    if use_rope:
        kr = _rope(kr, cosk_ref[...], sink_ref[...], even)
    kr = pltpu.roll(kr, D_NOPE, axis=1)
    for hf in range(2):
        kh = kv[:, hf * 256:(hf + 1) * 256]
        kh = kh * lax.rsqrt(_group_ms(kh, b64, inv64) + EPS) * dkg_ref[:, hf * 256:(hf + 1) * 256]
        kd_ref[:, hf * 256:hf * 256 + 128] = (kh[:, 0:128] + kr).astype(BF16)
        kd_ref[:, hf * 256 + 128:(hf + 1) * 256] = (kh[:, 128:256] + kr).astype(BF16)


def _layer_spec(a, layer):
    zeros = (0,) * (a.ndim - 1)
    return pl.BlockSpec((None,) + a.shape[1:], lambda *_: (layer,) + zeros)


def _mod_spec(layer, mod_index):
    return pl.BlockSpec((None, 1, 6, D_MODEL), lambda i, *_: (layer, mod_index(i), 0, 0))


def _in_call(x2d, mod, lw, layer, rope, *, tm, seq_len, mod_index):
    n = x2d.shape[0]
    nt = n // tm
    nb8 = n // HALO
    tb = tm // HALO
    use_rope = rope is not None
    cur = lambda i: jnp.minimum(i, nt - 1)
    prev = lambda i: jnp.maximum(i - 1, 0)
    row = lambda i: (prev(i), 0)
    in_specs = [
        pl.BlockSpec((tm, D_MODEL), lambda i: (cur(i), 0)),
        pl.BlockSpec((HALO, D_MODEL), lambda i: (jnp.maximum(cur(i) * tb - 1, 0), 0)),
        pl.BlockSpec((HALO, D_MODEL), lambda i: (jnp.minimum((cur(i) + 1) * tb, nb8 - 1), 0)),
        _mod_spec(layer, lambda i: mod_index(cur(i))),
    ]
    args = [x2d, x2d, x2d, mod]
    for name in ("g_attn", "w_in", "sgu_w", "sgu_b", "v_gain", "conv_w", "g_go", "cq_gain", "ck_gain",
                 "qa_gain", "w_qb", "kva_gain", "w_kvb", "dq_gain", "dk_gain", "dkr_gain"):
        in_specs.append(_layer_spec(lw[name], layer))
        args.append(lw[name])
    for a in (_ones_blocks([64] * 4), _ones_blocks([64, 32, 32] * 2), _inv_counts_q()):
        in_specs.append(pl.BlockSpec(a.shape, lambda i: (0, 0)))
        args.append(a)
    if use_rope:
        ntab = seq_len // tm
        for a in rope:
            in_specs.append(pl.BlockSpec((tm, LANES), lambda i: (prev(i) % ntab, 0)))
            args.append(a)
    widths = (256, 256, 256, 128, 128, 512, 512, 256)
    out_specs = [pl.BlockSpec((tm, w), row) for w in widths]
    out_shape = [jax.ShapeDtypeStruct((n, w), BF16) for w in widths]
    return pl.pallas_call(
        functools.partial(_in_kernel, tm=tm, seq_len=seq_len, use_rope=use_rope),
        grid=(nt + 1,),
        in_specs=in_specs,
        out_specs=out_specs,
        out_shape=out_shape,
        scratch_shapes=[pltpu.VMEM((tm, IN_W), F32), pltpu.VMEM((tm, IN_W), F32),
                        pltpu.VMEM((2 * HALO, GROUP_W), F32), pltpu.VMEM((2 * HALO, GROUP_W), F32)],
        compiler_params=pltpu.CompilerParams(dimension_semantics=("arbitrary",),
                                             vmem_limit_bytes=VMEM_LIMIT),
        name="in_proj",
    )(*args)


def _attn_kernel(*refs, mode, nseg, tq):
    q_ref = refs[0]
    k_refs = refs[1:1 + nseg]
    v_refs = refs[1 + nseg:1 + 2 * nseg]
    gain_ref = refs[1 + 2 * nseg]
    o_ref = refs[2 + 2 * nseg]
    p_ref = refs[3 + 2 * nseg]
    nks = [r.shape[1] for r in k_refs]
    offs = [sum(nks[:j]) for j in range(nseg)]
    n_chunks = tq // ATTN_ROWS

    vs = [r[0] for r in v_refs]
    if mode == "C":
        ks_c = [r[0] for r in k_refs]
        first_half = lax.broadcasted_iota(jnp.int32, (tq, LANES), 1) < 64
        heads = []
        for sl in range(2):
            qs = q_ref[0, :, sl * 128:(sl + 1) * 128]
            heads.append((jnp.where(first_half, qs, jnp.zeros_like(qs)), ks_c))
            heads.append((jnp.where(first_half, jnp.zeros_like(qs), qs), ks_c))
    else:
        heads = [(q_ref[0, :, h * 128:(h + 1) * 128], [r[0, :, h * 128:(h + 1) * 128] for r in k_refs])
                 for h in range(D_HEADS)]

    def score(h):
        qm, ks = heads[h]
        return [_nt_dot(qm, kk) for kk in ks]

    def softmax(h, ss):
        sums = []
        for c in range(n_chunks):
            rows = slice(c * ATTN_ROWS, (c + 1) * ATTN_ROWS)
            m = ss[0][rows].max(axis=-1, keepdims=True)
            for s in ss[1:]:
                m = jnp.maximum(m, s[rows].max(axis=-1, keepdims=True))
            tot = None
            for s, off, nk in zip(ss, offs, nks):
                p = jnp.exp2(s[rows] - m).astype(BF16)
                ps = p.astype(F32).sum(axis=-1, keepdims=True)
                tot = ps if tot is None else tot + ps
                p_ref[h, rows, off:off + nk] = p
            sums.append(tot)
        return jnp.concatenate(sums, axis=0)

    def values(h, row_sum):
        o = None
        for off, nk, vv in zip(offs, nks, vs):
            part = _dot(p_ref[h, :, off:off + nk], vv)
            o = part if o is None else o + part
        return o / row_sum

    n_heads = len(heads)
    scores = {0: score(0), 1: score(1)}
    row_sums = {}
    outs = []
    for h in range(n_heads):
        row_sums[h] = softmax(h, scores.pop(h))
        if h + 2 < n_heads:
            scores[h + 2] = score(h + 2)
        if h >= 1:
            outs.append(values(h - 1, row_sums.pop(h - 1)))
    outs.append(values(n_heads - 1, row_sums.pop(n_heads - 1)))

    if mode == "C":
        out = jnp.concatenate([jnp.where(first_half, outs[0], outs[1]),
                               jnp.where(first_half, outs[2], outs[3])], axis=1)
    else:
        lane_head = lax.broadcasted_iota(jnp.int32, (tq, GROUP_W), 1) // D_V
        out = outs[0]
        for h in range(1, D_HEADS):
            out = jnp.where(lane_head == h, outs[h], out)
    o_ref[0] = _rms_full(out, gain_ref[...]).astype(BF16)


def _attn_call(q, ks, vs, g_go, layer, *, mode, tq):
    b, s, wq = q.shape
    nseg = len(ks)
    gain_block = 2 if mode == "C" else 3
    in_specs = [pl.BlockSpec((1, tq, wq), lambda bi, qi: (bi, qi, 0))]
    for a in list(ks) + list(vs):
        in_specs.append(pl.BlockSpec((1,) + a.shape[1:], lambda bi, qi: (bi, 0, 0)))
    in_specs.append(pl.BlockSpec((None, 1, GROUP_W), lambda bi, qi: (layer, 0, gain_block)))
    return pl.pallas_call(
        functools.partial(_attn_kernel, mode=mode, nseg=nseg, tq=tq),
        grid=(b, s // tq),
        in_specs=in_specs,
        out_specs=pl.BlockSpec((1, tq, GROUP_W), lambda bi, qi: (bi, qi, 0)),
        out_shape=jax.ShapeDtypeStruct((b, s, GROUP_W), BF16),
        scratch_shapes=[pltpu.VMEM((4, tq, sum(a.shape[1] for a in ks)), BF16)],
        compiler_params=pltpu.CompilerParams(dimension_semantics=("parallel", "parallel"),
                                             vmem_limit_bytes=VMEM_LIMIT),
        name="attn_" + mode,
    )(q, *ks, *vs, g_go)


def _merge_kernel(x_ref, ya_ref, yb_ref, yc_ref, yd_ref, mod_ref, wout_ref, gffn_ref, wr_ref, br_ref,
                  xo_ref, h2_ref, route_ref, cnt_ref, *, tm):
    nb = tm // MERGE_ROWS
    blocks = [slice(j * MERGE_ROWS, (j + 1) * MERGE_ROWS) for j in range(nb)]
    ys = []
    for rs in blocks:
        y = _dot(ya_ref[rs, :], wout_ref[0:256, :])
        y = y + _dot(yb_ref[rs, :], wout_ref[256:512, :])
        y = y + _dot(yc_ref[rs, :], wout_ref[512:768, :])
        y = y + _dot(yd_ref[rs, :], wout_ref[768:1024, :])
        ys.append(y)
    counts = None
    for rs, y in zip(blocks, ys):
        sel = _merge_block(rs, y, x_ref, mod_ref, gffn_ref, wr_ref, br_ref, xo_ref, h2_ref, route_ref)
        part = sel.sum(axis=0, keepdims=True)
        counts = part if counts is None else counts + part
    cnt_ref[0] = jnp.broadcast_to(counts, (HALO, LANES))


def _merge_block(rs, y, x_ref, mod_ref, gffn_ref, wr_ref, br_ref, xo_ref, h2_ref, route_ref):
    rows = rs.stop - rs.start
    xn = x_ref[rs, :] + mod_ref[0, 2:3, :] * y
    xo_ref[rs, :] = xn
    h2 = _rms_full(xn, gffn_ref[...]) * (1.0 + mod_ref[0, 4:5, :]) + mod_ref[0, 3:4, :]
    h2b = h2.astype(BF16)
    h2_ref[rs, :] = h2b

    h2lo = (h2 - h2b.astype(F32)).astype(BF16)
    part = _nt_dot(wr_ref[...], h2b) + _nt_dot(wr_ref[...], h2lo)
    logit = part[0:ROUTE_SEL_LANE] + part[ROUTE_SEL_LANE:2 * ROUTE_SEL_LANE] + br_ref[...]
    row_of = lambda j: logit[j:j + 1, :]
    lg = [row_of(N_EXPERTS + g) for g in range(N_GROUPS)]

    def first_max(vals):
        m = vals[0]
        for v in vals[1:]:
            m = jnp.maximum(m, v)
        idx = jnp.float32(len(vals) - 1)
        for j in range(len(vals) - 2, -1, -1):
            idx = jnp.where(vals[j] == m, jnp.float32(j), idx)
        return m, idx

    gmax, g_idx = first_max(lg)
    g_w = 1.0 / sum(jnp.exp(v - gmax) for v in lg)
    cand = []
    for j in range(EXPERTS_PER_GROUP):
        c = row_of((N_GROUPS - 1) * EXPERTS_PER_GROUP + j)
        for g in range(N_GROUPS - 2, -1, -1):
            c = jnp.where(g_idx == g, row_of(g * EXPERTS_PER_GROUP + j), c)
        cand.append(c)
    m1, i1 = first_max(cand)
    neg = jnp.float32(-jnp.inf)
    m2, i2 = first_max([jnp.where(i1 == j, neg, c) for j, c in enumerate(cand)])
    r = jnp.exp(m2 - m1)
    w1 = g_w / (1.0 + r)
    w2 = g_w * r / (1.0 + r)
    e1 = g_idx * EXPERTS_PER_GROUP + i1
    e2 = g_idx * EXPERTS_PER_GROUP + i2
    eio = lax.broadcasted_iota(jnp.int32, (N_EXPERTS, rows), 0).astype(F32)
    cmb_t = jnp.where(eio == e1, w1, jnp.where(eio == e2, w2, 0.0))
    sel_t = jnp.where((eio == e1) | (eio == e2), 1.0, 0.0)
    zeros = lambda n: jnp.zeros((n, rows), F32)
    route_t = jnp.concatenate([cmb_t, zeros(ROUTE_SEL_LANE - N_EXPERTS), sel_t,
                               zeros(LANES - ROUTE_SEL_LANE - N_EXPERTS)], axis=0)
    route = route_t.T
    route_ref[rs, :] = route
    lane = lax.broadcasted_iota(jnp.int32, (rows, LANES), 1)
    return jnp.where(lane >= ROUTE_SEL_LANE, route, 0.0)


def _merge_call(x2d, ya, yb, yc, yd, mod, lw, layer, *, tm, mod_index):
    n = x2d.shape[0]
    row = lambda i: (i, 0)
    in_specs = [pl.BlockSpec((tm, D_MODEL), row)]
    in_specs += [pl.BlockSpec((tm, GROUP_W), row)] * 4
    in_specs += [_mod_spec(layer, mod_index)]
    in_specs += [_layer_spec(lw[k], layer) for k in ("w_out", "g_ffn", "w_router", "b_router")]
    return pl.pallas_call(
        functools.partial(_merge_kernel, tm=tm),
        grid=(n // tm,),
        in_specs=in_specs,
        out_specs=[pl.BlockSpec((tm, D_MODEL), row), pl.BlockSpec((tm, D_MODEL), row),
                   pl.BlockSpec((tm, LANES), row), pl.BlockSpec((1, HALO, LANES), lambda i: (i, 0, 0))],
        out_shape=[jax.ShapeDtypeStruct((n, D_MODEL), F32), jax.ShapeDtypeStruct((n, D_MODEL), BF16),
                   jax.ShapeDtypeStruct((n, LANES), F32),
                   jax.ShapeDtypeStruct((n // tm, HALO, LANES), F32)],
        compiler_params=pltpu.CompilerParams(dimension_semantics=("parallel",),
                                             vmem_limit_bytes=VMEM_LIMIT),
        name="merge",
    )(x2d, ya, yb, yc, yd, mod, lw["w_out"], lw["g_ffn"], lw["w_router"], lw["b_router"])


def _segment_positions(route, loc, tm):
    lane = lax.broadcasted_iota(jnp.int32, (tm, LANES), 1)
    is_sel = (lane >= ROUTE_SEL_LANE) & (lane < ROUTE_SEL_LANE + N_EXPERTS)
    selm = is_sel & (route > 0.5)
    selb = jnp.where(selm, 1.0, 0.0).astype(BF16)
    r_i = lax.broadcasted_iota(jnp.int32, (tm, tm), 0)
    c_i = lax.broadcasted_iota(jnp.int32, (tm, tm), 1)
    earlier = jnp.where(c_i < r_i, 1.0, 0.0).astype(BF16)
    dest = loc + _dot(earlier, selb)
    big = jnp.float32(1e9)
    d_a = jnp.where(selm, dest, big).min(axis=-1, keepdims=True)
    d_b = jnp.where(selm, dest, -big).max(axis=-1, keepdims=True)
    return selm, dest, d_a, d_b


def _to_row(col, tm):
    r_i = lax.broadcasted_iota(jnp.int32, (tm, tm), 0)
    c_i = lax.broadcasted_iota(jnp.int32, (tm, tm), 1)
    return jnp.where(r_i == c_i, col, 0.0).sum(axis=0, keepdims=True)


def _chunk_copy_out(xs_ref, xbuf_ref, sem_ref, slot, src_chunk, dst_chunk):
    return pltpu.make_async_copy(xs_ref.at[slot, pl.ds(src_chunk * MOE_CHUNK, MOE_CHUNK), :],
                                 xbuf_ref.at[pl.ds(dst_chunk * MOE_CHUNK, MOE_CHUNK), :],
                                 sem_ref.at[slot])


def _dispatch_kernel(nch_ref, dst_ref, plo_ref, phi_ref, nused_ref, h2_ref, route_ref, loc_ref, xbuf_ref,
                     xs_ref, sem_ref, *, tm, n_tiles, n_work):
    i = pl.program_id(0)
    slot = i % 2
    zero_chunk = MOE_RLOC // MOE_CHUNK

    def start_tile(t_idx, slot_):
        def body(j, carry):
            _chunk_copy_out(xs_ref, xbuf_ref, sem_ref, slot_, j, dst_ref[t_idx * MOE_NCH + j]).start()
            return carry
        lax.fori_loop(0, nch_ref[t_idx], body, 0)

    def wait_tile(t_idx, slot_):
        def body(j, carry):
            _chunk_copy_out(xs_ref, xbuf_ref, sem_ref, slot_, 0, 0).wait()
            return carry
        lax.fori_loop(0, nch_ref[t_idx], body, 0)

    @pl.when(i == 0)
    def _():
        xs_ref[2, :, :] = jnp.zeros((MOE_RLOC + MOE_CHUNK, MOE_XW), BF16)

        def pad_start(j, carry):
            _chunk_copy_out(xs_ref, xbuf_ref, sem_ref, 2, zero_chunk, j).start()
            return carry

        def pad_wait(j, carry):
            _chunk_copy_out(xs_ref, xbuf_ref, sem_ref, 2, zero_chunk, 0).wait()
            return carry
        for e in range(N_EXPERTS):
            lax.fori_loop(plo_ref[e], phi_ref[e], pad_start, 0)
        for e in range(N_EXPERTS):
            lax.fori_loop(plo_ref[e], phi_ref[e], pad_wait, 0)

        def tail_copy(w):
            return pltpu.make_async_copy(xs_ref.at[2, pl.ds(0, MOE_WORK), :],
                                         xbuf_ref.at[pl.ds(w * MOE_WORK, MOE_WORK), :], sem_ref.at[2])

        def tail_start(w, carry):
            tail_copy(w).start()
            return carry

        def tail_wait(w, carry):
            tail_copy(w).wait()
            return carry
        lax.fori_loop(nused_ref[0], n_work, tail_start, 0)
        lax.fori_loop(nused_ref[0], n_work, tail_wait, 0)

    @pl.when(i >= 2)
    def _():
        wait_tile(i - 2, slot)

    route = route_ref[...]
    selm, dest, d_a, d_b = _segment_positions(route, loc_ref[0, 0:1, :], tm)
    cmb_al = pltpu.roll(route, ROUTE_SEL_LANE, axis=1)
    w_a = jnp.where(selm & (dest == d_a), cmb_al, 0.0).sum(axis=-1, keepdims=True)
    w_b = jnp.where(selm & (dest == d_b), cmb_al, 0.0).sum(axis=-1, keepdims=True)
    da_r, db_r, wa_r, wb_r = (_to_row(v, tm) for v in (d_a, d_b, w_a, w_b))
    h2 = h2_ref[...]
    lane = lax.broadcasted_iota(jnp.int32, (MOE_RBLK, LANES), 1)
    for r0 in range(0, MOE_RLOC, MOE_RBLK):
        rio = (lax.broadcasted_iota(jnp.int32, (MOE_RBLK, tm), 0) + r0).astype(F32)
        hit_a = rio == da_r
        hit_b = rio == db_r
        perm = jnp.where(hit_a | hit_b, 1.0, 0.0).astype(BF16)
        xs_ref[slot, r0:r0 + MOE_RBLK, 0:D_MODEL] = _dot(perm, h2).astype(BF16)
        w_sorted = (jnp.where(hit_a, wa_r, 0.0) + jnp.where(hit_b, wb_r, 0.0)).sum(axis=-1, keepdims=True)
        w_hi = w_sorted.astype(BF16).astype(F32)
        r1 = w_sorted - w_hi
        w_mid = r1.astype(BF16).astype(F32)
        w_lo = r1 - w_mid
        slab = jnp.where(lane == 0, w_hi, jnp.where(lane == 1, w_mid, jnp.where(lane == 2, w_lo, 0.0)))
        xs_ref[slot, r0:r0 + MOE_RBLK, D_MODEL:MOE_XW] = slab.astype(BF16)
    start_tile(i, slot)

    @pl.when(i == n_tiles - 1)
    def _():
        if n_tiles > 1:
            wait_tile(i - 1, 1 - slot)
        wait_tile(i, slot)


def _dispatch_call(h2, route, tabs, *, tm, n_rows):
    n = h2.shape[0]
    n_tiles = n // tm
    grid_spec = pltpu.PrefetchScalarGridSpec(
        num_scalar_prefetch=5,
        grid=(n_tiles,),
        in_specs=[pl.BlockSpec((tm, D_MODEL), lambda i, *_: (i, 0)),
                  pl.BlockSpec((tm, LANES), lambda i, *_: (i, 0)),
                  pl.BlockSpec((1, HALO, LANES), lambda i, *_: (i, 0, 0))],
        out_specs=pl.BlockSpec(memory_space=pl.ANY),
        scratch_shapes=[pltpu.VMEM((3, MOE_RLOC + MOE_CHUNK, MOE_XW), BF16),
                        pltpu.SemaphoreType.DMA((3,))],
    )
    return pl.pallas_call(
        functools.partial(_dispatch_kernel, tm=tm, n_tiles=n_tiles, n_work=n_rows // MOE_WORK),
        grid_spec=grid_spec,
        out_shape=jax.ShapeDtypeStruct((n_rows, MOE_XW), BF16),
        compiler_params=pltpu.CompilerParams(dimension_semantics=("arbitrary",),
                                             vmem_limit_bytes=VMEM_LIMIT),
        name="moe_dispatch",
    )(tabs["nch"], tabs["dst"], tabs["pad_lo"], tabs["pad_hi"], tabs["n_used"], h2, route, tabs["loc"])


def _expert_kernel(te_ref, nused_ref, x_ref, wup_ref, wdn_ref, y_ref, wup_bf, wdn_bf):
    w = pl.program_id(0)
    used = w < nused_ref[0]

    @pl.when((w == 0) | (te_ref[w] != te_ref[jnp.maximum(w - 1, 0)]))
    def _():
        wup_bf[...] = wup_ref[...].astype(BF16)
        wdn_bf[...] = wdn_ref[...].astype(BF16)

    @pl.when(used)
    def _():
        wt = x_ref[:, D_MODEL:MOE_XW].astype(F32).sum(axis=-1, keepdims=True)
        gu = _dot(x_ref[:, 0:D_MODEL], wup_bf[...])
        g = gu[:, 0:D_EXPERT]
        a = (g * jax.nn.sigmoid(g)) * gu[:, D_EXPERT:2 * D_EXPERT] * wt
        y_ref[...] = _dot(a.astype(BF16), wdn_bf[...]).astype(BF16)

    @pl.when(jnp.logical_not(used))
    def _():
        y_ref[...] = jnp.zeros(y_ref.shape, BF16)


def _expert_call(xbuf, tabs, w_up, w_down, layer):
    n_rows = xbuf.shape[0]
    n_work = n_rows // MOE_WORK
    base = layer * N_EXPERTS
    grid_spec = pltpu.PrefetchScalarGridSpec(
        num_scalar_prefetch=2,
        grid=(n_work,),
        in_specs=[pl.BlockSpec((MOE_WORK, MOE_XW), lambda w, te, nu: (jnp.minimum(w, nu[0] - 1), 0)),
                  pl.BlockSpec((None, D_MODEL, 2 * D_EXPERT), lambda w, te, nu: (base + te[w], 0, 0)),
                  pl.BlockSpec((None, D_EXPERT, D_MODEL), lambda w, te, nu: (base + te[w], 0, 0))],
        out_specs=pl.BlockSpec((MOE_WORK, D_MODEL), lambda w, te, nu: (w, 0)),
        scratch_shapes=[pltpu.VMEM((D_MODEL, 2 * D_EXPERT), BF16), pltpu.VMEM((D_EXPERT, D_MODEL), BF16)],
    )
    return pl.pallas_call(
        _expert_kernel,
        grid_spec=grid_spec,
        out_shape=jax.ShapeDtypeStruct((n_rows, D_MODEL), BF16),
        compiler_params=pltpu.CompilerParams(dimension_semantics=("arbitrary",),
                                             vmem_limit_bytes=VMEM_LIMIT),
        name="moe_experts",
    )(tabs["tile_expert"], tabs["n_used"], xbuf, w_up, w_down)


def _chunk_copy_in(ybuf_ref, ys_ref, sem_ref, slot, src_chunk, dst_chunk):
    return pltpu.make_async_copy(ybuf_ref.at[pl.ds(src_chunk * MOE_CHUNK, MOE_CHUNK), :],
                                 ys_ref.at[slot, pl.ds(dst_chunk * MOE_CHUNK, MOE_CHUNK), :],
                                 sem_ref.at[slot])


def _combine_kernel(nch_ref, dst_ref, route_ref, loc_ref, x_ref, mod_ref, ybuf_ref, o_ref,
                    ys_ref, sem_ref, *, tm, n_tiles):
    i = pl.program_id(0)
    slot = i % 2

    def start_tile(t_idx, slot_):
        def body(j, carry):
            _chunk_copy_in(ybuf_ref, ys_ref, sem_ref, slot_, dst_ref[t_idx * MOE_NCH + j], j).start()
            return carry
        lax.fori_loop(0, nch_ref[t_idx], body, 0)

    @pl.when(i == 0)
    def _():
        ys_ref[...] = jnp.zeros(ys_ref.shape, BF16)
        start_tile(0, 0)

    @pl.when(i + 1 < n_tiles)
    def _():
        start_tile(i + 1, 1 - slot)

    def wait_body(j, carry):
        _chunk_copy_in(ybuf_ref, ys_ref, sem_ref, slot, 0, 0).wait()
        return carry
    lax.fori_loop(0, nch_ref[i], wait_body, 0)

    _, _, d_a, d_b = _segment_positions(route_ref[...], loc_ref[0, 0:1, :], tm)
    y = None
    for r0 in range(0, MOE_RLOC, MOE_RBLK):
        lio = (lax.broadcasted_iota(jnp.int32, (tm, MOE_RBLK), 1) + r0).astype(F32)
        perm_t = jnp.where((lio == d_a) | (lio == d_b), 1.0, 0.0).astype(BF16)
        part = _dot(perm_t, ys_ref[slot, r0:r0 + MOE_RBLK, :])
        y = part if y is None else y + part
    o_ref[...] = x_ref[...] + mod_ref[0, 5:6, :] * y


def _combine_call(ybuf, route, x2d, mod, layer, tabs, *, tm, mod_index):
    n = x2d.shape[0]
    n_tiles = n // tm
    grid_spec = pltpu.PrefetchScalarGridSpec(
        num_scalar_prefetch=2,
        grid=(n_tiles,),
        in_specs=[pl.BlockSpec((tm, LANES), lambda i, *_: (i, 0)),
                  pl.BlockSpec((1, HALO, LANES), lambda i, *_: (i, 0, 0)),
                  pl.BlockSpec((tm, D_MODEL), lambda i, *_: (i, 0)),
                  _mod_spec(layer, mod_index),
                  pl.BlockSpec(memory_space=pl.ANY)],
        out_specs=pl.BlockSpec((tm, D_MODEL), lambda i, *_: (i, 0)),
        scratch_shapes=[pltpu.VMEM((2, MOE_RLOC, D_MODEL), BF16),
                        pltpu.SemaphoreType.DMA((2,))],
    )
    return pl.pallas_call(
        functools.partial(_combine_kernel, tm=tm, n_tiles=n_tiles),
        grid_spec=grid_spec,
        out_shape=jax.ShapeDtypeStruct((n, D_MODEL), F32),
        compiler_params=pltpu.CompilerParams(dimension_semantics=("arbitrary",),
                                             vmem_limit_bytes=VMEM_LIMIT),
        name="moe_combine",
    )(tabs["nch"], tabs["dst"], route, tabs["loc"], x2d, mod, ybuf)


def _moe_rows(n_tok):
    n_tiles = n_tok // MOE_T
    chunks = 2 * n_tok // MOE_CHUNK + (MOE_CHUNK - 1) * n_tiles + N_EXPERTS * (MOE_CPW - 1)
    return -(-chunks // MOE_CPW) * MOE_WORK


def _route_tables(counts, n_tok):
    n_work = _moe_rows(n_tok) // MOE_WORK
    cnt = counts[:, 0, ROUTE_SEL_LANE:ROUTE_SEL_LANE + N_EXPERTS].astype(jnp.int32)
    n_tiles = cnt.shape[0]
    nchunk = (cnt + MOE_CHUNK - 1) // MOE_CHUNK
    ei = jnp.arange(N_EXPERTS, dtype=jnp.int32)
    ti = jnp.arange(n_tiles, dtype=jnp.int32)
    upto_e = (ei[:, None] <= ei[None, :]).astype(jnp.int32)
    upto_t = (ti[None, :] <= ti[:, None]).astype(jnp.int32)
    ends = (nchunk[:, :, None] * upto_e[None, :, :]).sum(axis=1)
    loc = ends - nchunk
    nch = nchunk.sum(axis=1)
    per_e = nchunk.sum(axis=0)
    padded_e = (per_e + MOE_CPW - 1) // MOE_CPW * MOE_CPW
    e_end = (padded_e[:, None] * upto_e).sum(axis=0)
    e_base = e_end - padded_e
    glob = e_base[None, :] + (upto_t[:, :, None] * nchunk[None, :, :]).sum(axis=1) - nchunk
    j = jnp.arange(MOE_NCH, dtype=jnp.int32)
    e_of_j = (ends[:, None, :] <= j[None, :, None]).sum(axis=-1)
    seg = e_of_j[:, :, None] == ei[None, None, :]
    dst = jnp.where(seg, (glob - loc)[:, None, :], 0).sum(axis=-1) + j[None, :]
    dst = jnp.where(j[None, :] < nch[:, None], dst, 0)
    n_used = padded_e.sum() // MOE_CPW
    w = jnp.minimum(jnp.arange(n_work, dtype=jnp.int32), n_used - 1)
    tile_expert = jnp.minimum(((e_end // MOE_CPW)[None, :] <= w[:, None]).sum(axis=-1), N_EXPERTS - 1)
    loc_rows = jnp.pad((loc * MOE_CHUNK).astype(F32),
                       ((0, 0), (ROUTE_SEL_LANE, LANES - ROUTE_SEL_LANE - N_EXPERTS)))
    loc_rows = jnp.broadcast_to(loc_rows[:, None, :], (n_tiles, HALO, LANES))
    return {"nch": nch.astype(jnp.int32), "dst": dst.reshape(-1).astype(jnp.int32),
            "pad_lo": (e_base + per_e).astype(jnp.int32), "pad_hi": e_end.astype(jnp.int32),
            "tile_expert": tile_expert.astype(jnp.int32), "n_used": n_used.astype(jnp.int32).reshape(1),
            "loc": loc_rows}


def _moe_call(h2, route, counts, x2d, mod, w_up, w_down, layer, *, mod_index):
    n = x2d.shape[0]
    tabs = _route_tables(counts, n)
    xbuf = _dispatch_call(h2, route, tabs, tm=MOE_T, n_rows=_moe_rows(n))
    ybuf = _expert_call(xbuf, tabs, w_up, w_down, layer)
    return _combine_call(ybuf, route, x2d, mod, layer, tabs, tm=MOE_T, mod_index=mod_index)


def _ones_blocks(sizes):
    n = sum(sizes)
    m = np.zeros((n, n), np.float32)
    o = 0
    for s in sizes:
        m[o:o + s, o:o + s] = 1.0
        o += s
    return jnp.asarray(m, BF16)


def _inv_counts_q():
    return jnp.asarray(np.tile(np.concatenate([np.full(64, 1 / 64.0), np.full(64, 1 / 32.0)]), 2)[None, :], F32)


def _rope_tables(rows):
    r = np.repeat(np.arange(rows, dtype=np.float32), GRID_W)
    c = np.tile(np.arange(GRID_W, dtype=np.float32), rows)

    def tab(rot_dim):
        nf = rot_dim // 4
        inv = (np.float32(ROPE_THETA) ** (-np.arange(nf, dtype=np.float32) / np.float32(nf))).astype(np.float32)
        ang = np.concatenate([r[:, None] * inv, c[:, None] * inv], axis=-1).astype(np.float32)
        cos = np.repeat(np.cos(ang), 2, axis=-1)
        sin = np.repeat(np.sin(ang), 2, axis=-1) * np.tile(np.asarray([-1.0, 1.0], np.float32), rot_dim // 2)
        return cos.astype(np.float32), sin.astype(np.float32)

    s = rows * GRID_W
    cc, sc = tab(C_HEAD_DIM)
    cd, sd = tab(D_ROPE)
    one = lambda w: np.ones((s, w), np.float32)
    zero = lambda w: np.zeros((s, w), np.float32)
    tabs = (np.tile(cc, (1, 2)), np.tile(sc, (1, 2)),
            np.concatenate([one(64), cd, one(32)], axis=1), np.concatenate([zero(64), sd, zero(32)], axis=1),
            np.concatenate([cd, one(96)], axis=1), np.concatenate([sd, zero(96)], axis=1))
    return tuple(jnp.asarray(t) for t in tabs)


def _stacked_weights(p):
    (g_attn_norm, w_in, sgu_v_gain, sgu_w, sgu_b, conv_w, gqa_q_gain, gqa_k_gain, mla_qa_gain, mla_w_qb,
     mla_kva_gain, mla_w_kvb, mla_q_gain, mla_k_gain, g_group_out, w_out, g_ffn_norm,
     w_rg, b_rg, w_re, b_re) = p
    depth = w_in.shape[0]

    def heads_0213(a, axis, off):
        sl = lambda lo: lax.slice_in_dim(a, off + lo, off + lo + 64, axis=axis)
        return [sl(0), sl(128), sl(64), sl(192)]

    zeros = lambda *shape: jnp.zeros((depth,) + shape, F32)
    row = lambda a: a[:, None, :]
    w_in_p = jnp.concatenate([w_in[:, :, :1280]] + heads_0213(w_in, 2, 1280) +
                             [w_in[:, :, 1536:2208], zeros(D_MODEL, IN_W - 2208)], axis=2).astype(BF16)
    wqb = jnp.concatenate([jnp.concatenate([mla_w_qb[:, :, h * 96:h * 96 + 96], zeros(D_Q_LORA, 32)], axis=2)
                           for h in range(D_HEADS)], axis=2)
    wk = jnp.concatenate([jnp.concatenate([mla_w_kvb[:, :, h * 128:h * 128 + 64], zeros(D_KV_LORA, 64)], axis=2)
                          for h in range(D_HEADS)], axis=2)
    wv = jnp.concatenate([mla_w_kvb[:, :, h * 128 + 64:h * 128 + 128] for h in range(D_HEADS)], axis=2)
    qslab = jnp.concatenate([mla_q_gain, zeros(32)], axis=1)
    kslab = jnp.concatenate([mla_k_gain[:, :64], zeros(64)], axis=1)
    g_go = jnp.concatenate([g_group_out[:, :512]] + heads_0213(g_group_out, 1, 512) + [g_group_out[:, 768:]],
                           axis=1)
    w_out_p = jnp.concatenate([w_out[:, :512]] + heads_0213(w_out, 1, 512) + [w_out[:, 768:]],
                              axis=1).astype(BF16)
    w_r = jnp.concatenate([w_re, w_rg], axis=2)
    w_r_hi = w_r.astype(BF16)
    w_r_lo = (w_r - w_r_hi.astype(F32)).astype(BF16)
    n_r = N_EXPERTS + N_GROUPS
    zb = lambda w: jnp.zeros((depth, D_MODEL, w), BF16)
    w_router = jnp.concatenate([w_r_hi, zb(ROUTE_SEL_LANE - n_r), w_r_lo, zb(LANES - ROUTE_SEL_LANE - n_r)], axis=2)
    w_router = jnp.swapaxes(w_router, 1, 2)
    b_router = jnp.concatenate([b_re, b_rg, zeros(ROUTE_SEL_LANE - n_r)], axis=1)
    b_router = jnp.broadcast_to(b_router[:, :, None], (depth, ROUTE_SEL_LANE, MERGE_ROWS))
    sgub_full = jnp.repeat(jnp.swapaxes(sgu_b, 1, 2), 64, axis=2)
    return {
        "g_attn": row(g_attn_norm), "w_in": w_in_p,
        "sgu_w": sgu_w.reshape(depth, A_HEADS * CHUNK, CHUNK).astype(BF16), "sgu_b": sgub_full,
        "v_gain": row(sgu_v_gain), "conv_w": conv_w, "g_go": row(g_go),
        "cq_gain": row(jnp.tile(gqa_q_gain, (1, 4))), "ck_gain": row(jnp.tile(gqa_k_gain, (1, 2))),
        "qa_gain": row(mla_qa_gain), "w_qb": wqb.astype(BF16), "kva_gain": row(mla_kva_gain),
        "w_kvb": jnp.concatenate([wk, wv], axis=2).astype(BF16),
        "dq_gain": row(jnp.tile(qslab, (1, 4))), "dk_gain": row(jnp.tile(kslab, (1, 4))),
        "dkr_gain": row(jnp.concatenate([mla_k_gain[:, 64:], zeros(96)], axis=1)),
        "w_out": w_out_p, "g_ffn": row(g_ffn_norm), "w_router": w_router, "b_router": b_router,
    }


def kernel(x, c, ctx, c_ctx, w_mod, b_mod, g_attn_norm, w_in, sgu_v_gain, sgu_w, sgu_b, conv_w, gqa_q_gain,
           gqa_k_gain, mla_qa_gain, mla_w_qb, mla_kva_gain, mla_w_kvb, mla_q_gain, mla_k_gain, g_group_out,
           w_out, g_ffn_norm, w_router_group, b_router_group, w_router_expert, b_router_expert,
           w_expert_up, w_expert_down):
    b, s, d = x.shape
    lc = ctx.shape[1]
    depth = w_mod.shape[0]
    lw = _stacked_weights((g_attn_norm, w_in, sgu_v_gain, sgu_w, sgu_b, conv_w, gqa_q_gain, gqa_k_gain,
                           mla_qa_gain, mla_w_qb, mla_kva_gain, mla_w_kvb, mla_q_gain, mla_k_gain, g_group_out,
                           w_out, g_ffn_norm, w_router_group, b_router_group, w_router_expert, b_router_expert))
    w_up = w_expert_up.reshape(depth * N_EXPERTS, d, 2 * D_EXPERT)
    w_down = w_expert_down.reshape(depth * N_EXPERTS, D_EXPERT, d)

    c_all = jnp.concatenate([c, c_ctx[None, :], jnp.zeros((16 - b - 1, d), F32)], axis=0)
    mod = _modulation(c_all, w_mod, b_mod).reshape(depth, 16, 6, d)
    rope = _rope_tables(s // GRID_W)

    tm = MOE_T
    tq = 512
    tpb = s // tm
    lat_idx = lambda i: i // tpb
    ctx_idx = lambda i: b
    r3 = lambda a, n: a.reshape(b, n, a.shape[-1])

    xl = x.reshape(b * s, d)
    xc = ctx.reshape(b * lc, d)
    for l in range(depth):
        update_ctx = l < depth - 1
        ya_c, yb_c, qc_c, kc_c, vc_c, qd_c, kd_c, vd_c = _in_call(
            xc, mod, lw, l, None, tm=tm, seq_len=lc, mod_index=ctx_idx)
        ya, yb, qc, kc, vc, qd, kd, vd = _in_call(xl, mod, lw, l, rope, tm=IN_T, seq_len=s,
                                                  mod_index=lambda i: i // (s // IN_T))
        yc = _attn_call(r3(qc, s), [r3(kc_c, lc), r3(kc, s)], [r3(vc_c, lc), r3(vc, s)], lw["g_go"], l,
                        mode="C", tq=tq)
        yd = _attn_call(r3(qd, s), [r3(kd_c, lc), r3(kd, s)], [r3(vd_c, lc), r3(vd, s)], lw["g_go"], l,
                        mode="D", tq=tq)
        xl, h2, route, counts = _merge_call(xl, ya, yb, yc.reshape(b * s, GROUP_W), yd.reshape(b * s, GROUP_W),
                                            mod, lw, l, tm=tm, mod_index=lat_idx)
        xl = _moe_call(h2, route, counts, xl, mod, w_up, w_down, l, mod_index=lat_idx)
        if update_ctx:
            yc_c = _attn_call(r3(qc_c, lc), [r3(kc_c, lc)], [r3(vc_c, lc)], lw["g_go"], l, mode="C", tq=lc)
            yd_c = _attn_call(r3(qd_c, lc), [r3(kd_c, lc)], [r3(vd_c, lc)], lw["g_go"], l, mode="D", tq=lc)
            xc, h2c, route_c, counts_c = _merge_call(xc, ya_c, yb_c, yc_c.reshape(b * lc, GROUP_W),
                                                     yd_c.reshape(b * lc, GROUP_W), mod, lw, l, tm=tm,
                                                     mod_index=ctx_idx)
            xc = _moe_call(h2c, route_c, counts_c, xc, mod, w_up, w_down, l, mod_index=ctx_idx)
    return xl.reshape(b, s, d)
```

```python
import functools

import numpy as np
import jax
import jax.numpy as jnp
from jax import lax
from jax.experimental import pallas as pl
from jax.experimental.pallas import tpu as pltpu

F32 = jnp.float32
BF16 = jnp.bfloat16

D_MODEL = 1024
GRID_W = 64
EPS = 1e-6
ROPE_THETA = 10000.0
GROUP_W = 256
CHUNK = 128
A_HEADS = 4
C_HEAD_DIM = 64
C_SCALE = C_HEAD_DIM ** -0.5
D_HEADS = 4
D_NOPE = 64
D_ROPE = 32
D_V = 64
D_Q_LORA = 256
D_KV_LORA = 128
D_SCALE = (D_NOPE + D_ROPE) ** -0.5
LOG2E = 1.4426950408889634
ATTN_ROWS = 16
MERGE_ROWS = 128
N_GROUPS = 4
EXPERTS_PER_GROUP = 4
N_EXPERTS = 16
D_EXPERT = 256
LANES = 128
HALO = 8

IN_W = 2304
IN_T = 512
IN_SECTIONS = ((0, 256), (256, 512), (512, 768), (768, 1024), (1024, 1280), (1280, 1536), (1536, 1664),
               (1664, 1792), (1792, 2048), (2048, 2176), (2176, 2304))

ROUTE_SEL_LANE = 32
MOE_T = 512
MOE_CHUNK = 16
MOE_RLOC = 1280
MOE_NCH = MOE_RLOC // MOE_CHUNK
MOE_RBLK = 256
MOE_XW = D_MODEL + LANES
MOE_WORK = 1024
MOE_CPW = MOE_WORK // MOE_CHUNK
VMEM_LIMIT = 56 * 1024 * 1024


def _nt_dot(a, b):
    return lax.dot_general(a, b, (((1,), (1,)), ((), ())), preferred_element_type=F32)


def _dot(a, b):
    return jnp.dot(a, b, preferred_element_type=F32)


def _rms_full(x, gain):
    return x * lax.rsqrt(jnp.mean(x * x, axis=-1, keepdims=True) + EPS) * gain


def _group_ms(x, ones_blocks, inv_count):
    return _dot((x * x).astype(BF16), ones_blocks) * inv_count


def _pair_swap(x, even):
    nxt = pltpu.roll(x, x.shape[1] - 1, axis=1)
    prv = pltpu.roll(x, 1, axis=1)
    return jnp.where(even, nxt, prv)


def _rope(x, cos, sin, even):
    return x * cos + _pair_swap(x, even) * sin


def _mod_kernel(c_ref, w_ref, b_ref, o_ref):
    cv = c_ref[...]
    sc = cv * jax.nn.sigmoid(cv)
    o_ref[0] = jnp.dot(sc, w_ref[0], preferred_element_type=F32,
                       precision=lax.Precision.HIGHEST) + b_ref[0]


def _modulation(c_all, w_mod, b_mod):
    depth = w_mod.shape[0]
    nblk = 4
    wblk = 6 * D_MODEL // nblk
    return pl.pallas_call(
        _mod_kernel,
        grid=(depth, nblk),
        in_specs=[pl.BlockSpec((16, D_MODEL), lambda l, j: (0, 0)),
                  pl.BlockSpec((1, D_MODEL, wblk), lambda l, j: (l, 0, j)),
                  pl.BlockSpec((1, 1, wblk), lambda l, j: (l, 0, j))],
        out_specs=pl.BlockSpec((1, 16, wblk), lambda l, j: (l, 0, j)),
        out_shape=jax.ShapeDtypeStruct((depth, 16, 6 * D_MODEL), F32),
        compiler_params=pltpu.CompilerParams(dimension_semantics=("arbitrary", "arbitrary"),
                                             vmem_limit_bytes=VMEM_LIMIT),
        name="mod",
    )(c_all, w_mod, b_mod.reshape(depth, 1, 6 * D_MODEL))


def _in_kernel(*refs, tm, seq_len, use_rope):
    n_io = len(refs) - 4
    pj0_ref, pj1_ref, zh0_ref, zh1_ref = refs[n_io:]
    i = pl.program_id(0)

    @pl.when(i == 0)
    def _():
        pj1_ref[...] = jnp.zeros(pj1_ref.shape, F32)
        zh1_ref[...] = jnp.zeros(zh1_ref.shape, F32)

    @pl.when(i % 2 == 0)
    def _():
        _in_stages(refs[:n_io], pj0_ref, zh0_ref, pj1_ref, zh1_ref, tm=tm, seq_len=seq_len, use_rope=use_rope)

    @pl.when(i % 2 == 1)
    def _():
        _in_stages(refs[:n_io], pj1_ref, zh1_ref, pj0_ref, zh0_ref, tm=tm, seq_len=seq_len, use_rope=use_rope)


def _in_stages(refs, pj_w, zh_w, pj_r, zh_r, *, tm, seq_len, use_rope):
    (x_ref, xp_ref, xn_ref, mod_ref, gattn_ref, win_ref, sguw_ref, sgub_ref, vgain_ref, convw_ref,
     ggo_ref, cqg_ref, ckg_ref, qag_ref, wqb_ref, kvag_ref, wkvb_ref, dqg_ref, dkg_ref, dkrg_ref,
     b64_ref, bq_ref, invq_ref) = refs[:23]
    pos = 23
    if use_rope:
        cosc_ref, sinc_ref, cosq_ref, sinq_ref, cosk_ref, sink_ref = refs[pos:pos + 6]
        pos += 6
    ya_ref, yb_ref, qc_ref, kc_ref, vc_ref, qd_ref, kd_ref, vd_ref = refs[pos:pos + 8]

    shift = mod_ref[0, 0:1, :]
    scale = mod_ref[0, 1:2, :]
    gattn = gattn_ref[...]

    def prenorm(xv):
        return (_rms_full(xv, gattn) * (1.0 + scale) + shift).astype(BF16)

    hb = prenorm(x_ref[...])

    pending = list(IN_SECTIONS)

    def project(count):
        for _ in range(min(count, len(pending))):
            lo, hi = pending.pop(0)
            pj_w[:, lo:hi] = _dot(hb, win_ref[:, lo:hi])

    project(1)
    hh = prenorm(jnp.concatenate([xp_ref[...], xn_ref[...]], axis=0))
    zh_w[...] = _dot(hh, win_ref[:, 768:1024]) * _dot(hh, win_ref[:, 1024:1280])

    i = jnp.maximum(pl.program_id(0) - 1, 0)
    proj = {ab: pj_r[:, ab[0]:ab[1]] for ab in IN_SECTIONS}
    zh = zh_r[...]
    b64 = b64_ref[...]
    inv64 = 1.0 / 64.0
    lane128 = lax.broadcasted_iota(jnp.int32, (tm, LANES), 1)
    even = (lane128 & 1) == 0

    u = jax.nn.gelu(proj[(0, 256)])
    v = jax.nn.gelu(proj[(256, 512)])
    project(1)
    v = (v * lax.rsqrt(_group_ms(v, b64, inv64) + EPS) * vgain_ref[...]).astype(BF16)
    project(1)
    lane_head = lax.broadcasted_iota(jnp.int32, (CHUNK, GROUP_W), 1) // 64
    sguw = sguw_ref[...]
    mixed_chunks = []
    for ci in range(tm // CHUNK):
        r = _dot(sguw, v[ci * CHUNK:(ci + 1) * CHUNK, :])
        m = r[0:CHUNK]
        for h in range(1, A_HEADS):
            m = jnp.where(lane_head == h, r[h * CHUNK:(h + 1) * CHUNK], m)
        mixed_chunks.append(m + sgub_ref[...])
    ya = u * jnp.concatenate(mixed_chunks, axis=0)
    ya_ref[...] = _rms_full(ya, ggo_ref[:, 0:256]).astype(BF16)

    project(1)
    bg = proj[(512, 768)]
    z = proj[(768, 1024)] * proj[(1024, 1280)]
    row = lax.broadcasted_iota(jnp.int32, (tm, GROUP_W), 0)
    posn = (i * tm + row) % seq_len
    z_prev = jnp.where(row == 0, zh[HALO - 1:HALO, :], pltpu.roll(z, 1, axis=0))
    z_prev = jnp.where(posn == 0, 0.0, z_prev)
    z_next = jnp.where(row == tm - 1, zh[HALO:HALO + 1, :], pltpu.roll(z, tm - 1, axis=0))
    z_next = jnp.where(posn == seq_len - 1, 0.0, z_next)
    yb = bg * (convw_ref[0:1, :] * z_prev + convw_ref[1:2, :] * z + convw_ref[2:3, :] * z_next)
    yb_ref[...] = _rms_full(yb, ggo_ref[:, 256:512]).astype(BF16)

    project(1)
    q = proj[(1280, 1536)]
    q = q * lax.rsqrt(_group_ms(q, b64, inv64) + EPS) * cqg_ref[...]
    project(1)
    k = proj[(1536, 1664)]
    k = k * lax.rsqrt(_group_ms(k, b64[0:128, 0:128], inv64) + EPS) * ckg_ref[...]
    if use_rope:
        cosc, sinc = cosc_ref[...], sinc_ref[...]
        q = jnp.concatenate([_rope(q[:, 0:128], cosc, sinc, even),
                             _rope(q[:, 128:256], cosc, sinc, even)], axis=1)
        k = _rope(k, cosc, sinc, even)
    qc_ref[...] = (q * (C_SCALE * LOG2E)).astype(BF16)
    kc_ref[...] = k.astype(BF16)
    vc_ref[...] = proj[(1664, 1792)].astype(BF16)

    project(1)
    cq = _rms_full(proj[(1792, 2048)], qag_ref[...]).astype(BF16)
    project(1)
    qd = _dot(cq, wqb_ref[...])
    bq = bq_ref[...]
    invq = invq_ref[...]
    halves = []
    for hf in range(2):
        qh = qd[:, hf * 256:(hf + 1) * 256]
        qh = qh * lax.rsqrt(_group_ms(qh, bq, invq) + EPS) * dqg_ref[:, hf * 256:(hf + 1) * 256]
        if use_rope:
            cosq, sinq = cosq_ref[...], sinq_ref[...]
            qh = jnp.concatenate([_rope(qh[:, 0:128], cosq, sinq, even),
                                  _rope(qh[:, 128:256], cosq, sinq, even)], axis=1)
        halves.append(qh)
    qd_ref[...] = (jnp.concatenate(halves, axis=1) * (D_SCALE * LOG2E)).astype(BF16)

    project(1)
    ckv = _rms_full(proj[(2048, 2176)], kvag_ref[...]).astype(BF16)
    project(1)
    kv = _dot(ckv, wkvb_ref[...])
    vd_ref[...] = kv[:, 512:768].astype(BF16)
    project(len(pending))
    kr = proj[(2176, 2304)]
    kr = kr * lax.rsqrt(jnp.sum(kr * kr, axis=-1, keepdims=True) * (1.0 / D_ROPE) + EPS) * dkrg_ref[...]
    if use_rope:
        kr = _rope(kr, cosk_ref[...], sink_ref[...], even)
    kr = pltpu.roll(kr, D_NOPE, axis=1)
    for hf in range(2):
        kh = kv[:, hf * 256:(hf + 1) * 256]
        kh = kh * lax.rsqrt(_group_ms(kh, b64, inv64) + EPS) * dkg_ref[:, hf * 256:(hf + 1) * 256]
        kd_ref[:, hf * 256:hf * 256 + 128] = (kh[:, 0:128] + kr).astype(BF16)
        kd_ref[:, hf * 256 + 128:(hf + 1) * 256] = (kh[:, 128:256] + kr).astype(BF16)


def _layer_spec(a, layer):
    zeros = (0,) * (a.ndim - 1)
    return pl.BlockSpec((None,) + a.shape[1:], lambda *_: (layer,) + zeros)


def _mod_spec(layer, mod_index):
    return pl.BlockSpec((None, 1, 6, D_MODEL), lambda i, *_: (layer, mod_index(i), 0, 0))


def _in_call(x2d, mod, lw, layer, rope, *, tm, seq_len, mod_index):
    n = x2d.shape[0]
    nt = n // tm
    nb8 = n // HALO
    tb = tm // HALO
    use_rope = rope is not None
    cur = lambda i: jnp.minimum(i, nt - 1)
    prev = lambda i: jnp.maximum(i - 1, 0)
    row = lambda i: (prev(i), 0)
    in_specs = [
        pl.BlockSpec((tm, D_MODEL), lambda i: (cur(i), 0)),
        pl.BlockSpec((HALO, D_MODEL), lambda i: (jnp.maximum(cur(i) * tb - 1, 0), 0)),
        pl.BlockSpec((HALO, D_MODEL), lambda i: (jnp.minimum((cur(i) + 1) * tb, nb8 - 1), 0)),
        _mod_spec(layer, lambda i: mod_index(cur(i))),
    ]
    args = [x2d, x2d, x2d, mod]
    for name in ("g_attn", "w_in", "sgu_w", "sgu_b", "v_gain", "conv_w", "g_go", "cq_gain", "ck_gain",
                 "qa_gain", "w_qb", "kva_gain", "w_kvb", "dq_gain", "dk_gain", "dkr_gain"):
        in_specs.append(_layer_spec(lw[name], layer))
        args.append(lw[name])
    for a in (_ones_blocks([64] * 4), _ones_blocks([64, 32, 32] * 2), _inv_counts_q()):
        in_specs.append(pl.BlockSpec(a.shape, lambda i: (0, 0)))
        args.append(a)
    if use_rope:
        ntab = seq_len // tm
        for a in rope:
            in_specs.append(pl.BlockSpec((tm, LANES), lambda i: (prev(i) % ntab, 0)))
            args.append(a)
    widths = (256, 256, 256, 128, 128, 512, 512, 256)
    out_specs = [pl.BlockSpec((tm, w), row) for w in widths]
    out_shape = [jax.ShapeDtypeStruct((n, w), BF16) for w in widths]
    return pl.pallas_call(
        functools.partial(_in_kernel, tm=tm, seq_len=seq_len, use_rope=use_rope),
        grid=(nt + 1,),
        in_specs=in_specs,
        out_specs=out_specs,
        out_shape=out_shape,
        scratch_shapes=[pltpu.VMEM((tm, IN_W), F32), pltpu.VMEM((tm, IN_W), F32),
                        pltpu.VMEM((2 * HALO, GROUP_W), F32), pltpu.VMEM((2 * HALO, GROUP_W), F32)],
        compiler_params=pltpu.CompilerParams(dimension_semantics=("arbitrary",),
                                             vmem_limit_bytes=VMEM_LIMIT),
        name="in_proj",
    )(*args)


def _attn_kernel(*refs, mode, nseg, tq):
    q_ref = refs[0]
    k_refs = refs[1:1 + nseg]
    v_refs = refs[1 + nseg:1 + 2 * nseg]
    gain_ref = refs[1 + 2 * nseg]
    o_ref = refs[2 + 2 * nseg]
    p_ref = refs[3 + 2 * nseg]
    nks = [r.shape[1] for r in k_refs]
    offs = [sum(nks[:j]) for j in range(nseg)]
    n_chunks = tq // ATTN_ROWS

    vs = [r[0] for r in v_refs]
    if mode == "C":
        ks_c = [r[0] for r in k_refs]
        first_half = lax.broadcasted_iota(jnp.int32, (tq, LANES), 1) < 64
        heads = []
        for sl in range(2):
            qs = q_ref[0, :, sl * 128:(sl + 1) * 128]
            heads.append((jnp.where(first_half, qs, jnp.zeros_like(qs)), ks_c))
            heads.append((jnp.where(first_half, jnp.zeros_like(qs), qs), ks_c))
    else:
        heads = [(q_ref[0, :, h * 128:(h + 1) * 128], [r[0, :, h * 128:(h + 1) * 128] for r in k_refs])
                 for h in range(D_HEADS)]

    def score(h):
        qm, ks = heads[h]
        return [_nt_dot(qm, kk) for kk in ks]

    def softmax(h, ss):
        sums = []
        for c in range(n_chunks):
            rows = slice(c * ATTN_ROWS, (c + 1) * ATTN_ROWS)
            m = ss[0][rows].max(axis=-1, keepdims=True)
            for s in ss[1:]:
                m = jnp.maximum(m, s[rows].max(axis=-1, keepdims=True))
            tot = None
            for s, off, nk in zip(ss, offs, nks):
                p = jnp.exp2(s[rows] - m).astype(BF16)
                ps = p.astype(F32).sum(axis=-1, keepdims=True)
                tot = ps if tot is None else tot + ps
                p_ref[h, rows, off:off + nk] = p
            sums.append(tot)
        return jnp.concatenate(sums, axis=0)

    def values(h, row_sum):
        o = None
        for off, nk, vv in zip(offs, nks, vs):
            part = _dot(p_ref[h, :, off:off + nk], vv)
            o = part if o is None else o + part
        return o / row_sum

    n_heads = len(heads)
    scores = {0: score(0), 1: score(1)}
    row_sums = {}
    outs = []
    for h in range(n_heads):
        row_sums[h] = softmax(h, scores.pop(h))
        if h + 2 < n_heads:
            scores[h + 2] = score(h + 2)
        if h >= 1:
            outs.append(values(h - 1, row_sums.pop(h - 1)))
    outs.append(values(n_heads - 1, row_sums.pop(n_heads - 1)))

    if mode == "C":
        out = jnp.concatenate([jnp.where(first_half, outs[0], outs[1]),
                               jnp.where(first_half, outs[2], outs[3])], axis=1)
    else:
        lane_head = lax.broadcasted_iota(jnp.int32, (tq, GROUP_W), 1) // D_V
        out = outs[0]
        for h in range(1, D_HEADS):
            out = jnp.where(lane_head == h, outs[h], out)
    o_ref[0] = _rms_full(out, gain_ref[...]).astype(BF16)


def _attn_call(q, ks, vs, g_go, layer, *, mode, tq):
    b, s, wq = q.shape
    nseg = len(ks)
    gain_block = 2 if mode == "C" else 3
    in_specs = [pl.BlockSpec((1, tq, wq), lambda bi, qi: (bi, qi, 0))]
    for a in list(ks) + list(vs):
        in_specs.append(pl.BlockSpec((1,) + a.shape[1:], lambda bi, qi: (bi, 0, 0)))
    in_specs.append(pl.BlockSpec((None, 1, GROUP_W), lambda bi, qi: (layer, 0, gain_block)))
    return pl.pallas_call(
        functools.partial(_attn_kernel, mode=mode, nseg=nseg, tq=tq),
        grid=(b, s // tq),
        in_specs=in_specs,
        out_specs=pl.BlockSpec((1, tq, GROUP_W), lambda bi, qi: (bi, qi, 0)),
        out_shape=jax.ShapeDtypeStruct((b, s, GROUP_W), BF16),
        scratch_shapes=[pltpu.VMEM((4, tq, sum(a.shape[1] for a in ks)), BF16)],
        compiler_params=pltpu.CompilerParams(dimension_semantics=("parallel", "parallel"),
                                             vmem_limit_bytes=VMEM_LIMIT),
        name="attn_" + mode,
    )(q, *ks, *vs, g_go)


def _merge_kernel(x_ref, ya_ref, yb_ref, yc_ref, yd_ref, mod_ref, wout_ref, gffn_ref, wr_ref, br_ref,
                  xo_ref, h2_ref, route_ref, cnt_ref, *, tm):
    nb = tm // MERGE_ROWS
    blocks = [slice(j * MERGE_ROWS, (j + 1) * MERGE_ROWS) for j in range(nb)]
    ys = []
    for rs in blocks:
        y = _dot(ya_ref[rs, :], wout_ref[0:256, :])
        y = y + _dot(yb_ref[rs, :], wout_ref[256:512, :])
        y = y + _dot(yc_ref[rs, :], wout_ref[512:768, :])
        y = y + _dot(yd_ref[rs, :], wout_ref[768:1024, :])
        ys.append(y)
    counts = None
    for rs, y in zip(blocks, ys):
        sel = _merge_block(rs, y, x_ref, mod_ref, gffn_ref, wr_ref, br_ref, xo_ref, h2_ref, route_ref)
        part = sel.sum(axis=0, keepdims=True)
        counts = part if counts is None else counts + part
    cnt_ref[0] = jnp.broadcast_to(counts, (HALO, LANES))


def _merge_block(rs, y, x_ref, mod_ref, gffn_ref, wr_ref, br_ref, xo_ref, h2_ref, route_ref):
    rows = rs.stop - rs.start
    xn = x_ref[rs, :] + mod_ref[0, 2:3, :] * y
    xo_ref[rs, :] = xn
    h2 = _rms_full(xn, gffn_ref[...]) * (1.0 + mod_ref[0, 4:5, :]) + mod_ref[0, 3:4, :]
    h2b = h2.astype(BF16)
    h2_ref[rs, :] = h2b

    h2lo = (h2 - h2b.astype(F32)).astype(BF16)
    part = _nt_dot(wr_ref[...], h2b) + _nt_dot(wr_ref[...], h2lo)
    logit = part[0:ROUTE_SEL_LANE] + part[ROUTE_SEL_LANE:2 * ROUTE_SEL_LANE] + br_ref[...]
    row_of = lambda j: logit[j:j + 1, :]
    lg = [row_of(N_EXPERTS + g) for g in range(N_GROUPS)]

    def first_max(vals):
        m = vals[0]
        for v in vals[1:]:
            m = jnp.maximum(m, v)
        idx = jnp.float32(len(vals) - 1)
        for j in range(len(vals) - 2, -1, -1):
            idx = jnp.where(vals[j] == m, jnp.float32(j), idx)
        return m, idx

    gmax, g_idx = first_max(lg)
    g_w = 1.0 / sum(jnp.exp(v - gmax) for v in lg)
    cand = []
    for j in range(EXPERTS_PER_GROUP):
        c = row_of((N_GROUPS - 1) * EXPERTS_PER_GROUP + j)
        for g in range(N_GROUPS - 2, -1, -1):
            c = jnp.where(g_idx == g, row_of(g * EXPERTS_PER_GROUP + j), c)
        cand.append(c)
    m1, i1 = first_max(cand)
    neg = jnp.float32(-jnp.inf)
    m2, i2 = first_max([jnp.where(i1 == j, neg, c) for j, c in enumerate(cand)])
    r = jnp.exp(m2 - m1)
    w1 = g_w / (1.0 + r)
    w2 = g_w * r / (1.0 + r)
    e1 = g_idx * EXPERTS_PER_GROUP + i1
    e2 = g_idx * EXPERTS_PER_GROUP + i2
    eio = lax.broadcasted_iota(jnp.int32, (N_EXPERTS, rows), 0).astype(F32)
    cmb_t = jnp.where(eio == e1, w1, jnp.where(eio == e2, w2, 0.0))
    sel_t = jnp.where((eio == e1) | (eio == e2), 1.0, 0.0)
    zeros = lambda n: jnp.zeros((n, rows), F32)
    route_t = jnp.concatenate([cmb_t, zeros(ROUTE_SEL_LANE - N_EXPERTS), sel_t,
                               zeros(LANES - ROUTE_SEL_LANE - N_EXPERTS)], axis=0)
    route = route_t.T
    route_ref[rs, :] = route
    lane = lax.broadcasted_iota(jnp.int32, (rows, LANES), 1)
    return jnp.where(lane >= ROUTE_SEL_LANE, route, 0.0)


def _merge_call(x2d, ya, yb, yc, yd, mod, lw, layer, *, tm, mod_index):
    n = x2d.shape[0]
    row = lambda i: (i, 0)
    in_specs = [pl.BlockSpec((tm, D_MODEL), row)]
    in_specs += [pl.BlockSpec((tm, GROUP_W), row)] * 4
    in_specs += [_mod_spec(layer, mod_index)]
    in_specs += [_layer_spec(lw[k], layer) for k in ("w_out", "g_ffn", "w_router", "b_router")]
    return pl.pallas_call(
        functools.partial(_merge_kernel, tm=tm),
        grid=(n // tm,),
        in_specs=in_specs,
        out_specs=[pl.BlockSpec((tm, D_MODEL), row), pl.BlockSpec((tm, D_MODEL), row),
                   pl.BlockSpec((tm, LANES), row), pl.BlockSpec((1, HALO, LANES), lambda i: (i, 0, 0))],
        out_shape=[jax.ShapeDtypeStruct((n, D_MODEL), F32), jax.ShapeDtypeStruct((n, D_MODEL), BF16),
                   jax.ShapeDtypeStruct((n, LANES), F32),
                   jax.ShapeDtypeStruct((n // tm, HALO, LANES), F32)],
        compiler_params=pltpu.CompilerParams(dimension_semantics=("parallel",),
                                             vmem_limit_bytes=VMEM_LIMIT),
        name="merge",
    )(x2d, ya, yb, yc, yd, mod, lw["w_out"], lw["g_ffn"], lw["w_router"], lw["b_router"])


def _segment_positions(route, loc, tm):
    lane = lax.broadcasted_iota(jnp.int32, (tm, LANES), 1)
    is_sel = (lane >= ROUTE_SEL_LANE) & (lane < ROUTE_SEL_LANE + N_EXPERTS)
    selm = is_sel & (route > 0.5)
    selb = jnp.where(selm, 1.0, 0.0).astype(BF16)
    r_i = lax.broadcasted_iota(jnp.int32, (tm, tm), 0)
    c_i = lax.broadcasted_iota(jnp.int32, (tm, tm), 1)
    earlier = jnp.where(c_i < r_i, 1.0, 0.0).astype(BF16)
    dest = loc + _dot(earlier, selb)
    big = jnp.float32(1e9)
    d_a = jnp.where(selm, dest, big).min(axis=-1, keepdims=True)
    d_b = jnp.where(selm, dest, -big).max(axis=-1, keepdims=True)
    return selm, dest, d_a, d_b


def _to_row(col, tm):
    r_i = lax.broadcasted_iota(jnp.int32, (tm, tm), 0)
    c_i = lax.broadcasted_iota(jnp.int32, (tm, tm), 1)
    return jnp.where(r_i == c_i, col, 0.0).sum(axis=0, keepdims=True)


def _chunk_copy_out(xs_ref, xbuf_ref, sem_ref, slot, src_chunk, dst_chunk):
    return pltpu.make_async_copy(xs_ref.at[slot, src_chunk], xbuf_ref.at[dst_chunk], sem_ref.at[slot])


def _dispatch_kernel(nch_ref, dst_ref, plo_ref, phi_ref, nused_ref, h2_ref, route_ref, loc_ref, xbuf_ref,
                     xs_ref, sem_ref, *, tm, n_tiles, n_work):
    i = pl.program_id(0)
    slot = i % 2
    zero_chunk = MOE_RLOC // MOE_CHUNK

    def start_tile(t_idx, slot_):
        def body(j, carry):
            _chunk_copy_out(xs_ref, xbuf_ref, sem_ref, slot_, j, dst_ref[t_idx * MOE_NCH + j]).start()
            return carry
        lax.fori_loop(0, nch_ref[t_idx], body, 0)

    def wait_tile(t_idx, slot_):
        def body(j, carry):
            _chunk_copy_out(xs_ref, xbuf_ref, sem_ref, slot_, 0, 0).wait()
            return carry
        lax.fori_loop(0, nch_ref[t_idx], body, 0)

    @pl.when(i == 0)
    def _():
        xs_ref[2] = jnp.zeros(xs_ref.shape[1:], BF16)

        def pad_start(j, carry):
            _chunk_copy_out(xs_ref, xbuf_ref, sem_ref, 2, zero_chunk, j).start()
            return carry

        def pad_wait(j, carry):
            _chunk_copy_out(xs_ref, xbuf_ref, sem_ref, 2, zero_chunk, 0).wait()
            return carry
        for e in range(N_EXPERTS):
            lax.fori_loop(plo_ref[e], phi_ref[e], pad_start, 0)
        for e in range(N_EXPERTS):
            lax.fori_loop(plo_ref[e], phi_ref[e], pad_wait, 0)

        def tail_copy(w):
            return pltpu.make_async_copy(xs_ref.at[2, pl.ds(0, MOE_CPW)],
                                         xbuf_ref.at[pl.ds(w * MOE_CPW, MOE_CPW)], sem_ref.at[2])

        def tail_start(w, carry):
            tail_copy(w).start()
            return carry

        def tail_wait(w, carry):
            tail_copy(w).wait()
            return carry
        lax.fori_loop(nused_ref[0], n_work, tail_start, 0)
        lax.fori_loop(nused_ref[0], n_work, tail_wait, 0)

    @pl.when(i >= 2)
    def _():
        wait_tile(i - 2, slot)

    route = route_ref[...]
    selm, dest, d_a, d_b = _segment_positions(route, loc_ref[0, 0:1, :], tm)
    cmb_al = pltpu.roll(route, ROUTE_SEL_LANE, axis=1)
    w_a = jnp.where(selm & (dest == d_a), cmb_al, 0.0).sum(axis=-1, keepdims=True)
    w_b = jnp.where(selm & (dest == d_b), cmb_al, 0.0).sum(axis=-1, keepdims=True)
    da_r, db_r, wa_r, wb_r = (_to_row(v, tm) for v in (d_a, d_b, w_a, w_b))
    h2 = h2_ref[...]
    lane = lax.broadcasted_iota(jnp.int32, (MOE_RBLK, LANES), 1)
    for r0 in range(0, MOE_RLOC, MOE_RBLK):
        rio = (lax.broadcasted_iota(jnp.int32, (MOE_RBLK, tm), 0) + r0).astype(F32)
        hit_a = rio == da_r
        hit_b = rio == db_r
        perm = jnp.where(hit_a | hit_b, 1.0, 0.0).astype(BF16)
        c0, nc = r0 // MOE_CHUNK, MOE_RBLK // MOE_CHUNK
        xs_ref[slot, c0:c0 + nc, :, 0:D_MODEL] = _dot(perm, h2).astype(BF16).reshape(nc, MOE_CHUNK, D_MODEL)
        w_sorted = (jnp.where(hit_a, wa_r, 0.0) + jnp.where(hit_b, wb_r, 0.0)).sum(axis=-1, keepdims=True)
        w_hi = w_sorted.astype(BF16).astype(F32)
        r1 = w_sorted - w_hi
        w_mid = r1.astype(BF16).astype(F32)
        w_lo = r1 - w_mid
        slab = jnp.where(lane == 0, w_hi, jnp.where(lane == 1, w_mid, jnp.where(lane == 2, w_lo, 0.0)))
        xs_ref[slot, c0:c0 + nc, :, D_MODEL:MOE_XW] = slab.astype(BF16).reshape(nc, MOE_CHUNK, LANES)
    start_tile(i, slot)

    @pl.when(i == n_tiles - 1)
    def _():
        if n_tiles > 1:
            wait_tile(i - 1, 1 - slot)
        wait_tile(i, slot)


def _dispatch_call(h2, route, tabs, *, tm, n_rows):
    n = h2.shape[0]
    n_tiles = n // tm
    grid_spec = pltpu.PrefetchScalarGridSpec(
        num_scalar_prefetch=5,
        grid=(n_tiles,),
        in_specs=[pl.BlockSpec((tm, D_MODEL), lambda i, *_: (i, 0)),
                  pl.BlockSpec((tm, LANES), lambda i, *_: (i, 0)),
                  pl.BlockSpec((1, HALO, LANES), lambda i, *_: (i, 0, 0))],
        out_specs=pl.BlockSpec(memory_space=pl.ANY),
        scratch_shapes=[pltpu.VMEM((3, max(MOE_NCH + 1, MOE_CPW), MOE_CHUNK, MOE_XW), BF16),
                        pltpu.SemaphoreType.DMA((3,))],
    )
    xbuf = pl.pallas_call(
        functools.partial(_dispatch_kernel, tm=tm, n_tiles=n_tiles, n_work=n_rows // MOE_WORK),
        grid_spec=grid_spec,
        out_shape=jax.ShapeDtypeStruct((n_rows // MOE_CHUNK, MOE_CHUNK, MOE_XW), BF16),
        compiler_params=pltpu.CompilerParams(dimension_semantics=("arbitrary",),
                                             vmem_limit_bytes=VMEM_LIMIT),
        name="moe_dispatch",
    )(tabs["nch"], tabs["dst"], tabs["pad_lo"], tabs["pad_hi"], tabs["n_used"], h2, route, tabs["loc"])
    return xbuf.reshape(n_rows, MOE_XW)


def _expert_kernel(te_ref, nused_ref, x_ref, wup_ref, wdn_ref, y_ref, wup_bf, wdn_bf):
    w = pl.program_id(0)
    used = w < nused_ref[0]

    @pl.when((w == 0) | (te_ref[w] != te_ref[jnp.maximum(w - 1, 0)]))
    def _():
        wup_bf[...] = wup_ref[...].astype(BF16)
        wdn_bf[...] = wdn_ref[...].astype(BF16)

    @pl.when(used)
    def _():
        wt = x_ref[:, D_MODEL:MOE_XW].astype(F32).sum(axis=-1, keepdims=True)
        gu = _dot(x_ref[:, 0:D_MODEL], wup_bf[...])
        g = gu[:, 0:D_EXPERT]
        a = (g * jax.nn.sigmoid(g)) * gu[:, D_EXPERT:2 * D_EXPERT] * wt
        y_ref[...] = _dot(a.astype(BF16), wdn_bf[...]).astype(BF16)

    @pl.when(jnp.logical_not(used))
    def _():
        y_ref[...] = jnp.zeros(y_ref.shape, BF16)


def _expert_call(xbuf, tabs, w_up, w_down, layer):
    n_rows = xbuf.shape[0]
    n_work = n_rows // MOE_WORK
    base = layer * N_EXPERTS
    grid_spec = pltpu.PrefetchScalarGridSpec(
        num_scalar_prefetch=2,
        grid=(n_work,),
        in_specs=[pl.BlockSpec((MOE_WORK, MOE_XW), lambda w, te, nu: (jnp.minimum(w, nu[0] - 1), 0)),
                  pl.BlockSpec((None, D_MODEL, 2 * D_EXPERT), lambda w, te, nu: (base + te[w], 0, 0)),
                  pl.BlockSpec((None, D_EXPERT, D_MODEL), lambda w, te, nu: (base + te[w], 0, 0))],
        out_specs=pl.BlockSpec((MOE_WORK, D_MODEL), lambda w, te, nu: (w, 0)),
        scratch_shapes=[pltpu.VMEM((D_MODEL, 2 * D_EXPERT), BF16), pltpu.VMEM((D_EXPERT, D_MODEL), BF16)],
    )
    return pl.pallas_call(
        _expert_kernel,
        grid_spec=grid_spec,
        out_shape=jax.ShapeDtypeStruct((n_rows, D_MODEL), BF16),
        compiler_params=pltpu.CompilerParams(dimension_semantics=("arbitrary",),
                                             vmem_limit_bytes=VMEM_LIMIT),
        name="moe_experts",
    )(tabs["tile_expert"], tabs["n_used"], xbuf, w_up, w_down)


def _chunk_copy_in(ybuf_ref, ys_ref, sem_ref, slot, src_chunk, dst_chunk):
    return pltpu.make_async_copy(ybuf_ref.at[src_chunk], ys_ref.at[slot, dst_chunk], sem_ref.at[slot])


def _combine_kernel(nch_ref, dst_ref, route_ref, loc_ref, x_ref, mod_ref, ybuf_ref, o_ref,
                    ys_ref, sem_ref, *, tm, n_tiles):
    i = pl.program_id(0)
    slot = i % 2

    def start_tile(t_idx, slot_):
        def body(j, carry):
            _chunk_copy_in(ybuf_ref, ys_ref, sem_ref, slot_, dst_ref[t_idx * MOE_NCH + j], j).start()
            return carry
        lax.fori_loop(0, nch_ref[t_idx], body, 0)

    @pl.when(i == 0)
    def _():
        ys_ref[...] = jnp.zeros(ys_ref.shape, BF16)
        start_tile(0, 0)

    @pl.when(i + 1 < n_tiles)
    def _():
        start_tile(i + 1, 1 - slot)

    def wait_body(j, carry):
        _chunk_copy_in(ybuf_ref, ys_ref, sem_ref, slot, 0, 0).wait()
        return carry
    lax.fori_loop(0, nch_ref[i], wait_body, 0)

    _, _, d_a, d_b = _segment_positions(route_ref[...], loc_ref[0, 0:1, :], tm)
    y = None
    for r0 in range(0, MOE_RLOC, MOE_RBLK):
        lio = (lax.broadcasted_iota(jnp.int32, (tm, MOE_RBLK), 1) + r0).astype(F32)
        perm_t = jnp.where((lio == d_a) | (lio == d_b), 1.0, 0.0).astype(BF16)
        c0 = r0 // MOE_CHUNK
        part = _dot(perm_t, ys_ref[slot, c0:c0 + MOE_RBLK // MOE_CHUNK].reshape(MOE_RBLK, D_MODEL))
        y = part if y is None else y + part
    o_ref[...] = x_ref[...] + mod_ref[0, 5:6, :] * y


def _combine_call(ybuf, route, x2d, mod, layer, tabs, *, tm, mod_index):
    n = x2d.shape[0]
    n_tiles = n // tm
    grid_spec = pltpu.PrefetchScalarGridSpec(
        num_scalar_prefetch=2,
        grid=(n_tiles,),
        in_specs=[pl.BlockSpec((tm, LANES), lambda i, *_: (i, 0)),
                  pl.BlockSpec((1, HALO, LANES), lambda i, *_: (i, 0, 0)),
                  pl.BlockSpec((tm, D_MODEL), lambda i, *_: (i, 0)),
                  _mod_spec(layer, mod_index),
                  pl.BlockSpec(memory_space=pl.ANY)],
        out_specs=pl.BlockSpec((tm, D_MODEL), lambda i, *_: (i, 0)),
        scratch_shapes=[pltpu.VMEM((2, MOE_NCH, MOE_CHUNK, D_MODEL), BF16),
                        pltpu.SemaphoreType.DMA((2,))],
    )
    return pl.pallas_call(
        functools.partial(_combine_kernel, tm=tm, n_tiles=n_tiles),
        grid_spec=grid_spec,
        out_shape=jax.ShapeDtypeStruct((n, D_MODEL), F32),
        compiler_params=pltpu.CompilerParams(dimension_semantics=("arbitrary",),
                                             vmem_limit_bytes=VMEM_LIMIT),
        name="moe_combine",
    )(tabs["nch"], tabs["dst"], route, tabs["loc"], x2d, mod,
      ybuf.reshape(ybuf.shape[0] // MOE_CHUNK, MOE_CHUNK, D_MODEL))


def _moe_rows(n_tok):
    n_tiles = n_tok // MOE_T
    chunks = 2 * n_tok // MOE_CHUNK + (MOE_CHUNK - 1) * n_tiles + N_EXPERTS * (MOE_CPW - 1)
    return -(-chunks // MOE_CPW) * MOE_WORK


def _route_tables(counts, n_tok):
    n_work = _moe_rows(n_tok) // MOE_WORK
    cnt = counts[:, 0, ROUTE_SEL_LANE:ROUTE_SEL_LANE + N_EXPERTS].astype(jnp.int32)
    n_tiles = cnt.shape[0]
    nchunk = (cnt + MOE_CHUNK - 1) // MOE_CHUNK
    ei = jnp.arange(N_EXPERTS, dtype=jnp.int32)
    ti = jnp.arange(n_tiles, dtype=jnp.int32)
    upto_e = (ei[:, None] <= ei[None, :]).astype(jnp.int32)
    upto_t = (ti[None, :] <= ti[:, None]).astype(jnp.int32)
    ends = (nchunk[:, :, None] * upto_e[None, :, :]).sum(axis=1)
    loc = ends - nchunk
    nch = nchunk.sum(axis=1)
    per_e = nchunk.sum(axis=0)
    padded_e = (per_e + MOE_CPW - 1) // MOE_CPW * MOE_CPW
    e_end = (padded_e[:, None] * upto_e).sum(axis=0)
    e_base = e_end - padded_e
    glob = e_base[None, :] + (upto_t[:, :, None] * nchunk[None, :, :]).sum(axis=1) - nchunk
    j = jnp.arange(MOE_NCH, dtype=jnp.int32)
    e_of_j = (ends[:, None, :] <= j[None, :, None]).sum(axis=-1)
    seg = e_of_j[:, :, None] == ei[None, None, :]
    dst = jnp.where(seg, (glob - loc)[:, None, :], 0).sum(axis=-1) + j[None, :]
    dst = jnp.where(j[None, :] < nch[:, None], dst, 0)
    n_used = padded_e.sum() // MOE_CPW
    w = jnp.minimum(jnp.arange(n_work, dtype=jnp.int32), n_used - 1)
    tile_expert = jnp.minimum(((e_end // MOE_CPW)[None, :] <= w[:, None]).sum(axis=-1), N_EXPERTS - 1)
    loc_rows = jnp.pad((loc * MOE_CHUNK).astype(F32),
                       ((0, 0), (ROUTE_SEL_LANE, LANES - ROUTE_SEL_LANE - N_EXPERTS)))
    loc_rows = jnp.broadcast_to(loc_rows[:, None, :], (n_tiles, HALO, LANES))
    return {"nch": nch.astype(jnp.int32), "dst": dst.reshape(-1).astype(jnp.int32),
            "pad_lo": (e_base + per_e).astype(jnp.int32), "pad_hi": e_end.astype(jnp.int32),
            "tile_expert": tile_expert.astype(jnp.int32), "n_used": n_used.astype(jnp.int32).reshape(1),
            "loc": loc_rows}


def _moe_call(h2, route, counts, x2d, mod, w_up, w_down, layer, *, mod_index):
    n = x2d.shape[0]
    tabs = _route_tables(counts, n)
    xbuf = _dispatch_call(h2, route, tabs, tm=MOE_T, n_rows=_moe_rows(n))
    ybuf = _expert_call(xbuf, tabs, w_up, w_down, layer)
    return _combine_call(ybuf, route, x2d, mod, layer, tabs, tm=MOE_T, mod_index=mod_index)


def _ones_blocks(sizes):
    n = sum(sizes)
    m = np.zeros((n, n), np.float32)
    o = 0
    for s in sizes:
        m[o:o + s, o:o + s] = 1.0
        o += s
    return jnp.asarray(m, BF16)


def _inv_counts_q():
    return jnp.asarray(np.tile(np.concatenate([np.full(64, 1 / 64.0), np.full(64, 1 / 32.0)]), 2)[None, :], F32)


def _rope_tables(rows):
    r = np.repeat(np.arange(rows, dtype=np.float32), GRID_W)
    c = np.tile(np.arange(GRID_W, dtype=np.float32), rows)

    def tab(rot_dim):
        nf = rot_dim // 4
        inv = (np.float32(ROPE_THETA) ** (-np.arange(nf, dtype=np.float32) / np.float32(nf))).astype(np.float32)
        ang = np.concatenate([r[:, None] * inv, c[:, None] * inv], axis=-1).astype(np.float32)
        cos = np.repeat(np.cos(ang), 2, axis=-1)
        sin = np.repeat(np.sin(ang), 2, axis=-1) * np.tile(np.asarray([-1.0, 1.0], np.float32), rot_dim // 2)
        return cos.astype(np.float32), sin.astype(np.float32)

    s = rows * GRID_W
    cc, sc = tab(C_HEAD_DIM)
    cd, sd = tab(D_ROPE)
    one = lambda w: np.ones((s, w), np.float32)
    zero = lambda w: np.zeros((s, w), np.float32)
    tabs = (np.tile(cc, (1, 2)), np.tile(sc, (1, 2)),
            np.concatenate([one(64), cd, one(32)], axis=1), np.concatenate([zero(64), sd, zero(32)], axis=1),
            np.concatenate([cd, one(96)], axis=1), np.concatenate([sd, zero(96)], axis=1))
    return tuple(jnp.asarray(t) for t in tabs)


def _stacked_weights(p):
    (g_attn_norm, w_in, sgu_v_gain, sgu_w, sgu_b, conv_w, gqa_q_gain, gqa_k_gain, mla_qa_gain, mla_w_qb,
     mla_kva_gain, mla_w_kvb, mla_q_gain, mla_k_gain, g_group_out, w_out, g_ffn_norm,
     w_rg, b_rg, w_re, b_re) = p
    depth = w_in.shape[0]

    def heads_0213(a, axis, off):
        sl = lambda lo: lax.slice_in_dim(a, off + lo, off + lo + 64, axis=axis)
        return [sl(0), sl(128), sl(64), sl(192)]

    zeros = lambda *shape: jnp.zeros((depth,) + shape, F32)
    row = lambda a: a[:, None, :]
    w_in_p = jnp.concatenate([w_in[:, :, :1280]] + heads_0213(w_in, 2, 1280) +
                             [w_in[:, :, 1536:2208], zeros(D_MODEL, IN_W - 2208)], axis=2).astype(BF16)
    wqb = jnp.concatenate([jnp.concatenate([mla_w_qb[:, :, h * 96:h * 96 + 96], zeros(D_Q_LORA, 32)], axis=2)
                           for h in range(D_HEADS)], axis=2)
    wk = jnp.concatenate([jnp.concatenate([mla_w_kvb[:, :, h * 128:h * 128 + 64], zeros(D_KV_LORA, 64)], axis=2)
                          for h in range(D_HEADS)], axis=2)
    wv = jnp.concatenate([mla_w_kvb[:, :, h * 128 + 64:h * 128 + 128] for h in range(D_HEADS)], axis=2)
    qslab = jnp.concatenate([mla_q_gain, zeros(32)], axis=1)
    kslab = jnp.concatenate([mla_k_gain[:, :64], zeros(64)], axis=1)
    g_go = jnp.concatenate([g_group_out[:, :512]] + heads_0213(g_group_out, 1, 512) + [g_group_out[:, 768:]],
                           axis=1)
    w_out_p = jnp.concatenate([w_out[:, :512]] + heads_0213(w_out, 1, 512) + [w_out[:, 768:]],
                              axis=1).astype(BF16)
    w_r = jnp.concatenate([w_re, w_rg], axis=2)
    w_r_hi = w_r.astype(BF16)
    w_r_lo = (w_r - w_r_hi.astype(F32)).astype(BF16)
    n_r = N_EXPERTS + N_GROUPS
    zb = lambda w: jnp.zeros((depth, D_MODEL, w), BF16)
    w_router = jnp.concatenate([w_r_hi, zb(ROUTE_SEL_LANE - n_r), w_r_lo, zb(LANES - ROUTE_SEL_LANE - n_r)], axis=2)
    w_router = jnp.swapaxes(w_router, 1, 2)
    b_router = jnp.concatenate([b_re, b_rg, zeros(ROUTE_SEL_LANE - n_r)], axis=1)
    b_router = jnp.broadcast_to(b_router[:, :, None], (depth, ROUTE_SEL_LANE, MERGE_ROWS))
    sgub_full = jnp.repeat(jnp.swapaxes(sgu_b, 1, 2), 64, axis=2)
    return {
        "g_attn": row(g_attn_norm), "w_in": w_in_p,
        "sgu_w": sgu_w.reshape(depth, A_HEADS * CHUNK, CHUNK).astype(BF16), "sgu_b": sgub_full,
        "v_gain": row(sgu_v_gain), "conv_w": conv_w, "g_go": row(g_go),
        "cq_gain": row(jnp.tile(gqa_q_gain, (1, 4))), "ck_gain": row(jnp.tile(gqa_k_gain, (1, 2))),
        "qa_gain": row(mla_qa_gain), "w_qb": wqb.astype(BF16), "kva_gain": row(mla_kva_gain),
        "w_kvb": jnp.concatenate([wk, wv], axis=2).astype(BF16),
        "dq_gain": row(jnp.tile(qslab, (1, 4))), "dk_gain": row(jnp.tile(kslab, (1, 4))),
        "dkr_gain": row(jnp.concatenate([mla_k_gain[:, 64:], zeros(96)], axis=1)),
        "w_out": w_out_p, "g_ffn": row(g_ffn_norm), "w_router": w_router, "b_router": b_router,
    }


def kernel(x, c, ctx, c_ctx, w_mod, b_mod, g_attn_norm, w_in, sgu_v_gain, sgu_w, sgu_b, conv_w, gqa_q_gain,
           gqa_k_gain, mla_qa_gain, mla_w_qb, mla_kva_gain, mla_w_kvb, mla_q_gain, mla_k_gain, g_group_out,
           w_out, g_ffn_norm, w_router_group, b_router_group, w_router_expert, b_router_expert,
           w_expert_up, w_expert_down):
    b, s, d = x.shape
    lc = ctx.shape[1]
    depth = w_mod.shape[0]
    lw = _stacked_weights((g_attn_norm, w_in, sgu_v_gain, sgu_w, sgu_b, conv_w, gqa_q_gain, gqa_k_gain,
                           mla_qa_gain, mla_w_qb, mla_kva_gain, mla_w_kvb, mla_q_gain, mla_k_gain, g_group_out,
                           w_out, g_ffn_norm, w_router_group, b_router_group, w_router_expert, b_router_expert))
    w_up = w_expert_up.reshape(depth * N_EXPERTS, d, 2 * D_EXPERT)
    w_down = w_expert_down.reshape(depth * N_EXPERTS, D_EXPERT, d)

    c_all = jnp.concatenate([c, c_ctx[None, :], jnp.zeros((16 - b - 1, d), F32)], axis=0)
    mod = _modulation(c_all, w_mod, b_mod).reshape(depth, 16, 6, d)
    rope = _rope_tables(s // GRID_W)

    tm = MOE_T
    tq = 512
    tpb = s // tm
    lat_idx = lambda i: i // tpb
    ctx_idx = lambda i: b
    r3 = lambda a, n: a.reshape(b, n, a.shape[-1])

    xl = x.reshape(b * s, d)
    xc = ctx.reshape(b * lc, d)
    for l in range(depth):
        update_ctx = l < depth - 1
        ya_c, yb_c, qc_c, kc_c, vc_c, qd_c, kd_c, vd_c = _in_call(
            xc, mod, lw, l, None, tm=tm, seq_len=lc, mod_index=ctx_idx)
        ya, yb, qc, kc, vc, qd, kd, vd = _in_call(xl, mod, lw, l, rope, tm=IN_T, seq_len=s,
                                                  mod_index=lambda i: i // (s // IN_T))
        yc = _attn_call(r3(qc, s), [r3(kc_c, lc), r3(kc, s)], [r3(vc_c, lc), r3(vc, s)], lw["g_go"], l,
                        mode="C", tq=tq)
        yd = _attn_call(r3(qd, s), [r3(kd_c, lc), r3(kd, s)], [r3(vd_c, lc), r3(vd, s)], lw["g_go"], l,
                        mode="D", tq=tq)
        xl, h2, route, counts = _merge_call(xl, ya, yb, yc.reshape(b * s, GROUP_W), yd.reshape(b * s, GROUP_W),
                                            mod, lw, l, tm=tm, mod_index=lat_idx)
        xl = _moe_call(h2, route, counts, xl, mod, w_up, w_down, l, mod_index=lat_idx)
        if update_ctx:
            yc_c = _attn_call(r3(qc_c, lc), [r3(kc_c, lc)], [r3(vc_c, lc)], lw["g_go"], l, mode="C", tq=lc)
            yd_c = _attn_call(r3(qd_c, lc), [r3(kd_c, lc)], [r3(vd_c, lc)], lw["g_go"], l, mode="D", tq=lc)
            xc, h2c, route_c, counts_c = _merge_call(xc, ya_c, yb_c, yc_c.reshape(b * lc, GROUP_W),
                                                     yd_c.reshape(b * lc, GROUP_W), mod, lw, l, tm=tm,
                                                     mod_index=ctx_idx)
            xc = _moe_call(h2c, route_c, counts_c, xc, mod, w_up, w_down, l, mod_index=ctx_idx)
    return xl.reshape(b, s, d)
```

```python
import functools

import numpy as np
import jax
import jax.numpy as jnp
from jax import lax
from jax.experimental import pallas as pl
from jax.experimental.pallas import tpu as pltpu

F32 = jnp.float32
BF16 = jnp.bfloat16

D_MODEL = 1024
GRID_W = 64
EPS = 1e-6
ROPE_THETA = 10000.0
GROUP_W = 256
CHUNK = 128
A_HEADS = 4
C_HEAD_DIM = 64
C_SCALE = C_HEAD_DIM ** -0.5
D_HEADS = 4
D_NOPE = 64
D_ROPE = 32
D_V = 64
D_Q_LORA = 256
D_KV_LORA = 128
D_SCALE = (D_NOPE + D_ROPE) ** -0.5
LOG2E = 1.4426950408889634
ATTN_ROWS = 16
MERGE_ROWS = 128
N_GROUPS = 4
EXPERTS_PER_GROUP = 4
N_EXPERTS = 16
D_EXPERT = 256
LANES = 128
HALO = 8

IN_W = 2304
IN_T = 512
IN_SECTIONS = ((0, 256), (256, 512), (512, 768), (768, 1024), (1024, 1280), (1280, 1536), (1536, 1664),
               (1664, 1792), (1792, 2048), (2048, 2176), (2176, 2304))

ROUTE_SEL_LANE = 32
MOE_T = 512
MOE_CHUNK = 16
MOE_RLOC = 1280
MOE_NCH = MOE_RLOC // MOE_CHUNK
MOE_RBLK = 256
MOE_XW = D_MODEL + LANES
MOE_WORK = 1024
MOE_CPW = MOE_WORK // MOE_CHUNK
VMEM_LIMIT = 56 * 1024 * 1024


def _nt_dot(a, b):
    return lax.dot_general(a, b, (((1,), (1,)), ((), ())), preferred_element_type=F32)


def _dot(a, b):
    return jnp.dot(a, b, preferred_element_type=F32)


def _rms_full(x, gain):
    return x * lax.rsqrt(jnp.mean(x * x, axis=-1, keepdims=True) + EPS) * gain


def _group_ms(x, ones_blocks, inv_count):
    return _dot((x * x).astype(BF16), ones_blocks) * inv_count


def _pair_swap(x, even):
    nxt = pltpu.roll(x, x.shape[1] - 1, axis=1)
    prv = pltpu.roll(x, 1, axis=1)
    return jnp.where(even, nxt, prv)


def _rope(x, cos, sin, even):
    return x * cos + _pair_swap(x, even) * sin


def _mod_kernel(c_ref, w_ref, b_ref, o_ref):
    cv = c_ref[...]
    sc = cv * jax.nn.sigmoid(cv)
    o_ref[0] = jnp.dot(sc, w_ref[0], preferred_element_type=F32,
                       precision=lax.Precision.HIGHEST) + b_ref[0]


def _modulation(c_all, w_mod, b_mod):
    depth = w_mod.shape[0]
    nblk = 4
    wblk = 6 * D_MODEL // nblk
    return pl.pallas_call(
        _mod_kernel,
        grid=(depth, nblk),
        in_specs=[pl.BlockSpec((16, D_MODEL), lambda l, j: (0, 0)),
                  pl.BlockSpec((1, D_MODEL, wblk), lambda l, j: (l, 0, j)),
                  pl.BlockSpec((1, 1, wblk), lambda l, j: (l, 0, j))],
        out_specs=pl.BlockSpec((1, 16, wblk), lambda l, j: (l, 0, j)),
        out_shape=jax.ShapeDtypeStruct((depth, 16, 6 * D_MODEL), F32),
        compiler_params=pltpu.CompilerParams(dimension_semantics=("arbitrary", "arbitrary"),
                                             vmem_limit_bytes=VMEM_LIMIT),
        name="mod",
    )(c_all, w_mod, b_mod.reshape(depth, 1, 6 * D_MODEL))


def _in_kernel(*refs, tm, seq_len, use_rope):
    n_io = len(refs) - 4
    pj0_ref, pj1_ref, zh0_ref, zh1_ref = refs[n_io:]
    i = pl.program_id(0)

    @pl.when(i == 0)
    def _():
        pj1_ref[...] = jnp.zeros(pj1_ref.shape, F32)
        zh1_ref[...] = jnp.zeros(zh1_ref.shape, F32)

    @pl.when(i % 2 == 0)
    def _():
        _in_stages(refs[:n_io], pj0_ref, zh0_ref, pj1_ref, zh1_ref, tm=tm, seq_len=seq_len, use_rope=use_rope)

    @pl.when(i % 2 == 1)
    def _():
        _in_stages(refs[:n_io], pj1_ref, zh1_ref, pj0_ref, zh0_ref, tm=tm, seq_len=seq_len, use_rope=use_rope)


def _in_stages(refs, pj_w, zh_w, pj_r, zh_r, *, tm, seq_len, use_rope):
    (x_ref, xp_ref, xn_ref, mod_ref, gattn_ref, win_ref, sguw_ref, sgub_ref, vgain_ref, convw_ref,
     ggo_ref, cqg_ref, ckg_ref, qag_ref, wqb_ref, kvag_ref, wkvb_ref, dqg_ref, dkg_ref, dkrg_ref,
     b64_ref, bq_ref, invq_ref) = refs[:23]
    pos = 23
    if use_rope:
        cosc_ref, sinc_ref, cosq_ref, sinq_ref, cosk_ref, sink_ref = refs[pos:pos + 6]
        pos += 6
    ya_ref, yb_ref, qc_ref, kc_ref, vc_ref, qd_ref, kd_ref, vd_ref = refs[pos:pos + 8]

    shift = mod_ref[0, 0:1, :]
    scale = mod_ref[0, 1:2, :]
    gattn = gattn_ref[...]

    def prenorm(xv):
        return (_rms_full(xv, gattn) * (1.0 + scale) + shift).astype(BF16)

    hb = prenorm(x_ref[...])

    pending = list(IN_SECTIONS)

    def project(count):
        for _ in range(min(count, len(pending))):
            lo, hi = pending.pop(0)
            pj_w[:, lo:hi] = _dot(hb, win_ref[:, lo:hi])

    project(1)
    hh = prenorm(jnp.concatenate([xp_ref[...], xn_ref[...]], axis=0))
    zh_w[...] = _dot(hh, win_ref[:, 768:1024]) * _dot(hh, win_ref[:, 1024:1280])

    i = jnp.maximum(pl.program_id(0) - 1, 0)
    proj = {ab: pj_r[:, ab[0]:ab[1]] for ab in IN_SECTIONS}
    zh = zh_r[...]
    b64 = b64_ref[...]
    inv64 = 1.0 / 64.0
    lane128 = lax.broadcasted_iota(jnp.int32, (tm, LANES), 1)
    even = (lane128 & 1) == 0

    u = jax.nn.gelu(proj[(0, 256)])
    v = jax.nn.gelu(proj[(256, 512)])
    project(1)
    v = (v * lax.rsqrt(_group_ms(v, b64, inv64) + EPS) * vgain_ref[...]).astype(BF16)
    project(1)
    lane_head = lax.broadcasted_iota(jnp.int32, (CHUNK, GROUP_W), 1) // 64
    sguw = sguw_ref[...]
    mixed_chunks = []
    for ci in range(tm // CHUNK):
        r = _dot(sguw, v[ci * CHUNK:(ci + 1) * CHUNK, :])
        m = r[0:CHUNK]
        for h in range(1, A_HEADS):
            m = jnp.where(lane_head == h, r[h * CHUNK:(h + 1) * CHUNK], m)
        mixed_chunks.append(m + sgub_ref[...])
    ya = u * jnp.concatenate(mixed_chunks, axis=0)
    ya_ref[...] = _rms_full(ya, ggo_ref[:, 0:256]).astype(BF16)

    project(1)
    bg = proj[(512, 768)]
    z = proj[(768, 1024)] * proj[(1024, 1280)]
    row = lax.broadcasted_iota(jnp.int32, (tm, GROUP_W), 0)
    posn = (i * tm + row) % seq_len
    z_prev = jnp.where(row == 0, zh[HALO - 1:HALO, :], pltpu.roll(z, 1, axis=0))
    z_prev = jnp.where(posn == 0, 0.0, z_prev)
    z_next = jnp.where(row == tm - 1, zh[HALO:HALO + 1, :], pltpu.roll(z, tm - 1, axis=0))
    z_next = jnp.where(posn == seq_len - 1, 0.0, z_next)
    yb = bg * (convw_ref[0:1, :] * z_prev + convw_ref[1:2, :] * z + convw_ref[2:3, :] * z_next)
    yb_ref[...] = _rms_full(yb, ggo_ref[:, 256:512]).astype(BF16)

    project(1)
    q = proj[(1280, 1536)]
    q = q * lax.rsqrt(_group_ms(q, b64, inv64) + EPS) * cqg_ref[...]
    project(1)
    k = proj[(1536, 1664)]
    k = k * lax.rsqrt(_group_ms(k, b64[0:128, 0:128], inv64) + EPS) * ckg_ref[...]
    if use_rope:
        cosc, sinc = cosc_ref[...], sinc_ref[...]
        q = jnp.concatenate([_rope(q[:, 0:128], cosc, sinc, even),
                             _rope(q[:, 128:256], cosc, sinc, even)], axis=1)
        k = _rope(k, cosc, sinc, even)
    qc_ref[...] = (q * (C_SCALE * LOG2E)).astype(BF16)
    kc_ref[...] = k.astype(BF16)
    vc_ref[...] = proj[(1664, 1792)].astype(BF16)

    project(1)
    cq = _rms_full(proj[(1792, 2048)], qag_ref[...]).astype(BF16)
    project(1)
    qd = _dot(cq, wqb_ref[...])
    bq = bq_ref[...]
    invq = invq_ref[...]
    halves = []
    for hf in range(2):
        qh = qd[:, hf * 256:(hf + 1) * 256]
        qh = qh * lax.rsqrt(_group_ms(qh, bq, invq) + EPS) * dqg_ref[:, hf * 256:(hf + 1) * 256]
        if use_rope:
            cosq, sinq = cosq_ref[...], sinq_ref[...]
            qh = jnp.concatenate([_rope(qh[:, 0:128], cosq, sinq, even),
                                  _rope(qh[:, 128:256], cosq, sinq, even)], axis=1)
        halves.append(qh)
    qd_ref[...] = (jnp.concatenate(halves, axis=1) * (D_SCALE * LOG2E)).astype(BF16)

    project(1)
    ckv = _rms_full(proj[(2048, 2176)], kvag_ref[...]).astype(BF16)
    project(1)
    kv = _dot(ckv, wkvb_ref[...])
    vd_ref[...] = kv[:, 512:768].astype(BF16)
    project(len(pending))
    kr = proj[(2176, 2304)]
    kr = kr * lax.rsqrt(jnp.sum(kr * kr, axis=-1, keepdims=True) * (1.0 / D_ROPE) + EPS) * dkrg_ref[...]
    if use_rope:
        kr = _rope(kr, cosk_ref[...], sink_ref[...], even)
    kr = pltpu.roll(kr, D_NOPE, axis=1)
    for hf in range(2):
        kh = kv[:, hf * 256:(hf + 1) * 256]
        kh = kh * lax.rsqrt(_group_ms(kh, b64, inv64) + EPS) * dkg_ref[:, hf * 256:(hf + 1) * 256]
        kd_ref[:, hf * 256:hf * 256 + 128] = (kh[:, 0:128] + kr).astype(BF16)
        kd_ref[:, hf * 256 + 128:(hf + 1) * 256] = (kh[:, 128:256] + kr).astype(BF16)


def _layer_spec(a, layer):
    zeros = (0,) * (a.ndim - 1)
    return pl.BlockSpec((None,) + a.shape[1:], lambda *_: (layer,) + zeros)


def _mod_spec(layer, mod_index):
    return pl.BlockSpec((None, 1, 6, D_MODEL), lambda i, *_: (layer, mod_index(i), 0, 0))


def _in_call(x2d, mod, lw, layer, rope, *, tm, seq_len, mod_index):
    n = x2d.shape[0]
    nt = n // tm
    nb8 = n // HALO
    tb = tm // HALO
    use_rope = rope is not None
    cur = lambda i: jnp.minimum(i, nt - 1)
    prev = lambda i: jnp.maximum(i - 1, 0)
    row = lambda i: (prev(i), 0)
    in_specs = [
        pl.BlockSpec((tm, D_MODEL), lambda i: (cur(i), 0)),
        pl.BlockSpec((HALO, D_MODEL), lambda i: (jnp.maximum(cur(i) * tb - 1, 0), 0)),
        pl.BlockSpec((HALO, D_MODEL), lambda i: (jnp.minimum((cur(i) + 1) * tb, nb8 - 1), 0)),
        _mod_spec(layer, lambda i: mod_index(cur(i))),
    ]
    args = [x2d, x2d, x2d, mod]
    for name in ("g_attn", "w_in", "sgu_w", "sgu_b", "v_gain", "conv_w", "g_go", "cq_gain", "ck_gain",
                 "qa_gain", "w_qb", "kva_gain", "w_kvb", "dq_gain", "dk_gain", "dkr_gain"):
        in_specs.append(_layer_spec(lw[name], layer))
        args.append(lw[name])
    for a in (_ones_blocks([64] * 4), _ones_blocks([64, 32, 32] * 2), _inv_counts_q()):
        in_specs.append(pl.BlockSpec(a.shape, lambda i: (0, 0)))
        args.append(a)
    if use_rope:
        ntab = seq_len // tm
        for a in rope:
            in_specs.append(pl.BlockSpec((tm, LANES), lambda i: (prev(i) % ntab, 0)))
            args.append(a)
    widths = (256, 256, 256, 128, 128, 512, 512, 256)
    out_specs = [pl.BlockSpec((tm, w), row) for w in widths]
    out_shape = [jax.ShapeDtypeStruct((n, w), BF16) for w in widths]
    return pl.pallas_call(
        functools.partial(_in_kernel, tm=tm, seq_len=seq_len, use_rope=use_rope),
        grid=(nt + 1,),
        in_specs=in_specs,
        out_specs=out_specs,
        out_shape=out_shape,
        scratch_shapes=[pltpu.VMEM((tm, IN_W), F32), pltpu.VMEM((tm, IN_W), F32),
                        pltpu.VMEM((2 * HALO, GROUP_W), F32), pltpu.VMEM((2 * HALO, GROUP_W), F32)],
        compiler_params=pltpu.CompilerParams(dimension_semantics=("arbitrary",),
                                             vmem_limit_bytes=VMEM_LIMIT),
        name="in_proj",
    )(*args)


def _attn_kernel(*refs, mode, nseg, tq):
    q_ref = refs[0]
    k_refs = refs[1:1 + nseg]
    v_refs = refs[1 + nseg:1 + 2 * nseg]
    gain_ref = refs[1 + 2 * nseg]
    o_ref = refs[2 + 2 * nseg]
    p_ref = refs[3 + 2 * nseg]
    nks = [r.shape[1] for r in k_refs]
    offs = [sum(nks[:j]) for j in range(nseg)]
    n_chunks = tq // ATTN_ROWS

    vs = [r[0] for r in v_refs]
    if mode == "C":
        ks_c = [r[0] for r in k_refs]
        first_half = lax.broadcasted_iota(jnp.int32, (tq, LANES), 1) < 64
        heads = []
        for sl in range(2):
            qs = q_ref[0, :, sl * 128:(sl + 1) * 128]
            heads.append((jnp.where(first_half, qs, jnp.zeros_like(qs)), ks_c))
            heads.append((jnp.where(first_half, jnp.zeros_like(qs), qs), ks_c))
    else:
        heads = [(q_ref[0, :, h * 128:(h + 1) * 128], [r[0, :, h * 128:(h + 1) * 128] for r in k_refs])
                 for h in range(D_HEADS)]

    def score(h):
        qm, ks = heads[h]
        return [_nt_dot(qm, kk) for kk in ks]

    def softmax(h, ss):
        for c in range(n_chunks):
            rows = slice(c * ATTN_ROWS, (c + 1) * ATTN_ROWS)
            m = ss[0][rows].max(axis=-1, keepdims=True)
            for s in ss[1:]:
                m = jnp.maximum(m, s[rows].max(axis=-1, keepdims=True))
            for s, off, nk in zip(ss, offs, nks):
                p_ref[h, rows, off:off + nk] = jnp.exp2(s[rows] - m).astype(BF16)

    if mode == "C":
        ones_lane = [LANES] * 4
        one_col = jnp.where(lax.broadcasted_iota(jnp.int32, (1, LANES), 1) == 0, 1.0, 0.0).astype(BF16)
        v_ext = [[jnp.concatenate([vv, jnp.broadcast_to(one_col, vv.shape)], axis=1) for vv in vs]] * 4
    else:
        ones_lane = [((h + 1) % D_HEADS) * D_V for h in range(D_HEADS)]
        v_ext = []
        lane_v = lax.broadcasted_iota(jnp.int32, (1, GROUP_W), 1)
        for h in range(D_HEADS):
            one = jnp.where(lane_v == ones_lane[h], 1.0, 0.0).astype(BF16)
            keep = jnp.where(lane_v == ones_lane[h], 0.0, 1.0).astype(BF16)
            v_ext.append([vv * keep + one for vv in vs])

    def values(h):
        o = None
        for off, nk, vv in zip(offs, nks, v_ext[h]):
            part = _dot(p_ref[h, :, off:off + nk], vv)
            o = part if o is None else o + part
        out_w = LANES if mode == "C" else GROUP_W
        return o[:, 0:out_w] / o[:, ones_lane[h]:ones_lane[h] + 1]

    n_heads = len(heads)
    scores = {0: score(0), 1: score(1)}
    outs = []
    for h in range(n_heads):
        softmax(h, scores.pop(h))
        if h + 2 < n_heads:
            scores[h + 2] = score(h + 2)
        if h >= 1:
            outs.append(values(h - 1))
    outs.append(values(n_heads - 1))

    if mode == "C":
        out = jnp.concatenate([jnp.where(first_half, outs[0], outs[1]),
                               jnp.where(first_half, outs[2], outs[3])], axis=1)
    else:
        lane_head = lax.broadcasted_iota(jnp.int32, (tq, GROUP_W), 1) // D_V
        out = outs[0]
        for h in range(1, D_HEADS):
            out = jnp.where(lane_head == h, outs[h], out)
    o_ref[0] = _rms_full(out, gain_ref[...]).astype(BF16)


def _attn_call(q, ks, vs, g_go, layer, *, mode, tq):
    b, s, wq = q.shape
    nseg = len(ks)
    gain_block = 2 if mode == "C" else 3
    in_specs = [pl.BlockSpec((1, tq, wq), lambda bi, qi: (bi, qi, 0))]
    for a in list(ks) + list(vs):
        in_specs.append(pl.BlockSpec((1,) + a.shape[1:], lambda bi, qi: (bi, 0, 0)))
    in_specs.append(pl.BlockSpec((None, 1, GROUP_W), lambda bi, qi: (layer, 0, gain_block)))
    return pl.pallas_call(
        functools.partial(_attn_kernel, mode=mode, nseg=nseg, tq=tq),
        grid=(b, s // tq),
        in_specs=in_specs,
        out_specs=pl.BlockSpec((1, tq, GROUP_W), lambda bi, qi: (bi, qi, 0)),
        out_shape=jax.ShapeDtypeStruct((b, s, GROUP_W), BF16),
        scratch_shapes=[pltpu.VMEM((4, tq, sum(a.shape[1] for a in ks)), BF16)],
        compiler_params=pltpu.CompilerParams(dimension_semantics=("parallel", "parallel"),
                                             vmem_limit_bytes=VMEM_LIMIT),
        name="attn_" + mode,
    )(q, *ks, *vs, g_go)


def _merge_kernel(x_ref, ya_ref, yb_ref, yc_ref, yd_ref, mod_ref, wout_ref, gffn_ref, wr_ref, br_ref,
                  xo_ref, h2_ref, route_ref, cnt_ref, *, tm):
    nb = tm // MERGE_ROWS
    blocks = [slice(j * MERGE_ROWS, (j + 1) * MERGE_ROWS) for j in range(nb)]
    ys = []
    for rs in blocks:
        y = _dot(ya_ref[rs, :], wout_ref[0:256, :])
        y = y + _dot(yb_ref[rs, :], wout_ref[256:512, :])
        y = y + _dot(yc_ref[rs, :], wout_ref[512:768, :])
        y = y + _dot(yd_ref[rs, :], wout_ref[768:1024, :])
        ys.append(y)
    counts = None
    for rs, y in zip(blocks, ys):
        sel = _merge_block(rs, y, x_ref, mod_ref, gffn_ref, wr_ref, br_ref, xo_ref, h2_ref, route_ref)
        part = sel.sum(axis=0, keepdims=True)
        counts = part if counts is None else counts + part
    cnt_ref[0] = jnp.broadcast_to(counts, (HALO, LANES))


def _merge_block(rs, y, x_ref, mod_ref, gffn_ref, wr_ref, br_ref, xo_ref, h2_ref, route_ref):
    rows = rs.stop - rs.start
    xn = x_ref[rs, :] + mod_ref[0, 2:3, :] * y
    xo_ref[rs, :] = xn
    h2 = _rms_full(xn, gffn_ref[...]) * (1.0 + mod_ref[0, 4:5, :]) + mod_ref[0, 3:4, :]
    h2b = h2.astype(BF16)
    h2_ref[rs, :] = h2b

    h2lo = (h2 - h2b.astype(F32)).astype(BF16)
    part = _nt_dot(wr_ref[...], h2b) + _nt_dot(wr_ref[...], h2lo)
    logit = part[0:ROUTE_SEL_LANE] + part[ROUTE_SEL_LANE:2 * ROUTE_SEL_LANE] + br_ref[...]
    row_of = lambda j: logit[j:j + 1, :]
    lg = [row_of(N_EXPERTS + g) for g in range(N_GROUPS)]

    def first_max(vals):
        m = vals[0]
        for v in vals[1:]:
            m = jnp.maximum(m, v)
        idx = jnp.float32(len(vals) - 1)
        for j in range(len(vals) - 2, -1, -1):
            idx = jnp.where(vals[j] == m, jnp.float32(j), idx)
        return m, idx

    gmax, g_idx = first_max(lg)
    g_w = 1.0 / sum(jnp.exp(v - gmax) for v in lg)
    cand = []
    for j in range(EXPERTS_PER_GROUP):
        c = row_of((N_GROUPS - 1) * EXPERTS_PER_GROUP + j)
        for g in range(N_GROUPS - 2, -1, -1):
            c = jnp.where(g_idx == g, row_of(g * EXPERTS_PER_GROUP + j), c)
        cand.append(c)
    m1, i1 = first_max(cand)
    neg = jnp.float32(-jnp.inf)
    m2, i2 = first_max([jnp.where(i1 == j, neg, c) for j, c in enumerate(cand)])
    r = jnp.exp(m2 - m1)
    w1 = g_w / (1.0 + r)
    w2 = g_w * r / (1.0 + r)
    e1 = g_idx * EXPERTS_PER_GROUP + i1
    e2 = g_idx * EXPERTS_PER_GROUP + i2
    eio = lax.broadcasted_iota(jnp.int32, (N_EXPERTS, rows), 0).astype(F32)
    cmb_t = jnp.where(eio == e1, w1, jnp.where(eio == e2, w2, 0.0))
    sel_t = jnp.where((eio == e1) | (eio == e2), 1.0, 0.0)
    zeros = lambda n: jnp.zeros((n, rows), F32)
    route_t = jnp.concatenate([cmb_t, zeros(ROUTE_SEL_LANE - N_EXPERTS), sel_t,
                               zeros(LANES - ROUTE_SEL_LANE - N_EXPERTS)], axis=0)
    route = route_t.T
    route_ref[rs, :] = route
    lane = lax.broadcasted_iota(jnp.int32, (rows, LANES), 1)
    return jnp.where(lane >= ROUTE_SEL_LANE, route, 0.0)


def _merge_call(x2d, ya, yb, yc, yd, mod, lw, layer, *, tm, mod_index):
    n = x2d.shape[0]
    row = lambda i: (i, 0)
    in_specs = [pl.BlockSpec((tm, D_MODEL), row)]
    in_specs += [pl.BlockSpec((tm, GROUP_W), row)] * 4
    in_specs += [_mod_spec(layer, mod_index)]
    in_specs += [_layer_spec(lw[k], layer) for k in ("w_out", "g_ffn", "w_router", "b_router")]
    return pl.pallas_call(
        functools.partial(_merge_kernel, tm=tm),
        grid=(n // tm,),
        in_specs=in_specs,
        out_specs=[pl.BlockSpec((tm, D_MODEL), row), pl.BlockSpec((tm, D_MODEL), row),
                   pl.BlockSpec((tm, LANES), row), pl.BlockSpec((1, HALO, LANES), lambda i: (i, 0, 0))],
        out_shape=[jax.ShapeDtypeStruct((n, D_MODEL), F32), jax.ShapeDtypeStruct((n, D_MODEL), BF16),
                   jax.ShapeDtypeStruct((n, LANES), F32),
                   jax.ShapeDtypeStruct((n // tm, HALO, LANES), F32)],
        compiler_params=pltpu.CompilerParams(dimension_semantics=("parallel",),
                                             vmem_limit_bytes=VMEM_LIMIT),
        name="merge",
    )(x2d, ya, yb, yc, yd, mod, lw["w_out"], lw["g_ffn"], lw["w_router"], lw["b_router"])


def _segment_positions(route, loc, tm):
    lane = lax.broadcasted_iota(jnp.int32, (tm, LANES), 1)
    is_sel = (lane >= ROUTE_SEL_LANE) & (lane < ROUTE_SEL_LANE + N_EXPERTS)
    selm = is_sel & (route > 0.5)
    selb = jnp.where(selm, 1.0, 0.0).astype(BF16)
    r_i = lax.broadcasted_iota(jnp.int32, (tm, tm), 0)
    c_i = lax.broadcasted_iota(jnp.int32, (tm, tm), 1)
    earlier = jnp.where(c_i < r_i, 1.0, 0.0).astype(BF16)
    dest = loc + _dot(earlier, selb)
    big = jnp.float32(1e9)
    d_a = jnp.where(selm, dest, big).min(axis=-1, keepdims=True)
    d_b = jnp.where(selm, dest, -big).max(axis=-1, keepdims=True)
    return selm, dest, d_a, d_b


def _to_row(col, tm):
    r_i = lax.broadcasted_iota(jnp.int32, (tm, tm), 0)
    c_i = lax.broadcasted_iota(jnp.int32, (tm, tm), 1)
    return jnp.where(r_i == c_i, col, 0.0).sum(axis=0, keepdims=True)


def _chunk_copy_out(xs_ref, xbuf_ref, sem_ref, slot, src_chunk, dst_chunk):
    return pltpu.make_async_copy(xs_ref.at[slot, src_chunk], xbuf_ref.at[dst_chunk], sem_ref.at[slot])


def _dispatch_kernel(nch_ref, dst_ref, plo_ref, phi_ref, nused_ref, h2_ref, route_ref, loc_ref, xbuf_ref,
                     xs_ref, sem_ref, *, tm, n_tiles, n_work):
    i = pl.program_id(0)
    slot = i % 2
    zero_chunk = MOE_RLOC // MOE_CHUNK

    def start_tile(t_idx, slot_):
        def body(j, carry):
            _chunk_copy_out(xs_ref, xbuf_ref, sem_ref, slot_, j, dst_ref[t_idx * MOE_NCH + j]).start()
            return carry
        lax.fori_loop(0, nch_ref[t_idx], body, 0)

    def wait_tile(t_idx, slot_):
        def body(j, carry):
            _chunk_copy_out(xs_ref, xbuf_ref, sem_ref, slot_, 0, 0).wait()
            return carry
        lax.fori_loop(0, nch_ref[t_idx], body, 0)

    @pl.when(i == 0)
    def _():
        xs_ref[2] = jnp.zeros(xs_ref.shape[1:], BF16)

        def pad_start(j, carry):
            _chunk_copy_out(xs_ref, xbuf_ref, sem_ref, 2, zero_chunk, j).start()
            return carry

        def pad_wait(j, carry):
            _chunk_copy_out(xs_ref, xbuf_ref, sem_ref, 2, zero_chunk, 0).wait()
            return carry
        for e in range(N_EXPERTS):
            lax.fori_loop(plo_ref[e], phi_ref[e], pad_start, 0)
        for e in range(N_EXPERTS):
            lax.fori_loop(plo_ref[e], phi_ref[e], pad_wait, 0)

        def tail_copy(w):
            return pltpu.make_async_copy(xs_ref.at[2, pl.ds(0, MOE_CPW)],
                                         xbuf_ref.at[pl.ds(w * MOE_CPW, MOE_CPW)], sem_ref.at[2])

        def tail_start(w, carry):
            tail_copy(w).start()
            return carry

        def tail_wait(w, carry):
            tail_copy(w).wait()
            return carry
        lax.fori_loop(nused_ref[0], n_work, tail_start, 0)
        lax.fori_loop(nused_ref[0], n_work, tail_wait, 0)

    @pl.when(i >= 2)
    def _():
        wait_tile(i - 2, slot)

    route = route_ref[...]
    selm, dest, d_a, d_b = _segment_positions(route, loc_ref[0, 0:1, :], tm)
    cmb_al = pltpu.roll(route, ROUTE_SEL_LANE, axis=1)
    w_a = jnp.where(selm & (dest == d_a), cmb_al, 0.0).sum(axis=-1, keepdims=True)
    w_b = jnp.where(selm & (dest == d_b), cmb_al, 0.0).sum(axis=-1, keepdims=True)
    da_r, db_r, wa_r, wb_r = (_to_row(v, tm) for v in (d_a, d_b, w_a, w_b))
    h2 = h2_ref[...]
    lane = lax.broadcasted_iota(jnp.int32, (MOE_RBLK, LANES), 1)
    for r0 in range(0, MOE_RLOC, MOE_RBLK):
        rio = (lax.broadcasted_iota(jnp.int32, (MOE_RBLK, tm), 0) + r0).astype(F32)
        hit_a = rio == da_r
        hit_b = rio == db_r
        perm = jnp.where(hit_a | hit_b, 1.0, 0.0).astype(BF16)
        c0, nc = r0 // MOE_CHUNK, MOE_RBLK // MOE_CHUNK
        xs_ref[slot, c0:c0 + nc, :, 0:D_MODEL] = _dot(perm, h2).astype(BF16).reshape(nc, MOE_CHUNK, D_MODEL)
        w_sorted = (jnp.where(hit_a, wa_r, 0.0) + jnp.where(hit_b, wb_r, 0.0)).sum(axis=-1, keepdims=True)
        w_hi = w_sorted.astype(BF16).astype(F32)
        r1 = w_sorted - w_hi
        w_mid = r1.astype(BF16).astype(F32)
        w_lo = r1 - w_mid
        slab = jnp.where(lane == 0, w_hi, jnp.where(lane == 1, w_mid, jnp.where(lane == 2, w_lo, 0.0)))
        xs_ref[slot, c0:c0 + nc, :, D_MODEL:MOE_XW] = slab.astype(BF16).reshape(nc, MOE_CHUNK, LANES)
    start_tile(i, slot)

    @pl.when(i == n_tiles - 1)
    def _():
        if n_tiles > 1:
            wait_tile(i - 1, 1 - slot)
        wait_tile(i, slot)


def _dispatch_call(h2, route, tabs, *, tm, n_rows):
    n = h2.shape[0]
    n_tiles = n // tm
    grid_spec = pltpu.PrefetchScalarGridSpec(
        num_scalar_prefetch=5,
        grid=(n_tiles,),
        in_specs=[pl.BlockSpec((tm, D_MODEL), lambda i, *_: (i, 0)),
                  pl.BlockSpec((tm, LANES), lambda i, *_: (i, 0)),
                  pl.BlockSpec((1, HALO, LANES), lambda i, *_: (i, 0, 0))],
        out_specs=pl.BlockSpec(memory_space=pl.ANY),
        scratch_shapes=[pltpu.VMEM((3, max(MOE_NCH + 1, MOE_CPW), MOE_CHUNK, MOE_XW), BF16),
                        pltpu.SemaphoreType.DMA((3,))],
    )
    xbuf = pl.pallas_call(
        functools.partial(_dispatch_kernel, tm=tm, n_tiles=n_tiles, n_work=n_rows // MOE_WORK),
        grid_spec=grid_spec,
        out_shape=jax.ShapeDtypeStruct((n_rows // MOE_CHUNK, MOE_CHUNK, MOE_XW), BF16),
        compiler_params=pltpu.CompilerParams(dimension_semantics=("arbitrary",),
                                             vmem_limit_bytes=VMEM_LIMIT),
        name="moe_dispatch",
    )(tabs["nch"], tabs["dst"], tabs["pad_lo"], tabs["pad_hi"], tabs["n_used"], h2, route, tabs["loc"])
    return xbuf.reshape(n_rows, MOE_XW)


def _expert_kernel(te_ref, nused_ref, x_ref, wup_ref, wdn_ref, y_ref, wup_bf, wdn_bf):
    w = pl.program_id(0)
    used = w < nused_ref[0]

    @pl.when((w == 0) | (te_ref[w] != te_ref[jnp.maximum(w - 1, 0)]))
    def _():
        wup_bf[...] = wup_ref[...].astype(BF16)
        wdn_bf[...] = wdn_ref[...].astype(BF16)

    @pl.when(used)
    def _():
        wt = x_ref[:, D_MODEL:MOE_XW].astype(F32).sum(axis=-1, keepdims=True)
        gu = _dot(x_ref[:, 0:D_MODEL], wup_bf[...])
        g = gu[:, 0:D_EXPERT]
        a = (g * jax.nn.sigmoid(g)) * gu[:, D_EXPERT:2 * D_EXPERT] * wt
        y_ref[...] = _dot(a.astype(BF16), wdn_bf[...]).astype(BF16)

    @pl.when(jnp.logical_not(used))
    def _():
        y_ref[...] = jnp.zeros(y_ref.shape, BF16)


def _expert_call(xbuf, tabs, w_up, w_down, layer):
    n_rows = xbuf.shape[0]
    n_work = n_rows // MOE_WORK
    base = layer * N_EXPERTS
    grid_spec = pltpu.PrefetchScalarGridSpec(
        num_scalar_prefetch=2,
        grid=(n_work,),
        in_specs=[pl.BlockSpec((MOE_WORK, MOE_XW), lambda w, te, nu: (jnp.minimum(w, nu[0] - 1), 0)),
                  pl.BlockSpec((None, D_MODEL, 2 * D_EXPERT), lambda w, te, nu: (base + te[w], 0, 0)),
                  pl.BlockSpec((None, D_EXPERT, D_MODEL), lambda w, te, nu: (base + te[w], 0, 0))],
        out_specs=pl.BlockSpec((MOE_WORK, D_MODEL), lambda w, te, nu: (w, 0)),
        scratch_shapes=[pltpu.VMEM((D_MODEL, 2 * D_EXPERT), BF16), pltpu.VMEM((D_EXPERT, D_MODEL), BF16)],
    )
    return pl.pallas_call(
        _expert_kernel,
        grid_spec=grid_spec,
        out_shape=jax.ShapeDtypeStruct((n_rows, D_MODEL), BF16),
        compiler_params=pltpu.CompilerParams(dimension_semantics=("arbitrary",),
                                             vmem_limit_bytes=VMEM_LIMIT),
        name="moe_experts",
    )(tabs["tile_expert"], tabs["n_used"], xbuf, w_up, w_down)


def _chunk_copy_in(ybuf_ref, ys_ref, sem_ref, slot, src_chunk, dst_chunk):
    return pltpu.make_async_copy(ybuf_ref.at[src_chunk], ys_ref.at[slot, dst_chunk], sem_ref.at[slot])


def _combine_kernel(nch_ref, dst_ref, route_ref, loc_ref, x_ref, mod_ref, ybuf_ref, o_ref,
                    ys_ref, sem_ref, *, tm, n_tiles):
    i = pl.program_id(0)
    slot = i % 2

    def start_tile(t_idx, slot_):
        def body(j, carry):
            _chunk_copy_in(ybuf_ref, ys_ref, sem_ref, slot_, dst_ref[t_idx * MOE_NCH + j], j).start()
            return carry
        lax.fori_loop(0, nch_ref[t_idx], body, 0)

    @pl.when(i == 0)
    def _():
        ys_ref[...] = jnp.zeros(ys_ref.shape, BF16)
        start_tile(0, 0)

    @pl.when(i + 1 < n_tiles)
    def _():
        start_tile(i + 1, 1 - slot)

    def wait_body(j, carry):
        _chunk_copy_in(ybuf_ref, ys_ref, sem_ref, slot, 0, 0).wait()
        return carry
    lax.fori_loop(0, nch_ref[i], wait_body, 0)

    _, _, d_a, d_b = _segment_positions(route_ref[...], loc_ref[0, 0:1, :], tm)
    y = None
    for r0 in range(0, MOE_RLOC, MOE_RBLK):
        lio = (lax.broadcasted_iota(jnp.int32, (tm, MOE_RBLK), 1) + r0).astype(F32)
        perm_t = jnp.where((lio == d_a) | (lio == d_b), 1.0, 0.0).astype(BF16)
        c0 = r0 // MOE_CHUNK
        part = _dot(perm_t, ys_ref[slot, c0:c0 + MOE_RBLK // MOE_CHUNK].reshape(MOE_RBLK, D_MODEL))
        y = part if y is None else y + part
    o_ref[...] = x_ref[...] + mod_ref[0, 5:6, :] * y


def _combine_call(ybuf, route, x2d, mod, layer, tabs, *, tm, mod_index):
    n = x2d.shape[0]
    n_tiles = n // tm
    grid_spec = pltpu.PrefetchScalarGridSpec(
        num_scalar_prefetch=2,
        grid=(n_tiles,),
        in_specs=[pl.BlockSpec((tm, LANES), lambda i, *_: (i, 0)),
                  pl.BlockSpec((1, HALO, LANES), lambda i, *_: (i, 0, 0)),
                  pl.BlockSpec((tm, D_MODEL), lambda i, *_: (i, 0)),
                  _mod_spec(layer, mod_index),
                  pl.BlockSpec(memory_space=pl.ANY)],
        out_specs=pl.BlockSpec((tm, D_MODEL), lambda i, *_: (i, 0)),
        scratch_shapes=[pltpu.VMEM((2, MOE_NCH, MOE_CHUNK, D_MODEL), BF16),
                        pltpu.SemaphoreType.DMA((2,))],
    )
    return pl.pallas_call(
        functools.partial(_combine_kernel, tm=tm, n_tiles=n_tiles),
        grid_spec=grid_spec,
        out_shape=jax.ShapeDtypeStruct((n, D_MODEL), F32),
        compiler_params=pltpu.CompilerParams(dimension_semantics=("arbitrary",),
                                             vmem_limit_bytes=VMEM_LIMIT),
        name="moe_combine",
    )(tabs["nch"], tabs["dst"], route, tabs["loc"], x2d, mod,
      ybuf.reshape(ybuf.shape[0] // MOE_CHUNK, MOE_CHUNK, D_MODEL))


def _moe_rows(n_tok):
    n_tiles = n_tok // MOE_T
    chunks = 2 * n_tok // MOE_CHUNK + (MOE_CHUNK - 1) * n_tiles + N_EXPERTS * (MOE_CPW - 1)
    return -(-chunks // MOE_CPW) * MOE_WORK


def _route_tables(counts, n_tok):
    n_work = _moe_rows(n_tok) // MOE_WORK
    cnt = counts[:, 0, ROUTE_SEL_LANE:ROUTE_SEL_LANE + N_EXPERTS].astype(jnp.int32)
    n_tiles = cnt.shape[0]
    nchunk = (cnt + MOE_CHUNK - 1) // MOE_CHUNK
    ei = jnp.arange(N_EXPERTS, dtype=jnp.int32)
    ti = jnp.arange(n_tiles, dtype=jnp.int32)
    upto_e = (ei[:, None] <= ei[None, :]).astype(jnp.int32)
    upto_t = (ti[None, :] <= ti[:, None]).astype(jnp.int32)
    ends = (nchunk[:, :, None] * upto_e[None, :, :]).sum(axis=1)
    loc = ends - nchunk
    nch = nchunk.sum(axis=1)
    per_e = nchunk.sum(axis=0)
    padded_e = (per_e + MOE_CPW - 1) // MOE_CPW * MOE_CPW
    e_end = (padded_e[:, None] * upto_e).sum(axis=0)
    e_base = e_end - padded_e
    glob = e_base[None, :] + (upto_t[:, :, None] * nchunk[None, :, :]).sum(axis=1) - nchunk
    j = jnp.arange(MOE_NCH, dtype=jnp.int32)
    e_of_j = (ends[:, None, :] <= j[None, :, None]).sum(axis=-1)
    seg = e_of_j[:, :, None] == ei[None, None, :]
    dst = jnp.where(seg, (glob - loc)[:, None, :], 0).sum(axis=-1) + j[None, :]
    dst = jnp.where(j[None, :] < nch[:, None], dst, 0)
    n_used = padded_e.sum() // MOE_CPW
    w = jnp.minimum(jnp.arange(n_work, dtype=jnp.int32), n_used - 1)
    tile_expert = jnp.minimum(((e_end // MOE_CPW)[None, :] <= w[:, None]).sum(axis=-1), N_EXPERTS - 1)
    loc_rows = jnp.pad((loc * MOE_CHUNK).astype(F32),
                       ((0, 0), (ROUTE_SEL_LANE, LANES - ROUTE_SEL_LANE - N_EXPERTS)))
    loc_rows = jnp.broadcast_to(loc_rows[:, None, :], (n_tiles, HALO, LANES))
    return {"nch": nch.astype(jnp.int32), "dst": dst.reshape(-1).astype(jnp.int32),
            "pad_lo": (e_base + per_e).astype(jnp.int32), "pad_hi": e_end.astype(jnp.int32),
            "tile_expert": tile_expert.astype(jnp.int32), "n_used": n_used.astype(jnp.int32).reshape(1),
            "loc": loc_rows}


def _moe_call(h2, route, counts, x2d, mod, w_up, w_down, layer, *, mod_index):
    n = x2d.shape[0]
    tabs = _route_tables(counts, n)
    xbuf = _dispatch_call(h2, route, tabs, tm=MOE_T, n_rows=_moe_rows(n))
    ybuf = _expert_call(xbuf, tabs, w_up, w_down, layer)
    return _combine_call(ybuf, route, x2d, mod, layer, tabs, tm=MOE_T, mod_index=mod_index)


def _ones_blocks(sizes):
    n = sum(sizes)
    m = np.zeros((n, n), np.float32)
    o = 0
    for s in sizes:
        m[o:o + s, o:o + s] = 1.0
        o += s
    return jnp.asarray(m, BF16)


def _inv_counts_q():
    return jnp.asarray(np.tile(np.concatenate([np.full(64, 1 / 64.0), np.full(64, 1 / 32.0)]), 2)[None, :], F32)


def _rope_tables(rows):
    r = np.repeat(np.arange(rows, dtype=np.float32), GRID_W)
    c = np.tile(np.arange(GRID_W, dtype=np.float32), rows)

    def tab(rot_dim):
        nf = rot_dim // 4
        inv = (np.float32(ROPE_THETA) ** (-np.arange(nf, dtype=np.float32) / np.float32(nf))).astype(np.float32)
        ang = np.concatenate([r[:, None] * inv, c[:, None] * inv], axis=-1).astype(np.float32)
        cos = np.repeat(np.cos(ang), 2, axis=-1)
        sin = np.repeat(np.sin(ang), 2, axis=-1) * np.tile(np.asarray([-1.0, 1.0], np.float32), rot_dim // 2)
        return cos.astype(np.float32), sin.astype(np.float32)

    s = rows * GRID_W
    cc, sc = tab(C_HEAD_DIM)
    cd, sd = tab(D_ROPE)
    one = lambda w: np.ones((s, w), np.float32)
    zero = lambda w: np.zeros((s, w), np.float32)
    tabs = (np.tile(cc, (1, 2)), np.tile(sc, (1, 2)),
            np.concatenate([one(64), cd, one(32)], axis=1), np.concatenate([zero(64), sd, zero(32)], axis=1),
            np.concatenate([cd, one(96)], axis=1), np.concatenate([sd, zero(96)], axis=1))
    return tuple(jnp.asarray(t) for t in tabs)


def _stacked_weights(p):
    (g_attn_norm, w_in, sgu_v_gain, sgu_w, sgu_b, conv_w, gqa_q_gain, gqa_k_gain, mla_qa_gain, mla_w_qb,
     mla_kva_gain, mla_w_kvb, mla_q_gain, mla_k_gain, g_group_out, w_out, g_ffn_norm,
     w_rg, b_rg, w_re, b_re) = p
    depth = w_in.shape[0]

    def heads_0213(a, axis, off):
        sl = lambda lo: lax.slice_in_dim(a, off + lo, off + lo + 64, axis=axis)
        return [sl(0), sl(128), sl(64), sl(192)]

    zeros = lambda *shape: jnp.zeros((depth,) + shape, F32)
    row = lambda a: a[:, None, :]
    w_in_p = jnp.concatenate([w_in[:, :, :1280]] + heads_0213(w_in, 2, 1280) +
                             [w_in[:, :, 1536:2208], zeros(D_MODEL, IN_W - 2208)], axis=2).astype(BF16)
    wqb = jnp.concatenate([jnp.concatenate([mla_w_qb[:, :, h * 96:h * 96 + 96], zeros(D_Q_LORA, 32)], axis=2)
                           for h in range(D_HEADS)], axis=2)
    wk = jnp.concatenate([jnp.concatenate([mla_w_kvb[:, :, h * 128:h * 128 + 64], zeros(D_KV_LORA, 64)], axis=2)
                          for h in range(D_HEADS)], axis=2)
    wv = jnp.concatenate([mla_w_kvb[:, :, h * 128 + 64:h * 128 + 128] for h in range(D_HEADS)], axis=2)
    qslab = jnp.concatenate([mla_q_gain, zeros(32)], axis=1)
    kslab = jnp.concatenate([mla_k_gain[:, :64], zeros(64)], axis=1)
    g_go = jnp.concatenate([g_group_out[:, :512]] + heads_0213(g_group_out, 1, 512) + [g_group_out[:, 768:]],
                           axis=1)
    w_out_p = jnp.concatenate([w_out[:, :512]] + heads_0213(w_out, 1, 512) + [w_out[:, 768:]],
                              axis=1).astype(BF16)
    w_r = jnp.concatenate([w_re, w_rg], axis=2)
    w_r_hi = w_r.astype(BF16)
    w_r_lo = (w_r - w_r_hi.astype(F32)).astype(BF16)
    n_r = N_EXPERTS + N_GROUPS
    zb = lambda w: jnp.zeros((depth, D_MODEL, w), BF16)
    w_router = jnp.concatenate([w_r_hi, zb(ROUTE_SEL_LANE - n_r), w_r_lo, zb(LANES - ROUTE_SEL_LANE - n_r)], axis=2)
    w_router = jnp.swapaxes(w_router, 1, 2)
    b_router = jnp.concatenate([b_re, b_rg, zeros(ROUTE_SEL_LANE - n_r)], axis=1)
    b_router = jnp.broadcast_to(b_router[:, :, None], (depth, ROUTE_SEL_LANE, MERGE_ROWS))
    sgub_full = jnp.repeat(jnp.swapaxes(sgu_b, 1, 2), 64, axis=2)
    return {
        "g_attn": row(g_attn_norm), "w_in": w_in_p,
        "sgu_w": sgu_w.reshape(depth, A_HEADS * CHUNK, CHUNK).astype(BF16), "sgu_b": sgub_full,
        "v_gain": row(sgu_v_gain), "conv_w": conv_w, "g_go": row(g_go),
        "cq_gain": row(jnp.tile(gqa_q_gain, (1, 4))), "ck_gain": row(jnp.tile(gqa_k_gain, (1, 2))),
        "qa_gain": row(mla_qa_gain), "w_qb": wqb.astype(BF16), "kva_gain": row(mla_kva_gain),
        "w_kvb": jnp.concatenate([wk, wv], axis=2).astype(BF16),
        "dq_gain": row(jnp.tile(qslab, (1, 4))), "dk_gain": row(jnp.tile(kslab, (1, 4))),
        "dkr_gain": row(jnp.concatenate([mla_k_gain[:, 64:], zeros(96)], axis=1)),
        "w_out": w_out_p, "g_ffn": row(g_ffn_norm), "w_router": w_router, "b_router": b_router,
    }


def kernel(x, c, ctx, c_ctx, w_mod, b_mod, g_attn_norm, w_in, sgu_v_gain, sgu_w, sgu_b, conv_w, gqa_q_gain,
           gqa_k_gain, mla_qa_gain, mla_w_qb, mla_kva_gain, mla_w_kvb, mla_q_gain, mla_k_gain, g_group_out,
           w_out, g_ffn_norm, w_router_group, b_router_group, w_router_expert, b_router_expert,
           w_expert_up, w_expert_down):
    b, s, d = x.shape
    lc = ctx.shape[1]
    depth = w_mod.shape[0]
    lw = _stacked_weights((g_attn_norm, w_in, sgu_v_gain, sgu_w, sgu_b, conv_w, gqa_q_gain, gqa_k_gain,
                           mla_qa_gain, mla_w_qb, mla_kva_gain, mla_w_kvb, mla_q_gain, mla_k_gain, g_group_out,
                           w_out, g_ffn_norm, w_router_group, b_router_group, w_router_expert, b_router_expert))
    w_up = w_expert_up.reshape(depth * N_EXPERTS, d, 2 * D_EXPERT)
    w_down = w_expert_down.reshape(depth * N_EXPERTS, D_EXPERT, d)

    c_all = jnp.concatenate([c, c_ctx[None, :], jnp.zeros((16 - b - 1, d), F32)], axis=0)
    mod = _modulation(c_all, w_mod, b_mod).reshape(depth, 16, 6, d)
    rope = _rope_tables(s // GRID_W)

    tm = MOE_T
    tq = 512
    tpb = s // tm
    lat_idx = lambda i: i // tpb
    ctx_idx = lambda i: b
    r3 = lambda a, n: a.reshape(b, n, a.shape[-1])

    xl = x.reshape(b * s, d)
    xc = ctx.reshape(b * lc, d)
    for l in range(depth):
        update_ctx = l < depth - 1
        ya_c, yb_c, qc_c, kc_c, vc_c, qd_c, kd_c, vd_c = _in_call(
            xc, mod, lw, l, None, tm=tm, seq_len=lc, mod_index=ctx_idx)
        ya, yb, qc, kc, vc, qd, kd, vd = _in_call(xl, mod, lw, l, rope, tm=IN_T, seq_len=s,
                                                  mod_index=lambda i: i // (s // IN_T))
        yc = _attn_call(r3(qc, s), [r3(kc_c, lc), r3(kc, s)], [r3(vc_c, lc), r3(vc, s)], lw["g_go"], l,
                        mode="C", tq=tq)
        yd = _attn_call(r3(qd, s), [r3(kd_c, lc), r3(kd, s)], [r3(vd_c, lc), r3(vd, s)], lw["g_go"], l,
                        mode="D", tq=tq)
        xl, h2, route, counts = _merge_call(xl, ya, yb, yc.reshape(b * s, GROUP_W), yd.reshape(b * s, GROUP_W),
                                            mod, lw, l, tm=tm, mod_index=lat_idx)
        xl = _moe_call(h2, route, counts, xl, mod, w_up, w_down, l, mod_index=lat_idx)
        if update_ctx:
            yc_c = _attn_call(r3(qc_c, lc), [r3(kc_c, lc)], [r3(vc_c, lc)], lw["g_go"], l, mode="C", tq=lc)
            yd_c = _attn_call(r3(qd_c, lc), [r3(kd_c, lc)], [r3(vd_c, lc)], lw["g_go"], l, mode="D", tq=lc)
            xc, h2c, route_c, counts_c = _merge_call(xc, ya_c, yb_c, yc_c.reshape(b * lc, GROUP_W),
                                                     yd_c.reshape(b * lc, GROUP_W), mod, lw, l, tm=tm,
                                                     mod_index=ctx_idx)
            xc = _moe_call(h2c, route_c, counts_c, xc, mod, w_up, w_down, l, mod_index=ctx_idx)
    return xl.reshape(b, s, d)
```

```python
import functools

import numpy as np
import jax
import jax.numpy as jnp
from jax import lax
from jax.experimental import pallas as pl
from jax.experimental.pallas import tpu as pltpu

F32 = jnp.float32
BF16 = jnp.bfloat16

D_MODEL = 1024
GRID_W = 64
EPS = 1e-6
ROPE_THETA = 10000.0
GROUP_W = 256
CHUNK = 128
A_HEADS = 4
C_HEAD_DIM = 64
C_SCALE = C_HEAD_DIM ** -0.5
D_HEADS = 4
D_NOPE = 64
D_ROPE = 32
D_V = 64
D_Q_LORA = 256
D_KV_LORA = 128
D_SCALE = (D_NOPE + D_ROPE) ** -0.5
LOG2E = 1.4426950408889634
ATTN_ROWS = 16
MERGE_ROWS = 128
N_GROUPS = 4
EXPERTS_PER_GROUP = 4
N_EXPERTS = 16
D_EXPERT = 256
LANES = 128
HALO = 8

IN_W = 2304
IN_T = 512
IN_SECTIONS = ((0, 256), (256, 512), (512, 768), (768, 1024), (1024, 1280), (1280, 1536), (1536, 1664),
               (1664, 1792), (1792, 2048), (2048, 2176), (2176, 2304))

ROUTE_SEL_LANE = 32
MOE_T = 512
MOE_CHUNK = 16
MOE_RLOC = 1280
MOE_NCH = MOE_RLOC // MOE_CHUNK
MOE_RBLK = 256
MOE_XW = D_MODEL + LANES
MOE_WORK = 1024
MOE_CPW = MOE_WORK // MOE_CHUNK
VMEM_LIMIT = 56 * 1024 * 1024


def _nt_dot(a, b):
    return lax.dot_general(a, b, (((1,), (1,)), ((), ())), preferred_element_type=F32)


def _dot(a, b):
    return jnp.dot(a, b, preferred_element_type=F32)


def _rms_full(x, gain):
    return x * lax.rsqrt(jnp.mean(x * x, axis=-1, keepdims=True) + EPS) * gain


def _group_ms(x, ones_blocks, inv_count):
    return _dot((x * x).astype(BF16), ones_blocks) * inv_count


def _pair_swap(x, even):
    nxt = pltpu.roll(x, x.shape[1] - 1, axis=1)
    prv = pltpu.roll(x, 1, axis=1)
    return jnp.where(even, nxt, prv)


def _rope(x, cos, sin, even):
    return x * cos + _pair_swap(x, even) * sin


def _mod_kernel(c_ref, w_ref, b_ref, o_ref):
    cv = c_ref[...]
    sc = cv * jax.nn.sigmoid(cv)
    o_ref[0] = jnp.dot(sc, w_ref[0], preferred_element_type=F32,
                       precision=lax.Precision.HIGHEST) + b_ref[0]


def _modulation(c_all, w_mod, b_mod):
    depth = w_mod.shape[0]
    nblk = 4
    wblk = 6 * D_MODEL // nblk
    return pl.pallas_call(
        _mod_kernel,
        grid=(depth, nblk),
        in_specs=[pl.BlockSpec((16, D_MODEL), lambda l, j: (0, 0)),
                  pl.BlockSpec((1, D_MODEL, wblk), lambda l, j: (l, 0, j)),
                  pl.BlockSpec((1, 1, wblk), lambda l, j: (l, 0, j))],
        out_specs=pl.BlockSpec((1, 16, wblk), lambda l, j: (l, 0, j)),
        out_shape=jax.ShapeDtypeStruct((depth, 16, 6 * D_MODEL), F32),
        compiler_params=pltpu.CompilerParams(dimension_semantics=("arbitrary", "arbitrary"),
                                             vmem_limit_bytes=VMEM_LIMIT),
        name="mod",
    )(c_all, w_mod, b_mod.reshape(depth, 1, 6 * D_MODEL))


def _in_kernel(*refs, tm, seq_len, use_rope):
    n_io = len(refs) - 4
    pj0_ref, pj1_ref, zh0_ref, zh1_ref = refs[n_io:]
    i = pl.program_id(0)

    @pl.when(i == 0)
    def _():
        pj1_ref[...] = jnp.zeros(pj1_ref.shape, F32)
        zh1_ref[...] = jnp.zeros(zh1_ref.shape, F32)

    @pl.when(i % 2 == 0)
    def _():
        _in_stages(refs[:n_io], pj0_ref, zh0_ref, pj1_ref, zh1_ref, tm=tm, seq_len=seq_len, use_rope=use_rope)

    @pl.when(i % 2 == 1)
    def _():
        _in_stages(refs[:n_io], pj1_ref, zh1_ref, pj0_ref, zh0_ref, tm=tm, seq_len=seq_len, use_rope=use_rope)


def _in_stages(refs, pj_w, zh_w, pj_r, zh_r, *, tm, seq_len, use_rope):
    (x_ref, xp_ref, xn_ref, mod_ref, gattn_ref, win_ref, sguw_ref, sgub_ref, vgain_ref, convw_ref,
     ggo_ref, cqg_ref, ckg_ref, qag_ref, wqb_ref, kvag_ref, wkvb_ref, dqg_ref, dkg_ref, dkrg_ref,
     b64_ref, bq_ref, invq_ref) = refs[:23]
    pos = 23
    if use_rope:
        cosc_ref, sinc_ref, cosq_ref, sinq_ref, cosk_ref, sink_ref = refs[pos:pos + 6]
        pos += 6
    ya_ref, yb_ref, qc_ref, kc_ref, vc_ref, qd_ref, kd_ref, vd_ref = refs[pos:pos + 8]

    shift = mod_ref[0, 0:1, :]
    scale = mod_ref[0, 1:2, :]
    gattn = gattn_ref[...]

    def prenorm(xv):
        return (_rms_full(xv, gattn) * (1.0 + scale) + shift).astype(BF16)

    hb = prenorm(x_ref[...])

    pending = list(IN_SECTIONS)

    def project(count):
        for _ in range(min(count, len(pending))):
            lo, hi = pending.pop(0)
            pj_w[:, lo:hi] = _dot(hb, win_ref[:, lo:hi])

    project(1)
    hh = prenorm(jnp.concatenate([xp_ref[...], xn_ref[...]], axis=0))
    zh_w[...] = _dot(hh, win_ref[:, 768:1024]) * _dot(hh, win_ref[:, 1024:1280])

    i = jnp.maximum(pl.program_id(0) - 1, 0)
    proj = {ab: pj_r[:, ab[0]:ab[1]] for ab in IN_SECTIONS}
    zh = zh_r[...]
    b64 = b64_ref[...]
    inv64 = 1.0 / 64.0
    lane128 = lax.broadcasted_iota(jnp.int32, (tm, LANES), 1)
    even = (lane128 & 1) == 0

    u = jax.nn.gelu(proj[(0, 256)])
    v = jax.nn.gelu(proj[(256, 512)])
    project(1)
    v = (v * lax.rsqrt(_group_ms(v, b64, inv64) + EPS) * vgain_ref[...]).astype(BF16)
    project(1)
    lane_head = lax.broadcasted_iota(jnp.int32, (CHUNK, GROUP_W), 1) // 64
    sguw = sguw_ref[...]
    mixed_chunks = []
    for ci in range(tm // CHUNK):
        r = _dot(sguw, v[ci * CHUNK:(ci + 1) * CHUNK, :])
        m = r[0:CHUNK]
        for h in range(1, A_HEADS):
            m = jnp.where(lane_head == h, r[h * CHUNK:(h + 1) * CHUNK], m)
        mixed_chunks.append(m + sgub_ref[...])
    ya = u * jnp.concatenate(mixed_chunks, axis=0)
    ya_ref[...] = _rms_full(ya, ggo_ref[:, 0:256]).astype(BF16)

    project(1)
    bg = proj[(512, 768)]
    z = proj[(768, 1024)] * proj[(1024, 1280)]
    row = lax.broadcasted_iota(jnp.int32, (tm, GROUP_W), 0)
    posn = (i * tm + row) % seq_len
    z_prev = jnp.where(row == 0, zh[HALO - 1:HALO, :], pltpu.roll(z, 1, axis=0))
    z_prev = jnp.where(posn == 0, 0.0, z_prev)
    z_next = jnp.where(row == tm - 1, zh[HALO:HALO + 1, :], pltpu.roll(z, tm - 1, axis=0))
    z_next = jnp.where(posn == seq_len - 1, 0.0, z_next)
    yb = bg * (convw_ref[0:1, :] * z_prev + convw_ref[1:2, :] * z + convw_ref[2:3, :] * z_next)
    yb_ref[...] = _rms_full(yb, ggo_ref[:, 256:512]).astype(BF16)

    project(1)
    q = proj[(1280, 1536)]
    q = q * lax.rsqrt(_group_ms(q, b64, inv64) + EPS) * cqg_ref[...]
    project(1)
    k = proj[(1536, 1664)]
    k = k * lax.rsqrt(_group_ms(k, b64[0:128, 0:128], inv64) + EPS) * ckg_ref[...]
    if use_rope:
        cosc, sinc = cosc_ref[...], sinc_ref[...]
        q = jnp.concatenate([_rope(q[:, 0:128], cosc, sinc, even),
                             _rope(q[:, 128:256], cosc, sinc, even)], axis=1)
        k = _rope(k, cosc, sinc, even)
    qc_ref[...] = (q * (C_SCALE * LOG2E)).astype(BF16)
    first_half = lane128 < 64

    def per_slab(a):
        swapped = pltpu.roll(a, 64, axis=1)
        return jnp.concatenate([jnp.where(first_half, a, swapped), jnp.where(first_half, swapped, a)], axis=1)

    kc_ref[...] = per_slab(k).astype(BF16)
    vc_ref[...] = per_slab(proj[(1664, 1792)]).astype(BF16)

    project(1)
    cq = _rms_full(proj[(1792, 2048)], qag_ref[...]).astype(BF16)
    project(1)
    qd = _dot(cq, wqb_ref[...])
    bq = bq_ref[...]
    invq = invq_ref[...]
    halves = []
    for hf in range(2):
        qh = qd[:, hf * 256:(hf + 1) * 256]
        qh = qh * lax.rsqrt(_group_ms(qh, bq, invq) + EPS) * dqg_ref[:, hf * 256:(hf + 1) * 256]
        if use_rope:
            cosq, sinq = cosq_ref[...], sinq_ref[...]
            qh = jnp.concatenate([_rope(qh[:, 0:128], cosq, sinq, even),
                                  _rope(qh[:, 128:256], cosq, sinq, even)], axis=1)
        halves.append(qh)
    qd_ref[...] = (jnp.concatenate(halves, axis=1) * (D_SCALE * LOG2E)).astype(BF16)

    project(1)
    ckv = _rms_full(proj[(2048, 2176)], kvag_ref[...]).astype(BF16)
    project(1)
    kv = _dot(ckv, wkvb_ref[...])
    vd_ref[...] = kv[:, 512:768].astype(BF16)
    project(len(pending))
    kr = proj[(2176, 2304)]
    kr = kr * lax.rsqrt(jnp.sum(kr * kr, axis=-1, keepdims=True) * (1.0 / D_ROPE) + EPS) * dkrg_ref[...]
    if use_rope:
        kr = _rope(kr, cosk_ref[...], sink_ref[...], even)
    kr = pltpu.roll(kr, D_NOPE, axis=1)
    for hf in range(2):
        kh = kv[:, hf * 256:(hf + 1) * 256]
        kh = kh * lax.rsqrt(_group_ms(kh, b64, inv64) + EPS) * dkg_ref[:, hf * 256:(hf + 1) * 256]
        kd_ref[:, hf * 256:hf * 256 + 128] = (kh[:, 0:128] + kr).astype(BF16)
        kd_ref[:, hf * 256 + 128:(hf + 1) * 256] = (kh[:, 128:256] + kr).astype(BF16)


def _layer_spec(a, layer):
    zeros = (0,) * (a.ndim - 1)
    return pl.BlockSpec((None,) + a.shape[1:], lambda *_: (layer,) + zeros)


def _mod_spec(layer, mod_index):
    return pl.BlockSpec((None, 1, 6, D_MODEL), lambda i, *_: (layer, mod_index(i), 0, 0))


def _in_call(x2d, mod, lw, layer, rope, *, tm, seq_len, mod_index):
    n = x2d.shape[0]
    nt = n // tm
    nb8 = n // HALO
    tb = tm // HALO
    use_rope = rope is not None
    cur = lambda i: jnp.minimum(i, nt - 1)
    prev = lambda i: jnp.maximum(i - 1, 0)
    row = lambda i: (prev(i), 0)
    in_specs = [
        pl.BlockSpec((tm, D_MODEL), lambda i: (cur(i), 0)),
        pl.BlockSpec((HALO, D_MODEL), lambda i: (jnp.maximum(cur(i) * tb - 1, 0), 0)),
        pl.BlockSpec((HALO, D_MODEL), lambda i: (jnp.minimum((cur(i) + 1) * tb, nb8 - 1), 0)),
        _mod_spec(layer, lambda i: mod_index(cur(i))),
    ]
    args = [x2d, x2d, x2d, mod]
    for name in ("g_attn", "w_in", "sgu_w", "sgu_b", "v_gain", "conv_w", "g_go", "cq_gain", "ck_gain",
                 "qa_gain", "w_qb", "kva_gain", "w_kvb", "dq_gain", "dk_gain", "dkr_gain"):
        in_specs.append(_layer_spec(lw[name], layer))
        args.append(lw[name])
    for a in (_ones_blocks([64] * 4), _ones_blocks([64, 32, 32] * 2), _inv_counts_q()):
        in_specs.append(pl.BlockSpec(a.shape, lambda i: (0, 0)))
        args.append(a)
    if use_rope:
        ntab = seq_len // tm
        for a in rope:
            in_specs.append(pl.BlockSpec((tm, LANES), lambda i: (prev(i) % ntab, 0)))
            args.append(a)
    widths = (256, 256, 256, 256, 256, 512, 512, 256)
    out_specs = [pl.BlockSpec((tm, w), row) for w in widths]
    out_shape = [jax.ShapeDtypeStruct((n, w), BF16) for w in widths]
    return pl.pallas_call(
        functools.partial(_in_kernel, tm=tm, seq_len=seq_len, use_rope=use_rope),
        grid=(nt + 1,),
        in_specs=in_specs,
        out_specs=out_specs,
        out_shape=out_shape,
        scratch_shapes=[pltpu.VMEM((tm, IN_W), F32), pltpu.VMEM((tm, IN_W), F32),
                        pltpu.VMEM((2 * HALO, GROUP_W), F32), pltpu.VMEM((2 * HALO, GROUP_W), F32)],
        compiler_params=pltpu.CompilerParams(dimension_semantics=("arbitrary",),
                                             vmem_limit_bytes=VMEM_LIMIT),
        name="in_proj",
    )(*args)


def _attn_kernel(*refs, mode, nseg, tq):
    q_ref = refs[0]
    k_refs = refs[1:1 + nseg]
    v_refs = refs[1 + nseg:1 + 2 * nseg]
    gain_ref = refs[1 + 2 * nseg]
    o_ref = refs[2 + 2 * nseg]
    p_ref = refs[3 + 2 * nseg]
    nks = [r.shape[1] for r in k_refs]
    offs = [sum(nks[:j]) for j in range(nseg)]
    n_chunks = tq // ATTN_ROWS

    vs = [r[0] for r in v_refs]
    if mode == "C":
        first_half = lax.broadcasted_iota(jnp.int32, (tq, LANES), 1) < 64
        heads = []
        for sl in range(2):
            qs = q_ref[0, :, sl * 128:(sl + 1) * 128]
            ks_sl = [r[0, :, sl * 128:(sl + 1) * 128] for r in k_refs]
            heads.append((jnp.where(first_half, qs, jnp.zeros_like(qs)), ks_sl))
            heads.append((jnp.where(first_half, jnp.zeros_like(qs), qs), ks_sl))
    else:
        heads = [(q_ref[0, :, h * 128:(h + 1) * 128], [r[0, :, h * 128:(h + 1) * 128] for r in k_refs])
                 for h in range(D_HEADS)]

    def score(h):
        qm, ks = heads[h]
        return [_nt_dot(qm, kk) for kk in ks]

    def softmax(h, ss):
        for c in range(n_chunks):
            rows = slice(c * ATTN_ROWS, (c + 1) * ATTN_ROWS)
            m = ss[0][rows].max(axis=-1, keepdims=True)
            for s in ss[1:]:
                m = jnp.maximum(m, s[rows].max(axis=-1, keepdims=True))
            for s, off, nk in zip(ss, offs, nks):
                p_ref[h, rows, off:off + nk] = jnp.exp2(s[rows] - m).astype(BF16)

    if mode == "C":
        ones_lane = [LANES] * 4
        one_col = jnp.where(lax.broadcasted_iota(jnp.int32, (1, LANES), 1) == 0, 1.0, 0.0).astype(BF16)
        v_ext = []
        for sl in range(2):
            ext = [jnp.concatenate([vv[:, sl * 128:(sl + 1) * 128], jnp.broadcast_to(one_col, (vv.shape[0], LANES))],
                                   axis=1) for vv in vs]
            v_ext += [ext, ext]
    else:
        ones_lane = [((h + 1) % D_HEADS) * D_V for h in range(D_HEADS)]
        v_ext = []
        lane_v = lax.broadcasted_iota(jnp.int32, (1, GROUP_W), 1)
        for h in range(D_HEADS):
            one = jnp.where(lane_v == ones_lane[h], 1.0, 0.0).astype(BF16)
            keep = jnp.where(lane_v == ones_lane[h], 0.0, 1.0).astype(BF16)
            v_ext.append([vv * keep + one for vv in vs])

    def values(h):
        o = None
        for off, nk, vv in zip(offs, nks, v_ext[h]):
            part = _dot(p_ref[h, :, off:off + nk], vv)
            o = part if o is None else o + part
        out_w = LANES if mode == "C" else GROUP_W
        return o[:, 0:out_w] / o[:, ones_lane[h]:ones_lane[h] + 1]

    n_heads = len(heads)
    scores = {0: score(0), 1: score(1)}
    outs = []
    for h in range(n_heads):
        softmax(h, scores.pop(h))
        if h + 2 < n_heads:
            scores[h + 2] = score(h + 2)
        if h >= 1:
            outs.append(values(h - 1))
    outs.append(values(n_heads - 1))

    if mode == "C":
        out = jnp.concatenate([jnp.where(first_half, outs[0], outs[1]),
                               jnp.where(first_half, outs[2], outs[3])], axis=1)
    else:
        lane_head = lax.broadcasted_iota(jnp.int32, (tq, GROUP_W), 1) // D_V
        out = outs[0]
        for h in range(1, D_HEADS):
            out = jnp.where(lane_head == h, outs[h], out)
    o_ref[0] = _rms_full(out, gain_ref[...]).astype(BF16)


def _attn_call(q, ks, vs, g_go, layer, *, mode, tq):
    b, s, wq = q.shape
    nseg = len(ks)
    gain_block = 2 if mode == "C" else 3
    in_specs = [pl.BlockSpec((1, tq, wq), lambda bi, qi: (bi, qi, 0))]
    for a in list(ks) + list(vs):
        in_specs.append(pl.BlockSpec((1,) + a.shape[1:], lambda bi, qi: (bi, 0, 0)))
    in_specs.append(pl.BlockSpec((None, 1, GROUP_W), lambda bi, qi: (layer, 0, gain_block)))
    return pl.pallas_call(
        functools.partial(_attn_kernel, mode=mode, nseg=nseg, tq=tq),
        grid=(b, s // tq),
        in_specs=in_specs,
        out_specs=pl.BlockSpec((1, tq, GROUP_W), lambda bi, qi: (bi, qi, 0)),
        out_shape=jax.ShapeDtypeStruct((b, s, GROUP_W), BF16),
        scratch_shapes=[pltpu.VMEM((4, tq, sum(a.shape[1] for a in ks)), BF16)],
        compiler_params=pltpu.CompilerParams(dimension_semantics=("parallel", "parallel"),
                                             vmem_limit_bytes=VMEM_LIMIT),
        name="attn_" + mode,
    )(q, *ks, *vs, g_go)


def _merge_kernel(x_ref, ya_ref, yb_ref, yc_ref, yd_ref, mod_ref, wout_ref, gffn_ref, wr_ref, br_ref,
                  xo_ref, h2_ref, route_ref, cnt_ref, *, tm):
    nb = tm // MERGE_ROWS
    blocks = [slice(j * MERGE_ROWS, (j + 1) * MERGE_ROWS) for j in range(nb)]
    ys = []
    for rs in blocks:
        y = _dot(ya_ref[rs, :], wout_ref[0:256, :])
        y = y + _dot(yb_ref[rs, :], wout_ref[256:512, :])
        y = y + _dot(yc_ref[rs, :], wout_ref[512:768, :])
        y = y + _dot(yd_ref[rs, :], wout_ref[768:1024, :])
        ys.append(y)
    counts = None
    for rs, y in zip(blocks, ys):
        sel = _merge_block(rs, y, x_ref, mod_ref, gffn_ref, wr_ref, br_ref, xo_ref, h2_ref, route_ref)
        part = sel.sum(axis=0, keepdims=True)
        counts = part if counts is None else counts + part
    cnt_ref[0] = jnp.broadcast_to(counts, (HALO, LANES))


def _merge_block(rs, y, x_ref, mod_ref, gffn_ref, wr_ref, br_ref, xo_ref, h2_ref, route_ref):
    rows = rs.stop - rs.start
    xn = x_ref[rs, :] + mod_ref[0, 2:3, :] * y
    xo_ref[rs, :] = xn
    h2 = _rms_full(xn, gffn_ref[...]) * (1.0 + mod_ref[0, 4:5, :]) + mod_ref[0, 3:4, :]
    h2b = h2.astype(BF16)
    h2_ref[rs, :] = h2b

    h2lo = (h2 - h2b.astype(F32)).astype(BF16)
    part = _nt_dot(wr_ref[...], h2b) + _nt_dot(wr_ref[...], h2lo)
    logit = part[0:ROUTE_SEL_LANE] + part[ROUTE_SEL_LANE:2 * ROUTE_SEL_LANE] + br_ref[...]
    row_of = lambda j: logit[j:j + 1, :]
    lg = [row_of(N_EXPERTS + g) for g in range(N_GROUPS)]

    def first_max(vals):
        m = vals[0]
        for v in vals[1:]:
            m = jnp.maximum(m, v)
        idx = jnp.float32(len(vals) - 1)
        for j in range(len(vals) - 2, -1, -1):
            idx = jnp.where(vals[j] == m, jnp.float32(j), idx)
        return m, idx

    gmax, g_idx = first_max(lg)
    g_w = 1.0 / sum(jnp.exp(v - gmax) for v in lg)
    cand = []
    for j in range(EXPERTS_PER_GROUP):
        c = row_of((N_GROUPS - 1) * EXPERTS_PER_GROUP + j)
        for g in range(N_GROUPS - 2, -1, -1):
            c = jnp.where(g_idx == g, row_of(g * EXPERTS_PER_GROUP + j), c)
        cand.append(c)
    m1, i1 = first_max(cand)
    neg = jnp.float32(-jnp.inf)
    m2, i2 = first_max([jnp.where(i1 == j, neg, c) for j, c in enumerate(cand)])
    r = jnp.exp(m2 - m1)
    w1 = g_w / (1.0 + r)
    w2 = g_w * r / (1.0 + r)
    e1 = g_idx * EXPERTS_PER_GROUP + i1
    e2 = g_idx * EXPERTS_PER_GROUP + i2
    eio = lax.broadcasted_iota(jnp.int32, (N_EXPERTS, rows), 0).astype(F32)
    cmb_t = jnp.where(eio == e1, w1, jnp.where(eio == e2, w2, 0.0))
    sel_t = jnp.where((eio == e1) | (eio == e2), 1.0, 0.0)
    zeros = lambda n: jnp.zeros((n, rows), F32)
    route_t = jnp.concatenate([cmb_t, zeros(ROUTE_SEL_LANE - N_EXPERTS), sel_t,
                               zeros(LANES - ROUTE_SEL_LANE - N_EXPERTS)], axis=0)
    route = route_t.T
    route_ref[rs, :] = route
    lane = lax.broadcasted_iota(jnp.int32, (rows, LANES), 1)
    return jnp.where(lane >= ROUTE_SEL_LANE, route, 0.0)


def _merge_call(x2d, ya, yb, yc, yd, mod, lw, layer, *, tm, mod_index):
    n = x2d.shape[0]
    row = lambda i: (i, 0)
    in_specs = [pl.BlockSpec((tm, D_MODEL), row)]
    in_specs += [pl.BlockSpec((tm, GROUP_W), row)] * 4
    in_specs += [_mod_spec(layer, mod_index)]
    in_specs += [_layer_spec(lw[k], layer) for k in ("w_out", "g_ffn", "w_router", "b_router")]
    return pl.pallas_call(
        functools.partial(_merge_kernel, tm=tm),
        grid=(n // tm,),
        in_specs=in_specs,
        out_specs=[pl.BlockSpec((tm, D_MODEL), row), pl.BlockSpec((tm, D_MODEL), row),
                   pl.BlockSpec((tm, LANES), row), pl.BlockSpec((1, HALO, LANES), lambda i: (i, 0, 0))],
        out_shape=[jax.ShapeDtypeStruct((n, D_MODEL), F32), jax.ShapeDtypeStruct((n, D_MODEL), BF16),
                   jax.ShapeDtypeStruct((n, LANES), F32),
                   jax.ShapeDtypeStruct((n // tm, HALO, LANES), F32)],
        compiler_params=pltpu.CompilerParams(dimension_semantics=("parallel",),
                                             vmem_limit_bytes=VMEM_LIMIT),
        name="merge",
    )(x2d, ya, yb, yc, yd, mod, lw["w_out"], lw["g_ffn"], lw["w_router"], lw["b_router"])


def _segment_positions(route, loc, tm):
    lane = lax.broadcasted_iota(jnp.int32, (tm, LANES), 1)
    is_sel = (lane >= ROUTE_SEL_LANE) & (lane < ROUTE_SEL_LANE + N_EXPERTS)
    selm = is_sel & (route > 0.5)
    selb = jnp.where(selm, 1.0, 0.0).astype(BF16)
    r_i = lax.broadcasted_iota(jnp.int32, (tm, tm), 0)
    c_i = lax.broadcasted_iota(jnp.int32, (tm, tm), 1)
    earlier = jnp.where(c_i < r_i, 1.0, 0.0).astype(BF16)
    dest = loc + _dot(earlier, selb)
    big = jnp.float32(1e9)
    d_a = jnp.where(selm, dest, big).min(axis=-1, keepdims=True)
    d_b = jnp.where(selm, dest, -big).max(axis=-1, keepdims=True)
    return selm, dest, d_a, d_b


def _to_row(col, tm):
    r_i = lax.broadcasted_iota(jnp.int32, (tm, tm), 0)
    c_i = lax.broadcasted_iota(jnp.int32, (tm, tm), 1)
    return jnp.where(r_i == c_i, col, 0.0).sum(axis=0, keepdims=True)


def _chunk_copy_out(xs_ref, xbuf_ref, sem_ref, slot, src_chunk, dst_chunk):
    return pltpu.make_async_copy(xs_ref.at[slot, src_chunk], xbuf_ref.at[dst_chunk], sem_ref.at[slot])


def _dispatch_kernel(nch_ref, dst_ref, plo_ref, phi_ref, nused_ref, h2_ref, route_ref, loc_ref, xbuf_ref,
                     xs_ref, sem_ref, *, tm, n_tiles, n_work):
    i = pl.program_id(0)
    slot = i % 2
    zero_chunk = MOE_RLOC // MOE_CHUNK

    def start_tile(t_idx, slot_):
        def body(j, carry):
            _chunk_copy_out(xs_ref, xbuf_ref, sem_ref, slot_, j, dst_ref[t_idx * MOE_NCH + j]).start()
            return carry
        lax.fori_loop(0, nch_ref[t_idx], body, 0)

    def wait_tile(t_idx, slot_):
        def body(j, carry):
            _chunk_copy_out(xs_ref, xbuf_ref, sem_ref, slot_, 0, 0).wait()
            return carry
        lax.fori_loop(0, nch_ref[t_idx], body, 0)

    @pl.when(i == 0)
    def _():
        xs_ref[2] = jnp.zeros(xs_ref.shape[1:], BF16)

        def pad_start(j, carry):
            _chunk_copy_out(xs_ref, xbuf_ref, sem_ref, 2, zero_chunk, j).start()
            return carry

        def pad_wait(j, carry):
            _chunk_copy_out(xs_ref, xbuf_ref, sem_ref, 2, zero_chunk, 0).wait()
            return carry
        for e in range(N_EXPERTS):
            lax.fori_loop(plo_ref[e], phi_ref[e], pad_start, 0)
        for e in range(N_EXPERTS):
            lax.fori_loop(plo_ref[e], phi_ref[e], pad_wait, 0)

        def tail_copy(w):
            return pltpu.make_async_copy(xs_ref.at[2, pl.ds(0, MOE_CPW)],
                                         xbuf_ref.at[pl.ds(w * MOE_CPW, MOE_CPW)], sem_ref.at[2])

        def tail_start(w, carry):
            tail_copy(w).start()
            return carry

        def tail_wait(w, carry):
            tail_copy(w).wait()
            return carry
        lax.fori_loop(nused_ref[0], n_work, tail_start, 0)
        lax.fori_loop(nused_ref[0], n_work, tail_wait, 0)

    @pl.when(i >= 2)
    def _():
        wait_tile(i - 2, slot)

    route = route_ref[...]
    selm, dest, d_a, d_b = _segment_positions(route, loc_ref[0, 0:1, :], tm)
    cmb_al = pltpu.roll(route, ROUTE_SEL_LANE, axis=1)
    w_a = jnp.where(selm & (dest == d_a), cmb_al, 0.0).sum(axis=-1, keepdims=True)
    w_b = jnp.where(selm & (dest == d_b), cmb_al, 0.0).sum(axis=-1, keepdims=True)
    da_r, db_r, wa_r, wb_r = (_to_row(v, tm) for v in (d_a, d_b, w_a, w_b))
    h2 = h2_ref[...]
    lane = lax.broadcasted_iota(jnp.int32, (MOE_RBLK, LANES), 1)
    for r0 in range(0, MOE_RLOC, MOE_RBLK):
        rio = (lax.broadcasted_iota(jnp.int32, (MOE_RBLK, tm), 0) + r0).astype(F32)
        hit_a = rio == da_r
        hit_b = rio == db_r
        perm = jnp.where(hit_a | hit_b, 1.0, 0.0).astype(BF16)
        c0, nc = r0 // MOE_CHUNK, MOE_RBLK // MOE_CHUNK
        xs_ref[slot, c0:c0 + nc, :, 0:D_MODEL] = _dot(perm, h2).astype(BF16).reshape(nc, MOE_CHUNK, D_MODEL)
        w_sorted = (jnp.where(hit_a, wa_r, 0.0) + jnp.where(hit_b, wb_r, 0.0)).sum(axis=-1, keepdims=True)
        w_hi = w_sorted.astype(BF16).astype(F32)
        r1 = w_sorted - w_hi
        w_mid = r1.astype(BF16).astype(F32)
        w_lo = r1 - w_mid
        slab = jnp.where(lane == 0, w_hi, jnp.where(lane == 1, w_mid, jnp.where(lane == 2, w_lo, 0.0)))
        xs_ref[slot, c0:c0 + nc, :, D_MODEL:MOE_XW] = slab.astype(BF16).reshape(nc, MOE_CHUNK, LANES)
    start_tile(i, slot)

    @pl.when(i == n_tiles - 1)
    def _():
        if n_tiles > 1:
            wait_tile(i - 1, 1 - slot)
        wait_tile(i, slot)


def _dispatch_call(h2, route, tabs, *, tm, n_rows):
    n = h2.shape[0]
    n_tiles = n // tm
    grid_spec = pltpu.PrefetchScalarGridSpec(
        num_scalar_prefetch=5,
        grid=(n_tiles,),
        in_specs=[pl.BlockSpec((tm, D_MODEL), lambda i, *_: (i, 0)),
                  pl.BlockSpec((tm, LANES), lambda i, *_: (i, 0)),
                  pl.BlockSpec((1, HALO, LANES), lambda i, *_: (i, 0, 0))],
        out_specs=pl.BlockSpec(memory_space=pl.ANY),
        scratch_shapes=[pltpu.VMEM((3, max(MOE_NCH + 1, MOE_CPW), MOE_CHUNK, MOE_XW), BF16),
                        pltpu.SemaphoreType.DMA((3,))],
    )
    xbuf = pl.pallas_call(
        functools.partial(_dispatch_kernel, tm=tm, n_tiles=n_tiles, n_work=n_rows // MOE_WORK),
        grid_spec=grid_spec,
        out_shape=jax.ShapeDtypeStruct((n_rows // MOE_CHUNK, MOE_CHUNK, MOE_XW), BF16),
        compiler_params=pltpu.CompilerParams(dimension_semantics=("arbitrary",),
                                             vmem_limit_bytes=VMEM_LIMIT),
        name="moe_dispatch",
    )(tabs["nch"], tabs["dst"], tabs["pad_lo"], tabs["pad_hi"], tabs["n_used"], h2, route, tabs["loc"])
    return xbuf.reshape(n_rows, MOE_XW)


def _expert_kernel(te_ref, nused_ref, x_ref, wup_ref, wdn_ref, y_ref, wup_bf, wdn_bf):
    w = pl.program_id(0)
    used = w < nused_ref[0]

    @pl.when((w == 0) | (te_ref[w] != te_ref[jnp.maximum(w - 1, 0)]))
    def _():
        wup_bf[...] = wup_ref[...].astype(BF16)
        wdn_bf[...] = wdn_ref[...].astype(BF16)

    @pl.when(used)
    def _():
        wt = x_ref[:, D_MODEL:MOE_XW].astype(F32).sum(axis=-1, keepdims=True)
        gu = _dot(x_ref[:, 0:D_MODEL], wup_bf[...])
        g = gu[:, 0:D_EXPERT]
        a = (g * jax.nn.sigmoid(g)) * gu[:, D_EXPERT:2 * D_EXPERT] * wt
        y_ref[...] = _dot(a.astype(BF16), wdn_bf[...]).astype(BF16)

    @pl.when(jnp.logical_not(used))
    def _():
        y_ref[...] = jnp.zeros(y_ref.shape, BF16)


def _expert_call(xbuf, tabs, w_up, w_down, layer):
    n_rows = xbuf.shape[0]
    n_work = n_rows // MOE_WORK
    base = layer * N_EXPERTS
    grid_spec = pltpu.PrefetchScalarGridSpec(
        num_scalar_prefetch=2,
        grid=(n_work,),
        in_specs=[pl.BlockSpec((MOE_WORK, MOE_XW), lambda w, te, nu: (jnp.minimum(w, nu[0] - 1), 0)),
                  pl.BlockSpec((None, D_MODEL, 2 * D_EXPERT), lambda w, te, nu: (base + te[w], 0, 0)),
                  pl.BlockSpec((None, D_EXPERT, D_MODEL), lambda w, te, nu: (base + te[w], 0, 0))],
        out_specs=pl.BlockSpec((MOE_WORK, D_MODEL), lambda w, te, nu: (w, 0)),
        scratch_shapes=[pltpu.VMEM((D_MODEL, 2 * D_EXPERT), BF16), pltpu.VMEM((D_EXPERT, D_MODEL), BF16)],
    )
    return pl.pallas_call(
        _expert_kernel,
        grid_spec=grid_spec,
        out_shape=jax.ShapeDtypeStruct((n_rows, D_MODEL), BF16),
        compiler_params=pltpu.CompilerParams(dimension_semantics=("arbitrary",),
                                             vmem_limit_bytes=VMEM_LIMIT),
        name="moe_experts",
    )(tabs["tile_expert"], tabs["n_used"], xbuf, w_up, w_down)


def _chunk_copy_in(ybuf_ref, ys_ref, sem_ref, slot, src_chunk, dst_chunk):
    return pltpu.make_async_copy(ybuf_ref.at[src_chunk], ys_ref.at[slot, dst_chunk], sem_ref.at[slot])


def _combine_kernel(nch_ref, dst_ref, route_ref, loc_ref, x_ref, mod_ref, ybuf_ref, o_ref,
                    ys_ref, sem_ref, *, tm, n_tiles):
    i = pl.program_id(0)
    slot = i % 2

    def start_tile(t_idx, slot_):
        def body(j, carry):
            _chunk_copy_in(ybuf_ref, ys_ref, sem_ref, slot_, dst_ref[t_idx * MOE_NCH + j], j).start()
            return carry
        lax.fori_loop(0, nch_ref[t_idx], body, 0)

    @pl.when(i == 0)
    def _():
        ys_ref[...] = jnp.zeros(ys_ref.shape, BF16)
        start_tile(0, 0)

    @pl.when(i + 1 < n_tiles)
    def _():
        start_tile(i + 1, 1 - slot)

    def wait_body(j, carry):
        _chunk_copy_in(ybuf_ref, ys_ref, sem_ref, slot, 0, 0).wait()
        return carry
    lax.fori_loop(0, nch_ref[i], wait_body, 0)

    _, _, d_a, d_b = _segment_positions(route_ref[...], loc_ref[0, 0:1, :], tm)
    y = None
    for r0 in range(0, MOE_RLOC, MOE_RBLK):
        lio = (lax.broadcasted_iota(jnp.int32, (tm, MOE_RBLK), 1) + r0).astype(F32)
        perm_t = jnp.where((lio == d_a) | (lio == d_b), 1.0, 0.0).astype(BF16)
        c0 = r0 // MOE_CHUNK
        part = _dot(perm_t, ys_ref[slot, c0:c0 + MOE_RBLK // MOE_CHUNK].reshape(MOE_RBLK, D_MODEL))
        y = part if y is None else y + part
    o_ref[...] = x_ref[...] + mod_ref[0, 5:6, :] * y


def _combine_call(ybuf, route, x2d, mod, layer, tabs, *, tm, mod_index):
    n = x2d.shape[0]
    n_tiles = n // tm
    grid_spec = pltpu.PrefetchScalarGridSpec(
        num_scalar_prefetch=2,
        grid=(n_tiles,),
        in_specs=[pl.BlockSpec((tm, LANES), lambda i, *_: (i, 0)),
                  pl.BlockSpec((1, HALO, LANES), lambda i, *_: (i, 0, 0)),
                  pl.BlockSpec((tm, D_MODEL), lambda i, *_: (i, 0)),
                  _mod_spec(layer, mod_index),
                  pl.BlockSpec(memory_space=pl.ANY)],
        out_specs=pl.BlockSpec((tm, D_MODEL), lambda i, *_: (i, 0)),
        scratch_shapes=[pltpu.VMEM((2, MOE_NCH, MOE_CHUNK, D_MODEL), BF16),
                        pltpu.SemaphoreType.DMA((2,))],
    )
    return pl.pallas_call(
        functools.partial(_combine_kernel, tm=tm, n_tiles=n_tiles),
        grid_spec=grid_spec,
        out_shape=jax.ShapeDtypeStruct((n, D_MODEL), F32),
        compiler_params=pltpu.CompilerParams(dimension_semantics=("arbitrary",),
                                             vmem_limit_bytes=VMEM_LIMIT),
        name="moe_combine",
    )(tabs["nch"], tabs["dst"], route, tabs["loc"], x2d, mod,
      ybuf.reshape(ybuf.shape[0] // MOE_CHUNK, MOE_CHUNK, D_MODEL))


def _moe_rows(n_tok):
    n_tiles = n_tok // MOE_T
    chunks = 2 * n_tok // MOE_CHUNK + (MOE_CHUNK - 1) * n_tiles + N_EXPERTS * (MOE_CPW - 1)
    return -(-chunks // MOE_CPW) * MOE_WORK


def _route_tables(counts, n_tok):
    n_work = _moe_rows(n_tok) // MOE_WORK
    cnt = counts[:, 0, ROUTE_SEL_LANE:ROUTE_SEL_LANE + N_EXPERTS].astype(jnp.int32)
    n_tiles = cnt.shape[0]
    nchunk = (cnt + MOE_CHUNK - 1) // MOE_CHUNK
    ei = jnp.arange(N_EXPERTS, dtype=jnp.int32)
    ti = jnp.arange(n_tiles, dtype=jnp.int32)
    upto_e = (ei[:, None] <= ei[None, :]).astype(jnp.int32)
    upto_t = (ti[None, :] <= ti[:, None]).astype(jnp.int32)
    ends = (nchunk[:, :, None] * upto_e[None, :, :]).sum(axis=1)
    loc = ends - nchunk
    nch = nchunk.sum(axis=1)
    per_e = nchunk.sum(axis=0)
    padded_e = (per_e + MOE_CPW - 1) // MOE_CPW * MOE_CPW
    e_end = (padded_e[:, None] * upto_e).sum(axis=0)
    e_base = e_end - padded_e
    glob = e_base[None, :] + (upto_t[:, :, None] * nchunk[None, :, :]).sum(axis=1) - nchunk
    j = jnp.arange(MOE_NCH, dtype=jnp.int32)
    e_of_j = (ends[:, None, :] <= j[None, :, None]).sum(axis=-1)
    seg = e_of_j[:, :, None] == ei[None, None, :]
    dst = jnp.where(seg, (glob - loc)[:, None, :], 0).sum(axis=-1) + j[None, :]
    dst = jnp.where(j[None, :] < nch[:, None], dst, 0)
    n_used = padded_e.sum() // MOE_CPW
    w = jnp.minimum(jnp.arange(n_work, dtype=jnp.int32), n_used - 1)
    tile_expert = jnp.minimum(((e_end // MOE_CPW)[None, :] <= w[:, None]).sum(axis=-1), N_EXPERTS - 1)
    loc_rows = jnp.pad((loc * MOE_CHUNK).astype(F32),
                       ((0, 0), (ROUTE_SEL_LANE, LANES - ROUTE_SEL_LANE - N_EXPERTS)))
    loc_rows = jnp.broadcast_to(loc_rows[:, None, :], (n_tiles, HALO, LANES))
    return {"nch": nch.astype(jnp.int32), "dst": dst.reshape(-1).astype(jnp.int32),
            "pad_lo": (e_base + per_e).astype(jnp.int32), "pad_hi": e_end.astype(jnp.int32),
            "tile_expert": tile_expert.astype(jnp.int32), "n_used": n_used.astype(jnp.int32).reshape(1),
            "loc": loc_rows}


def _moe_call(h2, route, counts, x2d, mod, w_up, w_down, layer, *, mod_index):
    n = x2d.shape[0]
    tabs = _route_tables(counts, n)
    xbuf = _dispatch_call(h2, route, tabs, tm=MOE_T, n_rows=_moe_rows(n))
    ybuf = _expert_call(xbuf, tabs, w_up, w_down, layer)
    return _combine_call(ybuf, route, x2d, mod, layer, tabs, tm=MOE_T, mod_index=mod_index)


def _ones_blocks(sizes):
    n = sum(sizes)
    m = np.zeros((n, n), np.float32)
    o = 0
    for s in sizes:
        m[o:o + s, o:o + s] = 1.0
        o += s
    return jnp.asarray(m, BF16)


def _inv_counts_q():
    return jnp.asarray(np.tile(np.concatenate([np.full(64, 1 / 64.0), np.full(64, 1 / 32.0)]), 2)[None, :], F32)


def _rope_tables(rows):
    r = np.repeat(np.arange(rows, dtype=np.float32), GRID_W)
    c = np.tile(np.arange(GRID_W, dtype=np.float32), rows)

    def tab(rot_dim):
        nf = rot_dim // 4
        inv = (np.float32(ROPE_THETA) ** (-np.arange(nf, dtype=np.float32) / np.float32(nf))).astype(np.float32)
        ang = np.concatenate([r[:, None] * inv, c[:, None] * inv], axis=-1).astype(np.float32)
        cos = np.repeat(np.cos(ang), 2, axis=-1)
        sin = np.repeat(np.sin(ang), 2, axis=-1) * np.tile(np.asarray([-1.0, 1.0], np.float32), rot_dim // 2)
        return cos.astype(np.float32), sin.astype(np.float32)

    s = rows * GRID_W
    cc, sc = tab(C_HEAD_DIM)
    cd, sd = tab(D_ROPE)
    one = lambda w: np.ones((s, w), np.float32)
    zero = lambda w: np.zeros((s, w), np.float32)
    tabs = (np.tile(cc, (1, 2)), np.tile(sc, (1, 2)),
            np.concatenate([one(64), cd, one(32)], axis=1), np.concatenate([zero(64), sd, zero(32)], axis=1),
            np.concatenate([cd, one(96)], axis=1), np.concatenate([sd, zero(96)], axis=1))
    return tuple(jnp.asarray(t) for t in tabs)


def _stacked_weights(p):
    (g_attn_norm, w_in, sgu_v_gain, sgu_w, sgu_b, conv_w, gqa_q_gain, gqa_k_gain, mla_qa_gain, mla_w_qb,
     mla_kva_gain, mla_w_kvb, mla_q_gain, mla_k_gain, g_group_out, w_out, g_ffn_norm,
     w_rg, b_rg, w_re, b_re) = p
    depth = w_in.shape[0]


    zeros = lambda *shape: jnp.zeros((depth,) + shape, F32)
    row = lambda a: a[:, None, :]
    w_in_p = jnp.concatenate([w_in, zeros(D_MODEL, IN_W - w_in.shape[2])], axis=2).astype(BF16)
    wqb = jnp.concatenate([jnp.concatenate([mla_w_qb[:, :, h * 96:h * 96 + 96], zeros(D_Q_LORA, 32)], axis=2)
                           for h in range(D_HEADS)], axis=2)
    wk = jnp.concatenate([jnp.concatenate([mla_w_kvb[:, :, h * 128:h * 128 + 64], zeros(D_KV_LORA, 64)], axis=2)
                          for h in range(D_HEADS)], axis=2)
    wv = jnp.concatenate([mla_w_kvb[:, :, h * 128 + 64:h * 128 + 128] for h in range(D_HEADS)], axis=2)
    qslab = jnp.concatenate([mla_q_gain, zeros(32)], axis=1)
    kslab = jnp.concatenate([mla_k_gain[:, :64], zeros(64)], axis=1)
    g_go = g_group_out
    w_out_p = w_out.astype(BF16)
    w_r = jnp.concatenate([w_re, w_rg], axis=2)
    w_r_hi = w_r.astype(BF16)
    w_r_lo = (w_r - w_r_hi.astype(F32)).astype(BF16)
    n_r = N_EXPERTS + N_GROUPS
    zb = lambda w: jnp.zeros((depth, D_MODEL, w), BF16)
    w_router = jnp.concatenate([w_r_hi, zb(ROUTE_SEL_LANE - n_r), w_r_lo, zb(LANES - ROUTE_SEL_LANE - n_r)], axis=2)
    w_router = jnp.swapaxes(w_router, 1, 2)
    b_router = jnp.concatenate([b_re, b_rg, zeros(ROUTE_SEL_LANE - n_r)], axis=1)
    b_router = jnp.broadcast_to(b_router[:, :, None], (depth, ROUTE_SEL_LANE, MERGE_ROWS))
    sgub_full = jnp.repeat(jnp.swapaxes(sgu_b, 1, 2), 64, axis=2)
    return {
        "g_attn": row(g_attn_norm), "w_in": w_in_p,
        "sgu_w": sgu_w.reshape(depth, A_HEADS * CHUNK, CHUNK).astype(BF16), "sgu_b": sgub_full,
        "v_gain": row(sgu_v_gain), "conv_w": conv_w, "g_go": row(g_go),
        "cq_gain": row(jnp.tile(gqa_q_gain, (1, 4))), "ck_gain": row(jnp.tile(gqa_k_gain, (1, 2))),
        "qa_gain": row(mla_qa_gain), "w_qb": wqb.astype(BF16), "kva_gain": row(mla_kva_gain),
        "w_kvb": jnp.concatenate([wk, wv], axis=2).astype(BF16),
        "dq_gain": row(jnp.tile(qslab, (1, 4))), "dk_gain": row(jnp.tile(kslab, (1, 4))),
        "dkr_gain": row(jnp.concatenate([mla_k_gain[:, 64:], zeros(96)], axis=1)),
        "w_out": w_out_p, "g_ffn": row(g_ffn_norm), "w_router": w_router, "b_router": b_router,
    }


def kernel(x, c, ctx, c_ctx, w_mod, b_mod, g_attn_norm, w_in, sgu_v_gain, sgu_w, sgu_b, conv_w, gqa_q_gain,
           gqa_k_gain, mla_qa_gain, mla_w_qb, mla_kva_gain, mla_w_kvb, mla_q_gain, mla_k_gain, g_group_out,
           w_out, g_ffn_norm, w_router_group, b_router_group, w_router_expert, b_router_expert,
           w_expert_up, w_expert_down):
    b, s, d = x.shape
    lc = ctx.shape[1]
    depth = w_mod.shape[0]
    lw = _stacked_weights((g_attn_norm, w_in, sgu_v_gain, sgu_w, sgu_b, conv_w, gqa_q_gain, gqa_k_gain,
                           mla_qa_gain, mla_w_qb, mla_kva_gain, mla_w_kvb, mla_q_gain, mla_k_gain, g_group_out,
                           w_out, g_ffn_norm, w_router_group, b_router_group, w_router_expert, b_router_expert))
    w_up = w_expert_up.reshape(depth * N_EXPERTS, d, 2 * D_EXPERT)
    w_down = w_expert_down.reshape(depth * N_EXPERTS, D_EXPERT, d)

    c_all = jnp.concatenate([c, c_ctx[None, :], jnp.zeros((16 - b - 1, d), F32)], axis=0)
    mod = _modulation(c_all, w_mod, b_mod).reshape(depth, 16, 6, d)
    rope = _rope_tables(s // GRID_W)

    tm = MOE_T
    tq = 512
    tpb = s // tm
    lat_idx = lambda i: i // tpb
    ctx_idx = lambda i: b
    r3 = lambda a, n: a.reshape(b, n, a.shape[-1])

    xl = x.reshape(b * s, d)
    xc = ctx.reshape(b * lc, d)
    for l in range(depth):
        update_ctx = l < depth - 1
        ya_c, yb_c, qc_c, kc_c, vc_c, qd_c, kd_c, vd_c = _in_call(
            xc, mod, lw, l, None, tm=tm, seq_len=lc, mod_index=ctx_idx)
        ya, yb, qc, kc, vc, qd, kd, vd = _in_call(xl, mod, lw, l, rope, tm=IN_T, seq_len=s,
                                                  mod_index=lambda i: i // (s // IN_T))
        yc = _attn_call(r3(qc, s), [r3(kc_c, lc), r3(kc, s)], [r3(vc_c, lc), r3(vc, s)], lw["g_go"], l,
                        mode="C", tq=tq)
        yd = _attn_call(r3(qd, s), [r3(kd_c, lc), r3(kd, s)], [r3(vd_c, lc), r3(vd, s)], lw["g_go"], l,
                        mode="D", tq=tq)
        xl, h2, route, counts = _merge_call(xl, ya, yb, yc.reshape(b * s, GROUP_W), yd.reshape(b * s, GROUP_W),
                                            mod, lw, l, tm=tm, mod_index=lat_idx)
        xl = _moe_call(h2, route, counts, xl, mod, w_up, w_down, l, mod_index=lat_idx)
        if update_ctx:
            yc_c = _attn_call(r3(qc_c, lc), [r3(kc_c, lc)], [r3(vc_c, lc)], lw["g_go"], l, mode="C", tq=lc)
            yd_c = _attn_call(r3(qd_c, lc), [r3(kd_c, lc)], [r3(vd_c, lc)], lw["g_go"], l, mode="D", tq=lc)
            xc, h2c, route_c, counts_c = _merge_call(xc, ya_c, yb_c, yc_c.reshape(b * lc, GROUP_W),
                                                     yd_c.reshape(b * lc, GROUP_W), mod, lw, l, tm=tm,
                                                     mod_index=ctx_idx)
            xc = _moe_call(h2c, route_c, counts_c, xc, mod, w_up, w_down, l, mod_index=ctx_idx)
    return xl.reshape(b, s, d)
```

```python
import functools

import numpy as np
import jax
import jax.numpy as jnp
from jax import lax
from jax.experimental import pallas as pl
from jax.experimental.pallas import tpu as pltpu

F32 = jnp.float32
BF16 = jnp.bfloat16

D_MODEL = 1024
GRID_W = 64
EPS = 1e-6
ROPE_THETA = 10000.0
GROUP_W = 256
CHUNK = 128
A_HEADS = 4
C_HEAD_DIM = 64
C_SCALE = C_HEAD_DIM ** -0.5
D_HEADS = 4
D_NOPE = 64
D_ROPE = 32
D_V = 64
D_Q_LORA = 256
D_KV_LORA = 128
D_SCALE = (D_NOPE + D_ROPE) ** -0.5
LOG2E = 1.4426950408889634
ATTN_ROWS = 16
MERGE_ROWS = 128
N_GROUPS = 4
EXPERTS_PER_GROUP = 4
N_EXPERTS = 16
D_EXPERT = 256
LANES = 128
HALO = 8

IN_W = 2304
IN_T = 512
IN_SECTIONS = ((0, 256), (256, 512), (512, 768), (768, 1024), (1024, 1280), (1280, 1536), (1536, 1664),
               (1664, 1792), (1792, 2048), (2048, 2176), (2176, 2304))

ROUTE_SEL_LANE = 32
MOE_T = 512
MOE_CHUNK = 16
MOE_RLOC = 1280
MOE_NCH = MOE_RLOC // MOE_CHUNK
MOE_RBLK = 256
MOE_XW = D_MODEL + LANES
MOE_WORK = 1024
ATTN_TQ = 512
VMEM_LIMIT = 56 * 1024 * 1024


def _nt_dot(a, b):
    return lax.dot_general(a, b, (((1,), (1,)), ((), ())), preferred_element_type=F32)


def _dot(a, b):
    return jnp.dot(a, b, preferred_element_type=F32)


def _rms_full(x, gain):
    return x * lax.rsqrt(jnp.mean(x * x, axis=-1, keepdims=True) + EPS) * gain


def _group_ms(x, ones_blocks, inv_count):
    return _dot((x * x).astype(BF16), ones_blocks) * inv_count


def _pair_swap(x, even):
    nxt = pltpu.roll(x, x.shape[1] - 1, axis=1)
    prv = pltpu.roll(x, 1, axis=1)
    return jnp.where(even, nxt, prv)


def _rope(x, cos, sin, even):
    return x * cos + _pair_swap(x, even) * sin


def _mod_kernel(c_ref, w_ref, b_ref, o_ref):
    cv = c_ref[...]
    sc = cv * jax.nn.sigmoid(cv)
    o_ref[0] = jnp.dot(sc, w_ref[0], preferred_element_type=F32,
                       precision=lax.Precision.HIGHEST) + b_ref[0]


def _modulation(c_all, w_mod, b_mod):
    depth = w_mod.shape[0]
    nblk = 4
    wblk = 6 * D_MODEL // nblk
    return pl.pallas_call(
        _mod_kernel,
        grid=(depth, nblk),
        in_specs=[pl.BlockSpec((16, D_MODEL), lambda l, j: (0, 0)),
                  pl.BlockSpec((1, D_MODEL, wblk), lambda l, j: (l, 0, j)),
                  pl.BlockSpec((1, 1, wblk), lambda l, j: (l, 0, j))],
        out_specs=pl.BlockSpec((1, 16, wblk), lambda l, j: (l, 0, j)),
        out_shape=jax.ShapeDtypeStruct((depth, 16, 6 * D_MODEL), F32),
        compiler_params=pltpu.CompilerParams(dimension_semantics=("arbitrary", "arbitrary"),
                                             vmem_limit_bytes=VMEM_LIMIT),
        name="mod",
    )(c_all, w_mod, b_mod.reshape(depth, 1, 6 * D_MODEL))


def _in_kernel(*refs, tm, seq_len, use_rope):
    n_io = len(refs) - 4
    pj0_ref, pj1_ref, zh0_ref, zh1_ref = refs[n_io:]
    i = pl.program_id(0)

    @pl.when(i == 0)
    def _():
        pj1_ref[...] = jnp.zeros(pj1_ref.shape, F32)
        zh1_ref[...] = jnp.zeros(zh1_ref.shape, F32)

    @pl.when(i % 2 == 0)
    def _():
        _in_stages(refs[:n_io], pj0_ref, zh0_ref, pj1_ref, zh1_ref, tm=tm, seq_len=seq_len, use_rope=use_rope)

    @pl.when(i % 2 == 1)
    def _():
        _in_stages(refs[:n_io], pj1_ref, zh1_ref, pj0_ref, zh0_ref, tm=tm, seq_len=seq_len, use_rope=use_rope)


def _in_stages(refs, pj_w, zh_w, pj_r, zh_r, *, tm, seq_len, use_rope):
    (x_ref, xp_ref, xn_ref, mod_ref, gattn_ref, win_ref, sguw_ref, sgub_ref, vgain_ref, convw_ref,
     ggo_ref, cqg_ref, ckg_ref, qag_ref, wqb_ref, kvag_ref, wkvb_ref, dqg_ref, dkg_ref, dkrg_ref,
     b64_ref, bq_ref, invq_ref) = refs[:23]
    pos = 23
    if use_rope:
        cosc_ref, sinc_ref, cosq_ref, sinq_ref, cosk_ref, sink_ref = refs[pos:pos + 6]
        pos += 6
    ya_ref, yb_ref, qc_ref, kc_ref, vc_ref, qd_ref, kd_ref, vd_ref = refs[pos:pos + 8]

    shift = mod_ref[0, 0:1, :]
    scale = mod_ref[0, 1:2, :]
    gattn = gattn_ref[...]

    def prenorm(xv):
        return (_rms_full(xv, gattn) * (1.0 + scale) + shift).astype(BF16)

    hb = prenorm(x_ref[...])

    pending = list(IN_SECTIONS)

    def project(count):
        for _ in range(min(count, len(pending))):
            lo, hi = pending.pop(0)
            pj_w[:, lo:hi] = _dot(hb, win_ref[:, lo:hi])

    project(1)
    hh = prenorm(jnp.concatenate([xp_ref[...], xn_ref[...]], axis=0))
    zh_w[...] = _dot(hh, win_ref[:, 768:1024]) * _dot(hh, win_ref[:, 1024:1280])

    i = jnp.maximum(pl.program_id(0) - 1, 0)
    proj = {ab: pj_r[:, ab[0]:ab[1]] for ab in IN_SECTIONS}
    zh = zh_r[...]
    b64 = b64_ref[...]
    inv64 = 1.0 / 64.0
    lane128 = lax.broadcasted_iota(jnp.int32, (tm, LANES), 1)
    even = (lane128 & 1) == 0

    u = jax.nn.gelu(proj[(0, 256)])
    v = jax.nn.gelu(proj[(256, 512)])
    project(1)
    v = (v * lax.rsqrt(_group_ms(v, b64, inv64) + EPS) * vgain_ref[...]).astype(BF16)
    project(1)
    lane_head = lax.broadcasted_iota(jnp.int32, (CHUNK, GROUP_W), 1) // 64
    sguw = sguw_ref[...]
    mixed_chunks = []
    for ci in range(tm // CHUNK):
        r = _dot(sguw, v[ci * CHUNK:(ci + 1) * CHUNK, :])
        m = r[0:CHUNK]
        for h in range(1, A_HEADS):
            m = jnp.where(lane_head == h, r[h * CHUNK:(h + 1) * CHUNK], m)
        mixed_chunks.append(m + sgub_ref[...])
    ya = u * jnp.concatenate(mixed_chunks, axis=0)
    ya_ref[...] = _rms_full(ya, ggo_ref[:, 0:256]).astype(BF16)

    project(1)
    bg = proj[(512, 768)]
    z = proj[(768, 1024)] * proj[(1024, 1280)]
    row = lax.broadcasted_iota(jnp.int32, (tm, GROUP_W), 0)
    posn = (i * tm + row) % seq_len
    z_prev = jnp.where(row == 0, zh[HALO - 1:HALO, :], pltpu.roll(z, 1, axis=0))
    z_prev = jnp.where(posn == 0, 0.0, z_prev)
    z_next = jnp.where(row == tm - 1, zh[HALO:HALO + 1, :], pltpu.roll(z, tm - 1, axis=0))
    z_next = jnp.where(posn == seq_len - 1, 0.0, z_next)
    yb = bg * (convw_ref[0:1, :] * z_prev + convw_ref[1:2, :] * z + convw_ref[2:3, :] * z_next)
    yb_ref[...] = _rms_full(yb, ggo_ref[:, 256:512]).astype(BF16)

    project(1)
    q = proj[(1280, 1536)]
    q = q * lax.rsqrt(_group_ms(q, b64, inv64) + EPS) * cqg_ref[...]
    project(1)
    k = proj[(1536, 1664)]
    k = k * lax.rsqrt(_group_ms(k, b64[0:128, 0:128], inv64) + EPS) * ckg_ref[...]
    if use_rope:
        cosc, sinc = cosc_ref[...], sinc_ref[...]
        q = jnp.concatenate([_rope(q[:, 0:128], cosc, sinc, even),
                             _rope(q[:, 128:256], cosc, sinc, even)], axis=1)
        k = _rope(k, cosc, sinc, even)
    qc_ref[...] = (q * (C_SCALE * LOG2E)).astype(BF16)
    first_half = lane128 < 64

    def per_slab(a):
        swapped = pltpu.roll(a, 64, axis=1)
        return jnp.concatenate([jnp.where(first_half, a, swapped), jnp.where(first_half, swapped, a)], axis=1)

    kc_ref[...] = per_slab(k).astype(BF16)
    vc_ref[...] = per_slab(proj[(1664, 1792)]).astype(BF16)

    project(1)
    cq = _rms_full(proj[(1792, 2048)], qag_ref[...]).astype(BF16)
    project(1)
    qd = _dot(cq, wqb_ref[...])
    bq = bq_ref[...]
    invq = invq_ref[...]
    halves = []
    for hf in range(2):
        qh = qd[:, hf * 256:(hf + 1) * 256]
        qh = qh * lax.rsqrt(_group_ms(qh, bq, invq) + EPS) * dqg_ref[:, hf * 256:(hf + 1) * 256]
        if use_rope:
            cosq, sinq = cosq_ref[...], sinq_ref[...]
            qh = jnp.concatenate([_rope(qh[:, 0:128], cosq, sinq, even),
                                  _rope(qh[:, 128:256], cosq, sinq, even)], axis=1)
        halves.append(qh)
    qd_ref[...] = (jnp.concatenate(halves, axis=1) * (D_SCALE * LOG2E)).astype(BF16)

    project(1)
    ckv = _rms_full(proj[(2048, 2176)], kvag_ref[...]).astype(BF16)
    project(1)
    kv = _dot(ckv, wkvb_ref[...])
    vd_ref[...] = kv[:, 512:768].astype(BF16)
    project(len(pending))
    kr = proj[(2176, 2304)]
    kr = kr * lax.rsqrt(jnp.sum(kr * kr, axis=-1, keepdims=True) * (1.0 / D_ROPE) + EPS) * dkrg_ref[...]
    if use_rope:
        kr = _rope(kr, cosk_ref[...], sink_ref[...], even)
    kr = pltpu.roll(kr, D_NOPE, axis=1)
    for hf in range(2):
        kh = kv[:, hf * 256:(hf + 1) * 256]
        kh = kh * lax.rsqrt(_group_ms(kh, b64, inv64) + EPS) * dkg_ref[:, hf * 256:(hf + 1) * 256]
        kd_ref[:, hf * 256:hf * 256 + 128] = (kh[:, 0:128] + kr).astype(BF16)
        kd_ref[:, hf * 256 + 128:(hf + 1) * 256] = (kh[:, 128:256] + kr).astype(BF16)


def _layer_spec(a, layer):
    zeros = (0,) * (a.ndim - 1)
    return pl.BlockSpec((None,) + a.shape[1:], lambda *_: (layer,) + zeros)


def _mod_spec(layer, mod_index):
    return pl.BlockSpec((None, 1, 6, D_MODEL), lambda i, *_: (layer, mod_index(i), 0, 0))


def _in_call(x2d, mod, lw, layer, rope, *, tm, seq_len, mod_index):
    n = x2d.shape[0]
    nt = n // tm
    nb8 = n // HALO
    tb = tm // HALO
    use_rope = rope is not None
    cur = lambda i: jnp.minimum(i, nt - 1)
    prev = lambda i: jnp.maximum(i - 1, 0)
    row = lambda i: (prev(i), 0)
    in_specs = [
        pl.BlockSpec((tm, D_MODEL), lambda i: (cur(i), 0)),
        pl.BlockSpec((HALO, D_MODEL), lambda i: (jnp.maximum(cur(i) * tb - 1, 0), 0)),
        pl.BlockSpec((HALO, D_MODEL), lambda i: (jnp.minimum((cur(i) + 1) * tb, nb8 - 1), 0)),
        _mod_spec(layer, lambda i: mod_index(cur(i))),
    ]
    args = [x2d, x2d, x2d, mod]
    for name in ("g_attn", "w_in", "sgu_w", "sgu_b", "v_gain", "conv_w", "g_go", "cq_gain", "ck_gain",
                 "qa_gain", "w_qb", "kva_gain", "w_kvb", "dq_gain", "dk_gain", "dkr_gain"):
        in_specs.append(_layer_spec(lw[name], layer))
        args.append(lw[name])
    for a in (_ones_blocks([64] * 4), _ones_blocks([64, 32, 32] * 2), _inv_counts_q()):
        in_specs.append(pl.BlockSpec(a.shape, lambda i: (0, 0)))
        args.append(a)
    if use_rope:
        ntab = seq_len // tm
        for a in rope:
            in_specs.append(pl.BlockSpec((tm, LANES), lambda i: (prev(i) % ntab, 0)))
            args.append(a)
    widths = (256, 256, 256, 256, 256, 512, 512, 256)
    out_specs = [pl.BlockSpec((tm, w), row) for w in widths]
    out_shape = [jax.ShapeDtypeStruct((n, w), BF16) for w in widths]
    return pl.pallas_call(
        functools.partial(_in_kernel, tm=tm, seq_len=seq_len, use_rope=use_rope),
        grid=(nt + 1,),
        in_specs=in_specs,
        out_specs=out_specs,
        out_shape=out_shape,
        scratch_shapes=[pltpu.VMEM((tm, IN_W), F32), pltpu.VMEM((tm, IN_W), F32),
                        pltpu.VMEM((2 * HALO, GROUP_W), F32), pltpu.VMEM((2 * HALO, GROUP_W), F32)],
        compiler_params=pltpu.CompilerParams(dimension_semantics=("arbitrary",),
                                             vmem_limit_bytes=VMEM_LIMIT),
        name="in_proj",
    )(*args)


def _attn_kernel(*refs, mode, nseg, tq):
    q_ref = refs[0]
    k_refs = refs[1:1 + nseg]
    v_refs = refs[1 + nseg:1 + 2 * nseg]
    gain_ref = refs[1 + 2 * nseg]
    o_ref = refs[2 + 2 * nseg]
    p_ref = refs[3 + 2 * nseg]
    nks = [r.shape[1] for r in k_refs]
    offs = [sum(nks[:j]) for j in range(nseg)]
    n_chunks = tq // ATTN_ROWS

    vs = [r[0] for r in v_refs]
    if mode == "C":
        first_half = lax.broadcasted_iota(jnp.int32, (tq, LANES), 1) < 64
        heads = []
        for sl in range(2):
            qs = q_ref[0, :, sl * 128:(sl + 1) * 128]
            ks_sl = [r[0, :, sl * 128:(sl + 1) * 128] for r in k_refs]
            heads.append((jnp.where(first_half, qs, jnp.zeros_like(qs)), ks_sl))
            heads.append((jnp.where(first_half, jnp.zeros_like(qs), qs), ks_sl))
    else:
        heads = [(q_ref[0, :, h * 128:(h + 1) * 128], [r[0, :, h * 128:(h + 1) * 128] for r in k_refs])
                 for h in range(D_HEADS)]

    def score(h):
        qm, ks = heads[h]
        return [_nt_dot(qm, kk) for kk in ks]

    def softmax(h, ss):
        for c in range(n_chunks):
            rows = slice(c * ATTN_ROWS, (c + 1) * ATTN_ROWS)
            m = ss[0][rows].max(axis=-1, keepdims=True)
            for s in ss[1:]:
                m = jnp.maximum(m, s[rows].max(axis=-1, keepdims=True))
            for s, off, nk in zip(ss, offs, nks):
                p_ref[h, rows, off:off + nk] = jnp.exp2(s[rows] - m).astype(BF16)

    if mode == "C":
        ones_lane = [LANES] * 4
        one_col = jnp.where(lax.broadcasted_iota(jnp.int32, (1, LANES), 1) == 0, 1.0, 0.0).astype(BF16)
        v_ext = []
        for sl in range(2):
            ext = [jnp.concatenate([vv[:, sl * 128:(sl + 1) * 128], jnp.broadcast_to(one_col, (vv.shape[0], LANES))],
                                   axis=1) for vv in vs]
            v_ext += [ext, ext]
    else:
        ones_lane = [((h + 1) % D_HEADS) * D_V for h in range(D_HEADS)]
        v_ext = []
        lane_v = lax.broadcasted_iota(jnp.int32, (1, GROUP_W), 1)
        for h in range(D_HEADS):
            one = jnp.where(lane_v == ones_lane[h], 1.0, 0.0).astype(BF16)
            keep = jnp.where(lane_v == ones_lane[h], 0.0, 1.0).astype(BF16)
            v_ext.append([vv * keep + one for vv in vs])

    def values(h):
        o = None
        for off, nk, vv in zip(offs, nks, v_ext[h]):
            part = _dot(p_ref[h, :, off:off + nk], vv)
            o = part if o is None else o + part
        out_w = LANES if mode == "C" else GROUP_W
        return o[:, 0:out_w] / o[:, ones_lane[h]:ones_lane[h] + 1]

    n_heads = len(heads)
    scores = {0: score(0), 1: score(1)}
    outs = []
    for h in range(n_heads):
        softmax(h, scores.pop(h))
        if h + 2 < n_heads:
            scores[h + 2] = score(h + 2)
        if h >= 1:
            outs.append(values(h - 1))
    outs.append(values(n_heads - 1))

    if mode == "C":
        out = jnp.concatenate([jnp.where(first_half, outs[0], outs[1]),
                               jnp.where(first_half, outs[2], outs[3])], axis=1)
    else:
        lane_head = lax.broadcasted_iota(jnp.int32, (tq, GROUP_W), 1) // D_V
        out = outs[0]
        for h in range(1, D_HEADS):
            out = jnp.where(lane_head == h, outs[h], out)
    o_ref[0] = _rms_full(out, gain_ref[...]).astype(BF16)


def _attn_call(q, ks, vs, g_go, layer, *, mode, tq):
    b, s, wq = q.shape
    nseg = len(ks)
    gain_block = 2 if mode == "C" else 3
    in_specs = [pl.BlockSpec((1, tq, wq), lambda bi, qi: (bi, qi, 0))]
    for a in list(ks) + list(vs):
        in_specs.append(pl.BlockSpec((1,) + a.shape[1:], lambda bi, qi: (bi, 0, 0)))
    in_specs.append(pl.BlockSpec((None, 1, GROUP_W), lambda bi, qi: (layer, 0, gain_block)))
    return pl.pallas_call(
        functools.partial(_attn_kernel, mode=mode, nseg=nseg, tq=tq),
        grid=(b, s // tq),
        in_specs=in_specs,
        out_specs=pl.BlockSpec((1, tq, GROUP_W), lambda bi, qi: (bi, qi, 0)),
        out_shape=jax.ShapeDtypeStruct((b, s, GROUP_W), BF16),
        scratch_shapes=[pltpu.VMEM((4, tq, sum(a.shape[1] for a in ks)), BF16)],
        compiler_params=pltpu.CompilerParams(dimension_semantics=("parallel", "parallel"),
                                             vmem_limit_bytes=VMEM_LIMIT),
        name="attn_" + mode,
    )(q, *ks, *vs, g_go)


def _merge_kernel(x_ref, ya_ref, yb_ref, yc_ref, yd_ref, mod_ref, wout_ref, gffn_ref, wr_ref, br_ref,
                  xo_ref, h2_ref, route_ref, cnt_ref, *, tm):
    nb = tm // MERGE_ROWS
    blocks = [slice(j * MERGE_ROWS, (j + 1) * MERGE_ROWS) for j in range(nb)]
    ys = []
    for rs in blocks:
        y = _dot(ya_ref[rs, :], wout_ref[0:256, :])
        y = y + _dot(yb_ref[rs, :], wout_ref[256:512, :])
        y = y + _dot(yc_ref[rs, :], wout_ref[512:768, :])
        y = y + _dot(yd_ref[rs, :], wout_ref[768:1024, :])
        ys.append(y)
    counts = None
    for rs, y in zip(blocks, ys):
        sel = _merge_block(rs, y, x_ref, mod_ref, gffn_ref, wr_ref, br_ref, xo_ref, h2_ref, route_ref)
        part = sel.sum(axis=0, keepdims=True)
        counts = part if counts is None else counts + part
    cnt_ref[0] = jnp.broadcast_to(counts, (HALO, LANES))


def _merge_block(rs, y, x_ref, mod_ref, gffn_ref, wr_ref, br_ref, xo_ref, h2_ref, route_ref):
    rows = rs.stop - rs.start
    xn = x_ref[rs, :] + mod_ref[0, 2:3, :] * y
    xo_ref[rs, :] = xn
    h2 = _rms_full(xn, gffn_ref[...]) * (1.0 + mod_ref[0, 4:5, :]) + mod_ref[0, 3:4, :]
    h2b = h2.astype(BF16)
    h2_ref[rs, :] = h2b

    h2lo = (h2 - h2b.astype(F32)).astype(BF16)
    part = _nt_dot(wr_ref[...], h2b) + _nt_dot(wr_ref[...], h2lo)
    logit = part[0:ROUTE_SEL_LANE] + part[ROUTE_SEL_LANE:2 * ROUTE_SEL_LANE] + br_ref[...]
    row_of = lambda j: logit[j:j + 1, :]
    lg = [row_of(N_EXPERTS + g) for g in range(N_GROUPS)]

    def first_max(vals):
        m = vals[0]
        for v in vals[1:]:
            m = jnp.maximum(m, v)
        idx = jnp.float32(len(vals) - 1)
        for j in range(len(vals) - 2, -1, -1):
            idx = jnp.where(vals[j] == m, jnp.float32(j), idx)
        return m, idx

    gmax, g_idx = first_max(lg)
    g_w = 1.0 / sum(jnp.exp(v - gmax) for v in lg)
    cand = []
    for j in range(EXPERTS_PER_GROUP):
        c = row_of((N_GROUPS - 1) * EXPERTS_PER_GROUP + j)
        for g in range(N_GROUPS - 2, -1, -1):
            c = jnp.where(g_idx == g, row_of(g * EXPERTS_PER_GROUP + j), c)
        cand.append(c)
    m1, i1 = first_max(cand)
    neg = jnp.float32(-jnp.inf)
    m2, i2 = first_max([jnp.where(i1 == j, neg, c) for j, c in enumerate(cand)])
    r = jnp.exp(m2 - m1)
    w1 = g_w / (1.0 + r)
    w2 = g_w * r / (1.0 + r)
    e1 = g_idx * EXPERTS_PER_GROUP + i1
    e2 = g_idx * EXPERTS_PER_GROUP + i2
    eio = lax.broadcasted_iota(jnp.int32, (N_EXPERTS, rows), 0).astype(F32)
    cmb_t = jnp.where(eio == e1, w1, jnp.where(eio == e2, w2, 0.0))
    sel_t = jnp.where((eio == e1) | (eio == e2), 1.0, 0.0)
    zeros = lambda n: jnp.zeros((n, rows), F32)
    route_t = jnp.concatenate([cmb_t, zeros(ROUTE_SEL_LANE - N_EXPERTS), sel_t,
                               zeros(LANES - ROUTE_SEL_LANE - N_EXPERTS)], axis=0)
    route = route_t.T
    route_ref[rs, :] = route
    lane = lax.broadcasted_iota(jnp.int32, (rows, LANES), 1)
    return jnp.where(lane >= ROUTE_SEL_LANE, route, 0.0)


def _merge_call(x2d, ya, yb, yc, yd, mod, lw, layer, *, tm, mod_index):
    n = x2d.shape[0]
    row = lambda i: (i, 0)
    in_specs = [pl.BlockSpec((tm, D_MODEL), row)]
    in_specs += [pl.BlockSpec((tm, GROUP_W), row)] * 4
    in_specs += [_mod_spec(layer, mod_index)]
    in_specs += [_layer_spec(lw[k], layer) for k in ("w_out", "g_ffn", "w_router", "b_router")]
    return pl.pallas_call(
        functools.partial(_merge_kernel, tm=tm),
        grid=(n // tm,),
        in_specs=in_specs,
        out_specs=[pl.BlockSpec((tm, D_MODEL), row), pl.BlockSpec((tm, D_MODEL), row),
                   pl.BlockSpec((tm, LANES), row), pl.BlockSpec((1, HALO, LANES), lambda i: (i, 0, 0))],
        out_shape=[jax.ShapeDtypeStruct((n, D_MODEL), F32), jax.ShapeDtypeStruct((n, D_MODEL), BF16),
                   jax.ShapeDtypeStruct((n, LANES), F32),
                   jax.ShapeDtypeStruct((n // tm, HALO, LANES), F32)],
        compiler_params=pltpu.CompilerParams(dimension_semantics=("parallel",),
                                             vmem_limit_bytes=VMEM_LIMIT),
        name="merge",
    )(x2d, ya, yb, yc, yd, mod, lw["w_out"], lw["g_ffn"], lw["w_router"], lw["b_router"])


def _segment_positions(route, loc, tm):
    lane = lax.broadcasted_iota(jnp.int32, (tm, LANES), 1)
    is_sel = (lane >= ROUTE_SEL_LANE) & (lane < ROUTE_SEL_LANE + N_EXPERTS)
    selm = is_sel & (route > 0.5)
    selb = jnp.where(selm, 1.0, 0.0).astype(BF16)
    r_i = lax.broadcasted_iota(jnp.int32, (tm, tm), 0)
    c_i = lax.broadcasted_iota(jnp.int32, (tm, tm), 1)
    earlier = jnp.where(c_i < r_i, 1.0, 0.0).astype(BF16)
    dest = loc + _dot(earlier, selb)
    big = jnp.float32(1e9)
    d_a = jnp.where(selm, dest, big).min(axis=-1, keepdims=True)
    d_b = jnp.where(selm, dest, -big).max(axis=-1, keepdims=True)
    return selm, dest, d_a, d_b


def _to_row(col, tm):
    r_i = lax.broadcasted_iota(jnp.int32, (tm, tm), 0)
    c_i = lax.broadcasted_iota(jnp.int32, (tm, tm), 1)
    return jnp.where(r_i == c_i, col, 0.0).sum(axis=0, keepdims=True)


def _chunk_copy_out(xs_ref, xbuf_ref, sem_ref, slot, src_chunk, dst_chunk):
    return pltpu.make_async_copy(xs_ref.at[slot, src_chunk], xbuf_ref.at[dst_chunk], sem_ref.at[slot])


def _dispatch_kernel(nch_ref, dst_ref, plo_ref, phi_ref, nused_ref, h2_ref, route_ref, loc_ref, xbuf_ref,
                     xs_ref, sem_ref, *, tm, n_tiles, n_work, cpw):
    i = pl.program_id(0)
    slot = i % 2
    zero_chunk = MOE_RLOC // MOE_CHUNK

    def start_tile(t_idx, slot_):
        def body(j, carry):
            _chunk_copy_out(xs_ref, xbuf_ref, sem_ref, slot_, j, dst_ref[t_idx * MOE_NCH + j]).start()
            return carry
        lax.fori_loop(0, nch_ref[t_idx], body, 0)

    def wait_tile(t_idx, slot_):
        def body(j, carry):
            _chunk_copy_out(xs_ref, xbuf_ref, sem_ref, slot_, 0, 0).wait()
            return carry
        lax.fori_loop(0, nch_ref[t_idx], body, 0)

    @pl.when(i == 0)
    def _():
        xs_ref[2] = jnp.zeros(xs_ref.shape[1:], BF16)

        def pad_start(j, carry):
            _chunk_copy_out(xs_ref, xbuf_ref, sem_ref, 2, zero_chunk, j).start()
            return carry

        def pad_wait(j, carry):
            _chunk_copy_out(xs_ref, xbuf_ref, sem_ref, 2, zero_chunk, 0).wait()
            return carry
        for e in range(N_EXPERTS):
            lax.fori_loop(plo_ref[e], phi_ref[e], pad_start, 0)
        for e in range(N_EXPERTS):
            lax.fori_loop(plo_ref[e], phi_ref[e], pad_wait, 0)

        def tail_copy(w):
            return pltpu.make_async_copy(xs_ref.at[2, pl.ds(0, cpw)],
                                         xbuf_ref.at[pl.ds(w * cpw, cpw)], sem_ref.at[2])

        def tail_start(w, carry):
            tail_copy(w).start()
            return carry

        def tail_wait(w, carry):
            tail_copy(w).wait()
            return carry
        lax.fori_loop(nused_ref[0], n_work, tail_start, 0)
        lax.fori_loop(nused_ref[0], n_work, tail_wait, 0)

    @pl.when(i >= 2)
    def _():
        wait_tile(i - 2, slot)

    route = route_ref[...]
    selm, dest, d_a, d_b = _segment_positions(route, loc_ref[0, 0:1, :], tm)
    cmb_al = pltpu.roll(route, ROUTE_SEL_LANE, axis=1)
    w_a = jnp.where(selm & (dest == d_a), cmb_al, 0.0).sum(axis=-1, keepdims=True)
    w_b = jnp.where(selm & (dest == d_b), cmb_al, 0.0).sum(axis=-1, keepdims=True)
    da_r, db_r, wa_r, wb_r = (_to_row(v, tm) for v in (d_a, d_b, w_a, w_b))
    h2 = h2_ref[...]
    lane = lax.broadcasted_iota(jnp.int32, (MOE_RBLK, LANES), 1)
    for r0 in range(0, MOE_RLOC, MOE_RBLK):
        rio = (lax.broadcasted_iota(jnp.int32, (MOE_RBLK, tm), 0) + r0).astype(F32)
        hit_a = rio == da_r
        hit_b = rio == db_r
        perm = jnp.where(hit_a | hit_b, 1.0, 0.0).astype(BF16)
        c0, nc = r0 // MOE_CHUNK, MOE_RBLK // MOE_CHUNK
        xs_ref[slot, c0:c0 + nc, :, 0:D_MODEL] = _dot(perm, h2).astype(BF16).reshape(nc, MOE_CHUNK, D_MODEL)
        w_sorted = (jnp.where(hit_a, wa_r, 0.0) + jnp.where(hit_b, wb_r, 0.0)).sum(axis=-1, keepdims=True)
        w_hi = w_sorted.astype(BF16).astype(F32)
        r1 = w_sorted - w_hi
        w_mid = r1.astype(BF16).astype(F32)
        w_lo = r1 - w_mid
        slab = jnp.where(lane == 0, w_hi, jnp.where(lane == 1, w_mid, jnp.where(lane == 2, w_lo, 0.0)))
        xs_ref[slot, c0:c0 + nc, :, D_MODEL:MOE_XW] = slab.astype(BF16).reshape(nc, MOE_CHUNK, LANES)
    start_tile(i, slot)

    @pl.when(i == n_tiles - 1)
    def _():
        if n_tiles > 1:
            wait_tile(i - 1, 1 - slot)
        wait_tile(i, slot)


def _dispatch_call(h2, route, tabs, *, tm, n_rows, work):
    n = h2.shape[0]
    cpw = work // MOE_CHUNK
    n_tiles = n // tm
    grid_spec = pltpu.PrefetchScalarGridSpec(
        num_scalar_prefetch=5,
        grid=(n_tiles,),
        in_specs=[pl.BlockSpec((tm, D_MODEL), lambda i, *_: (i, 0)),
                  pl.BlockSpec((tm, LANES), lambda i, *_: (i, 0)),
                  pl.BlockSpec((1, HALO, LANES), lambda i, *_: (i, 0, 0))],
        out_specs=pl.BlockSpec(memory_space=pl.ANY),
        scratch_shapes=[pltpu.VMEM((3, max(MOE_NCH + 1, cpw), MOE_CHUNK, MOE_XW), BF16),
                        pltpu.SemaphoreType.DMA((3,))],
    )
    xbuf = pl.pallas_call(
        functools.partial(_dispatch_kernel, tm=tm, n_tiles=n_tiles, n_work=n_rows // work, cpw=cpw),
        grid_spec=grid_spec,
        out_shape=jax.ShapeDtypeStruct((n_rows // MOE_CHUNK, MOE_CHUNK, MOE_XW), BF16),
        compiler_params=pltpu.CompilerParams(dimension_semantics=("arbitrary",),
                                             vmem_limit_bytes=VMEM_LIMIT),
        name="moe_dispatch",
    )(tabs["nch"], tabs["dst"], tabs["pad_lo"], tabs["pad_hi"], tabs["n_used"], h2, route, tabs["loc"])
    return xbuf.reshape(n_rows, MOE_XW)


def _expert_kernel(te_ref, nused_ref, x_ref, wup_ref, wdn_ref, y_ref, wup_bf, wdn_bf):
    w = pl.program_id(0)
    used = w < nused_ref[0]

    @pl.when((w == 0) | (te_ref[w] != te_ref[jnp.maximum(w - 1, 0)]))
    def _():
        wup_bf[...] = wup_ref[...].astype(BF16)
        wdn_bf[...] = wdn_ref[...].astype(BF16)

    @pl.when(used)
    def _():
        wt = x_ref[:, D_MODEL:MOE_XW].astype(F32).sum(axis=-1, keepdims=True)
        gu = _dot(x_ref[:, 0:D_MODEL], wup_bf[...])
        g = gu[:, 0:D_EXPERT]
        a = (g * jax.nn.sigmoid(g)) * gu[:, D_EXPERT:2 * D_EXPERT] * wt
        y_ref[...] = _dot(a.astype(BF16), wdn_bf[...]).astype(BF16)

    @pl.when(jnp.logical_not(used))
    def _():
        y_ref[...] = jnp.zeros(y_ref.shape, BF16)


def _expert_call(xbuf, tabs, w_up, w_down, layer, work):
    n_rows = xbuf.shape[0]
    n_work = n_rows // work
    base = layer * N_EXPERTS
    grid_spec = pltpu.PrefetchScalarGridSpec(
        num_scalar_prefetch=2,
        grid=(n_work,),
        in_specs=[pl.BlockSpec((work, MOE_XW), lambda w, te, nu: (jnp.minimum(w, nu[0] - 1), 0)),
                  pl.BlockSpec((None, D_MODEL, 2 * D_EXPERT), lambda w, te, nu: (base + te[w], 0, 0)),
                  pl.BlockSpec((None, D_EXPERT, D_MODEL), lambda w, te, nu: (base + te[w], 0, 0))],
        out_specs=pl.BlockSpec((work, D_MODEL), lambda w, te, nu: (w, 0)),
        scratch_shapes=[pltpu.VMEM((D_MODEL, 2 * D_EXPERT), BF16), pltpu.VMEM((D_EXPERT, D_MODEL), BF16)],
    )
    return pl.pallas_call(
        _expert_kernel,
        grid_spec=grid_spec,
        out_shape=jax.ShapeDtypeStruct((n_rows, D_MODEL), BF16),
        compiler_params=pltpu.CompilerParams(dimension_semantics=("arbitrary",),
                                             vmem_limit_bytes=VMEM_LIMIT),
        name="moe_experts",
    )(tabs["tile_expert"], tabs["n_used"], xbuf, w_up, w_down)


def _chunk_copy_in(ybuf_ref, ys_ref, sem_ref, slot, src_chunk, dst_chunk):
    return pltpu.make_async_copy(ybuf_ref.at[src_chunk], ys_ref.at[slot, dst_chunk], sem_ref.at[slot])


def _combine_kernel(nch_ref, dst_ref, route_ref, loc_ref, x_ref, mod_ref, ybuf_ref, o_ref,
                    ys_ref, sem_ref, *, tm, n_tiles):
    i = pl.program_id(0)
    slot = i % 2

    def start_tile(t_idx, slot_):
        def body(j, carry):
            _chunk_copy_in(ybuf_ref, ys_ref, sem_ref, slot_, dst_ref[t_idx * MOE_NCH + j], j).start()
            return carry
        lax.fori_loop(0, nch_ref[t_idx], body, 0)

    @pl.when(i == 0)
    def _():
        ys_ref[...] = jnp.zeros(ys_ref.shape, BF16)
        start_tile(0, 0)

    @pl.when(i + 1 < n_tiles)
    def _():
        start_tile(i + 1, 1 - slot)

    def wait_body(j, carry):
        _chunk_copy_in(ybuf_ref, ys_ref, sem_ref, slot, 0, 0).wait()
        return carry
    lax.fori_loop(0, nch_ref[i], wait_body, 0)

    _, _, d_a, d_b = _segment_positions(route_ref[...], loc_ref[0, 0:1, :], tm)
    y = None
    for r0 in range(0, MOE_RLOC, MOE_RBLK):
        lio = (lax.broadcasted_iota(jnp.int32, (tm, MOE_RBLK), 1) + r0).astype(F32)
        perm_t = jnp.where((lio == d_a) | (lio == d_b), 1.0, 0.0).astype(BF16)
        c0 = r0 // MOE_CHUNK
        part = _dot(perm_t, ys_ref[slot, c0:c0 + MOE_RBLK // MOE_CHUNK].reshape(MOE_RBLK, D_MODEL))
        y = part if y is None else y + part
    o_ref[...] = x_ref[...] + mod_ref[0, 5:6, :] * y


def _combine_call(ybuf, route, x2d, mod, layer, tabs, *, tm, mod_index):
    n = x2d.shape[0]
    n_tiles = n // tm
    grid_spec = pltpu.PrefetchScalarGridSpec(
        num_scalar_prefetch=2,
        grid=(n_tiles,),
        in_specs=[pl.BlockSpec((tm, LANES), lambda i, *_: (i, 0)),
                  pl.BlockSpec((1, HALO, LANES), lambda i, *_: (i, 0, 0)),
                  pl.BlockSpec((tm, D_MODEL), lambda i, *_: (i, 0)),
                  _mod_spec(layer, mod_index),
                  pl.BlockSpec(memory_space=pl.ANY)],
        out_specs=pl.BlockSpec((tm, D_MODEL), lambda i, *_: (i, 0)),
        scratch_shapes=[pltpu.VMEM((2, MOE_NCH, MOE_CHUNK, D_MODEL), BF16),
                        pltpu.SemaphoreType.DMA((2,))],
    )
    return pl.pallas_call(
        functools.partial(_combine_kernel, tm=tm, n_tiles=n_tiles),
        grid_spec=grid_spec,
        out_shape=jax.ShapeDtypeStruct((n, D_MODEL), F32),
        compiler_params=pltpu.CompilerParams(dimension_semantics=("arbitrary",),
                                             vmem_limit_bytes=VMEM_LIMIT),
        name="moe_combine",
    )(tabs["nch"], tabs["dst"], route, tabs["loc"], x2d, mod,
      ybuf.reshape(ybuf.shape[0] // MOE_CHUNK, MOE_CHUNK, D_MODEL))


def _moe_work(n_tok):
    return MOE_WORK if n_tok >= 8 * MOE_WORK else MOE_WORK // 2


def _moe_rows(n_tok, work):
    n_tiles = n_tok // MOE_T
    cpw = work // MOE_CHUNK
    chunks = 2 * n_tok // MOE_CHUNK + (MOE_CHUNK - 1) * n_tiles + N_EXPERTS * (cpw - 1)
    return -(-chunks // cpw) * work


def _route_tables(counts, n_tok, work):
    cpw = work // MOE_CHUNK
    n_work = _moe_rows(n_tok, work) // work
    cnt = counts[:, 0, ROUTE_SEL_LANE:ROUTE_SEL_LANE + N_EXPERTS].astype(jnp.int32)
    n_tiles = cnt.shape[0]
    nchunk = (cnt + MOE_CHUNK - 1) // MOE_CHUNK
    ei = jnp.arange(N_EXPERTS, dtype=jnp.int32)
    ti = jnp.arange(n_tiles, dtype=jnp.int32)
    upto_e = (ei[:, None] <= ei[None, :]).astype(jnp.int32)
    upto_t = (ti[None, :] <= ti[:, None]).astype(jnp.int32)
    ends = (nchunk[:, :, None] * upto_e[None, :, :]).sum(axis=1)
    loc = ends - nchunk
    nch = nchunk.sum(axis=1)
    per_e = nchunk.sum(axis=0)
    padded_e = (per_e + cpw - 1) // cpw * cpw
    e_end = (padded_e[:, None] * upto_e).sum(axis=0)
    e_base = e_end - padded_e
    glob = e_base[None, :] + (upto_t[:, :, None] * nchunk[None, :, :]).sum(axis=1) - nchunk
    j = jnp.arange(MOE_NCH, dtype=jnp.int32)
    e_of_j = (ends[:, None, :] <= j[None, :, None]).sum(axis=-1)
    seg = e_of_j[:, :, None] == ei[None, None, :]
    dst = jnp.where(seg, (glob - loc)[:, None, :], 0).sum(axis=-1) + j[None, :]
    dst = jnp.where(j[None, :] < nch[:, None], dst, 0)
    n_used = padded_e.sum() // cpw
    w = jnp.minimum(jnp.arange(n_work, dtype=jnp.int32), n_used - 1)
    tile_expert = jnp.minimum(((e_end // cpw)[None, :] <= w[:, None]).sum(axis=-1), N_EXPERTS - 1)
    loc_rows = jnp.pad((loc * MOE_CHUNK).astype(F32),
                       ((0, 0), (ROUTE_SEL_LANE, LANES - ROUTE_SEL_LANE - N_EXPERTS)))
    loc_rows = jnp.broadcast_to(loc_rows[:, None, :], (n_tiles, HALO, LANES))
    return {"nch": nch.astype(jnp.int32), "dst": dst.reshape(-1).astype(jnp.int32),
            "pad_lo": (e_base + per_e).astype(jnp.int32), "pad_hi": e_end.astype(jnp.int32),
            "tile_expert": tile_expert.astype(jnp.int32), "n_used": n_used.astype(jnp.int32).reshape(1),
            "loc": loc_rows}


def _moe_call(h2, route, counts, x2d, mod, w_up, w_down, layer, *, mod_index):
    n = x2d.shape[0]
    work = _moe_work(n)
    tabs = _route_tables(counts, n, work)
    xbuf = _dispatch_call(h2, route, tabs, tm=MOE_T, n_rows=_moe_rows(n, work), work=work)
    ybuf = _expert_call(xbuf, tabs, w_up, w_down, layer, work)
    return _combine_call(ybuf, route, x2d, mod, layer, tabs, tm=MOE_T, mod_index=mod_index)


def _ones_blocks(sizes):
    n = sum(sizes)
    m = np.zeros((n, n), np.float32)
    o = 0
    for s in sizes:
        m[o:o + s, o:o + s] = 1.0
        o += s
    return jnp.asarray(m, BF16)


def _inv_counts_q():
    return jnp.asarray(np.tile(np.concatenate([np.full(64, 1 / 64.0), np.full(64, 1 / 32.0)]), 2)[None, :], F32)


def _rope_tables(rows):
    r = np.repeat(np.arange(rows, dtype=np.float32), GRID_W)
    c = np.tile(np.arange(GRID_W, dtype=np.float32), rows)

    def tab(rot_dim):
        nf = rot_dim // 4
        inv = (np.float32(ROPE_THETA) ** (-np.arange(nf, dtype=np.float32) / np.float32(nf))).astype(np.float32)
        ang = np.concatenate([r[:, None] * inv, c[:, None] * inv], axis=-1).astype(np.float32)
        cos = np.repeat(np.cos(ang), 2, axis=-1)
        sin = np.repeat(np.sin(ang), 2, axis=-1) * np.tile(np.asarray([-1.0, 1.0], np.float32), rot_dim // 2)
        return cos.astype(np.float32), sin.astype(np.float32)

    s = rows * GRID_W
    cc, sc = tab(C_HEAD_DIM)
    cd, sd = tab(D_ROPE)
    one = lambda w: np.ones((s, w), np.float32)
    zero = lambda w: np.zeros((s, w), np.float32)
    tabs = (np.tile(cc, (1, 2)), np.tile(sc, (1, 2)),
            np.concatenate([one(64), cd, one(32)], axis=1), np.concatenate([zero(64), sd, zero(32)], axis=1),
            np.concatenate([cd, one(96)], axis=1), np.concatenate([sd, zero(96)], axis=1))
    return tuple(jnp.asarray(t) for t in tabs)


def _stacked_weights(p):
    (g_attn_norm, w_in, sgu_v_gain, sgu_w, sgu_b, conv_w, gqa_q_gain, gqa_k_gain, mla_qa_gain, mla_w_qb,
     mla_kva_gain, mla_w_kvb, mla_q_gain, mla_k_gain, g_group_out, w_out, g_ffn_norm,
     w_rg, b_rg, w_re, b_re) = p
    depth = w_in.shape[0]


    zeros = lambda *shape: jnp.zeros((depth,) + shape, F32)
    row = lambda a: a[:, None, :]
    w_in_p = jnp.concatenate([w_in, zeros(D_MODEL, IN_W - w_in.shape[2])], axis=2).astype(BF16)
    wqb = jnp.concatenate([jnp.concatenate([mla_w_qb[:, :, h * 96:h * 96 + 96], zeros(D_Q_LORA, 32)], axis=2)
                           for h in range(D_HEADS)], axis=2)
    wk = jnp.concatenate([jnp.concatenate([mla_w_kvb[:, :, h * 128:h * 128 + 64], zeros(D_KV_LORA, 64)], axis=2)
                          for h in range(D_HEADS)], axis=2)
    wv = jnp.concatenate([mla_w_kvb[:, :, h * 128 + 64:h * 128 + 128] for h in range(D_HEADS)], axis=2)
    qslab = jnp.concatenate([mla_q_gain, zeros(32)], axis=1)
    kslab = jnp.concatenate([mla_k_gain[:, :64], zeros(64)], axis=1)
    g_go = g_group_out
    w_out_p = w_out.astype(BF16)
    w_r = jnp.concatenate([w_re, w_rg], axis=2)
    w_r_hi = w_r.astype(BF16)
    w_r_lo = (w_r - w_r_hi.astype(F32)).astype(BF16)
    n_r = N_EXPERTS + N_GROUPS
    zb = lambda w: jnp.zeros((depth, D_MODEL, w), BF16)
    w_router = jnp.concatenate([w_r_hi, zb(ROUTE_SEL_LANE - n_r), w_r_lo, zb(LANES - ROUTE_SEL_LANE - n_r)], axis=2)
    w_router = jnp.swapaxes(w_router, 1, 2)
    b_router = jnp.concatenate([b_re, b_rg, zeros(ROUTE_SEL_LANE - n_r)], axis=1)
    b_router = jnp.broadcast_to(b_router[:, :, None], (depth, ROUTE_SEL_LANE, MERGE_ROWS))
    sgub_full = jnp.repeat(jnp.swapaxes(sgu_b, 1, 2), 64, axis=2)
    return {
        "g_attn": row(g_attn_norm), "w_in": w_in_p,
        "sgu_w": sgu_w.reshape(depth, A_HEADS * CHUNK, CHUNK).astype(BF16), "sgu_b": sgub_full,
        "v_gain": row(sgu_v_gain), "conv_w": conv_w, "g_go": row(g_go),
        "cq_gain": row(jnp.tile(gqa_q_gain, (1, 4))), "ck_gain": row(jnp.tile(gqa_k_gain, (1, 2))),
        "qa_gain": row(mla_qa_gain), "w_qb": wqb.astype(BF16), "kva_gain": row(mla_kva_gain),
        "w_kvb": jnp.concatenate([wk, wv], axis=2).astype(BF16),
        "dq_gain": row(jnp.tile(qslab, (1, 4))), "dk_gain": row(jnp.tile(kslab, (1, 4))),
        "dkr_gain": row(jnp.concatenate([mla_k_gain[:, 64:], zeros(96)], axis=1)),
        "w_out": w_out_p, "g_ffn": row(g_ffn_norm), "w_router": w_router, "b_router": b_router,
    }


def kernel(x, c, ctx, c_ctx, w_mod, b_mod, g_attn_norm, w_in, sgu_v_gain, sgu_w, sgu_b, conv_w, gqa_q_gain,
           gqa_k_gain, mla_qa_gain, mla_w_qb, mla_kva_gain, mla_w_kvb, mla_q_gain, mla_k_gain, g_group_out,
           w_out, g_ffn_norm, w_router_group, b_router_group, w_router_expert, b_router_expert,
           w_expert_up, w_expert_down):
    b, s, d = x.shape
    lc = ctx.shape[1]
    depth = w_mod.shape[0]
    lw = _stacked_weights((g_attn_norm, w_in, sgu_v_gain, sgu_w, sgu_b, conv_w, gqa_q_gain, gqa_k_gain,
                           mla_qa_gain, mla_w_qb, mla_kva_gain, mla_w_kvb, mla_q_gain, mla_k_gain, g_group_out,
                           w_out, g_ffn_norm, w_router_group, b_router_group, w_router_expert, b_router_expert))
    w_up = w_expert_up.reshape(depth * N_EXPERTS, d, 2 * D_EXPERT)
    w_down = w_expert_down.reshape(depth * N_EXPERTS, D_EXPERT, d)

    c_all = jnp.concatenate([c, c_ctx[None, :], jnp.zeros((16 - b - 1, d), F32)], axis=0)
    mod = _modulation(c_all, w_mod, b_mod).reshape(depth, 16, 6, d)
    rope = _rope_tables(s // GRID_W)

    tm = MOE_T
    tq = ATTN_TQ
    tpb = s // tm
    lat_idx = lambda i: i // tpb
    ctx_idx = lambda i: b
    r3 = lambda a, n: a.reshape(b, n, a.shape[-1])

    xl = x.reshape(b * s, d)
    xc = ctx.reshape(b * lc, d)
    for l in range(depth):
        update_ctx = l < depth - 1
        ya_c, yb_c, qc_c, kc_c, vc_c, qd_c, kd_c, vd_c = _in_call(
            xc, mod, lw, l, None, tm=tm, seq_len=lc, mod_index=ctx_idx)
        ya, yb, qc, kc, vc, qd, kd, vd = _in_call(xl, mod, lw, l, rope, tm=IN_T, seq_len=s,
                                                  mod_index=lambda i: i // (s // IN_T))
        yc = _attn_call(r3(qc, s), [r3(kc_c, lc), r3(kc, s)], [r3(vc_c, lc), r3(vc, s)], lw["g_go"], l,
                        mode="C", tq=tq)
        yd = _attn_call(r3(qd, s), [r3(kd_c, lc), r3(kd, s)], [r3(vd_c, lc), r3(vd, s)], lw["g_go"], l,
                        mode="D", tq=tq)
        xl, h2, route, counts = _merge_call(xl, ya, yb, yc.reshape(b * s, GROUP_W), yd.reshape(b * s, GROUP_W),
                                            mod, lw, l, tm=tm, mod_index=lat_idx)
        xl = _moe_call(h2, route, counts, xl, mod, w_up, w_down, l, mod_index=lat_idx)
        if update_ctx:
            yc_c = _attn_call(r3(qc_c, lc), [r3(kc_c, lc)], [r3(vc_c, lc)], lw["g_go"], l, mode="C", tq=lc)
            yd_c = _attn_call(r3(qd_c, lc), [r3(kd_c, lc)], [r3(vd_c, lc)], lw["g_go"], l, mode="D", tq=lc)
            xc, h2c, route_c, counts_c = _merge_call(xc, ya_c, yb_c, yc_c.reshape(b * lc, GROUP_W),
                                                     yd_c.reshape(b * lc, GROUP_W), mod, lw, l, tm=tm,
                                                     mod_index=ctx_idx)
            xc = _moe_call(h2c, route_c, counts_c, xc, mod, w_up, w_down, l, mod_index=ctx_idx)
    return xl.reshape(b, s, d)
```

```python
import functools

import numpy as np
import jax
import jax.numpy as jnp
from jax import lax
from jax.experimental import pallas as pl
from jax.experimental.pallas import tpu as pltpu

F32 = jnp.float32
BF16 = jnp.bfloat16

D_MODEL = 1024
GRID_W = 64
EPS = 1e-6
ROPE_THETA = 10000.0
GROUP_W = 256
CHUNK = 128
A_HEADS = 4
C_HEAD_DIM = 64
C_SCALE = C_HEAD_DIM ** -0.5
D_HEADS = 4
D_NOPE = 64
D_ROPE = 32
D_V = 64
D_Q_LORA = 256
D_KV_LORA = 128
D_SCALE = (D_NOPE + D_ROPE) ** -0.5
LOG2E = 1.4426950408889634
ATTN_ROWS = 16
MERGE_ROWS = 128
N_GROUPS = 4
EXPERTS_PER_GROUP = 4
N_EXPERTS = 16
D_EXPERT = 256
LANES = 128
HALO = 8

IN_W = 2304
IN_T = 512
IN_SECTIONS = ((0, 256), (256, 512), (512, 768), (768, 1024), (1024, 1280), (1280, 1536), (1536, 1664),
               (1664, 1792), (1792, 2048), (2048, 2176), (2176, 2304))

ROUTE_SEL_LANE = 32
MOE_T = 512
MOE_CHUNK = 16
MOE_RLOC = 1280
MOE_NCH = MOE_RLOC // MOE_CHUNK
MOE_RBLK = 256
MOE_XW = D_MODEL + LANES
MOE_WORK = 1024
ATTN_TQ = 512
VMEM_LIMIT = 56 * 1024 * 1024


def _nt_dot(a, b):
    return lax.dot_general(a, b, (((1,), (1,)), ((), ())), preferred_element_type=F32)


def _dot(a, b):
    return jnp.dot(a, b, preferred_element_type=F32)


def _rms_full(x, gain):
    return x * lax.rsqrt(jnp.mean(x * x, axis=-1, keepdims=True) + EPS) * gain


def _group_ms(x, ones_blocks, inv_count):
    return _dot((x * x).astype(BF16), ones_blocks) * inv_count


def _pair_swap(x, even):
    nxt = pltpu.roll(x, x.shape[1] - 1, axis=1)
    prv = pltpu.roll(x, 1, axis=1)
    return jnp.where(even, nxt, prv)


def _rope(x, cos, sin, even):
    return x * cos + _pair_swap(x, even) * sin


def _mod_kernel(c_ref, w_ref, b_ref, o_ref):
    cv = c_ref[...]
    sc = cv * jax.nn.sigmoid(cv)
    o_ref[0] = jnp.dot(sc, w_ref[0], preferred_element_type=F32,
                       precision=lax.Precision.HIGHEST) + b_ref[0]


def _modulation(c_all, w_mod, b_mod):
    depth = w_mod.shape[0]
    nblk = 4
    wblk = 6 * D_MODEL // nblk
    return pl.pallas_call(
        _mod_kernel,
        grid=(depth, nblk),
        in_specs=[pl.BlockSpec((16, D_MODEL), lambda l, j: (0, 0)),
                  pl.BlockSpec((1, D_MODEL, wblk), lambda l, j: (l, 0, j)),
                  pl.BlockSpec((1, 1, wblk), lambda l, j: (l, 0, j))],
        out_specs=pl.BlockSpec((1, 16, wblk), lambda l, j: (l, 0, j)),
        out_shape=jax.ShapeDtypeStruct((depth, 16, 6 * D_MODEL), F32),
        compiler_params=pltpu.CompilerParams(dimension_semantics=("arbitrary", "arbitrary"),
                                             vmem_limit_bytes=VMEM_LIMIT),
        name="mod",
    )(c_all, w_mod, b_mod.reshape(depth, 1, 6 * D_MODEL))


def _in_kernel(*refs, tm, seq_len, use_rope):
    n_io = len(refs) - 4
    pj0_ref, pj1_ref, zh0_ref, zh1_ref = refs[n_io:]
    i = pl.program_id(0)

    @pl.when(i == 0)
    def _():
        pj1_ref[...] = jnp.zeros(pj1_ref.shape, F32)
        zh1_ref[...] = jnp.zeros(zh1_ref.shape, F32)

    @pl.when(i % 2 == 0)
    def _():
        _in_stages(refs[:n_io], pj0_ref, zh0_ref, pj1_ref, zh1_ref, tm=tm, seq_len=seq_len, use_rope=use_rope)

    @pl.when(i % 2 == 1)
    def _():
        _in_stages(refs[:n_io], pj1_ref, zh1_ref, pj0_ref, zh0_ref, tm=tm, seq_len=seq_len, use_rope=use_rope)


def _in_stages(refs, pj_w, zh_w, pj_r, zh_r, *, tm, seq_len, use_rope):
    (x_ref, xp_ref, xn_ref, mod_ref, gattn_ref, win_ref, sguw_ref, sgub_ref, vgain_ref, convw_ref,
     ggo_ref, cqg_ref, ckg_ref, qag_ref, wqb_ref, kvag_ref, wkvb_ref, dqg_ref, dkg_ref, dkrg_ref,
     b64_ref, bq_ref, invq_ref) = refs[:23]
    pos = 23
    if use_rope:
        cosc_ref, sinc_ref, cosq_ref, sinq_ref, cosk_ref, sink_ref = refs[pos:pos + 6]
        pos += 6
    ya_ref, yb_ref, qc_ref, kc_ref, vc_ref, qd_ref, kd_ref, vd_ref = refs[pos:pos + 8]

    shift = mod_ref[0, 0:1, :]
    scale = mod_ref[0, 1:2, :]
    gattn = gattn_ref[...]

    def prenorm(xv):
        return (_rms_full(xv, gattn) * (1.0 + scale) + shift).astype(BF16)

    hb = prenorm(x_ref[...])

    pending = list(IN_SECTIONS)

    def project(count):
        for _ in range(min(count, len(pending))):
            lo, hi = pending.pop(0)
            pj_w[:, lo:hi] = _dot(hb, win_ref[:, lo:hi])

    project(1)
    hh = prenorm(jnp.concatenate([xp_ref[...], xn_ref[...]], axis=0))
    zh_w[...] = _dot(hh, win_ref[:, 768:1024]) * _dot(hh, win_ref[:, 1024:1280])

    i = jnp.maximum(pl.program_id(0) - 1, 0)
    proj = {ab: pj_r[:, ab[0]:ab[1]] for ab in IN_SECTIONS}
    zh = zh_r[...]
    b64 = b64_ref[...]
    inv64 = 1.0 / 64.0
    lane128 = lax.broadcasted_iota(jnp.int32, (tm, LANES), 1)
    even = (lane128 & 1) == 0

    u = jax.nn.gelu(proj[(0, 256)])
    v = jax.nn.gelu(proj[(256, 512)])
    project(1)
    v = (v * lax.rsqrt(_group_ms(v, b64, inv64) + EPS) * vgain_ref[...]).astype(BF16)
    project(1)
    lane_head = lax.broadcasted_iota(jnp.int32, (CHUNK, GROUP_W), 1) // 64
    sguw = sguw_ref[...]
    mixed_chunks = []
    for ci in range(tm // CHUNK):
        r = _dot(sguw, v[ci * CHUNK:(ci + 1) * CHUNK, :])
        m = r[0:CHUNK]
        for h in range(1, A_HEADS):
            m = jnp.where(lane_head == h, r[h * CHUNK:(h + 1) * CHUNK], m)
        mixed_chunks.append(m + sgub_ref[...])
    ya = u * jnp.concatenate(mixed_chunks, axis=0)
    ya_ref[...] = _rms_full(ya, ggo_ref[:, 0:256]).astype(BF16)

    project(1)
    bg = proj[(512, 768)]
    z = proj[(768, 1024)] * proj[(1024, 1280)]
    row = lax.broadcasted_iota(jnp.int32, (tm, GROUP_W), 0)
    posn = (i * tm + row) % seq_len
    z_prev = jnp.where(row == 0, zh[HALO - 1:HALO, :], pltpu.roll(z, 1, axis=0))
    z_prev = jnp.where(posn == 0, 0.0, z_prev)
    z_next = jnp.where(row == tm - 1, zh[HALO:HALO + 1, :], pltpu.roll(z, tm - 1, axis=0))
    z_next = jnp.where(posn == seq_len - 1, 0.0, z_next)
    yb = bg * (convw_ref[0:1, :] * z_prev + convw_ref[1:2, :] * z + convw_ref[2:3, :] * z_next)
    yb_ref[...] = _rms_full(yb, ggo_ref[:, 256:512]).astype(BF16)

    project(1)
    q = proj[(1280, 1536)]
    q = q * lax.rsqrt(_group_ms(q, b64, inv64) + EPS) * cqg_ref[...]
    project(1)
    k = proj[(1536, 1664)]
    k = k * lax.rsqrt(_group_ms(k, b64[0:128, 0:128], inv64) + EPS) * ckg_ref[...]
    if use_rope:
        cosc, sinc = cosc_ref[...], sinc_ref[...]
        q = jnp.concatenate([_rope(q[:, 0:128], cosc, sinc, even),
                             _rope(q[:, 128:256], cosc, sinc, even)], axis=1)
        k = _rope(k, cosc, sinc, even)
    qc_ref[...] = (q * (C_SCALE * LOG2E)).astype(BF16)
    first_half = lane128 < 64

    def per_slab(a):
        swapped = pltpu.roll(a, 64, axis=1)
        return jnp.concatenate([jnp.where(first_half, a, swapped), jnp.where(first_half, swapped, a)], axis=1)

    kc_ref[...] = per_slab(k).astype(BF16)
    vc_ref[...] = per_slab(proj[(1664, 1792)]).astype(BF16)

    project(1)
    cq = _rms_full(proj[(1792, 2048)], qag_ref[...]).astype(BF16)
    project(1)
    qd = _dot(cq, wqb_ref[...])
    bq = bq_ref[...]
    invq = invq_ref[...]
    halves = []
    for hf in range(2):
        qh = qd[:, hf * 256:(hf + 1) * 256]
        qh = qh * lax.rsqrt(_group_ms(qh, bq, invq) + EPS) * dqg_ref[:, hf * 256:(hf + 1) * 256]
        if use_rope:
            cosq, sinq = cosq_ref[...], sinq_ref[...]
            qh = jnp.concatenate([_rope(qh[:, 0:128], cosq, sinq, even),
                                  _rope(qh[:, 128:256], cosq, sinq, even)], axis=1)
        halves.append(qh)
    qd_ref[...] = (jnp.concatenate(halves, axis=1) * (D_SCALE * LOG2E)).astype(BF16)

    project(1)
    ckv = _rms_full(proj[(2048, 2176)], kvag_ref[...]).astype(BF16)
    project(1)
    kv = _dot(ckv, wkvb_ref[...])
    vd_ref[...] = kv[:, 512:768].astype(BF16)
    project(len(pending))
    kr = proj[(2176, 2304)]
    kr = kr * lax.rsqrt(jnp.sum(kr * kr, axis=-1, keepdims=True) * (1.0 / D_ROPE) + EPS) * dkrg_ref[...]
    if use_rope:
        kr = _rope(kr, cosk_ref[...], sink_ref[...], even)
    kr = pltpu.roll(kr, D_NOPE, axis=1)
    for hf in range(2):
        kh = kv[:, hf * 256:(hf + 1) * 256]
        kh = kh * lax.rsqrt(_group_ms(kh, b64, inv64) + EPS) * dkg_ref[:, hf * 256:(hf + 1) * 256]
        kd_ref[:, hf * 256:hf * 256 + 128] = (kh[:, 0:128] + kr).astype(BF16)
        kd_ref[:, hf * 256 + 128:(hf + 1) * 256] = (kh[:, 128:256] + kr).astype(BF16)


def _layer_spec(a, layer):
    zeros = (0,) * (a.ndim - 1)
    return pl.BlockSpec((None,) + a.shape[1:], lambda *_: (layer,) + zeros)


def _mod_spec(layer, mod_index):
    return pl.BlockSpec((None, 1, 6, D_MODEL), lambda i, *_: (layer, mod_index(i), 0, 0))


def _in_call(x2d, mod, lw, layer, rope, *, tm, seq_len, mod_index):
    n = x2d.shape[0]
    nt = n // tm
    nb8 = n // HALO
    tb = tm // HALO
    use_rope = rope is not None
    cur = lambda i: jnp.minimum(i, nt - 1)
    prev = lambda i: jnp.maximum(i - 1, 0)
    row = lambda i: (prev(i), 0)
    in_specs = [
        pl.BlockSpec((tm, D_MODEL), lambda i: (cur(i), 0)),
        pl.BlockSpec((HALO, D_MODEL), lambda i: (jnp.maximum(cur(i) * tb - 1, 0), 0)),
        pl.BlockSpec((HALO, D_MODEL), lambda i: (jnp.minimum((cur(i) + 1) * tb, nb8 - 1), 0)),
        _mod_spec(layer, lambda i: mod_index(cur(i))),
    ]
    args = [x2d, x2d, x2d, mod]
    for name in ("g_attn", "w_in", "sgu_w", "sgu_b", "v_gain", "conv_w", "g_go", "cq_gain", "ck_gain",
                 "qa_gain", "w_qb", "kva_gain", "w_kvb", "dq_gain", "dk_gain", "dkr_gain"):
        in_specs.append(_layer_spec(lw[name], layer))
        args.append(lw[name])
    for a in (_ones_blocks([64] * 4), _ones_blocks([64, 32, 32] * 2), _inv_counts_q()):
        in_specs.append(pl.BlockSpec(a.shape, lambda i: (0, 0)))
        args.append(a)
    if use_rope:
        ntab = seq_len // tm
        for a in rope:
            in_specs.append(pl.BlockSpec((tm, LANES), lambda i: (prev(i) % ntab, 0)))
            args.append(a)
    widths = (256, 256, 256, 256, 256, 512, 512, 256)
    out_specs = [pl.BlockSpec((tm, w), row) for w in widths]
    out_shape = [jax.ShapeDtypeStruct((n, w), BF16) for w in widths]
    return pl.pallas_call(
        functools.partial(_in_kernel, tm=tm, seq_len=seq_len, use_rope=use_rope),
        grid=(nt + 1,),
        in_specs=in_specs,
        out_specs=out_specs,
        out_shape=out_shape,
        scratch_shapes=[pltpu.VMEM((tm, IN_W), F32), pltpu.VMEM((tm, IN_W), F32),
                        pltpu.VMEM((2 * HALO, GROUP_W), F32), pltpu.VMEM((2 * HALO, GROUP_W), F32)],
        compiler_params=pltpu.CompilerParams(dimension_semantics=("arbitrary",),
                                             vmem_limit_bytes=VMEM_LIMIT),
        name="in_proj",
    )(*args)


def _attn_kernel(*refs, mode, nseg, tq):
    q_ref = refs[0]
    k_refs = refs[1:1 + nseg]
    v_refs = refs[1 + nseg:1 + 2 * nseg]
    gain_ref = refs[1 + 2 * nseg]
    o_ref = refs[2 + 2 * nseg]
    p_ref = refs[3 + 2 * nseg]
    nks = [r.shape[1] for r in k_refs]
    offs = [sum(nks[:j]) for j in range(nseg)]
    n_chunks = tq // ATTN_ROWS

    vs = [r[0] for r in v_refs]
    if mode == "C":
        first_half = lax.broadcasted_iota(jnp.int32, (tq, LANES), 1) < 64
        heads = []
        for sl in range(2):
            qs = q_ref[0, :, sl * 128:(sl + 1) * 128]
            ks_sl = [r[0, :, sl * 128:(sl + 1) * 128] for r in k_refs]
            heads.append((jnp.where(first_half, qs, jnp.zeros_like(qs)), ks_sl))
            heads.append((jnp.where(first_half, jnp.zeros_like(qs), qs), ks_sl))
    else:
        heads = [(q_ref[0, :, h * 128:(h + 1) * 128], [r[0, :, h * 128:(h + 1) * 128] for r in k_refs])
                 for h in range(D_HEADS)]

    def score(h):
        qm, ks = heads[h]
        return [_nt_dot(qm, kk) for kk in ks]

    def softmax(h, ss):
        for c in range(n_chunks):
            rows = slice(c * ATTN_ROWS, (c + 1) * ATTN_ROWS)
            m = ss[0][rows].max(axis=-1, keepdims=True)
            for s in ss[1:]:
                m = jnp.maximum(m, s[rows].max(axis=-1, keepdims=True))
            for s, off, nk in zip(ss, offs, nks):
                p_ref[h, rows, off:off + nk] = jnp.exp2(s[rows] - m).astype(BF16)

    if mode == "C":
        ones_lane = [LANES] * 4
        one_col = jnp.where(lax.broadcasted_iota(jnp.int32, (1, LANES), 1) == 0, 1.0, 0.0).astype(BF16)
        v_ext = []
        for sl in range(2):
            ext = [jnp.concatenate([vv[:, sl * 128:(sl + 1) * 128], jnp.broadcast_to(one_col, (vv.shape[0], LANES))],
                                   axis=1) for vv in vs]
            v_ext += [ext, ext]
    else:
        ones_lane = [LANES, LANES, 0, 0]
        lane_v = lax.broadcasted_iota(jnp.int32, (1, GROUP_W), 1)
        variants = {}
        for ol in (LANES, 0):
            one = jnp.where(lane_v == ol, 1.0, 0.0).astype(BF16)
            keep = jnp.where(lane_v == ol, 0.0, 1.0).astype(BF16)
            variants[ol] = [vv * keep + one for vv in vs]
        v_ext = [variants[ol] for ol in ones_lane]

    def values(h):
        o = None
        for off, nk, vv in zip(offs, nks, v_ext[h]):
            part = _dot(p_ref[h, :, off:off + nk], vv)
            o = part if o is None else o + part
        out_w = LANES if mode == "C" else GROUP_W
        return o[:, 0:out_w] / o[:, ones_lane[h]:ones_lane[h] + 1]

    n_heads = len(heads)
    scores = {0: score(0), 1: score(1)}
    outs = []
    for h in range(n_heads):
        softmax(h, scores.pop(h))
        if h + 2 < n_heads:
            scores[h + 2] = score(h + 2)
        if h >= 1:
            outs.append(values(h - 1))
    outs.append(values(n_heads - 1))

    if mode == "C":
        out = jnp.concatenate([jnp.where(first_half, outs[0], outs[1]),
                               jnp.where(first_half, outs[2], outs[3])], axis=1)
    else:
        lane_head = lax.broadcasted_iota(jnp.int32, (tq, GROUP_W), 1) // D_V
        out = outs[0]
        for h in range(1, D_HEADS):
            out = jnp.where(lane_head == h, outs[h], out)
    o_ref[0] = _rms_full(out, gain_ref[...]).astype(BF16)


def _attn_call(q, ks, vs, g_go, layer, *, mode, tq):
    b, s, wq = q.shape
    nseg = len(ks)
    gain_block = 2 if mode == "C" else 3
    in_specs = [pl.BlockSpec((1, tq, wq), lambda bi, qi: (bi, qi, 0))]
    for a in list(ks) + list(vs):
        in_specs.append(pl.BlockSpec((1,) + a.shape[1:], lambda bi, qi: (bi, 0, 0)))
    in_specs.append(pl.BlockSpec((None, 1, GROUP_W), lambda bi, qi: (layer, 0, gain_block)))
    return pl.pallas_call(
        functools.partial(_attn_kernel, mode=mode, nseg=nseg, tq=tq),
        grid=(b, s // tq),
        in_specs=in_specs,
        out_specs=pl.BlockSpec((1, tq, GROUP_W), lambda bi, qi: (bi, qi, 0)),
        out_shape=jax.ShapeDtypeStruct((b, s, GROUP_W), BF16),
        scratch_shapes=[pltpu.VMEM((4, tq, sum(a.shape[1] for a in ks)), BF16)],
        compiler_params=pltpu.CompilerParams(dimension_semantics=("parallel", "parallel"),
                                             vmem_limit_bytes=VMEM_LIMIT),
        name="attn_" + mode,
    )(q, *ks, *vs, g_go)


def _merge_kernel(x_ref, ya_ref, yb_ref, yc_ref, yd_ref, mod_ref, wout_ref, gffn_ref, wr_ref, br_ref,
                  xo_ref, h2_ref, route_ref, cnt_ref, *, tm):
    nb = tm // MERGE_ROWS
    blocks = [slice(j * MERGE_ROWS, (j + 1) * MERGE_ROWS) for j in range(nb)]
    ys = []
    for rs in blocks:
        y = _dot(ya_ref[rs, :], wout_ref[0:256, :])
        y = y + _dot(yb_ref[rs, :], wout_ref[256:512, :])
        y = y + _dot(yc_ref[rs, :], wout_ref[512:768, :])
        y = y + _dot(yd_ref[rs, :], wout_ref[768:1024, :])
        ys.append(y)
    counts = None
    for rs, y in zip(blocks, ys):
        sel = _merge_block(rs, y, x_ref, mod_ref, gffn_ref, wr_ref, br_ref, xo_ref, h2_ref, route_ref)
        part = sel.sum(axis=0, keepdims=True)
        counts = part if counts is None else counts + part
    cnt_ref[0] = jnp.broadcast_to(counts, (HALO, LANES))


def _merge_block(rs, y, x_ref, mod_ref, gffn_ref, wr_ref, br_ref, xo_ref, h2_ref, route_ref):
    rows = rs.stop - rs.start
    xn = x_ref[rs, :] + mod_ref[0, 2:3, :] * y
    xo_ref[rs, :] = xn
    h2 = _rms_full(xn, gffn_ref[...]) * (1.0 + mod_ref[0, 4:5, :]) + mod_ref[0, 3:4, :]
    h2b = h2.astype(BF16)
    h2_ref[rs, :] = h2b

    h2lo = (h2 - h2b.astype(F32)).astype(BF16)
    part = _nt_dot(wr_ref[...], h2b) + _nt_dot(wr_ref[...], h2lo)
    logit = part[0:ROUTE_SEL_LANE] + part[ROUTE_SEL_LANE:2 * ROUTE_SEL_LANE] + br_ref[...]
    row_of = lambda j: logit[j:j + 1, :]
    lg = [row_of(N_EXPERTS + g) for g in range(N_GROUPS)]

    def first_max(vals):
        m = vals[0]
        for v in vals[1:]:
            m = jnp.maximum(m, v)
        idx = jnp.float32(len(vals) - 1)
        for j in range(len(vals) - 2, -1, -1):
            idx = jnp.where(vals[j] == m, jnp.float32(j), idx)
        return m, idx

    gmax, g_idx = first_max(lg)
    g_w = 1.0 / sum(jnp.exp(v - gmax) for v in lg)
    cand = []
    for j in range(EXPERTS_PER_GROUP):
        c = row_of((N_GROUPS - 1) * EXPERTS_PER_GROUP + j)
        for g in range(N_GROUPS - 2, -1, -1):
            c = jnp.where(g_idx == g, row_of(g * EXPERTS_PER_GROUP + j), c)
        cand.append(c)
    m1, i1 = first_max(cand)
    neg = jnp.float32(-jnp.inf)
    m2, i2 = first_max([jnp.where(i1 == j, neg, c) for j, c in enumerate(cand)])
    r = jnp.exp(m2 - m1)
    w1 = g_w / (1.0 + r)
    w2 = g_w * r / (1.0 + r)
    e1 = g_idx * EXPERTS_PER_GROUP + i1
    e2 = g_idx * EXPERTS_PER_GROUP + i2
    eio = lax.broadcasted_iota(jnp.int32, (N_EXPERTS, rows), 0).astype(F32)
    cmb_t = jnp.where(eio == e1, w1, jnp.where(eio == e2, w2, 0.0))
    sel_t = jnp.where((eio == e1) | (eio == e2), 1.0, 0.0)
    zeros = lambda n: jnp.zeros((n, rows), F32)
    route_t = jnp.concatenate([cmb_t, zeros(ROUTE_SEL_LANE - N_EXPERTS), sel_t,
                               zeros(LANES - ROUTE_SEL_LANE - N_EXPERTS)], axis=0)
    route = route_t.T
    route_ref[rs, :] = route
    lane = lax.broadcasted_iota(jnp.int32, (rows, LANES), 1)
    return jnp.where(lane >= ROUTE_SEL_LANE, route, 0.0)


def _merge_call(x2d, ya, yb, yc, yd, mod, lw, layer, *, tm, mod_index):
    n = x2d.shape[0]
    row = lambda i: (i, 0)
    in_specs = [pl.BlockSpec((tm, D_MODEL), row)]
    in_specs += [pl.BlockSpec((tm, GROUP_W), row)] * 4
    in_specs += [_mod_spec(layer, mod_index)]
    in_specs += [_layer_spec(lw[k], layer) for k in ("w_out", "g_ffn", "w_router", "b_router")]
    return pl.pallas_call(
        functools.partial(_merge_kernel, tm=tm),
        grid=(n // tm,),
        in_specs=in_specs,
        out_specs=[pl.BlockSpec((tm, D_MODEL), row), pl.BlockSpec((tm, D_MODEL), row),
                   pl.BlockSpec((tm, LANES), row), pl.BlockSpec((1, HALO, LANES), lambda i: (i, 0, 0))],
        out_shape=[jax.ShapeDtypeStruct((n, D_MODEL), F32), jax.ShapeDtypeStruct((n, D_MODEL), BF16),
                   jax.ShapeDtypeStruct((n, LANES), F32),
                   jax.ShapeDtypeStruct((n // tm, HALO, LANES), F32)],
        compiler_params=pltpu.CompilerParams(dimension_semantics=("parallel",),
                                             vmem_limit_bytes=VMEM_LIMIT),
        name="merge",
    )(x2d, ya, yb, yc, yd, mod, lw["w_out"], lw["g_ffn"], lw["w_router"], lw["b_router"])


def _segment_positions(route, loc, tm):
    lane = lax.broadcasted_iota(jnp.int32, (tm, LANES), 1)
    is_sel = (lane >= ROUTE_SEL_LANE) & (lane < ROUTE_SEL_LANE + N_EXPERTS)
    selm = is_sel & (route > 0.5)
    selb = jnp.where(selm, 1.0, 0.0).astype(BF16)
    r_i = lax.broadcasted_iota(jnp.int32, (tm, tm), 0)
    c_i = lax.broadcasted_iota(jnp.int32, (tm, tm), 1)
    earlier = jnp.where(c_i < r_i, 1.0, 0.0).astype(BF16)
    dest = loc + _dot(earlier, selb)
    big = jnp.float32(1e9)
    d_a = jnp.where(selm, dest, big).min(axis=-1, keepdims=True)
    d_b = jnp.where(selm, dest, -big).max(axis=-1, keepdims=True)
    return selm, dest, d_a, d_b


def _to_row(col, tm):
    r_i = lax.broadcasted_iota(jnp.int32, (tm, tm), 0)
    c_i = lax.broadcasted_iota(jnp.int32, (tm, tm), 1)
    return jnp.where(r_i == c_i, col, 0.0).sum(axis=0, keepdims=True)


def _chunk_copy_out(xs_ref, xbuf_ref, sem_ref, slot, src_chunk, dst_chunk):
    return pltpu.make_async_copy(xs_ref.at[slot, src_chunk], xbuf_ref.at[dst_chunk], sem_ref.at[slot])


def _dispatch_kernel(nch_ref, dst_ref, plo_ref, phi_ref, nused_ref, h2_ref, route_ref, loc_ref, xbuf_ref,
                     xs_ref, sem_ref, *, tm, n_tiles, n_work, cpw):
    i = pl.program_id(0)
    slot = i % 2
    zero_chunk = MOE_RLOC // MOE_CHUNK

    def start_tile(t_idx, slot_):
        def body(j, carry):
            _chunk_copy_out(xs_ref, xbuf_ref, sem_ref, slot_, j, dst_ref[t_idx * MOE_NCH + j]).start()
            return carry
        lax.fori_loop(0, nch_ref[t_idx], body, 0)

    def wait_tile(t_idx, slot_):
        def body(j, carry):
            _chunk_copy_out(xs_ref, xbuf_ref, sem_ref, slot_, 0, 0).wait()
            return carry
        lax.fori_loop(0, nch_ref[t_idx], body, 0)

    @pl.when(i == 0)
    def _():
        xs_ref[2] = jnp.zeros(xs_ref.shape[1:], BF16)

        def pad_start(j, carry):
            _chunk_copy_out(xs_ref, xbuf_ref, sem_ref, 2, zero_chunk, j).start()
            return carry

        def pad_wait(j, carry):
            _chunk_copy_out(xs_ref, xbuf_ref, sem_ref, 2, zero_chunk, 0).wait()
            return carry
        for e in range(N_EXPERTS):
            lax.fori_loop(plo_ref[e], phi_ref[e], pad_start, 0)
        for e in range(N_EXPERTS):
            lax.fori_loop(plo_ref[e], phi_ref[e], pad_wait, 0)

        def tail_copy(w):
            return pltpu.make_async_copy(xs_ref.at[2, pl.ds(0, cpw)],
                                         xbuf_ref.at[pl.ds(w * cpw, cpw)], sem_ref.at[2])

        def tail_start(w, carry):
            tail_copy(w).start()
            return carry

        def tail_wait(w, carry):
            tail_copy(w).wait()
            return carry
        lax.fori_loop(nused_ref[0], n_work, tail_start, 0)
        lax.fori_loop(nused_ref[0], n_work, tail_wait, 0)

    @pl.when(i >= 2)
    def _():
        wait_tile(i - 2, slot)

    route = route_ref[...]
    selm, dest, d_a, d_b = _segment_positions(route, loc_ref[0, 0:1, :], tm)
    cmb_al = pltpu.roll(route, ROUTE_SEL_LANE, axis=1)
    w_a = jnp.where(selm & (dest == d_a), cmb_al, 0.0).sum(axis=-1, keepdims=True)
    w_b = jnp.where(selm & (dest == d_b), cmb_al, 0.0).sum(axis=-1, keepdims=True)
    da_r, db_r, wa_r, wb_r = (_to_row(v, tm) for v in (d_a, d_b, w_a, w_b))
    h2 = h2_ref[...]
    lane = lax.broadcasted_iota(jnp.int32, (MOE_RBLK, LANES), 1)
    for r0 in range(0, MOE_RLOC, MOE_RBLK):
        rio = (lax.broadcasted_iota(jnp.int32, (MOE_RBLK, tm), 0) + r0).astype(F32)
        hit_a = rio == da_r
        hit_b = rio == db_r
        perm = jnp.where(hit_a | hit_b, 1.0, 0.0).astype(BF16)
        c0, nc = r0 // MOE_CHUNK, MOE_RBLK // MOE_CHUNK
        xs_ref[slot, c0:c0 + nc, :, 0:D_MODEL] = _dot(perm, h2).astype(BF16).reshape(nc, MOE_CHUNK, D_MODEL)
        w_sorted = (jnp.where(hit_a, wa_r, 0.0) + jnp.where(hit_b, wb_r, 0.0)).sum(axis=-1, keepdims=True)
        w_hi = w_sorted.astype(BF16).astype(F32)
        r1 = w_sorted - w_hi
        w_mid = r1.astype(BF16).astype(F32)
        w_lo = r1 - w_mid
        slab = jnp.where(lane == 0, w_hi, jnp.where(lane == 1, w_mid, jnp.where(lane == 2, w_lo, 0.0)))
        xs_ref[slot, c0:c0 + nc, :, D_MODEL:MOE_XW] = slab.astype(BF16).reshape(nc, MOE_CHUNK, LANES)
    start_tile(i, slot)

    @pl.when(i == n_tiles - 1)
    def _():
        if n_tiles > 1:
            wait_tile(i - 1, 1 - slot)
        wait_tile(i, slot)


def _dispatch_call(h2, route, tabs, *, tm, n_rows, work):
    n = h2.shape[0]
    cpw = work // MOE_CHUNK
    n_tiles = n // tm
    grid_spec = pltpu.PrefetchScalarGridSpec(
        num_scalar_prefetch=5,
        grid=(n_tiles,),
        in_specs=[pl.BlockSpec((tm, D_MODEL), lambda i, *_: (i, 0)),
                  pl.BlockSpec((tm, LANES), lambda i, *_: (i, 0)),
                  pl.BlockSpec((1, HALO, LANES), lambda i, *_: (i, 0, 0))],
        out_specs=pl.BlockSpec(memory_space=pl.ANY),
        scratch_shapes=[pltpu.VMEM((3, max(MOE_NCH + 1, cpw), MOE_CHUNK, MOE_XW), BF16),
                        pltpu.SemaphoreType.DMA((3,))],
    )
    xbuf = pl.pallas_call(
        functools.partial(_dispatch_kernel, tm=tm, n_tiles=n_tiles, n_work=n_rows // work, cpw=cpw),
        grid_spec=grid_spec,
        out_shape=jax.ShapeDtypeStruct((n_rows // MOE_CHUNK, MOE_CHUNK, MOE_XW), BF16),
        compiler_params=pltpu.CompilerParams(dimension_semantics=("arbitrary",),
                                             vmem_limit_bytes=VMEM_LIMIT),
        name="moe_dispatch",
    )(tabs["nch"], tabs["dst"], tabs["pad_lo"], tabs["pad_hi"], tabs["n_used"], h2, route, tabs["loc"])
    return xbuf.reshape(n_rows, MOE_XW)


def _expert_kernel(te_ref, nused_ref, x_ref, wup_ref, wdn_ref, y_ref, wup_bf, wdn_bf):
    w = pl.program_id(0)
    used = w < nused_ref[0]

    @pl.when((w == 0) | (te_ref[w] != te_ref[jnp.maximum(w - 1, 0)]))
    def _():
        wup_bf[...] = wup_ref[...].astype(BF16)
        wdn_bf[...] = wdn_ref[...].astype(BF16)

    @pl.when(used)
    def _():
        wt = x_ref[:, D_MODEL:MOE_XW].astype(F32).sum(axis=-1, keepdims=True)
        gu = _dot(x_ref[:, 0:D_MODEL], wup_bf[...])
        g = gu[:, 0:D_EXPERT]
        a = (g * jax.nn.sigmoid(g)) * gu[:, D_EXPERT:2 * D_EXPERT] * wt
        y_ref[...] = _dot(a.astype(BF16), wdn_bf[...]).astype(BF16)

    @pl.when(jnp.logical_not(used))
    def _():
        y_ref[...] = jnp.zeros(y_ref.shape, BF16)


def _expert_call(xbuf, tabs, w_up, w_down, layer, work):
    n_rows = xbuf.shape[0]
    n_work = n_rows // work
    base = layer * N_EXPERTS
    grid_spec = pltpu.PrefetchScalarGridSpec(
        num_scalar_prefetch=2,
        grid=(n_work,),
        in_specs=[pl.BlockSpec((work, MOE_XW), lambda w, te, nu: (jnp.minimum(w, nu[0] - 1), 0)),
                  pl.BlockSpec((None, D_MODEL, 2 * D_EXPERT), lambda w, te, nu: (base + te[w], 0, 0)),
                  pl.BlockSpec((None, D_EXPERT, D_MODEL), lambda w, te, nu: (base + te[w], 0, 0))],
        out_specs=pl.BlockSpec((work, D_MODEL), lambda w, te, nu: (w, 0)),
        scratch_shapes=[pltpu.VMEM((D_MODEL, 2 * D_EXPERT), BF16), pltpu.VMEM((D_EXPERT, D_MODEL), BF16)],
    )
    return pl.pallas_call(
        _expert_kernel,
        grid_spec=grid_spec,
        out_shape=jax.ShapeDtypeStruct((n_rows, D_MODEL), BF16),
        compiler_params=pltpu.CompilerParams(dimension_semantics=("arbitrary",),
                                             vmem_limit_bytes=VMEM_LIMIT),
        name="moe_experts",
    )(tabs["tile_expert"], tabs["n_used"], xbuf, w_up, w_down)


def _chunk_copy_in(ybuf_ref, ys_ref, sem_ref, slot, src_chunk, dst_chunk):
    return pltpu.make_async_copy(ybuf_ref.at[src_chunk], ys_ref.at[slot, dst_chunk], sem_ref.at[slot])


def _combine_kernel(nch_ref, dst_ref, route_ref, loc_ref, x_ref, mod_ref, ybuf_ref, o_ref,
                    ys_ref, sem_ref, *, tm, n_tiles):
    i = pl.program_id(0)
    slot = i % 2

    def start_tile(t_idx, slot_):
        def body(j, carry):
            _chunk_copy_in(ybuf_ref, ys_ref, sem_ref, slot_, dst_ref[t_idx * MOE_NCH + j], j).start()
            return carry
        lax.fori_loop(0, nch_ref[t_idx], body, 0)

    @pl.when(i == 0)
    def _():
        ys_ref[...] = jnp.zeros(ys_ref.shape, BF16)
        start_tile(0, 0)

    @pl.when(i + 1 < n_tiles)
    def _():
        start_tile(i + 1, 1 - slot)

    def wait_body(j, carry):
        _chunk_copy_in(ybuf_ref, ys_ref, sem_ref, slot, 0, 0).wait()
        return carry
    lax.fori_loop(0, nch_ref[i], wait_body, 0)

    _, _, d_a, d_b = _segment_positions(route_ref[...], loc_ref[0, 0:1, :], tm)
    y = None
    for r0 in range(0, MOE_RLOC, MOE_RBLK):
        lio = (lax.broadcasted_iota(jnp.int32, (tm, MOE_RBLK), 1) + r0).astype(F32)
        perm_t = jnp.where((lio == d_a) | (lio == d_b), 1.0, 0.0).astype(BF16)
        c0 = r0 // MOE_CHUNK
        part = _dot(perm_t, ys_ref[slot, c0:c0 + MOE_RBLK // MOE_CHUNK].reshape(MOE_RBLK, D_MODEL))
        y = part if y is None else y + part
    o_ref[...] = x_ref[...] + mod_ref[0, 5:6, :] * y


def _combine_call(ybuf, route, x2d, mod, layer, tabs, *, tm, mod_index):
    n = x2d.shape[0]
    n_tiles = n // tm
    grid_spec = pltpu.PrefetchScalarGridSpec(
        num_scalar_prefetch=2,
        grid=(n_tiles,),
        in_specs=[pl.BlockSpec((tm, LANES), lambda i, *_: (i, 0)),
                  pl.BlockSpec((1, HALO, LANES), lambda i, *_: (i, 0, 0)),
                  pl.BlockSpec((tm, D_MODEL), lambda i, *_: (i, 0)),
                  _mod_spec(layer, mod_index),
                  pl.BlockSpec(memory_space=pl.ANY)],
        out_specs=pl.BlockSpec((tm, D_MODEL), lambda i, *_: (i, 0)),
        scratch_shapes=[pltpu.VMEM((2, MOE_NCH, MOE_CHUNK, D_MODEL), BF16),
                        pltpu.SemaphoreType.DMA((2,))],
    )
    return pl.pallas_call(
        functools.partial(_combine_kernel, tm=tm, n_tiles=n_tiles),
        grid_spec=grid_spec,
        out_shape=jax.ShapeDtypeStruct((n, D_MODEL), F32),
        compiler_params=pltpu.CompilerParams(dimension_semantics=("arbitrary",),
                                             vmem_limit_bytes=VMEM_LIMIT),
        name="moe_combine",
    )(tabs["nch"], tabs["dst"], route, tabs["loc"], x2d, mod,
      ybuf.reshape(ybuf.shape[0] // MOE_CHUNK, MOE_CHUNK, D_MODEL))


def _moe_work(n_tok):
    return MOE_WORK if n_tok >= 8 * MOE_WORK else MOE_WORK // 2


def _moe_rows(n_tok, work):
    n_tiles = n_tok // MOE_T
    cpw = work // MOE_CHUNK
    chunks = 2 * n_tok // MOE_CHUNK + (MOE_CHUNK - 1) * n_tiles + N_EXPERTS * (cpw - 1)
    return -(-chunks // cpw) * work


def _route_tables(counts, n_tok, work):
    cpw = work // MOE_CHUNK
    n_work = _moe_rows(n_tok, work) // work
    cnt = counts[:, 0, ROUTE_SEL_LANE:ROUTE_SEL_LANE + N_EXPERTS].astype(jnp.int32)
    n_tiles = cnt.shape[0]
    nchunk = (cnt + MOE_CHUNK - 1) // MOE_CHUNK
    ei = jnp.arange(N_EXPERTS, dtype=jnp.int32)
    ti = jnp.arange(n_tiles, dtype=jnp.int32)
    upto_e = (ei[:, None] <= ei[None, :]).astype(jnp.int32)
    upto_t = (ti[None, :] <= ti[:, None]).astype(jnp.int32)
    ends = (nchunk[:, :, None] * upto_e[None, :, :]).sum(axis=1)
    loc = ends - nchunk
    nch = nchunk.sum(axis=1)
    per_e = nchunk.sum(axis=0)
    padded_e = (per_e + cpw - 1) // cpw * cpw
    e_end = (padded_e[:, None] * upto_e).sum(axis=0)
    e_base = e_end - padded_e
    glob = e_base[None, :] + (upto_t[:, :, None] * nchunk[None, :, :]).sum(axis=1) - nchunk
    j = jnp.arange(MOE_NCH, dtype=jnp.int32)
    e_of_j = (ends[:, None, :] <= j[None, :, None]).sum(axis=-1)
    seg = e_of_j[:, :, None] == ei[None, None, :]
    dst = jnp.where(seg, (glob - loc)[:, None, :], 0).sum(axis=-1) + j[None, :]
    dst = jnp.where(j[None, :] < nch[:, None], dst, 0)
    n_used = padded_e.sum() // cpw
    w = jnp.minimum(jnp.arange(n_work, dtype=jnp.int32), n_used - 1)
    tile_expert = jnp.minimum(((e_end // cpw)[None, :] <= w[:, None]).sum(axis=-1), N_EXPERTS - 1)
    loc_rows = jnp.pad((loc * MOE_CHUNK).astype(F32),
                       ((0, 0), (ROUTE_SEL_LANE, LANES - ROUTE_SEL_LANE - N_EXPERTS)))
    loc_rows = jnp.broadcast_to(loc_rows[:, None, :], (n_tiles, HALO, LANES))
    return {"nch": nch.astype(jnp.int32), "dst": dst.reshape(-1).astype(jnp.int32),
            "pad_lo": (e_base + per_e).astype(jnp.int32), "pad_hi": e_end.astype(jnp.int32),
            "tile_expert": tile_expert.astype(jnp.int32), "n_used": n_used.astype(jnp.int32).reshape(1),
            "loc": loc_rows}


def _moe_call(h2, route, counts, x2d, mod, w_up, w_down, layer, *, mod_index):
    n = x2d.shape[0]
    work = _moe_work(n)
    tabs = _route_tables(counts, n, work)
    xbuf = _dispatch_call(h2, route, tabs, tm=MOE_T, n_rows=_moe_rows(n, work), work=work)
    ybuf = _expert_call(xbuf, tabs, w_up, w_down, layer, work)
    return _combine_call(ybuf, route, x2d, mod, layer, tabs, tm=MOE_T, mod_index=mod_index)


def _ones_blocks(sizes):
    n = sum(sizes)
    m = np.zeros((n, n), np.float32)
    o = 0
    for s in sizes:
        m[o:o + s, o:o + s] = 1.0
        o += s
    return jnp.asarray(m, BF16)


def _inv_counts_q():
    return jnp.asarray(np.tile(np.concatenate([np.full(64, 1 / 64.0), np.full(64, 1 / 32.0)]), 2)[None, :], F32)


def _rope_tables(rows):
    r = np.repeat(np.arange(rows, dtype=np.float32), GRID_W)
    c = np.tile(np.arange(GRID_W, dtype=np.float32), rows)

    def tab(rot_dim):
        nf = rot_dim // 4
        inv = (np.float32(ROPE_THETA) ** (-np.arange(nf, dtype=np.float32) / np.float32(nf))).astype(np.float32)
        ang = np.concatenate([r[:, None] * inv, c[:, None] * inv], axis=-1).astype(np.float32)
        cos = np.repeat(np.cos(ang), 2, axis=-1)
        sin = np.repeat(np.sin(ang), 2, axis=-1) * np.tile(np.asarray([-1.0, 1.0], np.float32), rot_dim // 2)
        return cos.astype(np.float32), sin.astype(np.float32)

    s = rows * GRID_W
    cc, sc = tab(C_HEAD_DIM)
    cd, sd = tab(D_ROPE)
    one = lambda w: np.ones((s, w), np.float32)
    zero = lambda w: np.zeros((s, w), np.float32)
    tabs = (np.tile(cc, (1, 2)), np.tile(sc, (1, 2)),
            np.concatenate([one(64), cd, one(32)], axis=1), np.concatenate([zero(64), sd, zero(32)], axis=1),
            np.concatenate([cd, one(96)], axis=1), np.concatenate([sd, zero(96)], axis=1))
    return tuple(jnp.asarray(t) for t in tabs)


def _stacked_weights(p):
    (g_attn_norm, w_in, sgu_v_gain, sgu_w, sgu_b, conv_w, gqa_q_gain, gqa_k_gain, mla_qa_gain, mla_w_qb,
     mla_kva_gain, mla_w_kvb, mla_q_gain, mla_k_gain, g_group_out, w_out, g_ffn_norm,
     w_rg, b_rg, w_re, b_re) = p
    depth = w_in.shape[0]


    zeros = lambda *shape: jnp.zeros((depth,) + shape, F32)
    row = lambda a: a[:, None, :]
    w_in_p = jnp.concatenate([w_in, zeros(D_MODEL, IN_W - w_in.shape[2])], axis=2).astype(BF16)
    wqb = jnp.concatenate([jnp.concatenate([mla_w_qb[:, :, h * 96:h * 96 + 96], zeros(D_Q_LORA, 32)], axis=2)
                           for h in range(D_HEADS)], axis=2)
    wk = jnp.concatenate([jnp.concatenate([mla_w_kvb[:, :, h * 128:h * 128 + 64], zeros(D_KV_LORA, 64)], axis=2)
                          for h in range(D_HEADS)], axis=2)
    wv = jnp.concatenate([mla_w_kvb[:, :, h * 128 + 64:h * 128 + 128] for h in range(D_HEADS)], axis=2)
    qslab = jnp.concatenate([mla_q_gain, zeros(32)], axis=1)
    kslab = jnp.concatenate([mla_k_gain[:, :64], zeros(64)], axis=1)
    g_go = g_group_out
    w_out_p = w_out.astype(BF16)
    w_r = jnp.concatenate([w_re, w_rg], axis=2)
    w_r_hi = w_r.astype(BF16)
    w_r_lo = (w_r - w_r_hi.astype(F32)).astype(BF16)
    n_r = N_EXPERTS + N_GROUPS
    zb = lambda w: jnp.zeros((depth, D_MODEL, w), BF16)
    w_router = jnp.concatenate([w_r_hi, zb(ROUTE_SEL_LANE - n_r), w_r_lo, zb(LANES - ROUTE_SEL_LANE - n_r)], axis=2)
    w_router = jnp.swapaxes(w_router, 1, 2)
    b_router = jnp.concatenate([b_re, b_rg, zeros(ROUTE_SEL_LANE - n_r)], axis=1)
    b_router = jnp.broadcast_to(b_router[:, :, None], (depth, ROUTE_SEL_LANE, MERGE_ROWS))
    sgub_full = jnp.repeat(jnp.swapaxes(sgu_b, 1, 2), 64, axis=2)
    return {
        "g_attn": row(g_attn_norm), "w_in": w_in_p,
        "sgu_w": sgu_w.reshape(depth, A_HEADS * CHUNK, CHUNK).astype(BF16), "sgu_b": sgub_full,
        "v_gain": row(sgu_v_gain), "conv_w": conv_w, "g_go": row(g_go),
        "cq_gain": row(jnp.tile(gqa_q_gain, (1, 4))), "ck_gain": row(jnp.tile(gqa_k_gain, (1, 2))),
        "qa_gain": row(mla_qa_gain), "w_qb": wqb.astype(BF16), "kva_gain": row(mla_kva_gain),
        "w_kvb": jnp.concatenate([wk, wv], axis=2).astype(BF16),
        "dq_gain": row(jnp.tile(qslab, (1, 4))), "dk_gain": row(jnp.tile(kslab, (1, 4))),
        "dkr_gain": row(jnp.concatenate([mla_k_gain[:, 64:], zeros(96)], axis=1)),
        "w_out": w_out_p, "g_ffn": row(g_ffn_norm), "w_router": w_router, "b_router": b_router,
    }


def kernel(x, c, ctx, c_ctx, w_mod, b_mod, g_attn_norm, w_in, sgu_v_gain, sgu_w, sgu_b, conv_w, gqa_q_gain,
           gqa_k_gain, mla_qa_gain, mla_w_qb, mla_kva_gain, mla_w_kvb, mla_q_gain, mla_k_gain, g_group_out,
           w_out, g_ffn_norm, w_router_group, b_router_group, w_router_expert, b_router_expert,
           w_expert_up, w_expert_down):
    b, s, d = x.shape
    lc = ctx.shape[1]
    depth = w_mod.shape[0]
    lw = _stacked_weights((g_attn_norm, w_in, sgu_v_gain, sgu_w, sgu_b, conv_w, gqa_q_gain, gqa_k_gain,
                           mla_qa_gain, mla_w_qb, mla_kva_gain, mla_w_kvb, mla_q_gain, mla_k_gain, g_group_out,
                           w_out, g_ffn_norm, w_router_group, b_router_group, w_router_expert, b_router_expert))
    w_up = w_expert_up.reshape(depth * N_EXPERTS, d, 2 * D_EXPERT)
    w_down = w_expert_down.reshape(depth * N_EXPERTS, D_EXPERT, d)

    c_all = jnp.concatenate([c, c_ctx[None, :], jnp.zeros((16 - b - 1, d), F32)], axis=0)
    mod = _modulation(c_all, w_mod, b_mod).reshape(depth, 16, 6, d)
    rope = _rope_tables(s // GRID_W)

    tm = MOE_T
    tq = ATTN_TQ
    tpb = s // tm
    lat_idx = lambda i: i // tpb
    ctx_idx = lambda i: b
    r3 = lambda a, n: a.reshape(b, n, a.shape[-1])

    xl = x.reshape(b * s, d)
    xc = ctx.reshape(b * lc, d)
    for l in range(depth):
        update_ctx = l < depth - 1
        ya_c, yb_c, qc_c, kc_c, vc_c, qd_c, kd_c, vd_c = _in_call(
            xc, mod, lw, l, None, tm=tm, seq_len=lc, mod_index=ctx_idx)
        ya, yb, qc, kc, vc, qd, kd, vd = _in_call(xl, mod, lw, l, rope, tm=IN_T, seq_len=s,
                                                  mod_index=lambda i: i // (s // IN_T))
        yc = _attn_call(r3(qc, s), [r3(kc_c, lc), r3(kc, s)], [r3(vc_c, lc), r3(vc, s)], lw["g_go"], l,
                        mode="C", tq=tq)
        yd = _attn_call(r3(qd, s), [r3(kd_c, lc), r3(kd, s)], [r3(vd_c, lc), r3(vd, s)], lw["g_go"], l,
                        mode="D", tq=tq)
        xl, h2, route, counts = _merge_call(xl, ya, yb, yc.reshape(b * s, GROUP_W), yd.reshape(b * s, GROUP_W),
                                            mod, lw, l, tm=tm, mod_index=lat_idx)
        xl = _moe_call(h2, route, counts, xl, mod, w_up, w_down, l, mod_index=lat_idx)
        if update_ctx:
            yc_c = _attn_call(r3(qc_c, lc), [r3(kc_c, lc)], [r3(vc_c, lc)], lw["g_go"], l, mode="C", tq=lc)
            yd_c = _attn_call(r3(qd_c, lc), [r3(kd_c, lc)], [r3(vd_c, lc)], lw["g_go"], l, mode="D", tq=lc)
            xc, h2c, route_c, counts_c = _merge_call(xc, ya_c, yb_c, yc_c.reshape(b * lc, GROUP_W),
                                                     yd_c.reshape(b * lc, GROUP_W), mod, lw, l, tm=tm,
                                                     mod_index=ctx_idx)
            xc = _moe_call(h2c, route_c, counts_c, xc, mod, w_up, w_down, l, mod_index=ctx_idx)
    return xl.reshape(b, s, d)
```
